```python
import jax, jax.numpy as jnp
from jax import lax
import numpy as np

D_MODEL = 1024
BATCH = 8
SEQ = 8192
DEPTH = 4

RET_HEADS = 4
RET_QK_DIM = 64
RET_V_DIM = 128
RET_CHUNK = 128
MLA_HEADS = 4
MLA_Q_RANK = 384
MLA_KV_RANK = 256
MLA_NOPE_DIM = 128
MLA_ROPE_DIM = 64
MLA_V_DIM = 128
ATTN_BLOCK = 128
ROPE_DIM = 64
ROPE_BASE = 10000.0
CONV_WIDTH = D_MODEL
CONV_KERNEL = 31
EPS = 1e-6

RET_WIDTH = RET_HEADS * RET_V_DIM
MLA_WIDTH = MLA_HEADS * MLA_V_DIM
MIX_WIDTH = RET_WIDTH + MLA_WIDTH
EV_SPLITS = (RET_HEADS * RET_QK_DIM, RET_HEADS * RET_QK_DIM, RET_WIDTH, RET_WIDTH,
             MLA_Q_RANK, MLA_KV_RANK, MLA_ROPE_DIM, MLA_WIDTH)
EV_IN_WIDTH = 2 * RET_HEADS * RET_QK_DIM + 2 * RET_WIDTH + MLA_Q_RANK + MLA_KV_RANK + MLA_ROPE_DIM + MLA_WIDTH
OD_IN_WIDTH = 3 * CONV_WIDTH
N_EVEN = (DEPTH + 1) // 2
N_ODD = DEPTH // 2

kernel_name = 'hybrid_retention_mla_conformer_encoder'


def _split(z, sizes):
    outs = []
    start = 0
    for size in sizes:
        outs.append(z[..., start:start + size])
        start += size
    return outs


def _rmsnorm(x, g):
    xf = x.astype(jnp.float32)
    y = xf * lax.rsqrt(jnp.mean(xf * xf, axis=-1, keepdims=True) + EPS)
    return (y * g.astype(jnp.float32)).astype(x.dtype)


def _layernorm(x, g, b):
    xf = x.astype(jnp.float32)
    mu = jnp.mean(xf, axis=-1, keepdims=True)
    var = jnp.mean(jnp.square(xf - mu), axis=-1, keepdims=True)
    y = (xf - mu) * lax.rsqrt(var + EPS)
    return (y * g.astype(jnp.float32) + b.astype(jnp.float32)).astype(x.dtype)


def _rope_tables(positions):
    inv_freq = ROPE_BASE ** (-jnp.arange(0, ROPE_DIM, 2, dtype=jnp.float32) / ROPE_DIM)
    ang = positions.astype(jnp.float32)[..., None] * inv_freq
    return jnp.cos(ang)[:, :, None, :], jnp.sin(ang)[:, :, None, :]


def _rope(x, cos, sin):
    xf = x.astype(jnp.float32)
    x1, x2 = xf[..., :ROPE_DIM // 2], xf[..., ROPE_DIM // 2:]
    return jnp.concatenate([x1 * cos - x2 * sin, x2 * cos + x1 * sin], axis=-1).astype(x.dtype)


def _retention_dir(q, k, v, log_gamma, include_diag):
    B, S, H, dk = q.shape
    dv = v.shape[-1]
    C = RET_CHUNK
    N = S // C
    qc = q.reshape(B, N, C, H, dk)
    kc = k.reshape(B, N, C, H, dk)
    vc = v.reshape(B, N, C, H, dv)
    idx = jnp.arange(C, dtype=jnp.float32)
    diff = idx[:, None] - idx[None, :]
    mask = (diff >= 0) if include_diag else (diff > 0)
    decay = jnp.where(mask[None], jnp.exp(log_gamma[:, None, None] * jnp.maximum(diff, 0.0)[None]), 0.0)
    scores = jnp.einsum('bnihd,bnjhd->bnhij', qc, kc) * decay
    inner = jnp.einsum('bnhij,bnjhe->bnihe', scores, vc)
    k_w = jnp.exp(log_gamma[None, :] * (C - 1 - idx)[:, None])
    kv = jnp.einsum('bnjhd,jh,bnjhe->nbhde', kc, k_w, vc)
    chunk_decay = jnp.exp(log_gamma * C)[None, :, None, None]

    def step(state, kv_n):
        return state * chunk_decay + kv_n, state

    _, states = lax.scan(step, jnp.zeros((B, H, dk, dv), jnp.float32), kv)
    q_w = jnp.exp(log_gamma[None, :] * (idx + 1.0)[:, None])
    cross = jnp.einsum('bnihd,ih,nbhde->bnihe', qc, q_w, states)
    return (inner + cross).reshape(B, S, H, dv)


def _dense_attention_blocks(q, k, v):
    B, S, H, dq = q.shape
    dv = v.shape[-1]
    nb = S // ATTN_BLOCK
    qb = q.reshape(B, nb, ATTN_BLOCK, H, dq).transpose(1, 0, 2, 3, 4)
    scale = dq ** -0.5

    def one(qblk):
        s = jnp.einsum('bqhd,bkhd->bhqk', qblk, k, preferred_element_type=jnp.float32) * scale
        p = jax.nn.softmax(s, axis=-1).astype(v.dtype)
        return jnp.einsum('bhqk,bkhd->bqhd', p, v)

    o = lax.map(one, qb)
    return o.transpose(1, 0, 2, 3, 4).reshape(B, S, H, dv)


def _even_mixer(h, cos, sin, w_in, dec_f, dec_b, q_norm_g, w_uq, kv_norm_g, w_ukv, w_out):
    B, S, _ = h.shape
    z = h @ w_in
    rq, rk, rv, rg, cq, ckv, kr, mg = _split(z, EV_SPLITS)
    rq = _rope(rq.reshape(B, S, RET_HEADS, RET_QK_DIM), cos, sin).astype(jnp.float32)
    rk = _rope(rk.reshape(B, S, RET_HEADS, RET_QK_DIM), cos, sin).astype(jnp.float32) * (RET_QK_DIM ** -0.5)
    rv = rv.reshape(B, S, RET_HEADS, RET_V_DIM).astype(jnp.float32)
    lg_f = jax.nn.log_sigmoid(dec_f.astype(jnp.float32))
    lg_b = jax.nn.log_sigmoid(dec_b.astype(jnp.float32))
    o_f = _retention_dir(rq, rk, rv, lg_f, True)
    o_b = jnp.flip(_retention_dir(jnp.flip(rq, 1), jnp.flip(rk, 1), jnp.flip(rv, 1), lg_b, False), 1)
    o = o_f + o_b
    mu = jnp.mean(o, axis=-1, keepdims=True)
    var = jnp.mean(jnp.square(o - mu), axis=-1, keepdims=True)
    o = (o - mu) * lax.rsqrt(var + EPS)
    ret_out = o.reshape(B, S, RET_WIDTH).astype(h.dtype) * jax.nn.silu(rg)
    q = (_rmsnorm(cq, q_norm_g) @ w_uq).reshape(B, S, MLA_HEADS, MLA_NOPE_DIM + MLA_ROPE_DIM)
    q = jnp.concatenate([q[..., :MLA_NOPE_DIM], _rope(q[..., MLA_NOPE_DIM:], cos, sin)], axis=-1)
    kv = (_rmsnorm(ckv, kv_norm_g) @ w_ukv).reshape(B, S, MLA_HEADS, MLA_NOPE_DIM + MLA_V_DIM)
    k_nope, v = kv[..., :MLA_NOPE_DIM], kv[..., MLA_NOPE_DIM:]
    k_rope = _rope(kr[:, :, None, :], cos, sin)
    k = jnp.concatenate([k_nope, jnp.broadcast_to(k_rope, (B, S, MLA_HEADS, MLA_ROPE_DIM))], axis=-1)
    a = _dense_attention_blocks(q, k, v)
    mla_out = a.reshape(B, S, MLA_WIDTH) * jax.nn.silu(mg)
    return jnp.concatenate([ret_out, mla_out], axis=-1) @ w_out


def _odd_mixer(h, w_in, b_in, dw_w, dw_b, ln_g, ln_b, w_out):
    z = h @ w_in + b_in
    a, b, g = _split(z, (CONV_WIDTH, CONV_WIDTH, CONV_WIDTH))
    u = a * jax.nn.sigmoid(b)
    u = lax.conv_general_dilated(
        u, dw_w[:, None, :].astype(u.dtype), window_strides=(1,),
        padding=((CONV_KERNEL // 2, CONV_KERNEL // 2),),
        dimension_numbers=('NWC', 'WIO', 'NWC'),
        feature_group_count=CONV_WIDTH) + dw_b
    u = jax.nn.silu(_layernorm(u, ln_g, ln_b))
    return (u * jax.nn.silu(g)) @ w_out


def _fwd_setup_inputs(seed: int = 0) -> dict:
    key = jax.random.key(seed)
    ks = jax.random.split(key, 24)
    f32 = jnp.float32

    def nrm(k, shape, fan_in):
        return jax.random.normal(k, shape, f32) * (fan_in ** -0.5)

    def small(k, shape):
        return 0.02 * jax.random.normal(k, shape, f32)

    def gain(k, shape):
        return 1.0 + 0.02 * jax.random.normal(k, shape, f32)

    hidx = jnp.arange(RET_HEADS, dtype=f32)
    g0 = 1.0 - 2.0 ** (-5.0 - hidx)
    base_logit = jnp.log(g0) - jnp.log1p(-g0)
    return {
        'x': jax.random.normal(ks[0], (BATCH, SEQ, D_MODEL), f32),
        'c': jax.random.normal(ks[1], (BATCH, D_MODEL), f32),
        'positions': jnp.arange(SEQ, dtype=jnp.int32)[None, :]
                     + jax.random.randint(ks[2], (BATCH, 1), 0, SEQ, dtype=jnp.int32),
        'ada_w': 0.5 * nrm(ks[3], (DEPTH, D_MODEL, 3 * D_MODEL), D_MODEL),
        'ada_b': small(ks[4], (DEPTH, 3 * D_MODEL)),
        'pre_g': gain(ks[5], (DEPTH, D_MODEL)),
        'post_g': gain(ks[6], (DEPTH, D_MODEL)),
        'ev_w_in': nrm(ks[7], (N_EVEN, D_MODEL, EV_IN_WIDTH), D_MODEL),
        'ev_dec_f': base_logit[None, :] + 0.1 * jax.random.normal(ks[8], (N_EVEN, RET_HEADS), f32),
        'ev_dec_b': base_logit[None, :] + 0.1 * jax.random.normal(ks[9], (N_EVEN, RET_HEADS), f32),
        'ev_q_norm_g': gain(ks[10], (N_EVEN, MLA_Q_RANK)),
        'ev_w_uq': nrm(ks[11], (N_EVEN, MLA_Q_RANK, MLA_HEADS * (MLA_NOPE_DIM + MLA_ROPE_DIM)), MLA_Q_RANK),
        'ev_kv_norm_g': gain(ks[12], (N_EVEN, MLA_KV_RANK)),
        'ev_w_ukv': nrm(ks[13], (N_EVEN, MLA_KV_RANK, MLA_HEADS * (MLA_NOPE_DIM + MLA_V_DIM)), MLA_KV_RANK),
        'ev_w_out': nrm(ks[14], (N_EVEN, MIX_WIDTH, D_MODEL), MIX_WIDTH),
        'od_w_in': nrm(ks[15], (N_ODD, D_MODEL, OD_IN_WIDTH), D_MODEL),
        'od_b_in': small(ks[16], (N_ODD, OD_IN_WIDTH)),
        'od_dw_w': nrm(ks[17], (N_ODD, CONV_KERNEL, CONV_WIDTH), CONV_KERNEL),
        'od_dw_b': small(ks[18], (N_ODD, CONV_WIDTH)),
        'od_ln_g': gain(ks[19], (N_ODD, CONV_WIDTH)),
        'od_ln_b': small(ks[20], (N_ODD, CONV_WIDTH)),
        'od_w_out': nrm(ks[21], (N_ODD, CONV_WIDTH, D_MODEL), CONV_WIDTH),
    }


def _fwd_reference(x, c, positions, ada_w, ada_b, pre_g, post_g,
              ev_w_in, ev_dec_f, ev_dec_b, ev_q_norm_g, ev_w_uq, ev_kv_norm_g, ev_w_ukv, ev_w_out,
              od_w_in, od_b_in, od_dw_w, od_dw_b, od_ln_g, od_ln_b, od_w_out):
    cos, sin = _rope_tables(positions)
    c_act = jax.nn.silu(c)
    for layer in range(DEPTH):
        mod = c_act @ ada_w[layer] + ada_b[layer]
        shift, scale, gate = _split(mod, (D_MODEL, D_MODEL, D_MODEL))
        h = _rmsnorm(x, pre_g[layer]) * (1.0 + scale[:, None, :]) + shift[:, None, :]
        if layer % 2 == 0:
            i = layer // 2
            y = _even_mixer(h, cos, sin, ev_w_in[i], ev_dec_f[i], ev_dec_b[i], ev_q_norm_g[i],
                            ev_w_uq[i], ev_kv_norm_g[i], ev_w_ukv[i], ev_w_out[i])
        else:
            i = layer // 2
            y = _odd_mixer(h, od_w_in[i], od_b_in[i], od_dw_w[i], od_dw_b[i],
                           od_ln_g[i], od_ln_b[i], od_w_out[i])
        x = x + gate[:, None, :] * _rmsnorm(y, post_g[layer])
    return x


import jax as _jax
import jax.numpy as _jnp

TWIN_FORMAT = 'train_step'
FWD_PARAMS = ['x', 'c', 'positions', 'ada_w', 'ada_b', 'pre_g', 'post_g', 'ev_w_in', 'ev_dec_f', 'ev_dec_b', 'ev_q_norm_g', 'ev_w_uq', 'ev_kv_norm_g', 'ev_w_ukv', 'ev_w_out', 'od_w_in', 'od_b_in', 'od_dw_w', 'od_dw_b', 'od_ln_g', 'od_ln_b', 'od_w_out']
TWIN_WEIGHTS = ['ada_w', 'ada_b', 'pre_g', 'post_g', 'ev_w_in', 'ev_dec_f', 'ev_dec_b', 'ev_q_norm_g', 'ev_w_uq', 'ev_kv_norm_g', 'ev_w_ukv', 'ev_w_out', 'od_w_in', 'od_b_in', 'od_dw_w', 'od_dw_b', 'od_ln_g', 'od_ln_b', 'od_w_out']
TWIN_DIFF_INPUT = 'x'
TWIN_INPUTS = ['x', 'c', 'positions', 'ada_w', 'ada_b', 'pre_g', 'post_g', 'ev_w_in', 'ev_dec_f', 'ev_dec_b', 'ev_q_norm_g', 'ev_w_uq', 'ev_kv_norm_g', 'ev_w_ukv', 'ev_w_out', 'od_w_in', 'od_b_in', 'od_dw_w', 'od_dw_b', 'od_ln_g', 'od_ln_b', 'od_w_out', 'loss_target', 'm_ada_w', 'm_ada_b', 'm_pre_g', 'm_post_g', 'm_ev_w_in', 'm_ev_dec_f', 'm_ev_dec_b', 'm_ev_q_norm_g', 'm_ev_w_uq', 'm_ev_kv_norm_g', 'm_ev_w_ukv', 'm_ev_w_out', 'm_od_w_in', 'm_od_b_in', 'm_od_dw_w', 'm_od_dw_b', 'm_od_ln_g', 'm_od_ln_b', 'm_od_w_out', 'v_ada_w', 'v_ada_b', 'v_pre_g', 'v_post_g', 'v_ev_w_in', 'v_ev_dec_f', 'v_ev_dec_b', 'v_ev_q_norm_g', 'v_ev_w_uq', 'v_ev_kv_norm_g', 'v_ev_w_ukv', 'v_ev_w_out', 'v_od_w_in', 'v_od_b_in', 'v_od_dw_w', 'v_od_dw_b', 'v_od_ln_g', 'v_od_ln_b', 'v_od_w_out']
TWIN_OUTPUTS = ['loss', 'grad_x', 'grad_ada_w', 'grad_ada_b', 'grad_pre_g', 'grad_post_g', 'grad_ev_w_in', 'grad_ev_dec_f', 'grad_ev_dec_b', 'grad_ev_q_norm_g', 'grad_ev_w_uq', 'grad_ev_kv_norm_g', 'grad_ev_w_ukv', 'grad_ev_w_out', 'grad_od_w_in', 'grad_od_b_in', 'grad_od_dw_w', 'grad_od_dw_b', 'grad_od_ln_g', 'grad_od_ln_b', 'grad_od_w_out', 'delta_ada_w', 'delta_ada_b', 'delta_pre_g', 'delta_post_g', 'delta_ev_w_in', 'delta_ev_dec_f', 'delta_ev_dec_b', 'delta_ev_q_norm_g', 'delta_ev_w_uq', 'delta_ev_kv_norm_g', 'delta_ev_w_ukv', 'delta_ev_w_out', 'delta_od_w_in', 'delta_od_b_in', 'delta_od_dw_w', 'delta_od_dw_b', 'delta_od_ln_g', 'delta_od_ln_b', 'delta_od_w_out', 'new_m_ada_w', 'new_m_ada_b', 'new_m_pre_g', 'new_m_post_g', 'new_m_ev_w_in', 'new_m_ev_dec_f', 'new_m_ev_dec_b', 'new_m_ev_q_norm_g', 'new_m_ev_w_uq', 'new_m_ev_kv_norm_g', 'new_m_ev_w_ukv', 'new_m_ev_w_out', 'new_m_od_w_in', 'new_m_od_b_in', 'new_m_od_dw_w', 'new_m_od_dw_b', 'new_m_od_ln_g', 'new_m_od_ln_b', 'new_m_od_w_out', 'new_v_ada_w', 'new_v_ada_b', 'new_v_pre_g', 'new_v_post_g', 'new_v_ev_w_in', 'new_v_ev_dec_f', 'new_v_ev_dec_b', 'new_v_ev_q_norm_g', 'new_v_ev_w_uq', 'new_v_ev_kv_norm_g', 'new_v_ev_w_ukv', 'new_v_ev_w_out', 'new_v_od_w_in', 'new_v_od_b_in', 'new_v_od_dw_w', 'new_v_od_dw_b', 'new_v_od_ln_g', 'new_v_od_ln_b', 'new_v_od_w_out']
TWIN_LEAF_KINDS = {'loss': 'loss', 'grad_x': 'grad_x', 'grad_ada_w': 'grad_w', 'grad_ada_b': 'grad_w', 'grad_pre_g': 'grad_w', 'grad_post_g': 'grad_w', 'grad_ev_w_in': 'grad_w', 'grad_ev_dec_f': 'grad_w', 'grad_ev_dec_b': 'grad_w', 'grad_ev_q_norm_g': 'grad_w', 'grad_ev_w_uq': 'grad_w', 'grad_ev_kv_norm_g': 'grad_w', 'grad_ev_w_ukv': 'grad_w', 'grad_ev_w_out': 'grad_w', 'grad_od_w_in': 'grad_w', 'grad_od_b_in': 'grad_w', 'grad_od_dw_w': 'grad_w', 'grad_od_dw_b': 'grad_w', 'grad_od_ln_g': 'grad_w', 'grad_od_ln_b': 'grad_w', 'grad_od_w_out': 'grad_w', 'delta_ada_w': 'delta_w', 'delta_ada_b': 'delta_w', 'delta_pre_g': 'delta_w', 'delta_post_g': 'delta_w', 'delta_ev_w_in': 'delta_w', 'delta_ev_dec_f': 'delta_w', 'delta_ev_dec_b': 'delta_w', 'delta_ev_q_norm_g': 'delta_w', 'delta_ev_w_uq': 'delta_w', 'delta_ev_kv_norm_g': 'delta_w', 'delta_ev_w_ukv': 'delta_w', 'delta_ev_w_out': 'delta_w', 'delta_od_w_in': 'delta_w', 'delta_od_b_in': 'delta_w', 'delta_od_dw_w': 'delta_w', 'delta_od_dw_b': 'delta_w', 'delta_od_ln_g': 'delta_w', 'delta_od_ln_b': 'delta_w', 'delta_od_w_out': 'delta_w', 'new_m_ada_w': 'new_m', 'new_m_ada_b': 'new_m', 'new_m_pre_g': 'new_m', 'new_m_post_g': 'new_m', 'new_m_ev_w_in': 'new_m', 'new_m_ev_dec_f': 'new_m', 'new_m_ev_dec_b': 'new_m', 'new_m_ev_q_norm_g': 'new_m', 'new_m_ev_w_uq': 'new_m', 'new_m_ev_kv_norm_g': 'new_m', 'new_m_ev_w_ukv': 'new_m', 'new_m_ev_w_out': 'new_m', 'new_m_od_w_in': 'new_m', 'new_m_od_b_in': 'new_m', 'new_m_od_dw_w': 'new_m', 'new_m_od_dw_b': 'new_m', 'new_m_od_ln_g': 'new_m', 'new_m_od_ln_b': 'new_m', 'new_m_od_w_out': 'new_m', 'new_v_ada_w': 'new_v', 'new_v_ada_b': 'new_v', 'new_v_pre_g': 'new_v', 'new_v_post_g': 'new_v', 'new_v_ev_w_in': 'new_v', 'new_v_ev_dec_f': 'new_v', 'new_v_ev_dec_b': 'new_v', 'new_v_ev_q_norm_g': 'new_v', 'new_v_ev_w_uq': 'new_v', 'new_v_ev_kv_norm_g': 'new_v', 'new_v_ev_w_ukv': 'new_v', 'new_v_ev_w_out': 'new_v', 'new_v_od_w_in': 'new_v', 'new_v_od_b_in': 'new_v', 'new_v_od_dw_w': 'new_v', 'new_v_od_dw_b': 'new_v', 'new_v_od_ln_g': 'new_v', 'new_v_od_ln_b': 'new_v', 'new_v_od_w_out': 'new_v'}


def _forward(args):
    return _fwd_reference(*[args[k] for k in FWD_PARAMS])


def _output_shape():
    def fwd():
        inp = _fwd_setup_inputs(0)
        return _fwd_reference(*[inp[k] for k in FWD_PARAMS])
    out = _jax.eval_shape(fwd)
    return out.shape, out.dtype

N_MICROBATCH = 1
ADAM_LR = 0.001
ADAM_B1 = 0.9
ADAM_B2 = 0.999
ADAM_EPS = 1e-08
ADAM_WD = 0.01
ADAM_STEP = 10
PER_EXAMPLE_BATCH_AXIS = {'x': 0, 'c': 0, 'positions': 0, 'loss_target': 0}
SHARED_INPUTS = []
_WEIGHT_DTYPES = {'ada_w': _jnp.float32, 'ada_b': _jnp.float32, 'pre_g': _jnp.float32, 'post_g': _jnp.float32, 'ev_w_in': _jnp.float32, 'ev_dec_f': _jnp.float32, 'ev_dec_b': _jnp.float32, 'ev_q_norm_g': _jnp.float32, 'ev_w_uq': _jnp.float32, 'ev_kv_norm_g': _jnp.float32, 'ev_w_ukv': _jnp.float32, 'ev_w_out': _jnp.float32, 'od_w_in': _jnp.float32, 'od_b_in': _jnp.float32, 'od_dw_w': _jnp.float32, 'od_dw_b': _jnp.float32, 'od_ln_g': _jnp.float32, 'od_ln_b': _jnp.float32, 'od_w_out': _jnp.float32}
MOMENT_SCALE = {'ada_w': 2.375661e+00, 'ada_b': 5.227501e+00, 'pre_g': 2.352835e-01, 'post_g': 6.657351e+00, 'ev_w_in': 2.011466e-01, 'ev_dec_f': 1.310453e+00, 'ev_dec_b': 2.164842e+00, 'ev_q_norm_g': 2.568983e-02, 'ev_w_uq': 1.782689e-02, 'ev_kv_norm_g': 1.386133e-01, 'ev_w_ukv': 7.463231e-02, 'ev_w_out': 1.882776e-01, 'od_w_in': 1.285757e-01, 'od_b_in': 1.841668e-01, 'od_dw_w': 1.519205e-01, 'od_dw_b': 4.216328e-01, 'od_ln_g': 2.399015e-01, 'od_ln_b': 2.555697e-01, 'od_w_out': 1.807887e-01}


def _to_microbatches(a, axis):
    t = _jnp.moveaxis(a, axis, 0)
    t = t.reshape((N_MICROBATCH, t.shape[0] // N_MICROBATCH) + t.shape[1:])
    return _jnp.moveaxis(t, 1, axis + 1)


def setup_inputs(seed: int = 0) -> dict:
    inp = _fwd_setup_inputs(seed)
    key = _jax.random.fold_in(_jax.random.key(seed), 7919)
    shape, _ = _output_shape()
    out = dict(inp)
    out["loss_target"] = _jax.random.normal(_jax.random.fold_in(key, 0), shape, _jnp.float32)
    for i, name in enumerate(TWIN_WEIGHTS):
        w = inp[name].astype(_jnp.float32)
        if MOMENT_SCALE is None:
            s = _jnp.sqrt(_jnp.mean(_jnp.square(w)) + 1e-30)
        else:
            s = MOMENT_SCALE[name]
        km, kv = _jax.random.split(_jax.random.fold_in(key, i + 1))
        out[name] = w
        out["m_" + name] = s * _jax.random.normal(km, w.shape, _jnp.float32)
        out["v_" + name] = (s * s) * _jax.random.uniform(kv, w.shape, _jnp.float32, 0.5, 1.5)
    if N_MICROBATCH > 1:
        for name, axis in PER_EXAMPLE_BATCH_AXIS.items():
            out[name] = _to_microbatches(out[name], axis)
    return {'x': out['x'], 'c': out['c'], 'positions': out['positions'], 'ada_w': out['ada_w'], 'ada_b': out['ada_b'], 'pre_g': out['pre_g'], 'post_g': out['post_g'], 'ev_w_in': out['ev_w_in'], 'ev_dec_f': out['ev_dec_f'], 'ev_dec_b': out['ev_dec_b'], 'ev_q_norm_g': out['ev_q_norm_g'], 'ev_w_uq': out['ev_w_uq'], 'ev_kv_norm_g': out['ev_kv_norm_g'], 'ev_w_ukv': out['ev_w_ukv'], 'ev_w_out': out['ev_w_out'], 'od_w_in': out['od_w_in'], 'od_b_in': out['od_b_in'], 'od_dw_w': out['od_dw_w'], 'od_dw_b': out['od_dw_b'], 'od_ln_g': out['od_ln_g'], 'od_ln_b': out['od_ln_b'], 'od_w_out': out['od_w_out'], 'loss_target': out['loss_target'], 'm_ada_w': out['m_ada_w'], 'm_ada_b': out['m_ada_b'], 'm_pre_g': out['m_pre_g'], 'm_post_g': out['m_post_g'], 'm_ev_w_in': out['m_ev_w_in'], 'm_ev_dec_f': out['m_ev_dec_f'], 'm_ev_dec_b': out['m_ev_dec_b'], 'm_ev_q_norm_g': out['m_ev_q_norm_g'], 'm_ev_w_uq': out['m_ev_w_uq'], 'm_ev_kv_norm_g': out['m_ev_kv_norm_g'], 'm_ev_w_ukv': out['m_ev_w_ukv'], 'm_ev_w_out': out['m_ev_w_out'], 'm_od_w_in': out['m_od_w_in'], 'm_od_b_in': out['m_od_b_in'], 'm_od_dw_w': out['m_od_dw_w'], 'm_od_dw_b': out['m_od_dw_b'], 'm_od_ln_g': out['m_od_ln_g'], 'm_od_ln_b': out['m_od_ln_b'], 'm_od_w_out': out['m_od_w_out'], 'v_ada_w': out['v_ada_w'], 'v_ada_b': out['v_ada_b'], 'v_pre_g': out['v_pre_g'], 'v_post_g': out['v_post_g'], 'v_ev_w_in': out['v_ev_w_in'], 'v_ev_dec_f': out['v_ev_dec_f'], 'v_ev_dec_b': out['v_ev_dec_b'], 'v_ev_q_norm_g': out['v_ev_q_norm_g'], 'v_ev_w_uq': out['v_ev_w_uq'], 'v_ev_kv_norm_g': out['v_ev_kv_norm_g'], 'v_ev_w_ukv': out['v_ev_w_ukv'], 'v_ev_w_out': out['v_ev_w_out'], 'v_od_w_in': out['v_od_w_in'], 'v_od_b_in': out['v_od_b_in'], 'v_od_dw_w': out['v_od_dw_w'], 'v_od_dw_b': out['v_od_dw_b'], 'v_od_ln_g': out['v_od_ln_g'], 'v_od_ln_b': out['v_od_ln_b'], 'v_od_w_out': out['v_od_w_out']}


def _loss(weights, diff, rest, loss_target):
    with _jax.named_scope("forward"):
        args = {**rest, TWIN_DIFF_INPUT: diff, **{k: w.astype(_WEIGHT_DTYPES[k]) for k, w in weights.items()}}
        y = _forward(args)
    with _jax.named_scope("loss_head"):
        err = _jnp.square(y.astype(_jnp.float32) - loss_target)
        return 0.5 * _jnp.sum(_jnp.mean(err, axis=-1)) if err.ndim else 0.5 * err


def _adamw(w, g, m, v):
    m = ADAM_B1 * m + (1.0 - ADAM_B1) * g
    v = ADAM_B2 * v + (1.0 - ADAM_B2) * _jnp.square(g)
    m_hat = m / (1.0 - ADAM_B1 ** ADAM_STEP)
    v_hat = v / (1.0 - ADAM_B2 ** ADAM_STEP)
    delta = -ADAM_LR * (m_hat / (_jnp.sqrt(v_hat) + ADAM_EPS) + ADAM_WD * w)
    return delta, m, v


def reference(x, c, positions, ada_w, ada_b, pre_g, post_g, ev_w_in, ev_dec_f, ev_dec_b, ev_q_norm_g, ev_w_uq, ev_kv_norm_g, ev_w_ukv, ev_w_out, od_w_in, od_b_in, od_dw_w, od_dw_b, od_ln_g, od_ln_b, od_w_out, loss_target, m_ada_w, m_ada_b, m_pre_g, m_post_g, m_ev_w_in, m_ev_dec_f, m_ev_dec_b, m_ev_q_norm_g, m_ev_w_uq, m_ev_kv_norm_g, m_ev_w_ukv, m_ev_w_out, m_od_w_in, m_od_b_in, m_od_dw_w, m_od_dw_b, m_od_ln_g, m_od_ln_b, m_od_w_out, v_ada_w, v_ada_b, v_pre_g, v_post_g, v_ev_w_in, v_ev_dec_f, v_ev_dec_b, v_ev_q_norm_g, v_ev_w_uq, v_ev_kv_norm_g, v_ev_w_ukv, v_ev_w_out, v_od_w_in, v_od_b_in, v_od_dw_w, v_od_dw_b, v_od_ln_g, v_od_ln_b, v_od_w_out):
    given = dict(x=x, c=c, positions=positions, ada_w=ada_w, ada_b=ada_b, pre_g=pre_g, post_g=post_g, ev_w_in=ev_w_in, ev_dec_f=ev_dec_f, ev_dec_b=ev_dec_b, ev_q_norm_g=ev_q_norm_g, ev_w_uq=ev_w_uq, ev_kv_norm_g=ev_kv_norm_g, ev_w_ukv=ev_w_ukv, ev_w_out=ev_w_out, od_w_in=od_w_in, od_b_in=od_b_in, od_dw_w=od_dw_w, od_dw_b=od_dw_b, od_ln_g=od_ln_g, od_ln_b=od_ln_b, od_w_out=od_w_out, loss_target=loss_target, m_ada_w=m_ada_w, m_ada_b=m_ada_b, m_pre_g=m_pre_g, m_post_g=m_post_g, m_ev_w_in=m_ev_w_in, m_ev_dec_f=m_ev_dec_f, m_ev_dec_b=m_ev_dec_b, m_ev_q_norm_g=m_ev_q_norm_g, m_ev_w_uq=m_ev_w_uq, m_ev_kv_norm_g=m_ev_kv_norm_g, m_ev_w_ukv=m_ev_w_ukv, m_ev_w_out=m_ev_w_out, m_od_w_in=m_od_w_in, m_od_b_in=m_od_b_in, m_od_dw_w=m_od_dw_w, m_od_dw_b=m_od_dw_b, m_od_ln_g=m_od_ln_g, m_od_ln_b=m_od_ln_b, m_od_w_out=m_od_w_out, v_ada_w=v_ada_w, v_ada_b=v_ada_b, v_pre_g=v_pre_g, v_post_g=v_post_g, v_ev_w_in=v_ev_w_in, v_ev_dec_f=v_ev_dec_f, v_ev_dec_b=v_ev_dec_b, v_ev_q_norm_g=v_ev_q_norm_g, v_ev_w_uq=v_ev_w_uq, v_ev_kv_norm_g=v_ev_kv_norm_g, v_ev_w_ukv=v_ev_w_ukv, v_ev_w_out=v_ev_w_out, v_od_w_in=v_od_w_in, v_od_b_in=v_od_b_in, v_od_dw_w=v_od_dw_w, v_od_dw_b=v_od_dw_b, v_od_ln_g=v_od_ln_g, v_od_ln_b=v_od_ln_b, v_od_w_out=v_od_w_out)
    weights = {n: given[n] for n in TWIN_WEIGHTS}
    shared = {n: given[n] for n in SHARED_INPUTS}
    per_example = {n: given[n] for n in ['x', 'c', 'positions']}
    grad_fn = _jax.value_and_grad(_loss, argnums=(0, 1))

    def one_microbatch(ex, loss_target):
        ex = dict(ex)
        diff = ex.pop(TWIN_DIFF_INPUT)
        return grad_fn(weights, diff, {**shared, **ex}, loss_target)

    if N_MICROBATCH == 1:
        loss, (grad_w, grad_x) = one_microbatch(per_example, given["loss_target"])
    else:
        def body(carry, xs):
            loss_sum, grad_sum = carry
            l_k, (gw_k, gx_k) = one_microbatch(xs[0], xs[1])
            with _jax.named_scope("update"):
                return (loss_sum + l_k, _jax.tree.map(_jnp.add, grad_sum, gw_k)), gx_k

        init = (_jnp.zeros((), _jnp.float32), _jax.tree.map(_jnp.zeros_like, weights))
        (loss, grad_w), grad_x = _jax.lax.scan(body, init, (per_example, given["loss_target"]))
    with _jax.named_scope("update"):
        delta_w, new_m, new_v = {}, {}, {}
        for n in TWIN_WEIGHTS:
            delta_w[n], new_m[n], new_v[n] = _adamw(weights[n], grad_w[n], given["m_" + n], given["v_" + n])
    return (loss, grad_x, *[grad_w[n] for n in TWIN_WEIGHTS], *[delta_w[n] for n in TWIN_WEIGHTS],
            *[new_m[n] for n in TWIN_WEIGHTS], *[new_v[n] for n in TWIN_WEIGHTS])
```

```python
import functools

import numpy as np
import jax
import jax.numpy as jnp
from jax import lax
from jax.experimental import pallas as pl
from jax.experimental.pallas import tpu as pltpu

F32 = jnp.float32
BF16 = jnp.bfloat16
MESH = pl.DeviceIdType.MESH

N_DEV = 8
D = 1024
DEPTH = 4
EPS = 1e-6
RET_HEADS = 4
MLA_HEADS = 4
RET_SCALE = 64 ** -0.5
MLA_SCALE = 192 ** -0.5
CONV_K = 31
CONV_HALO = 16
ROPE_BASE = 10000.0

ADAM_LR = 0.001
ADAM_B1 = 0.9
ADAM_B2 = 0.999
ADAM_EPS = 1e-08
ADAM_WD = 0.01
ADAM_STEP = 10

LANES = 128
VMEM_LIMIT = 48 * 1024 * 1024

ZL_EV = dict(rv=(0, 512), rg=(512, 512), mg=(1024, 512), cq=(1536, 384), kr=(1920, 128),
             rq=(2048, 512), rk=(2560, 512), ckv=(3072, 256))
ZW_EV = 3328
ZW_OD = 3072


def _cparams(sem, vmem=VMEM_LIMIT):
    return pltpu.CompilerParams(dimension_semantics=sem, vmem_limit_bytes=vmem)


def _pick(n, prefs):
    for p in prefs:
        if n % p == 0:
            return p
    return n


def _sigmoid(x):
    return 0.5 * (jnp.tanh(0.5 * x) + 1.0)


def _silu(x):
    return x * _sigmoid(x)


def _rms(x, g):
    return x * lax.rsqrt(jnp.mean(x * x, axis=-1, keepdims=True) + EPS) * g


def _log_sigmoid(x):
    return jnp.minimum(x, 0.0) - jnp.log(1.0 + jnp.exp(jnp.minimum(x, -x)))


def _tile_lanes(t, width):
    reps = width // t.shape[1]
    return t if reps == 1 else jnp.concatenate([t] * reps, axis=1)


def _rot_half(x):
    w = x.shape[1]
    lane = lax.broadcasted_iota(jnp.int32, x.shape, 1)
    first = jnp.bitwise_and(lane, 63) < 32
    return jnp.where(first, pltpu.roll(x, w - 32, 1), pltpu.roll(x, 32, 1))


def _rope(x, cos, sin):
    w = x.shape[1]
    return x * _tile_lanes(cos, w) + _rot_half(x) * _tile_lanes(sin, w)


def _rope_t(dy, cos, sin):
    w = dy.shape[1]
    return dy * _tile_lanes(cos, w) + _rot_half(dy * _tile_lanes(sin, w))


_DN = {"nn": (((1,), (0,)), ((), ())), "nt": (((1,), (1,)), ((), ())), "tn": (((0,), (0,)), ((), ()))}


def _dot(a, b, mode):
    return lax.dot_general(a.astype(BF16), b.astype(BF16), _DN[mode], preferred_element_type=F32)


@functools.partial(jax.custom_vjp, nondiff_argnums=(2,))
def _bdot(a, b, mode):
    return _dot(a, b, mode)


def _bdot_fwd(a, b, mode):
    return _dot(a, b, mode), (a, b)


def _bdot_bwd(mode, res, g):
    a, b = res
    if mode == "nn":
        return _dot(g, b, "nt"), _dot(a, g, "tn")
    if mode == "nt":
        return _dot(g, b, "nn"), _dot(g, a, "tn")
    return _dot(b, g, "nt"), _dot(a, g, "nn")


_bdot.defvjp(_bdot_fwd, _bdot_bwd)


def _rowwise(name, body, row_ins, vec_ins, row_outs, red_outs=(), tile=512):
    s = row_ins[0][0].shape[0]
    tile = min(tile, s)
    nr, nv, no = len(row_ins), len(vec_ins), len(row_outs)

    def kern(*refs):
        rows = [r[...] for r in refs[:nr]]
        vecs = [r[...] for r in refs[nr:nr + nv]]
        outs, reds = body(rows, vecs)
        for r, o in zip(refs[nr + nv:nr + nv + no], outs):
            r[...] = o.astype(r.dtype)
        red_refs = refs[nr + nv + no:]
        if red_refs:
            @pl.when(pl.program_id(0) == 0)
            def _():
                for r in red_refs:
                    r[...] = jnp.zeros(r.shape, r.dtype)
            for r, v in zip(red_refs, reds):
                r[...] += v

    in_specs = [pl.BlockSpec((tile, w), (lambda i, cb=cb: (i, cb))) for (_, w, cb) in row_ins]
    in_specs += [pl.BlockSpec(v.shape, (lambda i, nd=v.ndim: (0,) * nd)) for v in vec_ins]
    out_specs = [pl.BlockSpec((tile, w), lambda i: (i, 0)) for (w, _) in row_outs]
    out_specs += [pl.BlockSpec(sh, lambda i: (0, 0)) for sh in red_outs]
    out_shape = [jax.ShapeDtypeStruct((s, w), dt) for (w, dt) in row_outs]
    out_shape += [jax.ShapeDtypeStruct(sh, F32) for sh in red_outs]
    res = pl.pallas_call(
        kern, name=name, grid=(s // tile,), in_specs=in_specs, out_specs=out_specs, out_shape=out_shape,
        compiler_params=_cparams(("arbitrary",)),
    )(*[a for (a, _, _) in row_ins], *vec_ins)
    return res[:no], res[no:]


def _mm(name, a, b, mode, out_dtype=F32, bias=None):
    if mode == "tn":
        k, m = a.shape
        n = b.shape[1]
        tm = m if m <= 512 else _pick(m, (512, 256, 128))
        tn = n if n <= 512 else _pick(n, (512, 256, 128))
        tk = min(k, 1024)
        nk = k // tk

        def kern(a_ref, b_ref, o_ref, acc_ref):
            kk = pl.program_id(2)
            part = _dot(a_ref[...], b_ref[...], "tn")

            @pl.when(kk == 0)
            def _():
                acc_ref[...] = part

            @pl.when(kk > 0)
            def _():
                acc_ref[...] += part

            @pl.when(kk == nk - 1)
            def _():
                o_ref[...] = acc_ref[...].astype(o_ref.dtype)

        return pl.pallas_call(
            kern, name=name, grid=(m // tm, n // tn, nk),
            in_specs=[pl.BlockSpec((tk, tm), lambda i, j, kk: (kk, i)),
                      pl.BlockSpec((tk, tn), lambda i, j, kk: (kk, j))],
            out_specs=pl.BlockSpec((tm, tn), lambda i, j, kk: (i, j)),
            out_shape=jax.ShapeDtypeStruct((m, n), out_dtype),
            scratch_shapes=[pltpu.VMEM((tm, tn), F32)],
            compiler_params=_cparams(("parallel", "parallel", "arbitrary")),
        )(a, b)

    m, k = a.shape
    n = b.shape[1] if mode == "nn" else b.shape[0]
    tm = min(m, 1024)
    tn = n if n <= 512 else _pick(n, (512, 256, 128))
    has_bias = bias is not None

    def kern(*refs):
        a_ref, b_ref = refs[0], refs[1]
        o_ref = refs[-1]
        r = _dot(a_ref[...], b_ref[...], mode)
        if has_bias:
            r = r + refs[2][...]
        o_ref[...] = r.astype(o_ref.dtype)

    b_spec = (pl.BlockSpec((k, tn), lambda i, j: (0, j)) if mode == "nn"
              else pl.BlockSpec((tn, k), lambda i, j: (j, 0)))
    in_specs = [pl.BlockSpec((tm, k), lambda i, j: (i, 0)), b_spec]
    args = [a, b]
    if has_bias:
        in_specs.append(pl.BlockSpec((1, tn), lambda i, j: (0, j)))
        args.append(bias)
    return pl.pallas_call(
        kern, name=name, grid=(m // tm, n // tn), in_specs=in_specs,
        out_specs=pl.BlockSpec((tm, tn), lambda i, j: (i, j)),
        out_shape=jax.ShapeDtypeStruct((m, n), out_dtype),
        compiler_params=_cparams(("parallel", "parallel")),
    )(*args)


def _peers():
    mx, my, mc = lax.axis_index("x"), lax.axis_index("y"), lax.axis_index("c")
    me = 4 * mx + 2 * my + mc
    out = []
    for k in range(1, N_DEV):
        px = 1 - mx if (k >> 2) & 1 else mx
        py = 1 - my if (k >> 1) & 1 else my
        pc = 1 - mc if k & 1 else mc
        out.append((k, (px, py, pc), 4 * px + 2 * py + pc))
    return me, out


def _exchange(name, x, scatter):
    blk = x.shape[1:] if scatter else x.shape

    def body(x_ref, out_ref, send_sems, recv_sems, local_sem):
        me, peers = _peers()
        mine = x_ref.at[me] if scatter else x_ref
        local = pltpu.make_async_copy(mine, out_ref.at[me], local_sem)
        local.start()
        sends = []
        for k, dev, p in peers:
            cp = pltpu.make_async_remote_copy(
                src_ref=x_ref.at[p] if scatter else x_ref, dst_ref=out_ref.at[me],
                send_sem=send_sems.at[k - 1], recv_sem=recv_sems.at[k - 1],
                device_id=dev, device_id_type=MESH)
            cp.start()
            sends.append(cp)
        for cp in sends:
            cp.wait_send()
        for k, dev, p in peers:
            pltpu.make_async_remote_copy(
                src_ref=mine, dst_ref=out_ref.at[p],
                send_sem=send_sems.at[k - 1], recv_sem=recv_sems.at[k - 1],
                device_id=dev, device_id_type=MESH).wait_recv()
        local.wait()

    return pl.pallas_call(
        body, name=name,
        in_specs=[pl.BlockSpec(memory_space=pl.ANY)],
        out_specs=pl.BlockSpec(memory_space=pl.ANY),
        out_shape=jax.ShapeDtypeStruct((N_DEV,) + tuple(blk), x.dtype),
        scratch_shapes=[pltpu.SemaphoreType.DMA((N_DEV - 1,)), pltpu.SemaphoreType.DMA((N_DEV - 1,)),
                        pltpu.SemaphoreType.DMA],
        compiler_params=pltpu.CompilerParams(has_side_effects=True),
    )(x)


def _sum_parts(name, x):
    p, r, c = x.shape
    tr = _pick(r, (256, 128, 112, 80, 64, 16, 8))

    def kern(x_ref, o_ref):
        acc = x_ref[0]
        for i in range(1, p):
            acc = acc + x_ref[i]
        o_ref[...] = acc

    return pl.pallas_call(
        kern, name=name, grid=(r // tr,),
        in_specs=[pl.BlockSpec((p, tr, c), lambda i: (0, i, 0))],
        out_specs=pl.BlockSpec((tr, c), lambda i: (i, 0)),
        out_shape=jax.ShapeDtypeStruct((r, c), F32),
        compiler_params=_cparams(("parallel",)),
    )(x)


def _adamw(name, w, m, v, g):
    r, c = w.shape
    parts = g.shape[0] if g.ndim == 3 else 0
    tr = 512 if (r > 512 and r % 512 == 0) else r

    def kern(w_ref, m_ref, v_ref, g_ref, go_ref, d_ref, mo_ref, vo_ref):
        if parts:
            gg = g_ref[0]
            for i in range(1, parts):
                gg = gg + g_ref[i]
        else:
            gg = g_ref[...]
        mm = ADAM_B1 * m_ref[...] + (1.0 - ADAM_B1) * gg
        vv = ADAM_B2 * v_ref[...] + (1.0 - ADAM_B2) * (gg * gg)
        m_hat = mm / (1.0 - ADAM_B1 ** ADAM_STEP)
        v_hat = vv / (1.0 - ADAM_B2 ** ADAM_STEP)
        go_ref[...] = gg
        d_ref[...] = -ADAM_LR * (m_hat / (jnp.sqrt(v_hat) + ADAM_EPS) + ADAM_WD * w_ref[...])
        mo_ref[...] = mm
        vo_ref[...] = vv

    spec = pl.BlockSpec((tr, c), lambda i: (i, 0))
    gspec = pl.BlockSpec((parts, tr, c), lambda i: (0, i, 0)) if parts else spec
    sh = jax.ShapeDtypeStruct((r, c), F32)
    return pl.pallas_call(
        kern, name=name, grid=(r // tr,), in_specs=[spec, spec, spec, gspec],
        out_specs=[spec] * 4, out_shape=[sh] * 4,
        compiler_params=_cparams(("parallel",)),
    )(w, m, v, g)


def _ret_chunk(q, k, v, st, dec_cc, dec_cd, dec_dd, reverse):
    c = q.shape[0]
    row = lax.broadcasted_iota(jnp.int32, (c, c), 0).astype(F32)
    col = lax.broadcasted_iota(jnp.int32, (c, c), 1).astype(F32)
    pos = lax.broadcasted_iota(jnp.int32, (c, LANES), 0).astype(F32)
    if reverse:
        diff, mask = col - row, col > row
        q_exp, k_exp = c - pos, pos
    else:
        diff, mask = row - col, row >= col
        q_exp, k_exp = pos + 1.0, c - 1.0 - pos
    decay = jnp.where(mask, jnp.exp(_log_sigmoid(dec_cc) * jnp.maximum(diff, 0.0)), 0.0)
    lam_cd = _log_sigmoid(dec_cd)
    scores = _bdot(q, k, "nt") * decay
    o = _bdot(scores, v, "nn") + _bdot(q * jnp.exp(lam_cd * q_exp), st, "nn")
    st_new = st * jnp.exp(_log_sigmoid(dec_dd) * float(c)) + _bdot(k * jnp.exp(lam_cd * k_exp), v, "tn")
    return o, st_new


def _ret_dec(dec_ref, h, c):
    d = dec_ref[:, h:h + 1]
    return (jnp.broadcast_to(d, (c, c)), jnp.broadcast_to(d, (c, LANES)), jnp.broadcast_to(d, (LANES, LANES)))


def _ret_fwd(name, q, k, z, dec, reverse, chunk):
    s = q.shape[0]
    chunk = min(chunk, s)
    n = s // chunk
    cmap = (lambda i: (n - 1 - i, 0)) if reverse else (lambda i: (i, 0))
    smap = (lambda i: (n - 1 - i, 0, 0, 0)) if reverse else (lambda i: (i, 0, 0, 0))

    def kern(q_ref, k_ref, v_ref, dec_ref, o_ref, st_out_ref, st_ref):
        @pl.when(pl.program_id(0) == 0)
        def _():
            st_ref[...] = jnp.zeros(st_ref.shape, F32)

        for h in range(RET_HEADS):
            sl = slice(LANES * h, LANES * (h + 1))
            st = st_ref[h]
            st_out_ref[h] = st
            o, st_new = _ret_chunk(q_ref[:, sl].astype(F32), k_ref[:, sl].astype(F32), v_ref[:, sl],
                                   st, *_ret_dec(dec_ref, h, chunk), reverse)
            o_ref[:, sl] = o
            st_ref[h] = st_new

    return pl.pallas_call(
        kern, name=name, grid=(n,),
        in_specs=[pl.BlockSpec((chunk, 512), cmap), pl.BlockSpec((chunk, 512), cmap),
                  pl.BlockSpec((chunk, 512), cmap), pl.BlockSpec((1, RET_HEADS), lambda i: (0, 0))],
        out_specs=[pl.BlockSpec((chunk, 512), cmap), pl.BlockSpec((None, RET_HEADS, LANES, LANES), smap)],
        out_shape=[jax.ShapeDtypeStruct((s, 512), F32), jax.ShapeDtypeStruct((n, RET_HEADS, LANES, LANES), F32)],
        scratch_shapes=[pltpu.VMEM((RET_HEADS, LANES, LANES), F32)],
        compiler_params=_cparams(("arbitrary",)),
    )(q, k, z, dec)


def _ret_bwd(name, q, k, z, dec, states, do, reverse, chunk):
    s = q.shape[0]
    chunk = min(chunk, s)
    n = s // chunk
    cmap = (lambda i: (i, 0)) if reverse else (lambda i: (n - 1 - i, 0))
    smap = (lambda i: (i, 0, 0, 0)) if reverse else (lambda i: (n - 1 - i, 0, 0, 0))

    def kern(q_ref, k_ref, v_ref, dec_ref, st_in_ref, do_ref, dq_ref, dk_ref, dv_ref, ddec_ref, dst_ref):
        @pl.when(pl.program_id(0) == 0)
        def _():
            dst_ref[...] = jnp.zeros(dst_ref.shape, F32)
            ddec_ref[...] = jnp.zeros(ddec_ref.shape, F32)

        lane = lax.broadcasted_iota(jnp.int32, (1, LANES), 1)
        ddec = jnp.zeros((1, LANES), F32)
        for h in range(RET_HEADS):
            sl = slice(LANES * h, LANES * (h + 1))
            fn = functools.partial(_ret_chunk, reverse=reverse)
            _, vjp = jax.vjp(fn, q_ref[:, sl].astype(F32), k_ref[:, sl].astype(F32), v_ref[:, sl],
                             st_in_ref[h], *_ret_dec(dec_ref, h, chunk))
            dq, dk, dv, dst, d_cc, d_cd, d_dd = vjp((do_ref[:, sl], dst_ref[h]))
            dq_ref[:, sl] = dq
            dk_ref[:, sl] = dk
            dv_ref[:, sl] = dv
            dst_ref[h] = dst
            tot = (jnp.sum(jnp.sum(d_cc, axis=1, keepdims=True), axis=0, keepdims=True)
                   + jnp.sum(jnp.sum(d_cd, axis=1, keepdims=True), axis=0, keepdims=True)
                   + jnp.sum(jnp.sum(d_dd, axis=1, keepdims=True), axis=0, keepdims=True))
            ddec = ddec + jnp.where(lane == h, tot, 0.0)
        ddec_ref[...] += ddec

    cspec = pl.BlockSpec((chunk, 512), cmap)
    return pl.pallas_call(
        kern, name=name, grid=(n,),
        in_specs=[cspec, cspec, cspec, pl.BlockSpec((1, RET_HEADS), lambda i: (0, 0)),
                  pl.BlockSpec((None, RET_HEADS, LANES, LANES), smap), cspec],
        out_specs=[cspec, cspec, cspec, pl.BlockSpec((1, LANES), lambda i: (0, 0))],
        out_shape=[jax.ShapeDtypeStruct((s, 512), F32)] * 3 + [jax.ShapeDtypeStruct((1, LANES), F32)],
        scratch_shapes=[pltpu.VMEM((RET_HEADS, LANES, LANES), F32)],
        compiler_params=_cparams(("arbitrary",)),
    )(q, k, z, dec, states, do)


def _fa_fwd(name, q, k, v, tq, tk):
    s = q.shape[0]
    tq, tk = min(tq, s), min(tk, s)
    nk = s // tk

    def kern(q_ref, k_ref, v_ref, o_ref, lse_ref, m_ref, l_ref, acc_ref):
        j = pl.program_id(2)

        @pl.when(j == 0)
        def _():
            m_ref[...] = jnp.full(m_ref.shape, -jnp.inf, F32)
            l_ref[...] = jnp.zeros(l_ref.shape, F32)
            acc_ref[...] = jnp.zeros(acc_ref.shape, F32)

        sc = _dot(q_ref[...], k_ref[...], "nt")
        m_prev = m_ref[...]
        m_new = jnp.maximum(m_prev, jnp.max(sc, axis=1, keepdims=True))
        alpha = jnp.exp(m_prev - m_new)
        p = jnp.exp(sc - m_new)
        l_ref[...] = alpha * l_ref[...] + jnp.sum(p, axis=1, keepdims=True)
        acc_ref[...] = alpha * acc_ref[...] + _dot(p, v_ref[...], "nn")
        m_ref[...] = m_new

        @pl.when(j == nk - 1)
        def _():
            o_ref[...] = acc_ref[...] / l_ref[...]
            lse_ref[...] = m_ref[...] + jnp.log(l_ref[...])

    return pl.pallas_call(
        kern, name=name, grid=(MLA_HEADS, s // tq, nk),
        in_specs=[pl.BlockSpec((tq, 256), lambda h, i, j: (i, h)),
                  pl.BlockSpec((tk, 256), lambda h, i, j: (j, h)),
                  pl.BlockSpec((tk, LANES), lambda h, i, j: (j, h))],
        out_specs=[pl.BlockSpec((tq, LANES), lambda h, i, j: (i, h)),
                   pl.BlockSpec((None, tq, 1), lambda h, i, j: (h, i, 0))],
        out_shape=[jax.ShapeDtypeStruct((s, 512), F32), jax.ShapeDtypeStruct((MLA_HEADS, s, 1), F32)],
        scratch_shapes=[pltpu.VMEM((tq, 1), F32), pltpu.VMEM((tq, 1), F32), pltpu.VMEM((tq, LANES), F32)],
        compiler_params=_cparams(("parallel", "parallel", "arbitrary")),
    )(q, k, v)


def _fa_bwd(name, q, k, kt, v, do, lse, delta, tq, tk):
    s = q.shape[0]
    tq, tk = min(tq, s), min(tk, s)
    nq = s // tq

    def kern(q_ref, k_ref, kt_ref, v_ref, do_ref, lse_ref, dl_ref, dqt_ref, dk_ref, dv_ref):
        j, i = pl.program_id(1), pl.program_id(2)
        qb, dob = q_ref[...], do_ref[...]
        st = _dot(k_ref[...], qb, "nt")
        pt = jnp.exp(st - lse_ref[...])
        dpt = _dot(v_ref[...], dob, "nt")
        dst = (pt * (dpt - dl_ref[...])).astype(BF16)
        dv_part = _dot(pt, dob, "nn")
        dk_part = _dot(dst, qb, "nn")
        dqt_part = _dot(kt_ref[...], dst, "nn")

        @pl.when(i == 0)
        def _():
            dv_ref[...] = dv_part
            dk_ref[...] = dk_part

        @pl.when(i > 0)
        def _():
            dv_ref[...] += dv_part
            dk_ref[...] += dk_part

        @pl.when(j == 0)
        def _():
            dqt_ref[i] = dqt_part

        @pl.when(j > 0)
        def _():
            dqt_ref[i] += dqt_part

    return pl.pallas_call(
        kern, name=name, grid=(MLA_HEADS, s // tk, nq),
        in_specs=[pl.BlockSpec((tq, 256), lambda h, j, i: (i, h)),
                  pl.BlockSpec((tk, 256), lambda h, j, i: (j, h)),
                  pl.BlockSpec((None, 256, tk), lambda h, j, i: (h, 0, j)),
                  pl.BlockSpec((tk, LANES), lambda h, j, i: (j, h)),
                  pl.BlockSpec((tq, LANES), lambda h, j, i: (i, h)),
                  pl.BlockSpec((None, 1, tq), lambda h, j, i: (h, 0, i)),
                  pl.BlockSpec((None, 1, tq), lambda h, j, i: (h, 0, i))],
        out_specs=[pl.BlockSpec((None, nq, 256, tq), lambda h, j, i: (h, 0, 0, 0)),
                   pl.BlockSpec((tk, 256), lambda h, j, i: (j, h)),
                   pl.BlockSpec((tk, LANES), lambda h, j, i: (j, h))],
        out_shape=[jax.ShapeDtypeStruct((MLA_HEADS, nq, 256, tq), F32),
                   jax.ShapeDtypeStruct((s, 1024), F32), jax.ShapeDtypeStruct((s, 512), F32)],
        compiler_params=_cparams(("parallel", "arbitrary", "arbitrary")),
    )(q, k, kt, v, do, lse, delta)


def _fill_padded(dst_ref, src_ref, s):
    zeros = jnp.zeros((CONV_HALO, LANES), F32)
    dst_ref[pl.ds(0, CONV_HALO), :] = zeros
    dst_ref[pl.ds(CONV_HALO + s, CONV_HALO), :] = zeros
    dst_ref[pl.ds(CONV_HALO, s), :] = src_ref[...]


def _shifted_windows(win):
    n = win.shape[0]
    return [win] + [pltpu.roll(win, n - b, 0) for b in range(1, 8)]


def _conv_fwd(name, u, w, bias, rc=256):
    s = u.shape[0]
    rc = min(rc, s)

    def kern(u_ref, w_ref, b_ref, o_ref, pad_ref):
        _fill_padded(pad_ref, u_ref, s)
        wv = w_ref[...]
        bv = b_ref[...]

        def chunk(r, carry):
            base = pl.multiple_of(r * rc, rc)
            wins = _shifted_windows(pad_ref[pl.ds(base, rc + 2 * CONV_HALO), :])
            acc = jnp.broadcast_to(bv, (rc, LANES))
            for kk in range(CONV_K):
                a, b = divmod(kk + 1, 8)
                acc = acc + wv[kk:kk + 1, :] * wins[b][8 * a:8 * a + rc]
            o_ref[pl.ds(base, rc), :] = acc
            return carry

        lax.fori_loop(0, s // rc, chunk, 0)

    return pl.pallas_call(
        kern, name=name, grid=(D // LANES,),
        in_specs=[pl.BlockSpec((s, LANES), lambda c: (0, c)), pl.BlockSpec((32, LANES), lambda c: (0, c)),
                  pl.BlockSpec((1, LANES), lambda c: (0, c))],
        out_specs=pl.BlockSpec((s, LANES), lambda c: (0, c)),
        out_shape=jax.ShapeDtypeStruct((s, D), F32),
        scratch_shapes=[pltpu.VMEM((s + 2 * CONV_HALO, LANES), F32)],
        compiler_params=_cparams(("parallel",)),
    )(u, w, bias)


def _conv_bwd(name, u, g, w, rc=256):
    s = u.shape[0]
    rc = min(rc, s)

    def kern(u_ref, g_ref, w_ref, du_ref, dw_ref, db_ref, upad_ref, gpad_ref, dwacc_ref):
        _fill_padded(upad_ref, u_ref, s)
        _fill_padded(gpad_ref, g_ref, s)
        dwacc_ref[...] = jnp.zeros(dwacc_ref.shape, F32)
        wv = w_ref[...]

        def chunk(r, carry):
            base = pl.multiple_of(r * rc, rc)
            gwins = _shifted_windows(gpad_ref[pl.ds(base, rc + 2 * CONV_HALO), :])
            uwins = _shifted_windows(upad_ref[pl.ds(base, rc + 2 * CONV_HALO), :])
            gc = g_ref[pl.ds(base, rc), :]
            acc = jnp.zeros((rc, LANES), F32)
            for kk in range(CONV_K):
                a, b = divmod(CONV_K - kk, 8)
                acc = acc + wv[kk:kk + 1, :] * gwins[b][8 * a:8 * a + rc]
                a, b = divmod(kk + 1, 8)
                prod = gc * uwins[b][8 * a:8 * a + rc]
                dwacc_ref[kk] += jnp.sum(prod.reshape(rc // 8, 8, LANES), axis=0)
            dwacc_ref[CONV_K] += jnp.sum(gc.reshape(rc // 8, 8, LANES), axis=0)
            du_ref[pl.ds(base, rc), :] = acc
            return carry

        lax.fori_loop(0, s // rc, chunk, 0)
        tot = jnp.sum(dwacc_ref[...], axis=1)
        lane_row = lax.broadcasted_iota(jnp.int32, (32, LANES), 0)
        dw_ref[...] = jnp.where(lane_row < CONV_K, tot, 0.0)
        db_ref[...] = tot[CONV_K:CONV_K + 1, :]

    cs = pl.BlockSpec((s, LANES), lambda c: (0, c))
    return pl.pallas_call(
        kern, name=name, grid=(D // LANES,),
        in_specs=[cs, cs, pl.BlockSpec((32, LANES), lambda c: (0, c))],
        out_specs=[cs, pl.BlockSpec((32, LANES), lambda c: (0, c)), pl.BlockSpec((1, LANES), lambda c: (0, c))],
        out_shape=[jax.ShapeDtypeStruct((s, D), F32), jax.ShapeDtypeStruct((32, D), F32),
                   jax.ShapeDtypeStruct((1, D), F32)],
        scratch_shapes=[pltpu.VMEM((s + 2 * CONV_HALO, LANES), F32), pltpu.VMEM((s + 2 * CONV_HALO, LANES), F32),
                        pltpu.VMEM((32, 8, LANES), F32)],
        compiler_params=_cparams(("parallel",)),
    )(u, g, w)


def _mod_local(name, c_all, ada_w):
    def kern(c_ref, w_ref, o_ref):
        o_ref[...] = jnp.dot(_silu(c_ref[...]), w_ref[...], preferred_element_type=F32,
                             precision=lax.Precision.HIGHEST)

    return pl.pallas_call(
        kern, name=name, grid=(DEPTH,),
        in_specs=[pl.BlockSpec((N_DEV, D), lambda l: (0, 0)), pl.BlockSpec((None, D, 384), lambda l: (l, 0, 0))],
        out_specs=pl.BlockSpec((None, N_DEV, 384), lambda l: (l, 0, 0)),
        out_shape=jax.ShapeDtypeStruct((DEPTH, N_DEV, 384), F32),
        compiler_params=_cparams(("parallel",)),
    )(c_all, ada_w)


def _ada_w_grad(name, c_all_t, dmod):
    def kern(c_ref, d_ref, o_ref):
        o_ref[...] = jnp.dot(_silu(c_ref[...]), d_ref[...], preferred_element_type=F32,
                             precision=lax.Precision.HIGHEST)

    return pl.pallas_call(
        kern, name=name, grid=(DEPTH,),
        in_specs=[pl.BlockSpec((D, LANES), lambda l: (0, 0)), pl.BlockSpec((None, LANES, 384), lambda l: (l, 0, 0))],
        out_specs=pl.BlockSpec((None, D, 384), lambda l: (l, 0, 0)),
        out_shape=jax.ShapeDtypeStruct((DEPTH, D, 384), F32),
        compiler_params=_cparams(("parallel",)),
    )(c_all_t, dmod)


def _pre_fn(x, g, scale, shift):
    return _rms(x, g) * (1.0 + scale) + shift


def _post_fn(y, g, gate):
    return gate * _rms(y, g)


def _ev_post_fn(o_heads, rg, a, mg):
    normed = []
    for oh in o_heads:
        mu = jnp.mean(oh, axis=-1, keepdims=True)
        var = jnp.mean(jnp.square(oh - mu), axis=-1, keepdims=True)
        normed.append((oh - mu) * lax.rsqrt(var + EPS))
    return jnp.concatenate([jnp.concatenate(normed, axis=1) * _silu(rg), a * _silu(mg)], axis=1)


def _od_post_fn(u, g, ln_g, ln_b):
    mu = jnp.mean(u, axis=-1, keepdims=True)
    var = jnp.mean(jnp.square(u - mu), axis=-1, keepdims=True)
    y = (u - mu) * lax.rsqrt(var + EPS) * ln_g + ln_b
    return _silu(y) * _silu(g)


def _glu_fn(a, b):
    return a * _sigmoid(b)


def _heads(x):
    return [x[:, LANES * h:LANES * (h + 1)] for h in range(4)]


def _colsum(x):
    return jnp.sum(x, axis=0, keepdims=True)


def _pad16(r):
    return -(-r // 16) * 16


def _pad_rows(a, rows):
    return a if a.shape[0] == rows else jnp.concatenate([a, jnp.zeros((rows - a.shape[0],) + a.shape[1:], a.dtype)], axis=0)


def _as_rows1024(a):
    return a.reshape(-1, D)


EV_PARTS = ((344, D), (96, 384), (128, 256), (128, D))
OD_PARTS = ((384, D), (128, D))


def _part_rows(parts):
    return [_pad16(r * c // D) for (r, c) in parts]


def _pack_parts(arrs, parts):
    out = []
    for a, rows in zip(arrs, _part_rows(parts)):
        out.append(_pad_rows(_as_rows1024(a), rows))
    return jnp.concatenate(out, axis=0)


def _unpack_parts(buf, parts):
    out, off = [], 0
    for (r, c), rows in zip(parts, _part_rows(parts)):
        real = r * c // D
        out.append(buf[:, off:off + real, :].reshape(N_DEV * r, c))
        off += rows
    return out


def _split_parts(buf, parts):
    out, off = [], 0
    for (r, c), rows in zip(parts, _part_rows(parts)):
        out.append(buf[off:off + r * c // D, :].reshape(r, c))
        off += rows
    return out


def _zrows(n, c):
    return jnp.zeros((n, c), BF16)


def _ev_win_layout(wt):
    rq = [p for h in range(4) for p in (wt[64 * h:64 * h + 64], _zrows(64, D))]
    rk = [p for h in range(4) for p in (wt[256 + 64 * h:256 + 64 * h + 64], _zrows(64, D))]
    return jnp.concatenate([wt[512:1024], wt[1024:1536], wt[2240:2752], wt[1536:1920],
                            wt[2176:2240], _zrows(64, D)] + rq + rk + [wt[1920:2176]], axis=0)


def _ev_win_unlayout(g):
    rq = [g[2048 + 128 * h:2048 + 128 * h + 64] for h in range(4)]
    rk = [g[2560 + 128 * h:2560 + 128 * h + 64] for h in range(4)]
    return jnp.concatenate(rq + rk + [g[0:512], g[512:1024], g[1536:1920], g[3072:3328],
                                      g[1920:1984], g[1024:1536]], axis=0)


def _uq_layout(wt):
    return jnp.concatenate([p for h in range(4) for p in (wt[192 * h:192 * h + 192], _zrows(64, 384))], axis=0)


def _uq_unlayout(g):
    return jnp.concatenate([g[256 * h:256 * h + 192] for h in range(4)], axis=0)


def _ukv_layout(wt):
    kpart = [p for h in range(4) for p in (wt[256 * h:256 * h + 128], _zrows(128, 256))]
    vpart = [wt[256 * h + 128:256 * h + 256] for h in range(4)]
    return jnp.concatenate(kpart + vpart, axis=0)


def _ukv_unlayout(g):
    return jnp.concatenate([p for h in range(4) for p in (g[256 * h:256 * h + 128], g[1024 + 128 * h:1024 + 128 * h + 128])],
                           axis=0)


def kernel(x, c, positions, ada_w, ada_b, pre_g, post_g, ev_w_in, ev_dec_f, ev_dec_b, ev_q_norm_g, ev_w_uq, ev_kv_norm_g, ev_w_ukv, ev_w_out, od_w_in, od_b_in, od_dw_w, od_dw_b, od_ln_g, od_ln_b, od_w_out, loss_target, m_ada_w, m_ada_b, m_pre_g, m_post_g, m_ev_w_in, m_ev_dec_f, m_ev_dec_b, m_ev_q_norm_g, m_ev_w_uq, m_ev_kv_norm_g, m_ev_w_ukv, m_ev_w_out, m_od_w_in, m_od_b_in, m_od_dw_w, m_od_dw_b, m_od_ln_g, m_od_ln_b, m_od_w_out, v_ada_w, v_ada_b, v_pre_g, v_post_g, v_ev_w_in, v_ev_dec_f, v_ev_dec_b, v_ev_q_norm_g, v_ev_w_uq, v_ev_kv_norm_g, v_ev_w_ukv, v_ev_w_out, v_od_w_in, v_od_b_in, v_od_dw_w, v_od_dw_b, v_od_ln_g, v_od_ln_b, v_od_w_out):
    s = x.shape[1]
    me = 4 * lax.axis_index("x") + 2 * lax.axis_index("y") + lax.axis_index("c")
    x0 = x.reshape(s, D)
    tgt = loss_target.reshape(s, D)
    ret_chunk = 256
    fa_tq, fa_tk = 512, 1024

    start_parts = [c.reshape(-1), od_b_in.reshape(-1), od_dw_w.reshape(-1), od_dw_b.reshape(-1),
                   od_ln_g.reshape(-1), od_ln_b.reshape(-1)]
    start_sizes = [p.shape[0] for p in start_parts]
    start_len = -(-sum(start_sizes) // 1024) * 1024
    start_vec = jnp.concatenate(start_parts + [jnp.zeros((start_len - sum(start_sizes),), F32)])
    start_all = _exchange("gather_start", start_vec.reshape(-1, LANES), False).reshape(N_DEV, start_len)
    offs = np.cumsum([0] + start_sizes)
    c_all = start_all[:, offs[0]:offs[1]]
    b_in_all = start_all[:, offs[1]:offs[2]].reshape(N_DEV, 2, 384).transpose(1, 0, 2).reshape(2, 1, ZW_OD)
    dw_w_all = start_all[:, offs[2]:offs[3]].reshape(N_DEV, 2, CONV_K, LANES).transpose(1, 2, 0, 3).reshape(2, CONV_K, D)
    dw_w_all = jnp.concatenate([dw_w_all, jnp.zeros((2, 1, D), F32)], axis=1)
    dw_b_all = start_all[:, offs[3]:offs[4]].reshape(N_DEV, 2, LANES).transpose(1, 0, 2).reshape(2, 1, D)
    ln_g_all = start_all[:, offs[4]:offs[5]].reshape(N_DEV, 2, LANES).transpose(1, 0, 2).reshape(2, 1, D)
    ln_b_all = start_all[:, offs[5]:offs[6]].reshape(N_DEV, 2, LANES).transpose(1, 0, 2).reshape(2, 1, D)

    mod_loc = _mod_local("mod_local", c_all, ada_w)
    mod_all = _exchange("gather_mod", mod_loc.reshape(DEPTH * N_DEV, 384), False).reshape(N_DEV, DEPTH, N_DEV, 384)
    mod = lax.dynamic_index_in_dim(mod_all, me, axis=2, keepdims=False)
    mod = mod.transpose(1, 0, 2).reshape(DEPTH, 3 * D) + ada_b
    shift = [mod[l:l + 1, 0:D] for l in range(DEPTH)]
    scale = [mod[l:l + 1, D:2 * D] for l in range(DEPTH)]
    gate = [mod[l:l + 1, 2 * D:3 * D] for l in range(DEPTH)]

    ev_w = []
    for i in range(2):
        buf = _pack_parts([ev_w_in[i].T.astype(BF16), ev_w_uq[i].T.astype(BF16), ev_w_ukv[i].T.astype(BF16),
                           ev_w_out[i].astype(BF16)], EV_PARTS)
        win_t, uq_t, ukv_t, wout = _unpack_parts(_exchange(f"gather_w_ev{i}", buf, False), EV_PARTS)
        ev_w.append((_ev_win_layout(win_t), _uq_layout(uq_t), _ukv_layout(ukv_t), wout))
    od_w = []
    for i in range(2):
        buf = _pack_parts([od_w_in[i].T.astype(BF16), od_w_out[i].astype(BF16)], OD_PARTS)
        od_w.append(tuple(_unpack_parts(_exchange(f"gather_w_od{i}", buf, False), OD_PARTS)))

    inv_freq = ROPE_BASE ** (-jnp.arange(0, 64, 2, dtype=F32) / 64)
    invf = jnp.tile(inv_freq, 4).reshape(1, LANES)
    sgn = jnp.tile(jnp.concatenate([-jnp.ones((32,), F32), jnp.ones((32,), F32)]), 2).reshape(1, LANES)

    def rope_body(rows, vecs):
        ang = rows[0].astype(F32) * vecs[0]
        return [jnp.cos(ang), jnp.sin(ang) * vecs[1]], []

    (cos_t, sin_t), _ = _rowwise("rope_tables", rope_body, [(positions.reshape(s, 1), 1, 0)], [invf, sgn],
                                 [(LANES, F32), (LANES, F32)])

    saved = []
    xl = x0
    for l in range(DEPTH):
        i = l // 2
        sv = dict(x=xl)

        def pre_body(rows, vecs):
            return [_pre_fn(rows[0], *vecs)], []

        (h,), _ = _rowwise(f"pre{l}", pre_body, [(xl, D, 0)], [pre_g[l:l + 1], scale[l], shift[l]], [(D, BF16)])
        sv["h"] = h
        if l % 2 == 0:
            win_t, uq_t, ukv_t, wout = ev_w[i]
            z = _mm(f"ev_in{l}", h, win_t, "nt")
            sv["z"] = z
            dec_f, dec_b = ev_dec_f[i:i + 1], ev_dec_b[i:i + 1]

            def prep_body(rows, vecs):
                rq, rk, cq, ckv, kr, cos, sin = rows
                return [_rope(rq, cos, sin), _rope(rk, cos, sin) * RET_SCALE, _rms(cq, vecs[0]), _rms(ckv, vecs[1]),
                        _rope(kr, cos, sin)], []

            (rq_r, rk_r, qn, kvn, krr), _ = _rowwise(
                f"ev_prep{l}", prep_body,
                [(z, 512, 4), (z, 512, 5), (z, 384, 4), (z, 256, 12), (z, LANES, 15), (cos_t, LANES, 0), (sin_t, LANES, 0)],
                [ev_q_norm_g[i:i + 1], ev_kv_norm_g[i:i + 1]],
                [(512, BF16), (512, BF16), (384, BF16), (256, BF16), (LANES, BF16)])
            sv.update(rq_r=rq_r, rk_r=rk_r, qn=qn, kvn=kvn)
            o_f, st_f = _ret_fwd(f"ret_f{l}", rq_r, rk_r, z, dec_f, False, ret_chunk)
            o_b, st_b = _ret_fwd(f"ret_b{l}", rq_r, rk_r, z, dec_b, True, ret_chunk)
            sv.update(o_f=o_f, o_b=o_b, st_f=st_f, st_b=st_b)
            q_pad = _mm(f"ev_uq{l}", qn, uq_t, "nt")
            kv_pad = _mm(f"ev_ukv{l}", kvn, ukv_t, "nt")

            def mla_prep_body(rows, vecs):
                qp, kk, vv, kr_r, cos, sin = rows
                qs, ks = [], []
                for hh in range(4):
                    qs += [qp[:, 256 * hh:256 * hh + 128], _rope(qp[:, 256 * hh + 128:256 * hh + 256], cos, sin)]
                    ks += [kk[:, 256 * hh:256 * hh + 128].astype(BF16), kr_r]
                return [jnp.concatenate(qs, axis=1) * MLA_SCALE, jnp.concatenate(ks, axis=1), vv], []

            (qcat, kcat, v_b), _ = _rowwise(
                f"mla_prep{l}", mla_prep_body,
                [(q_pad, 1024, 0), (kv_pad, 1024, 0), (kv_pad, 512, 2), (krr, LANES, 0), (cos_t, LANES, 0), (sin_t, LANES, 0)],
                [], [(1024, BF16), (1024, BF16), (512, BF16)])
            a_mla, lse = _fa_fwd(f"fa_fwd{l}", qcat, kcat, v_b, fa_tq, fa_tk)
            sv.update(qcat=qcat, kcat=kcat, v_b=v_b, a_mla=a_mla, lse=lse)

            def ev_post_body(rows, vecs):
                of, ob, rg, a, mg = rows
                return [_ev_post_fn(_heads(of + ob), rg, a, mg)], []

            (act,), _ = _rowwise(f"ev_post{l}", ev_post_body,
                                 [(o_f, 512, 0), (o_b, 512, 0), (z, 512, 1), (a_mla, 512, 0), (z, 512, 2)], [], [(D, BF16)])
        else:
            win_t, wout = od_w[i]
            z = _mm(f"od_in{l}", h, win_t, "nt", bias=b_in_all[i])
            sv["z"] = z

            def glu_body(rows, vecs):
                return [_glu_fn(rows[0], rows[1])], []

            (u,), _ = _rowwise(f"glu{l}", glu_body, [(z, D, 0), (z, D, 1)], [], [(D, F32)])
            u2 = _conv_fwd(f"conv{l}", u, dw_w_all[i], dw_b_all[i])
            sv.update(u=u, u2=u2)

            def od_post_body(rows, vecs):
                return [_od_post_fn(rows[0], rows[1], vecs[0], vecs[1])], []

            (act,), _ = _rowwise(f"od_post{l}", od_post_body, [(u2, D, 0), (z, D, 2)], [ln_g_all[i], ln_b_all[i]],
                                 [(D, BF16)])
        sv["act"] = act
        y = _mm(f"out{l}", act, wout, "nn")
        sv["y"] = y

        def post_body(rows, vecs):
            return [rows[0] + _post_fn(rows[1], vecs[0], vecs[1])], []

        (xl,), _ = _rowwise(f"post{l}", post_body, [(xl, D, 0), (y, D, 0)], [post_g[l:l + 1], gate[l]], [(D, F32)])
        saved.append(sv)

    def loss_body(rows, vecs):
        diff = rows[0] - rows[1]
        return [diff * (1.0 / D)], [_colsum(diff * diff) * (0.5 / D)]

    (dx,), (loss_lanes,) = _rowwise("loss", loss_body, [(xl, D, 0), (tgt, D, 0)], [], [(D, F32)], [(1, D)])
    loss = lax.psum(jnp.sum(loss_lanes), ("x", "y", "c"))

    g_pre, g_post, g_mod = [None] * DEPTH, [None] * DEPTH, [None] * DEPTH
    g_dec_f, g_dec_b, g_qn, g_kvn = [None] * 2, [None] * 2, [None] * 2, [None] * 2
    g_b_in, g_dw_w, g_dw_b, g_ln_g, g_ln_b = [None] * 2, [None] * 2, [None] * 2, [None] * 2, [None] * 2
    recv_ev, recv_od = [None] * 2, [None] * 2
    for l in reversed(range(DEPTH)):
        i = l // 2
        sv = saved[l]

        def post_bwd_body(rows, vecs):
            yv, dxn = rows
            r, vjp = jax.vjp(_post_fn, yv, vecs[0], vecs[1])
            dy, dg, dgate = vjp(dxn)
            return [dy], [dg, dgate]

        (dy,), (dpost, dgate) = _rowwise(f"post_bwd{l}", post_bwd_body, [(sv["y"], D, 0), (dx, D, 0)],
                                         [post_g[l:l + 1], gate[l]], [(D, BF16)], [(1, D), (1, D)])
        g_post[l] = dpost
        wout = ev_w[i][3] if l % 2 == 0 else od_w[i][1]
        dact = _mm(f"out_dgrad{l}", dy, wout, "nt")
        d_wout = _mm(f"out_wgrad{l}", sv["act"], dy, "tn")
        z = sv["z"]
        if l % 2 == 0:
            win_t, uq_t, ukv_t, _ = ev_w[i]
            dec_f, dec_b = ev_dec_f[i:i + 1], ev_dec_b[i:i + 1]

            def ev_post_bwd_body(rows, vecs):
                of, ob, rg, a, mg, da = rows
                _, vjp = jax.vjp(_ev_post_fn, _heads(of + ob), rg, a, mg)
                do_heads, drg, d_a, dmg = vjp(da)
                deltas = [jnp.sum(dh_ * ah_, axis=1, keepdims=True) for dh_, ah_ in zip(_heads(d_a), _heads(a))]
                return [jnp.concatenate(do_heads, axis=1), drg, d_a, dmg] + deltas, []

            (do_ret, drg, do_mla, dmg, dl0, dl1, dl2, dl3), _ = _rowwise(
                f"ev_post_bwd{l}", ev_post_bwd_body,
                [(sv["o_f"], 512, 0), (sv["o_b"], 512, 0), (z, 512, 1), (sv["a_mla"], 512, 0), (z, 512, 2), (dact, D, 0)],
                [], [(512, F32), (512, BF16), (512, BF16), (512, BF16)] + [(1, F32)] * 4)
            delta = jnp.stack([dl0, dl1, dl2, dl3]).reshape(MLA_HEADS, 1, s)
            lse = sv["lse"].reshape(MLA_HEADS, 1, s)
            kt = sv["kcat"].reshape(s, MLA_HEADS, 256).transpose(1, 2, 0)
            dqt, dkcat, dv = _fa_bwd(f"fa_bwd{l}", sv["qcat"], sv["kcat"], kt, sv["v_b"], do_mla, lse, delta, fa_tq, fa_tk)
            dqcat = dqt.transpose(1, 3, 0, 2).reshape(s, 1024)

            def mla_prep_bwd_body(rows, vecs):
                dq, dk, dvv, cos, sin = rows
                qs = []
                dkrr = jnp.zeros((dq.shape[0], LANES), F32)
                for hh in range(4):
                    qs += [dq[:, 256 * hh:256 * hh + 128], _rope_t(dq[:, 256 * hh + 128:256 * hh + 256], cos, sin)]
                    dkrr = dkrr + dk[:, 256 * hh + 128:256 * hh + 256]
                return [jnp.concatenate(qs, axis=1) * MLA_SCALE, jnp.concatenate([dk, dvv], axis=1), dkrr], []

            (dq_pad, dkv_pad, dkrr), _ = _rowwise(
                f"mla_prep_bwd{l}", mla_prep_bwd_body,
                [(dqcat, 1024, 0), (dkcat, 1024, 0), (dv, 512, 0), (cos_t, LANES, 0), (sin_t, LANES, 0)], [],
                [(1024, BF16), (1536, BF16), (LANES, F32)])
            dqn = _mm(f"uq_dgrad{l}", dq_pad, uq_t, "nn")
            d_uq = _mm(f"uq_wgrad{l}", dq_pad, sv["qn"], "tn")
            dkvn = _mm(f"ukv_dgrad{l}", dkv_pad, ukv_t, "nn")
            d_ukv = _mm(f"ukv_wgrad{l}", dkv_pad, sv["kvn"], "tn")
            dq_f, dk_f, dv_f, ddec_f = _ret_bwd(f"ret_f_bwd{l}", sv["rq_r"], sv["rk_r"], z, dec_f, sv["st_f"], do_ret,
                                                False, ret_chunk)
            dq_b, dk_b, dv_b, ddec_b = _ret_bwd(f"ret_b_bwd{l}", sv["rq_r"], sv["rk_r"], z, dec_b, sv["st_b"], do_ret,
                                                True, ret_chunk)
            g_dec_f[i], g_dec_b[i] = ddec_f[:, :RET_HEADS], ddec_b[:, :RET_HEADS]

            def prep_bwd_body(rows, vecs):
                cq, ckv, cos, sin, dqf, dqb, dkf, dkb, dvf, dvb, d_qn, d_kvn, d_krr = rows
                _, vjp_q = jax.vjp(_rms, cq, vecs[0])
                dcq, dgq = vjp_q(d_qn)
                _, vjp_kv = jax.vjp(_rms, ckv, vecs[1])
                dckv, dgkv = vjp_kv(d_kvn)
                return [dvf + dvb, dcq, _rope_t(d_krr, cos, sin), _rope_t(dqf + dqb, cos, sin),
                        _rope_t(dkf + dkb, cos, sin) * RET_SCALE, dckv], [dgq, dgkv]

            (drv, dcq, dkr, drq, drk, dckv), (dgq, dgkv) = _rowwise(
                f"ev_prep_bwd{l}", prep_bwd_body,
                [(z, 384, 4), (z, 256, 12), (cos_t, LANES, 0), (sin_t, LANES, 0), (dq_f, 512, 0), (dq_b, 512, 0),
                 (dk_f, 512, 0), (dk_b, 512, 0), (dv_f, 512, 0), (dv_b, 512, 0), (dqn, 384, 0), (dkvn, 256, 0),
                 (dkrr, LANES, 0)],
                [ev_q_norm_g[i:i + 1], ev_kv_norm_g[i:i + 1]],
                [(512, BF16), (384, BF16), (LANES, BF16), (512, BF16), (512, BF16), (256, BF16)], [(1, 384), (1, 256)])
            g_qn[i], g_kvn[i] = dgq, dgkv
            dz = jnp.concatenate([drv, drg, dmg, dcq, dkr, drq, drk, dckv], axis=1)
            d_win = _mm(f"in_wgrad{l}", dz, sv["h"], "tn")
            pieces = [_ev_win_unlayout(d_win).reshape(N_DEV, 344 * D // D, D),
                      _uq_unlayout(d_uq).reshape(N_DEV, 96 * 384 // D, D),
                      _ukv_unlayout(d_ukv).reshape(N_DEV, 128 * 256 // D, D),
                      d_wout.reshape(N_DEV, 128, D)]
            rows_p = _part_rows(EV_PARTS)
            pieces = [p if p.shape[1] == r else jnp.concatenate([p, jnp.zeros((N_DEV, r - p.shape[1], D), F32)], axis=1)
                      for p, r in zip(pieces, rows_p)]
            recv_ev[i] = _exchange(f"scatter_g_ev{i}", jnp.concatenate(pieces, axis=1), True)
        else:
            win_t, _ = od_w[i]

            def od_post_bwd_body(rows, vecs):
                u2, gg, da = rows
                _, vjp = jax.vjp(_od_post_fn, u2, gg, vecs[0], vecs[1])
                du2, dgg, dlg, dlb = vjp(da)
                return [du2, dgg], [dlg, dlb, _colsum(dgg)]

            (du2, dg_gate), (dlg, dlb, dbg) = _rowwise(
                f"od_post_bwd{l}", od_post_bwd_body, [(sv["u2"], D, 0), (z, D, 2), (dact, D, 0)],
                [ln_g_all[i], ln_b_all[i]], [(D, F32), (D, BF16)], [(1, D), (1, D), (1, D)])
            g_ln_g[i], g_ln_b[i] = dlg, dlb
            du, d_dw, d_dwb = _conv_bwd(f"conv_bwd{l}", sv["u"], du2, dw_w_all[i])
            g_dw_w[i], g_dw_b[i] = d_dw[:CONV_K], d_dwb

            def glu_bwd_body(rows, vecs):
                a, b, d_u = rows
                _, vjp = jax.vjp(_glu_fn, a, b)
                d_a, d_b = vjp(d_u)
                return [d_a, d_b], [_colsum(d_a), _colsum(d_b)]

            (d_a, d_b), (dba, dbb) = _rowwise(f"glu_bwd{l}", glu_bwd_body, [(z, D, 0), (z, D, 1), (du, D, 0)], [],
                                              [(D, BF16), (D, BF16)], [(1, D), (1, D)])
            g_b_in[i] = jnp.concatenate([dba, dbb, dbg], axis=1)
            dz = jnp.concatenate([d_a, d_b, dg_gate], axis=1)
            d_win = _mm(f"in_wgrad{l}", dz, sv["h"], "tn")
            recv_od[i] = _exchange(f"scatter_g_od{i}",
                                   jnp.concatenate([d_win.reshape(N_DEV, 384, D), d_wout.reshape(N_DEV, 128, D)], axis=1),
                                   True)
        dh = _mm(f"in_dgrad{l}", dz, win_t, "nn")

        def pre_bwd_body(rows, vecs):
            xv, d_h, dxn = rows
            _, vjp = jax.vjp(_pre_fn, xv, *vecs)
            d_x, dg, dsc, dsh = vjp(d_h)
            return [d_x + dxn], [dg, dsc, dsh]

        (dx,), (dpre, dscale, dshift) = _rowwise(f"pre_bwd{l}", pre_bwd_body, [(sv["x"], D, 0), (dh, D, 0), (dx, D, 0)],
                                                 [pre_g[l:l + 1], scale[l], shift[l]], [(D, F32)], [(1, D)] * 3)
        g_pre[l] = dpre
        g_mod[l] = jnp.concatenate([dshift, dscale, dgate], axis=1)

    grad_x = dx.reshape(1, s, D)

    end_parts = [jnp.concatenate(g_mod, axis=0), jnp.concatenate(g_pre, axis=0), jnp.concatenate(g_post, axis=0),
                 jnp.concatenate(g_dec_f, axis=0), jnp.concatenate(g_dec_b, axis=0), jnp.concatenate(g_qn, axis=0),
                 jnp.concatenate(g_kvn, axis=0), jnp.concatenate(g_b_in, axis=0), jnp.stack(g_dw_w),
                 jnp.concatenate(g_dw_b, axis=0), jnp.concatenate(g_ln_g, axis=0), jnp.concatenate(g_ln_b, axis=0)]
    end_shapes = [p.shape for p in end_parts]
    end_sizes = [int(np.prod(sh)) for sh in end_shapes]
    end_len = -(-sum(end_sizes) // 1024) * 1024
    end_vec = jnp.concatenate([p.reshape(-1) for p in end_parts] + [jnp.zeros((end_len - sum(end_sizes),), F32)])
    end_all = _exchange("gather_end", end_vec.reshape(-1, LANES), False).reshape(N_DEV, end_len)
    eo = np.cumsum([0] + end_sizes)
    ends = [end_all[:, eo[j]:eo[j + 1]].reshape((N_DEV,) + tuple(end_shapes[j])) for j in range(len(end_parts))]
    (dmod_all, pre_all, post_all, decf_all, decb_all, qn_all, kvn_all, bin_all, dww_all, dwb_all, lng_all,
     lnb_all) = ends

    def pack_rep(*ts):
        lead = ts[0].ndim - 2
        return jnp.concatenate([t.reshape(t.shape[:lead] + (-1,)) for t in ts], axis=-1)

    rep_sizes = [DEPTH * 3 * D, DEPTH * D, DEPTH * D, 8, 8, 2 * 384, 2 * 256]
    rep_len = -(-sum(rep_sizes) // 1024) * 1024
    rep_pad = rep_len - sum(rep_sizes)

    def rep_rows(flat):
        padz = jnp.zeros(flat.shape[:-1] + (rep_pad,), F32)
        return jnp.concatenate([flat, padz], axis=-1).reshape(flat.shape[:-1] + (rep_len // LANES, LANES))

    rep_w = rep_rows(pack_rep(ada_b, pre_g, post_g, ev_dec_f, ev_dec_b, ev_q_norm_g, ev_kv_norm_g))
    rep_m = rep_rows(pack_rep(m_ada_b, m_pre_g, m_post_g, m_ev_dec_f, m_ev_dec_b, m_ev_q_norm_g, m_ev_kv_norm_g))
    rep_v = rep_rows(pack_rep(v_ada_b, v_pre_g, v_post_g, v_ev_dec_f, v_ev_dec_b, v_ev_q_norm_g, v_ev_kv_norm_g))
    rep_g = rep_rows(pack_rep(dmod_all, pre_all, post_all, decf_all, decb_all, qn_all, kvn_all))
    rep_out = _adamw("adamw_rep", rep_w, rep_m, rep_v, rep_g)
    ro = np.cumsum([0] + rep_sizes)
    rep_shapes = [(DEPTH, 3 * D), (DEPTH, D), (DEPTH, D), (2, 4), (2, 4), (2, 384), (2, 256)]

    def unpack_rep(t):
        flat = t.reshape(-1)
        return [flat[ro[j]:ro[j + 1]].reshape(rep_shapes[j]) for j in range(len(rep_shapes))]

    rep_res = [unpack_rep(t) for t in rep_out]

    def my_cols(t, width):
        return lax.dynamic_slice_in_dim(t, me * width, width, axis=t.ndim - 1)

    def vec_adamw(name, w, m, v, g_all, width):
        g = my_cols(g_all, width)
        r = _adamw(name, w.reshape(-1, width), m.reshape(-1, width), v.reshape(-1, width),
                   g.reshape(N_DEV, -1, width))
        return [t.reshape(w.shape) for t in r]

    res_b_in = vec_adamw("adamw_b_in", od_b_in, m_od_b_in, v_od_b_in, bin_all, 384)
    res_dw_w = vec_adamw("adamw_dw_w", od_dw_w, m_od_dw_w, v_od_dw_w, dww_all, LANES)
    res_dw_b = vec_adamw("adamw_dw_b", od_dw_b, m_od_dw_b, v_od_dw_b, dwb_all, LANES)
    res_ln_g = vec_adamw("adamw_ln_g", od_ln_g, m_od_ln_g, v_od_ln_g, lng_all, LANES)
    res_ln_b = vec_adamw("adamw_ln_b", od_ln_b, m_od_ln_b, v_od_ln_b, lnb_all, LANES)

    dmod_mine = my_cols(dmod_all, 384).transpose(1, 0, 2)
    dmod_pad = jnp.concatenate([dmod_mine, jnp.zeros((DEPTH, LANES - N_DEV, 384), F32)], axis=1)
    c_all_t = jnp.concatenate([c_all.T, jnp.zeros((D, LANES - N_DEV), F32)], axis=1)
    g_ada_w = _ada_w_grad("ada_w_grad", c_all_t, dmod_pad)
    res_ada_w = [t.reshape(ada_w.shape) for t in
                 _adamw("adamw_ada_w", ada_w.reshape(-1, 384), m_ada_w.reshape(-1, 384), v_ada_w.reshape(-1, 384),
                        g_ada_w.reshape(-1, 384))]

    ev_sh = [_split_parts(_sum_parts(f"sum_g_ev{i}", recv_ev[i]), EV_PARTS) for i in range(2)]
    od_sh = [_split_parts(_sum_parts(f"sum_g_od{i}", recv_od[i]), OD_PARTS) for i in range(2)]

    def mat_adamw(name, w, m, v, g):
        r = _adamw(name, w.reshape(-1, w.shape[-1]), m.reshape(-1, w.shape[-1]), v.reshape(-1, w.shape[-1]),
                   g.reshape(-1, w.shape[-1]))
        return [t.reshape(w.shape) for t in r]

    res_ev_w_in = mat_adamw("adamw_ev_w_in", ev_w_in, m_ev_w_in, v_ev_w_in, jnp.stack([ev_sh[i][0].T for i in range(2)]))
    res_ev_w_uq = mat_adamw("adamw_ev_w_uq", ev_w_uq, m_ev_w_uq, v_ev_w_uq, jnp.stack([ev_sh[i][1].T for i in range(2)]))
    res_ev_w_ukv = mat_adamw("adamw_ev_w_ukv", ev_w_ukv, m_ev_w_ukv, v_ev_w_ukv,
                             jnp.stack([ev_sh[i][2].T for i in range(2)]))
    res_ev_w_out = mat_adamw("adamw_ev_w_out", ev_w_out, m_ev_w_out, v_ev_w_out, jnp.stack([ev_sh[i][3] for i in range(2)]))
    res_od_w_in = mat_adamw("adamw_od_w_in", od_w_in, m_od_w_in, v_od_w_in, jnp.stack([od_sh[i][0].T for i in range(2)]))
    res_od_w_out = mat_adamw("adamw_od_w_out", od_w_out, m_od_w_out, v_od_w_out, jnp.stack([od_sh[i][1] for i in range(2)]))

    per_weight = [res_ada_w] + [[rep_res[t][j] for t in range(4)] for j in range(3)]
    per_weight += [res_ev_w_in, [rep_res[t][3] for t in range(4)], [rep_res[t][4] for t in range(4)],
                   [rep_res[t][5] for t in range(4)], res_ev_w_uq, [rep_res[t][6] for t in range(4)], res_ev_w_ukv,
                   res_ev_w_out, res_od_w_in, res_b_in, res_dw_w, res_dw_b, res_ln_g, res_ln_b, res_od_w_out]
    outs = [loss, grad_x]
    for t in range(4):
        outs += [pw[t] for pw in per_weight]
    return tuple(outs)
```

```python
import functools

import numpy as np
import jax
import jax.numpy as jnp
from jax import lax
from jax.experimental import pallas as pl
from jax.experimental.pallas import tpu as pltpu

F32 = jnp.float32
BF16 = jnp.bfloat16
MESH = pl.DeviceIdType.MESH

N_DEV = 8
D = 1024
DEPTH = 4
EPS = 1e-6
RET_HEADS = 4
MLA_HEADS = 4
RET_SCALE = 64 ** -0.5
MLA_SCALE = 192 ** -0.5
CONV_K = 31
CONV_HALO = 16
ROPE_BASE = 10000.0

ADAM_LR = 0.001
ADAM_B1 = 0.9
ADAM_B2 = 0.999
ADAM_EPS = 1e-08
ADAM_WD = 0.01
ADAM_STEP = 10

LANES = 128
VMEM_LIMIT = 48 * 1024 * 1024

ZL_EV = dict(rv=(0, 512), rg=(512, 512), mg=(1024, 512), cq=(1536, 384), kr=(1920, 128),
             rq=(2048, 512), rk=(2560, 512), ckv=(3072, 256))
ZW_EV = 3328
ZW_OD = 3072


def _cparams(sem, vmem=VMEM_LIMIT):
    return pltpu.CompilerParams(dimension_semantics=sem, vmem_limit_bytes=vmem)


def _pick(n, prefs):
    for p in prefs:
        if n % p == 0:
            return p
    return n


def _sigmoid(x):
    return 0.5 * (jnp.tanh(0.5 * x) + 1.0)


def _silu(x):
    return x * _sigmoid(x)


def _rms(x, g):
    return x * lax.rsqrt(jnp.mean(x * x, axis=-1, keepdims=True) + EPS) * g


def _log_sigmoid(x):
    return jnp.minimum(x, 0.0) - jnp.log(1.0 + jnp.exp(jnp.minimum(x, -x)))


def _tile_lanes(t, width):
    reps = width // t.shape[1]
    return t if reps == 1 else jnp.concatenate([t] * reps, axis=1)


def _rot_half(x):
    w = x.shape[1]
    lane = lax.broadcasted_iota(jnp.int32, x.shape, 1)
    first = jnp.bitwise_and(lane, 63) < 32
    return jnp.where(first, pltpu.roll(x, w - 32, 1), pltpu.roll(x, 32, 1))


def _rope(x, cos, sin):
    w = x.shape[1]
    return x * _tile_lanes(cos, w) + _rot_half(x) * _tile_lanes(sin, w)


def _rope_t(dy, cos, sin):
    w = dy.shape[1]
    return dy * _tile_lanes(cos, w) + _rot_half(dy * _tile_lanes(sin, w))


_DN = {"nn": (((1,), (0,)), ((), ())), "nt": (((1,), (1,)), ((), ())), "tn": (((0,), (0,)), ((), ()))}


def _dot(a, b, mode):
    return lax.dot_general(a.astype(BF16), b.astype(BF16), _DN[mode], preferred_element_type=F32)


@functools.partial(jax.custom_vjp, nondiff_argnums=(2,))
def _bdot(a, b, mode):
    return _dot(a, b, mode)


def _bdot_fwd(a, b, mode):
    return _dot(a, b, mode), (a, b)


def _bdot_bwd(mode, res, g):
    a, b = res
    if mode == "nn":
        return _dot(g, b, "nt"), _dot(a, g, "tn")
    if mode == "nt":
        return _dot(g, b, "nn"), _dot(g, a, "tn")
    return _dot(b, g, "nt"), _dot(a, g, "nn")


_bdot.defvjp(_bdot_fwd, _bdot_bwd)


def _rowwise(name, body, row_ins, vec_ins, row_outs, red_outs=(), tile=512):
    s = row_ins[0][0].shape[0]
    tile = min(tile, s)
    nr, nv, no = len(row_ins), len(vec_ins), len(row_outs)

    def kern(*refs):
        rows = [r[...] for r in refs[:nr]]
        vecs = [r[...] for r in refs[nr:nr + nv]]
        outs, reds = body(rows, vecs)
        for r, o in zip(refs[nr + nv:nr + nv + no], outs):
            r[...] = o.astype(r.dtype)
        red_refs = refs[nr + nv + no:]
        if red_refs:
            @pl.when(pl.program_id(0) == 0)
            def _():
                for r in red_refs:
                    r[...] = jnp.zeros(r.shape, r.dtype)
            for r, v in zip(red_refs, reds):
                r[...] += v

    in_specs = [pl.BlockSpec((tile, w), (lambda i, cb=cb: (i, cb))) for (_, w, cb) in row_ins]
    in_specs += [pl.BlockSpec(v.shape, (lambda i, nd=v.ndim: (0,) * nd)) for v in vec_ins]
    out_specs = [pl.BlockSpec((tile, w), lambda i: (i, 0)) for (w, _) in row_outs]
    out_specs += [pl.BlockSpec(sh, lambda i: (0, 0)) for sh in red_outs]
    out_shape = [jax.ShapeDtypeStruct((s, w), dt) for (w, dt) in row_outs]
    out_shape += [jax.ShapeDtypeStruct(sh, F32) for sh in red_outs]
    res = pl.pallas_call(
        kern, name=name, grid=(s // tile,), in_specs=in_specs, out_specs=out_specs, out_shape=out_shape,
        compiler_params=_cparams(("arbitrary",)),
    )(*[a for (a, _, _) in row_ins], *vec_ins)
    return res[:no], res[no:]


def _mm(name, a, b, mode, out_dtype=F32, bias=None):
    if mode == "tn":
        k, m = a.shape
        n = b.shape[1]
        tm = m if m <= 1664 else m // 2
        tk = min(k, 1024)
        nk = k // tk

        def kern(a_ref, b_ref, o_ref, acc_ref):
            kk = pl.program_id(1)
            part = _dot(a_ref[...], b_ref[...], "tn")

            @pl.when(kk == 0)
            def _():
                acc_ref[...] = part

            @pl.when(kk > 0)
            def _():
                acc_ref[...] += part

            @pl.when(kk == nk - 1)
            def _():
                o_ref[...] = acc_ref[...].astype(o_ref.dtype)

        return pl.pallas_call(
            kern, name=name, grid=(m // tm, nk),
            in_specs=[pl.BlockSpec((tk, tm), lambda i, kk: (kk, i)),
                      pl.BlockSpec((tk, n), lambda i, kk: (kk, 0))],
            out_specs=pl.BlockSpec((tm, n), lambda i, kk: (i, 0)),
            out_shape=jax.ShapeDtypeStruct((m, n), out_dtype),
            scratch_shapes=[pltpu.VMEM((tm, n), F32)],
            compiler_params=_cparams(("parallel", "arbitrary")),
        )(a, b)

    m, k = a.shape
    n = b.shape[1] if mode == "nn" else b.shape[0]
    tm = min(m, 1024)
    tn = n if n <= 512 else _pick(n, (512, 256, 128))
    has_bias = bias is not None

    def kern(*refs):
        a_ref, b_ref = refs[0], refs[1]
        o_ref = refs[-1]
        r = _dot(a_ref[...], b_ref[...], mode)
        if has_bias:
            r = r + refs[2][...]
        o_ref[...] = r.astype(o_ref.dtype)

    b_spec = (pl.BlockSpec((k, tn), lambda i, j: (0, j)) if mode == "nn"
              else pl.BlockSpec((tn, k), lambda i, j: (j, 0)))
    in_specs = [pl.BlockSpec((tm, k), lambda i, j: (i, 0)), b_spec]
    args = [a, b]
    if has_bias:
        in_specs.append(pl.BlockSpec((1, tn), lambda i, j: (0, j)))
        args.append(bias)
    return pl.pallas_call(
        kern, name=name, grid=(m // tm, n // tn), in_specs=in_specs,
        out_specs=pl.BlockSpec((tm, tn), lambda i, j: (i, j)),
        out_shape=jax.ShapeDtypeStruct((m, n), out_dtype),
        compiler_params=_cparams(("parallel", "parallel")),
    )(*args)


def _peers():
    mx, my, mc = lax.axis_index("x"), lax.axis_index("y"), lax.axis_index("c")
    me = 4 * mx + 2 * my + mc
    out = []
    for k in range(1, N_DEV):
        px = 1 - mx if (k >> 2) & 1 else mx
        py = 1 - my if (k >> 1) & 1 else my
        pc = 1 - mc if k & 1 else mc
        out.append((k, (px, py, pc), 4 * px + 2 * py + pc))
    return me, out


def _exchange(name, xs, scatter):
    n = len(xs)

    def body(*refs):
        x_refs, out_refs = refs[:n], refs[n:2 * n]
        send_sems, recv_sems, local_sems = refs[2 * n:]
        me, peers = _peers()
        locals_, sends = [], []
        for a in range(n):
            mine = x_refs[a].at[me] if scatter else x_refs[a]
            cp = pltpu.make_async_copy(mine, out_refs[a].at[me], local_sems.at[a])
            cp.start()
            locals_.append(cp)
        for k, dev, p in peers:
            for a in range(n):
                cp = pltpu.make_async_remote_copy(
                    src_ref=x_refs[a].at[p] if scatter else x_refs[a], dst_ref=out_refs[a].at[me],
                    send_sem=send_sems.at[a, k - 1], recv_sem=recv_sems.at[a, k - 1],
                    device_id=dev, device_id_type=MESH)
                cp.start()
                sends.append(cp)
        for cp in sends:
            cp.wait_send()
        for k, dev, p in peers:
            for a in range(n):
                mine = x_refs[a].at[me] if scatter else x_refs[a]
                pltpu.make_async_remote_copy(
                    src_ref=mine, dst_ref=out_refs[a].at[p],
                    send_sem=send_sems.at[a, k - 1], recv_sem=recv_sems.at[a, k - 1],
                    device_id=dev, device_id_type=MESH).wait_recv()
        for cp in locals_:
            cp.wait()

    return pl.pallas_call(
        body, name=name,
        in_specs=[pl.BlockSpec(memory_space=pl.ANY)] * n,
        out_specs=[pl.BlockSpec(memory_space=pl.ANY)] * n,
        out_shape=[jax.ShapeDtypeStruct((N_DEV,) + tuple(x.shape[1:] if scatter else x.shape), x.dtype) for x in xs],
        scratch_shapes=[pltpu.SemaphoreType.DMA((n, N_DEV - 1)), pltpu.SemaphoreType.DMA((n, N_DEV - 1)),
                        pltpu.SemaphoreType.DMA((n,))],
        compiler_params=pltpu.CompilerParams(has_side_effects=True),
    )(*xs)


def _sum_parts(name, x):
    p, r, c = x.shape
    tr = r if r * c * p * x.dtype.itemsize <= (8 << 20) else _pick(r, (256, 128, 64, 16))

    def kern(x_ref, o_ref):
        acc = x_ref[0].astype(F32)
        for i in range(1, p):
            acc = acc + x_ref[i].astype(F32)
        o_ref[...] = acc

    return pl.pallas_call(
        kern, name=name, grid=(r // tr,),
        in_specs=[pl.BlockSpec((p, tr, c), lambda i: (0, i, 0))],
        out_specs=pl.BlockSpec((tr, c), lambda i: (i, 0)),
        out_shape=jax.ShapeDtypeStruct((r, c), F32),
        compiler_params=_cparams(("parallel",)),
    )(x)


def _adamw(name, w, m, v, g):
    r, c = w.shape
    parts = g.shape[0] if g.ndim == 3 else 0
    tr = 512 if (r > 512 and r % 512 == 0) else r

    def kern(w_ref, m_ref, v_ref, g_ref, go_ref, d_ref, mo_ref, vo_ref):
        if parts:
            gg = g_ref[0]
            for i in range(1, parts):
                gg = gg + g_ref[i]
        else:
            gg = g_ref[...]
        mm = ADAM_B1 * m_ref[...] + (1.0 - ADAM_B1) * gg
        vv = ADAM_B2 * v_ref[...] + (1.0 - ADAM_B2) * (gg * gg)
        m_hat = mm / (1.0 - ADAM_B1 ** ADAM_STEP)
        v_hat = vv / (1.0 - ADAM_B2 ** ADAM_STEP)
        go_ref[...] = gg
        d_ref[...] = -ADAM_LR * (m_hat / (jnp.sqrt(v_hat) + ADAM_EPS) + ADAM_WD * w_ref[...])
        mo_ref[...] = mm
        vo_ref[...] = vv

    spec = pl.BlockSpec((tr, c), lambda i: (i, 0))
    gspec = pl.BlockSpec((parts, tr, c), lambda i: (0, i, 0)) if parts else spec
    sh = jax.ShapeDtypeStruct((r, c), F32)
    return pl.pallas_call(
        kern, name=name, grid=(r // tr,), in_specs=[spec, spec, spec, gspec],
        out_specs=[spec] * 4, out_shape=[sh] * 4,
        compiler_params=_cparams(("parallel",)),
    )(w, m, v, g)


def _ret_chunk(q, k, v, st, dec_cc, dec_cd, dec_dd, reverse):
    c = q.shape[0]
    row = lax.broadcasted_iota(jnp.int32, (c, c), 0).astype(F32)
    col = lax.broadcasted_iota(jnp.int32, (c, c), 1).astype(F32)
    pos = lax.broadcasted_iota(jnp.int32, (c, LANES), 0).astype(F32)
    if reverse:
        diff, mask = col - row, col > row
        q_exp, k_exp = c - pos, pos
    else:
        diff, mask = row - col, row >= col
        q_exp, k_exp = pos + 1.0, c - 1.0 - pos
    decay = jnp.where(mask, jnp.exp(_log_sigmoid(dec_cc) * jnp.maximum(diff, 0.0)), 0.0)
    lam_cd = _log_sigmoid(dec_cd)
    scores = _bdot(q, k, "nt") * decay
    o = _bdot(scores, v, "nn") + _bdot(q * jnp.exp(lam_cd * q_exp), st, "nn")
    st_new = st * jnp.exp(_log_sigmoid(dec_dd) * float(c)) + _bdot(k * jnp.exp(lam_cd * k_exp), v, "tn")
    return o, st_new


def _ret_dec(dec_ref, h, c):
    d = dec_ref[:, h:h + 1]
    return (jnp.broadcast_to(d, (c, c)), jnp.broadcast_to(d, (c, LANES)), jnp.broadcast_to(d, (LANES, LANES)))


def _ret_fwd(name, q, k, z, dec, reverse, chunk):
    s = q.shape[0]
    chunk = min(chunk, s)
    n = s // chunk
    cmap = (lambda i: (n - 1 - i, 0)) if reverse else (lambda i: (i, 0))
    smap = (lambda i: (n - 1 - i, 0, 0, 0)) if reverse else (lambda i: (i, 0, 0, 0))

    def kern(q_ref, k_ref, v_ref, dec_ref, o_ref, st_out_ref, st_ref):
        @pl.when(pl.program_id(0) == 0)
        def _():
            st_ref[...] = jnp.zeros(st_ref.shape, F32)

        for h in range(RET_HEADS):
            sl = slice(LANES * h, LANES * (h + 1))
            st = st_ref[h]
            st_out_ref[h] = st
            o, st_new = _ret_chunk(q_ref[:, sl].astype(F32), k_ref[:, sl].astype(F32), v_ref[:, sl],
                                   st, *_ret_dec(dec_ref, h, chunk), reverse)
            o_ref[:, sl] = o
            st_ref[h] = st_new

    return pl.pallas_call(
        kern, name=name, grid=(n,),
        in_specs=[pl.BlockSpec((chunk, 512), cmap), pl.BlockSpec((chunk, 512), cmap),
                  pl.BlockSpec((chunk, 512), cmap), pl.BlockSpec((1, RET_HEADS), lambda i: (0, 0))],
        out_specs=[pl.BlockSpec((chunk, 512), cmap), pl.BlockSpec((None, RET_HEADS, LANES, LANES), smap)],
        out_shape=[jax.ShapeDtypeStruct((s, 512), F32), jax.ShapeDtypeStruct((n, RET_HEADS, LANES, LANES), F32)],
        scratch_shapes=[pltpu.VMEM((RET_HEADS, LANES, LANES), F32)],
        compiler_params=_cparams(("arbitrary",)),
    )(q, k, z, dec)


def _ret_bwd(name, q, k, z, dec, states, do, reverse, chunk):
    s = q.shape[0]
    chunk = min(chunk, s)
    n = s // chunk
    cmap = (lambda i: (i, 0)) if reverse else (lambda i: (n - 1 - i, 0))
    smap = (lambda i: (i, 0, 0, 0)) if reverse else (lambda i: (n - 1 - i, 0, 0, 0))

    def kern(q_ref, k_ref, v_ref, dec_ref, st_in_ref, do_ref, dq_ref, dk_ref, dv_ref, ddec_ref, dst_ref):
        @pl.when(pl.program_id(0) == 0)
        def _():
            dst_ref[...] = jnp.zeros(dst_ref.shape, F32)
            ddec_ref[...] = jnp.zeros(ddec_ref.shape, F32)

        lane = lax.broadcasted_iota(jnp.int32, (1, LANES), 1)
        ddec = jnp.zeros((1, LANES), F32)
        for h in range(RET_HEADS):
            sl = slice(LANES * h, LANES * (h + 1))
            fn = functools.partial(_ret_chunk, reverse=reverse)
            _, vjp = jax.vjp(fn, q_ref[:, sl].astype(F32), k_ref[:, sl].astype(F32), v_ref[:, sl],
                             st_in_ref[h], *_ret_dec(dec_ref, h, chunk))
            dq, dk, dv, dst, d_cc, d_cd, d_dd = vjp((do_ref[:, sl], dst_ref[h]))
            dq_ref[:, sl] = dq
            dk_ref[:, sl] = dk
            dv_ref[:, sl] = dv
            dst_ref[h] = dst
            tot = (jnp.sum(jnp.sum(d_cc, axis=1, keepdims=True), axis=0, keepdims=True)
                   + jnp.sum(jnp.sum(d_cd, axis=1, keepdims=True), axis=0, keepdims=True)
                   + jnp.sum(jnp.sum(d_dd, axis=1, keepdims=True), axis=0, keepdims=True))
            ddec = ddec + jnp.where(lane == h, tot, 0.0)
        ddec_ref[...] += ddec

    cspec = pl.BlockSpec((chunk, 512), cmap)
    return pl.pallas_call(
        kern, name=name, grid=(n,),
        in_specs=[cspec, cspec, cspec, pl.BlockSpec((1, RET_HEADS), lambda i: (0, 0)),
                  pl.BlockSpec((None, RET_HEADS, LANES, LANES), smap), cspec],
        out_specs=[cspec, cspec, cspec, pl.BlockSpec((1, LANES), lambda i: (0, 0))],
        out_shape=[jax.ShapeDtypeStruct((s, 512), F32)] * 3 + [jax.ShapeDtypeStruct((1, LANES), F32)],
        scratch_shapes=[pltpu.VMEM((RET_HEADS, LANES, LANES), F32)],
        compiler_params=_cparams(("arbitrary",)),
    )(q, k, z, dec, states, do)


def _fa_fwd(name, q, k, vx, tq, tk, nsub):
    s = q.shape[0]
    tq, tk = min(tq, s), min(tk, s)
    nk = s // tk
    sq = tq // nsub

    def kern(q_ref, k_ref, v_ref, o_ref, lse_ref, m_ref, acc_ref):
        j = pl.program_id(2)

        @pl.when(j == 0)
        def _():
            m_ref[...] = jnp.full(m_ref.shape, -jnp.inf, F32)
            acc_ref[...] = jnp.zeros(acc_ref.shape, F32)

        kb, vb = k_ref[...], v_ref[...]
        for c in range(nsub):
            rows = pl.ds(c * sq, sq)
            sc = _dot(q_ref[rows, :], kb, "nt")
            m_prev = m_ref[rows, :]
            m_new = jnp.maximum(m_prev, jnp.max(sc, axis=1, keepdims=True))
            alpha = jnp.exp(m_prev - m_new)
            p = jnp.exp(sc - m_new)
            acc_ref[rows, :] = alpha * acc_ref[rows, :] + _dot(p, vb, "nn")
            m_ref[rows, :] = m_new

        @pl.when(j == nk - 1)
        def _():
            den = acc_ref[:, LANES:]
            o_ref[...] = acc_ref[:, :LANES] / den
            lse_ref[...] = m_ref[...] + jnp.log(den[:, :1])

    return pl.pallas_call(
        kern, name=name, grid=(MLA_HEADS, s // tq, nk),
        in_specs=[pl.BlockSpec((tq, 256), lambda h, i, j: (i, h)),
                  pl.BlockSpec((tk, 256), lambda h, i, j: (j, h)),
                  pl.BlockSpec((tk, 256), lambda h, i, j: (j, h))],
        out_specs=[pl.BlockSpec((tq, LANES), lambda h, i, j: (i, h)),
                   pl.BlockSpec((None, tq, 1), lambda h, i, j: (h, i, 0))],
        out_shape=[jax.ShapeDtypeStruct((s, 512), F32), jax.ShapeDtypeStruct((MLA_HEADS, s, 1), F32)],
        scratch_shapes=[pltpu.VMEM((tq, 1), F32), pltpu.VMEM((tq, 256), F32)],
        compiler_params=_cparams(("parallel", "parallel", "arbitrary")),
    )(q, k, vx)


def _fa_bwd(name, q, k, kt, vx, do, lse, delta, tq, tk, nsub):
    s = q.shape[0]
    tq, tk = min(tq, s), min(tk, s)
    nq = s // tq
    sk = tk // nsub

    def kern(q_ref, k_ref, kt_ref, v_ref, do_ref, lse_ref, dl_ref, dqt_ref, dk_ref, dv_ref):
        j, i = pl.program_id(1), pl.program_id(2)

        @pl.when(i == 0)
        def _():
            dv_ref[...] = jnp.zeros(dv_ref.shape, F32)
            dk_ref[...] = jnp.zeros(dk_ref.shape, F32)

        @pl.when(j == 0)
        def _():
            dqt_ref[i] = jnp.zeros((256, tq), F32)

        qb, dob = q_ref[...], do_ref[...]
        lse_row, dl_row = lse_ref[...], dl_ref[...]
        dqt = dqt_ref[i]
        for c in range(nsub):
            rows = pl.ds(c * sk, sk)
            st = _dot(k_ref[rows, :], qb, "nt")
            pt = jnp.exp(st - lse_row)
            dpt = _dot(v_ref[rows, :], dob, "nt")
            dst = (pt * (dpt - dl_row)).astype(BF16)
            dv_ref[rows, :] += _dot(pt, dob, "nn")
            dk_ref[rows, :] += _dot(dst, qb, "nn")
            dqt = dqt + _dot(kt_ref[:, rows], dst, "nn")
        dqt_ref[i] = dqt

    return pl.pallas_call(
        kern, name=name, grid=(MLA_HEADS, s // tk, nq),
        in_specs=[pl.BlockSpec((tq, 256), lambda h, j, i: (i, h)),
                  pl.BlockSpec((tk, 256), lambda h, j, i: (j, h)),
                  pl.BlockSpec((None, 256, tk), lambda h, j, i: (h, 0, j)),
                  pl.BlockSpec((tk, LANES), lambda h, j, i: (j, 2 * h)),
                  pl.BlockSpec((tq, LANES), lambda h, j, i: (i, h)),
                  pl.BlockSpec((None, 1, tq), lambda h, j, i: (h, 0, i)),
                  pl.BlockSpec((None, 1, tq), lambda h, j, i: (h, 0, i))],
        out_specs=[pl.BlockSpec((None, nq, 256, tq), lambda h, j, i: (h, 0, 0, 0)),
                   pl.BlockSpec((tk, 256), lambda h, j, i: (j, h)),
                   pl.BlockSpec((tk, LANES), lambda h, j, i: (j, h))],
        out_shape=[jax.ShapeDtypeStruct((MLA_HEADS, nq, 256, tq), F32),
                   jax.ShapeDtypeStruct((s, 1024), F32), jax.ShapeDtypeStruct((s, 512), F32)],
        compiler_params=_cparams(("parallel", "arbitrary", "arbitrary")),
    )(q, k, kt, vx, do, lse, delta)


def _fill_padded(dst_ref, src_ref, s):
    zeros = jnp.zeros((CONV_HALO, LANES), F32)
    dst_ref[pl.ds(0, CONV_HALO), :] = zeros
    dst_ref[pl.ds(CONV_HALO + s, CONV_HALO), :] = zeros
    dst_ref[pl.ds(CONV_HALO, s), :] = src_ref[...]


def _shifted_windows(win):
    n = win.shape[0]
    return [win] + [pltpu.roll(win, n - b, 0) for b in range(1, 8)]


def _conv_fwd(name, u, w, bias, rc=256):
    s = u.shape[0]
    rc = min(rc, s)

    def kern(u_ref, w_ref, b_ref, o_ref, pad_ref):
        _fill_padded(pad_ref, u_ref, s)
        wv = w_ref[...]
        bv = b_ref[...]

        def chunk(r, carry):
            base = pl.multiple_of(r * rc, rc)
            wins = _shifted_windows(pad_ref[pl.ds(base, rc + 2 * CONV_HALO), :])
            acc = jnp.broadcast_to(bv, (rc, LANES))
            for kk in range(CONV_K):
                a, b = divmod(kk + 1, 8)
                acc = acc + wv[kk:kk + 1, :] * wins[b][8 * a:8 * a + rc]
            o_ref[pl.ds(base, rc), :] = acc
            return carry

        lax.fori_loop(0, s // rc, chunk, 0)

    return pl.pallas_call(
        kern, name=name, grid=(D // LANES,),
        in_specs=[pl.BlockSpec((s, LANES), lambda c: (0, c)), pl.BlockSpec((32, LANES), lambda c: (0, c)),
                  pl.BlockSpec((1, LANES), lambda c: (0, c))],
        out_specs=pl.BlockSpec((s, LANES), lambda c: (0, c)),
        out_shape=jax.ShapeDtypeStruct((s, D), F32),
        scratch_shapes=[pltpu.VMEM((s + 2 * CONV_HALO, LANES), F32)],
        compiler_params=_cparams(("parallel",)),
    )(u, w, bias)


def _conv_bwd(name, u, g, w, rc=256):
    s = u.shape[0]
    rc = min(rc, s)

    def kern(u_ref, g_ref, w_ref, du_ref, dw_ref, db_ref, upad_ref, gpad_ref, dwacc_ref):
        _fill_padded(upad_ref, u_ref, s)
        _fill_padded(gpad_ref, g_ref, s)
        dwacc_ref[...] = jnp.zeros(dwacc_ref.shape, F32)
        wv = w_ref[...]

        def chunk(r, carry):
            base = pl.multiple_of(r * rc, rc)
            gwins = _shifted_windows(gpad_ref[pl.ds(base, rc + 2 * CONV_HALO), :])
            uwins = _shifted_windows(upad_ref[pl.ds(base, rc + 2 * CONV_HALO), :])
            gc = g_ref[pl.ds(base, rc), :]
            acc = jnp.zeros((rc, LANES), F32)
            for kk in range(CONV_K):
                a, b = divmod(CONV_K - kk, 8)
                acc = acc + wv[kk:kk + 1, :] * gwins[b][8 * a:8 * a + rc]
                a, b = divmod(kk + 1, 8)
                prod = gc * uwins[b][8 * a:8 * a + rc]
                dwacc_ref[kk] += jnp.sum(prod.reshape(rc // 8, 8, LANES), axis=0)
            dwacc_ref[CONV_K] += jnp.sum(gc.reshape(rc // 8, 8, LANES), axis=0)
            du_ref[pl.ds(base, rc), :] = acc
            return carry

        lax.fori_loop(0, s // rc, chunk, 0)
        tot = jnp.sum(dwacc_ref[...], axis=1)
        lane_row = lax.broadcasted_iota(jnp.int32, (32, LANES), 0)
        dw_ref[...] = jnp.where(lane_row < CONV_K, tot, 0.0)
        db_ref[...] = tot[CONV_K:CONV_K + 1, :]

    cs = pl.BlockSpec((s, LANES), lambda c: (0, c))
    return pl.pallas_call(
        kern, name=name, grid=(D // LANES,),
        in_specs=[cs, cs, pl.BlockSpec((32, LANES), lambda c: (0, c))],
        out_specs=[cs, pl.BlockSpec((32, LANES), lambda c: (0, c)), pl.BlockSpec((1, LANES), lambda c: (0, c))],
        out_shape=[jax.ShapeDtypeStruct((s, D), F32), jax.ShapeDtypeStruct((32, D), F32),
                   jax.ShapeDtypeStruct((1, D), F32)],
        scratch_shapes=[pltpu.VMEM((s + 2 * CONV_HALO, LANES), F32), pltpu.VMEM((s + 2 * CONV_HALO, LANES), F32),
                        pltpu.VMEM((32, 8, LANES), F32)],
        compiler_params=_cparams(("parallel",)),
    )(u, g, w)


def _mod_local(name, c_all, ada_w):
    def kern(c_ref, w_ref, o_ref):
        o_ref[...] = jnp.dot(_silu(c_ref[...]), w_ref[...], preferred_element_type=F32,
                             precision=lax.Precision.HIGHEST)

    return pl.pallas_call(
        kern, name=name, grid=(DEPTH,),
        in_specs=[pl.BlockSpec((N_DEV, D), lambda l: (0, 0)), pl.BlockSpec((None, D, 384), lambda l: (l, 0, 0))],
        out_specs=pl.BlockSpec((None, N_DEV, 384), lambda l: (l, 0, 0)),
        out_shape=jax.ShapeDtypeStruct((DEPTH, N_DEV, 384), F32),
        compiler_params=_cparams(("parallel",)),
    )(c_all, ada_w)


def _ada_w_grad(name, c_all_t, dmod):
    def kern(c_ref, d_ref, o_ref):
        o_ref[...] = jnp.dot(_silu(c_ref[...]), d_ref[...], preferred_element_type=F32,
                             precision=lax.Precision.HIGHEST)

    return pl.pallas_call(
        kern, name=name, grid=(DEPTH,),
        in_specs=[pl.BlockSpec((D, LANES), lambda l: (0, 0)), pl.BlockSpec((None, LANES, 384), lambda l: (l, 0, 0))],
        out_specs=pl.BlockSpec((None, D, 384), lambda l: (l, 0, 0)),
        out_shape=jax.ShapeDtypeStruct((DEPTH, D, 384), F32),
        compiler_params=_cparams(("parallel",)),
    )(c_all_t, dmod)


def _pre_fn(x, g, scale, shift):
    return _rms(x, g) * (1.0 + scale) + shift


def _post_fn(y, g, gate):
    return gate * _rms(y, g)


def _ev_post_fn(o_heads, rg, a, mg):
    normed = []
    for oh in o_heads:
        mu = jnp.mean(oh, axis=-1, keepdims=True)
        var = jnp.mean(jnp.square(oh - mu), axis=-1, keepdims=True)
        normed.append((oh - mu) * lax.rsqrt(var + EPS))
    return jnp.concatenate([jnp.concatenate(normed, axis=1) * _silu(rg), a * _silu(mg)], axis=1)


def _od_post_fn(u, g, ln_g, ln_b):
    mu = jnp.mean(u, axis=-1, keepdims=True)
    var = jnp.mean(jnp.square(u - mu), axis=-1, keepdims=True)
    y = (u - mu) * lax.rsqrt(var + EPS) * ln_g + ln_b
    return _silu(y) * _silu(g)


def _glu_fn(a, b):
    return a * _sigmoid(b)


def _heads(x):
    return [x[:, LANES * h:LANES * (h + 1)] for h in range(4)]


def _colsum(x):
    return jnp.sum(x, axis=0, keepdims=True)


def _zrows(n, c):
    return jnp.zeros((n, c), BF16)


def _ev_win_layout(wt):
    rq = [p for h in range(4) for p in (wt[64 * h:64 * h + 64], _zrows(64, D))]
    rk = [p for h in range(4) for p in (wt[256 + 64 * h:256 + 64 * h + 64], _zrows(64, D))]
    return jnp.concatenate([wt[512:1024], wt[1024:1536], wt[2240:2752], wt[1536:1920],
                            wt[2176:2240], _zrows(64, D)] + rq + rk + [wt[1920:2176]], axis=0)


def _ev_win_unlayout(g):
    rq = [g[2048 + 128 * h:2048 + 128 * h + 64] for h in range(4)]
    rk = [g[2560 + 128 * h:2560 + 128 * h + 64] for h in range(4)]
    return jnp.concatenate(rq + rk + [g[0:512], g[512:1024], g[1536:1920], g[3072:3328],
                                      g[1920:1984], g[1024:1536]], axis=0)


def _uq_layout(wt):
    return jnp.concatenate([p for h in range(4) for p in (wt[192 * h:192 * h + 192], _zrows(64, 384))], axis=0)


def _uq_unlayout(g):
    return jnp.concatenate([g[256 * h:256 * h + 192] for h in range(4)], axis=0)


def _ukv_layout(wt):
    kpart = [p for h in range(4) for p in (wt[256 * h:256 * h + 128], _zrows(128, 256))]
    vpart = [wt[256 * h + 128:256 * h + 256] for h in range(4)]
    return jnp.concatenate(kpart + vpart, axis=0)


def _ukv_unlayout(g):
    return jnp.concatenate([p for h in range(4) for p in (g[256 * h:256 * h + 128], g[1024 + 128 * h:1024 + 128 * h + 128])],
                           axis=0)


def kernel(x, c, positions, ada_w, ada_b, pre_g, post_g, ev_w_in, ev_dec_f, ev_dec_b, ev_q_norm_g, ev_w_uq, ev_kv_norm_g, ev_w_ukv, ev_w_out, od_w_in, od_b_in, od_dw_w, od_dw_b, od_ln_g, od_ln_b, od_w_out, loss_target, m_ada_w, m_ada_b, m_pre_g, m_post_g, m_ev_w_in, m_ev_dec_f, m_ev_dec_b, m_ev_q_norm_g, m_ev_w_uq, m_ev_kv_norm_g, m_ev_w_ukv, m_ev_w_out, m_od_w_in, m_od_b_in, m_od_dw_w, m_od_dw_b, m_od_ln_g, m_od_ln_b, m_od_w_out, v_ada_w, v_ada_b, v_pre_g, v_post_g, v_ev_w_in, v_ev_dec_f, v_ev_dec_b, v_ev_q_norm_g, v_ev_w_uq, v_ev_kv_norm_g, v_ev_w_ukv, v_ev_w_out, v_od_w_in, v_od_b_in, v_od_dw_w, v_od_dw_b, v_od_ln_g, v_od_ln_b, v_od_w_out):
    s = x.shape[1]
    me = 4 * lax.axis_index("x") + 2 * lax.axis_index("y") + lax.axis_index("c")
    x0 = x.reshape(s, D)
    tgt = loss_target.reshape(s, D)
    ret_chunk = 256
    fa_cfg_f = ((min(1024, s // 2), min(1024, s // 2), 2),) * 2
    fa_cfg_b = ((min(512, s // 2), min(2048, s // 2), 4),) * 2

    start_parts = [c.reshape(-1), od_b_in.reshape(-1), od_dw_w.reshape(-1), od_dw_b.reshape(-1),
                   od_ln_g.reshape(-1), od_ln_b.reshape(-1)]
    start_sizes = [p.shape[0] for p in start_parts]
    start_len = -(-sum(start_sizes) // 1024) * 1024
    start_vec = jnp.concatenate(start_parts + [jnp.zeros((start_len - sum(start_sizes),), F32)])
    start_all = _exchange("gather_start", [start_vec.reshape(-1, LANES)], False)[0].reshape(N_DEV, start_len)
    offs = np.cumsum([0] + start_sizes)
    c_all = start_all[:, offs[0]:offs[1]]
    b_in_all = start_all[:, offs[1]:offs[2]].reshape(N_DEV, 2, 384).transpose(1, 0, 2).reshape(2, 1, ZW_OD)
    dw_w_all = start_all[:, offs[2]:offs[3]].reshape(N_DEV, 2, CONV_K, LANES).transpose(1, 2, 0, 3).reshape(2, CONV_K, D)
    dw_w_all = jnp.concatenate([dw_w_all, jnp.zeros((2, 1, D), F32)], axis=1)
    dw_b_all = start_all[:, offs[3]:offs[4]].reshape(N_DEV, 2, LANES).transpose(1, 0, 2).reshape(2, 1, D)
    ln_g_all = start_all[:, offs[4]:offs[5]].reshape(N_DEV, 2, LANES).transpose(1, 0, 2).reshape(2, 1, D)
    ln_b_all = start_all[:, offs[5]:offs[6]].reshape(N_DEV, 2, LANES).transpose(1, 0, 2).reshape(2, 1, D)

    mod_loc = _mod_local("mod_local", c_all, ada_w)
    mod_all = _exchange("gather_mod", [mod_loc.reshape(DEPTH * N_DEV, 384)], False)[0].reshape(N_DEV, DEPTH, N_DEV, 384)
    mod = lax.dynamic_index_in_dim(mod_all, me, axis=2, keepdims=False)
    mod = mod.transpose(1, 0, 2).reshape(DEPTH, 3 * D) + ada_b
    shift = [mod[l:l + 1, 0:D] for l in range(DEPTH)]
    scale = [mod[l:l + 1, D:2 * D] for l in range(DEPTH)]
    gate = [mod[l:l + 1, 2 * D:3 * D] for l in range(DEPTH)]

    ev_w = []
    for i in range(2):
        got = _exchange(f"gather_w_ev{i}", [ev_w_in[i].T.astype(BF16), ev_w_uq[i].T.astype(BF16),
                                            ev_w_ukv[i].T.astype(BF16), ev_w_out[i].astype(BF16)], False)
        win_t, uq_t, ukv_t, wout = [g.reshape(N_DEV * g.shape[1], g.shape[2]) for g in got]
        ev_w.append((_ev_win_layout(win_t), _uq_layout(uq_t), _ukv_layout(ukv_t), wout))
    od_w = []
    for i in range(2):
        got = _exchange(f"gather_w_od{i}", [od_w_in[i].T.astype(BF16), od_w_out[i].astype(BF16)], False)
        od_w.append(tuple(g.reshape(N_DEV * g.shape[1], g.shape[2]) for g in got))

    inv_freq = ROPE_BASE ** (-jnp.arange(0, 64, 2, dtype=F32) / 64)
    invf = jnp.tile(inv_freq, 4).reshape(1, LANES)
    sgn = jnp.tile(jnp.concatenate([-jnp.ones((32,), F32), jnp.ones((32,), F32)]), 2).reshape(1, LANES)

    def rope_body(rows, vecs):
        ang = rows[0].astype(F32) * vecs[0]
        return [jnp.cos(ang), jnp.sin(ang) * vecs[1]], []

    (cos_t, sin_t), _ = _rowwise("rope_tables", rope_body, [(positions.reshape(s, 1), 1, 0)], [invf, sgn],
                                 [(LANES, F32), (LANES, F32)])

    saved = []
    xl = x0
    for l in range(DEPTH):
        i = l // 2
        sv = dict(x=xl)

        def pre_body(rows, vecs):
            return [_pre_fn(rows[0], *vecs)], []

        (h,), _ = _rowwise(f"pre{l}", pre_body, [(xl, D, 0)], [pre_g[l:l + 1], scale[l], shift[l]], [(D, BF16)])
        sv["h"] = h
        if l % 2 == 0:
            win_t, uq_t, ukv_t, wout = ev_w[i]
            z = _mm(f"ev_in{l}", h, win_t, "nt")
            sv["z"] = z
            dec_f, dec_b = ev_dec_f[i:i + 1], ev_dec_b[i:i + 1]

            def prep_body(rows, vecs):
                rq, rk, cq, ckv, kr, cos, sin = rows
                return [_rope(rq, cos, sin), _rope(rk, cos, sin) * RET_SCALE, _rms(cq, vecs[0]), _rms(ckv, vecs[1]),
                        _rope(kr, cos, sin)], []

            (rq_r, rk_r, qn, kvn, krr), _ = _rowwise(
                f"ev_prep{l}", prep_body,
                [(z, 512, 4), (z, 512, 5), (z, 384, 4), (z, 256, 12), (z, LANES, 15), (cos_t, LANES, 0), (sin_t, LANES, 0)],
                [ev_q_norm_g[i:i + 1], ev_kv_norm_g[i:i + 1]],
                [(512, BF16), (512, BF16), (384, BF16), (256, BF16), (LANES, BF16)])
            sv.update(rq_r=rq_r, rk_r=rk_r, qn=qn, kvn=kvn)
            o_f, st_f = _ret_fwd(f"ret_f{l}", rq_r, rk_r, z, dec_f, False, ret_chunk)
            o_b, st_b = _ret_fwd(f"ret_b{l}", rq_r, rk_r, z, dec_b, True, ret_chunk)
            sv.update(o_f=o_f, o_b=o_b, st_f=st_f, st_b=st_b)
            q_pad = _mm(f"ev_uq{l}", qn, uq_t, "nt")
            kv_pad = _mm(f"ev_ukv{l}", kvn, ukv_t, "nt")

            def mla_prep_body(rows, vecs):
                qp, kk, vv, kr_r, cos, sin = rows
                qs, ks, vs = [], [], []
                ones = jnp.ones((qp.shape[0], LANES), F32)
                for hh in range(4):
                    qs += [qp[:, 256 * hh:256 * hh + 128], _rope(qp[:, 256 * hh + 128:256 * hh + 256], cos, sin)]
                    ks += [kk[:, 256 * hh:256 * hh + 128].astype(BF16), kr_r]
                    vs += [vv[:, LANES * hh:LANES * hh + LANES], ones]
                return [jnp.concatenate(qs, axis=1) * MLA_SCALE, jnp.concatenate(ks, axis=1),
                        jnp.concatenate(vs, axis=1)], []

            (qcat, kcat, v_x), _ = _rowwise(
                f"mla_prep{l}", mla_prep_body,
                [(q_pad, 1024, 0), (kv_pad, 1024, 0), (kv_pad, 512, 2), (krr, LANES, 0), (cos_t, LANES, 0), (sin_t, LANES, 0)],
                [], [(1024, BF16), (1024, BF16), (1024, BF16)])
            a_mla, lse = _fa_fwd(f"fa_fwd{l}", qcat, kcat, v_x, *fa_cfg_f[i])
            sv.update(qcat=qcat, kcat=kcat, v_x=v_x, a_mla=a_mla, lse=lse)

            def ev_post_body(rows, vecs):
                of, ob, rg, a, mg = rows
                return [_ev_post_fn(_heads(of + ob), rg, a, mg)], []

            (act,), _ = _rowwise(f"ev_post{l}", ev_post_body,
                                 [(o_f, 512, 0), (o_b, 512, 0), (z, 512, 1), (a_mla, 512, 0), (z, 512, 2)], [], [(D, BF16)])
        else:
            win_t, wout = od_w[i]
            z = _mm(f"od_in{l}", h, win_t, "nt", bias=b_in_all[i])
            sv["z"] = z

            def glu_body(rows, vecs):
                return [_glu_fn(rows[0], rows[1])], []

            (u,), _ = _rowwise(f"glu{l}", glu_body, [(z, D, 0), (z, D, 1)], [], [(D, F32)])
            u2 = _conv_fwd(f"conv{l}", u, dw_w_all[i], dw_b_all[i])
            sv.update(u=u, u2=u2)

            def od_post_body(rows, vecs):
                return [_od_post_fn(rows[0], rows[1], vecs[0], vecs[1])], []

            (act,), _ = _rowwise(f"od_post{l}", od_post_body, [(u2, D, 0), (z, D, 2)], [ln_g_all[i], ln_b_all[i]],
                                 [(D, BF16)])
        sv["act"] = act
        y = _mm(f"out{l}", act, wout, "nn")
        sv["y"] = y

        def post_body(rows, vecs):
            return [rows[0] + _post_fn(rows[1], vecs[0], vecs[1])], []

        (xl,), _ = _rowwise(f"post{l}", post_body, [(xl, D, 0), (y, D, 0)], [post_g[l:l + 1], gate[l]], [(D, F32)])
        saved.append(sv)

    def loss_body(rows, vecs):
        diff = rows[0] - rows[1]
        return [diff * (1.0 / D)], [_colsum(diff * diff) * (0.5 / D)]

    (dx,), (loss_lanes,) = _rowwise("loss", loss_body, [(xl, D, 0), (tgt, D, 0)], [], [(D, F32)], [(1, D)])
    loss = lax.psum(jnp.sum(loss_lanes), ("x", "y", "c"))

    g_pre, g_post, g_mod = [None] * DEPTH, [None] * DEPTH, [None] * DEPTH
    g_dec_f, g_dec_b, g_qn, g_kvn = [None] * 2, [None] * 2, [None] * 2, [None] * 2
    g_b_in, g_dw_w, g_dw_b, g_ln_g, g_ln_b = [None] * 2, [None] * 2, [None] * 2, [None] * 2, [None] * 2
    recv_ev, recv_od = [None] * 2, [None] * 2
    for l in reversed(range(DEPTH)):
        i = l // 2
        sv = saved[l]

        def post_bwd_body(rows, vecs):
            yv, dxn = rows
            r, vjp = jax.vjp(_post_fn, yv, vecs[0], vecs[1])
            dy, dg, dgate = vjp(dxn)
            return [dy], [dg, dgate]

        (dy,), (dpost, dgate) = _rowwise(f"post_bwd{l}", post_bwd_body, [(sv["y"], D, 0), (dx, D, 0)],
                                         [post_g[l:l + 1], gate[l]], [(D, BF16)], [(1, D), (1, D)])
        g_post[l] = dpost
        wout = ev_w[i][3] if l % 2 == 0 else od_w[i][1]
        dact = _mm(f"out_dgrad{l}", dy, wout, "nt")
        d_wout = _mm(f"out_wgrad{l}", sv["act"], dy, "tn", out_dtype=BF16)
        z = sv["z"]
        if l % 2 == 0:
            win_t, uq_t, ukv_t, _ = ev_w[i]
            dec_f, dec_b = ev_dec_f[i:i + 1], ev_dec_b[i:i + 1]

            def ev_post_bwd_body(rows, vecs):
                of, ob, rg, a, mg, da = rows
                _, vjp = jax.vjp(_ev_post_fn, _heads(of + ob), rg, a, mg)
                do_heads, drg, d_a, dmg = vjp(da)
                deltas = [jnp.sum(dh_ * ah_, axis=1, keepdims=True) for dh_, ah_ in zip(_heads(d_a), _heads(a))]
                return [jnp.concatenate(do_heads, axis=1), drg, d_a, dmg] + deltas, []

            (do_ret, drg, do_mla, dmg, dl0, dl1, dl2, dl3), _ = _rowwise(
                f"ev_post_bwd{l}", ev_post_bwd_body,
                [(sv["o_f"], 512, 0), (sv["o_b"], 512, 0), (z, 512, 1), (sv["a_mla"], 512, 0), (z, 512, 2), (dact, D, 0)],
                [], [(512, F32), (512, BF16), (512, BF16), (512, BF16)] + [(1, F32)] * 4)
            delta = jnp.stack([dl0, dl1, dl2, dl3]).reshape(MLA_HEADS, 1, s)
            lse = sv["lse"].reshape(MLA_HEADS, 1, s)
            kt = sv["kcat"].reshape(s, MLA_HEADS, 256).transpose(1, 2, 0)
            dqt, dkcat, dv = _fa_bwd(f"fa_bwd{l}", sv["qcat"], sv["kcat"], kt, sv["v_x"], do_mla, lse, delta, *fa_cfg_b[i])
            dqcat = dqt.transpose(1, 3, 0, 2).reshape(s, 1024)

            def mla_prep_bwd_body(rows, vecs):
                dq, dk, dvv, cos, sin = rows
                qs = []
                dkrr = jnp.zeros((dq.shape[0], LANES), F32)
                for hh in range(4):
                    qs += [dq[:, 256 * hh:256 * hh + 128], _rope_t(dq[:, 256 * hh + 128:256 * hh + 256], cos, sin)]
                    dkrr = dkrr + dk[:, 256 * hh + 128:256 * hh + 256]
                return [jnp.concatenate(qs, axis=1) * MLA_SCALE, jnp.concatenate([dk, dvv], axis=1), dkrr], []

            (dq_pad, dkv_pad, dkrr), _ = _rowwise(
                f"mla_prep_bwd{l}", mla_prep_bwd_body,
                [(dqcat, 1024, 0), (dkcat, 1024, 0), (dv, 512, 0), (cos_t, LANES, 0), (sin_t, LANES, 0)], [],
                [(1024, BF16), (1536, BF16), (LANES, F32)])
            dqn = _mm(f"uq_dgrad{l}", dq_pad, uq_t, "nn")
            d_uq = _mm(f"uq_wgrad{l}", dq_pad, sv["qn"], "tn", out_dtype=BF16)
            dkvn = _mm(f"ukv_dgrad{l}", dkv_pad, ukv_t, "nn")
            d_ukv = _mm(f"ukv_wgrad{l}", dkv_pad, sv["kvn"], "tn", out_dtype=BF16)
            dq_f, dk_f, dv_f, ddec_f = _ret_bwd(f"ret_f_bwd{l}", sv["rq_r"], sv["rk_r"], z, dec_f, sv["st_f"], do_ret,
                                                False, ret_chunk)
            dq_b, dk_b, dv_b, ddec_b = _ret_bwd(f"ret_b_bwd{l}", sv["rq_r"], sv["rk_r"], z, dec_b, sv["st_b"], do_ret,
                                                True, ret_chunk)
            g_dec_f[i], g_dec_b[i] = ddec_f[:, :RET_HEADS], ddec_b[:, :RET_HEADS]

            def prep_bwd_body(rows, vecs):
                cq, ckv, cos, sin, dqf, dqb, dkf, dkb, dvf, dvb, d_qn, d_kvn, d_krr = rows
                _, vjp_q = jax.vjp(_rms, cq, vecs[0])
                dcq, dgq = vjp_q(d_qn)
                _, vjp_kv = jax.vjp(_rms, ckv, vecs[1])
                dckv, dgkv = vjp_kv(d_kvn)
                return [dvf + dvb, dcq, _rope_t(d_krr, cos, sin), _rope_t(dqf + dqb, cos, sin),
                        _rope_t(dkf + dkb, cos, sin) * RET_SCALE, dckv], [dgq, dgkv]

            (drv, dcq, dkr, drq, drk, dckv), (dgq, dgkv) = _rowwise(
                f"ev_prep_bwd{l}", prep_bwd_body,
                [(z, 384, 4), (z, 256, 12), (cos_t, LANES, 0), (sin_t, LANES, 0), (dq_f, 512, 0), (dq_b, 512, 0),
                 (dk_f, 512, 0), (dk_b, 512, 0), (dv_f, 512, 0), (dv_b, 512, 0), (dqn, 384, 0), (dkvn, 256, 0),
                 (dkrr, LANES, 0)],
                [ev_q_norm_g[i:i + 1], ev_kv_norm_g[i:i + 1]],
                [(512, BF16), (384, BF16), (LANES, BF16), (512, BF16), (512, BF16), (256, BF16)], [(1, 384), (1, 256)])
            g_qn[i], g_kvn[i] = dgq, dgkv
            dz = jnp.concatenate([drv, drg, dmg, dcq, dkr, drq, drk, dckv], axis=1)
            d_win = _mm(f"in_wgrad{l}", dz, sv["h"], "tn", out_dtype=BF16)
            recv_ev[i] = _exchange(f"scatter_g_ev{i}",
                                   [_ev_win_unlayout(d_win).reshape(N_DEV, 344, D), _uq_unlayout(d_uq).reshape(N_DEV, 96, 384),
                                    _ukv_unlayout(d_ukv).reshape(N_DEV, 128, 256), d_wout.reshape(N_DEV, 128, D)], True)
        else:
            win_t, _ = od_w[i]

            def od_post_bwd_body(rows, vecs):
                u2, gg, da = rows
                _, vjp = jax.vjp(_od_post_fn, u2, gg, vecs[0], vecs[1])
                du2, dgg, dlg, dlb = vjp(da)
                return [du2, dgg], [dlg, dlb, _colsum(dgg)]

            (du2, dg_gate), (dlg, dlb, dbg) = _rowwise(
                f"od_post_bwd{l}", od_post_bwd_body, [(sv["u2"], D, 0), (z, D, 2), (dact, D, 0)],
                [ln_g_all[i], ln_b_all[i]], [(D, F32), (D, BF16)], [(1, D), (1, D), (1, D)])
            g_ln_g[i], g_ln_b[i] = dlg, dlb
            du, d_dw, d_dwb = _conv_bwd(f"conv_bwd{l}", sv["u"], du2, dw_w_all[i])
            g_dw_w[i], g_dw_b[i] = d_dw[:CONV_K], d_dwb

            def glu_bwd_body(rows, vecs):
                a, b, d_u = rows
                _, vjp = jax.vjp(_glu_fn, a, b)
                d_a, d_b = vjp(d_u)
                return [d_a, d_b], [_colsum(d_a), _colsum(d_b)]

            (d_a, d_b), (dba, dbb) = _rowwise(f"glu_bwd{l}", glu_bwd_body, [(z, D, 0), (z, D, 1), (du, D, 0)], [],
                                              [(D, BF16), (D, BF16)], [(1, D), (1, D)])
            g_b_in[i] = jnp.concatenate([dba, dbb, dbg], axis=1)
            dz = jnp.concatenate([d_a, d_b, dg_gate], axis=1)
            d_win = _mm(f"in_wgrad{l}", dz, sv["h"], "tn", out_dtype=BF16)
            recv_od[i] = _exchange(f"scatter_g_od{i}", [d_win.reshape(N_DEV, 384, D), d_wout.reshape(N_DEV, 128, D)], True)
        dh = _mm(f"in_dgrad{l}", dz, win_t, "nn")

        def pre_bwd_body(rows, vecs):
            xv, d_h, dxn = rows
            _, vjp = jax.vjp(_pre_fn, xv, *vecs)
            d_x, dg, dsc, dsh = vjp(d_h)
            return [d_x + dxn], [dg, dsc, dsh]

        (dx,), (dpre, dscale, dshift) = _rowwise(f"pre_bwd{l}", pre_bwd_body, [(sv["x"], D, 0), (dh, D, 0), (dx, D, 0)],
                                                 [pre_g[l:l + 1], scale[l], shift[l]], [(D, F32)], [(1, D)] * 3)
        g_pre[l] = dpre
        g_mod[l] = jnp.concatenate([dshift, dscale, dgate], axis=1)

    grad_x = dx.reshape(1, s, D)

    end_parts = [jnp.concatenate(g_mod, axis=0), jnp.concatenate(g_pre, axis=0), jnp.concatenate(g_post, axis=0),
                 jnp.concatenate(g_dec_f, axis=0), jnp.concatenate(g_dec_b, axis=0), jnp.concatenate(g_qn, axis=0),
                 jnp.concatenate(g_kvn, axis=0), jnp.concatenate(g_b_in, axis=0), jnp.stack(g_dw_w),
                 jnp.concatenate(g_dw_b, axis=0), jnp.concatenate(g_ln_g, axis=0), jnp.concatenate(g_ln_b, axis=0)]
    end_shapes = [p.shape for p in end_parts]
    end_sizes = [int(np.prod(sh)) for sh in end_shapes]
    end_len = -(-sum(end_sizes) // 1024) * 1024
    end_vec = jnp.concatenate([p.reshape(-1) for p in end_parts] + [jnp.zeros((end_len - sum(end_sizes),), F32)])
    end_all = _exchange("gather_end", [end_vec.reshape(-1, LANES)], False)[0].reshape(N_DEV, end_len)
    eo = np.cumsum([0] + end_sizes)
    ends = [end_all[:, eo[j]:eo[j + 1]].reshape((N_DEV,) + tuple(end_shapes[j])) for j in range(len(end_parts))]
    (dmod_all, pre_all, post_all, decf_all, decb_all, qn_all, kvn_all, bin_all, dww_all, dwb_all, lng_all,
     lnb_all) = ends

    def pack_rep(*ts):
        lead = ts[0].ndim - 2
        return jnp.concatenate([t.reshape(t.shape[:lead] + (-1,)) for t in ts], axis=-1)

    rep_sizes = [DEPTH * 3 * D, DEPTH * D, DEPTH * D, 8, 8, 2 * 384, 2 * 256]
    rep_len = -(-sum(rep_sizes) // 1024) * 1024
    rep_pad = rep_len - sum(rep_sizes)

    def rep_rows(flat):
        padz = jnp.zeros(flat.shape[:-1] + (rep_pad,), F32)
        return jnp.concatenate([flat, padz], axis=-1).reshape(flat.shape[:-1] + (rep_len // LANES, LANES))

    rep_w = rep_rows(pack_rep(ada_b, pre_g, post_g, ev_dec_f, ev_dec_b, ev_q_norm_g, ev_kv_norm_g))
    rep_m = rep_rows(pack_rep(m_ada_b, m_pre_g, m_post_g, m_ev_dec_f, m_ev_dec_b, m_ev_q_norm_g, m_ev_kv_norm_g))
    rep_v = rep_rows(pack_rep(v_ada_b, v_pre_g, v_post_g, v_ev_dec_f, v_ev_dec_b, v_ev_q_norm_g, v_ev_kv_norm_g))
    rep_g = rep_rows(pack_rep(dmod_all, pre_all, post_all, decf_all, decb_all, qn_all, kvn_all))
    rep_out = _adamw("adamw_rep", rep_w, rep_m, rep_v, rep_g)
    ro = np.cumsum([0] + rep_sizes)
    rep_shapes = [(DEPTH, 3 * D), (DEPTH, D), (DEPTH, D), (2, 4), (2, 4), (2, 384), (2, 256)]

    def unpack_rep(t):
        flat = t.reshape(-1)
        return [flat[ro[j]:ro[j + 1]].reshape(rep_shapes[j]) for j in range(len(rep_shapes))]

    rep_res = [unpack_rep(t) for t in rep_out]

    def my_cols(t, width):
        return lax.dynamic_slice_in_dim(t, me * width, width, axis=t.ndim - 1)

    def vec_adamw(name, w, m, v, g_all, width):
        g = my_cols(g_all, width)
        r = _adamw(name, w.reshape(-1, width), m.reshape(-1, width), v.reshape(-1, width),
                   g.reshape(N_DEV, -1, width))
        return [t.reshape(w.shape) for t in r]

    res_b_in = vec_adamw("adamw_b_in", od_b_in, m_od_b_in, v_od_b_in, bin_all, 384)
    res_dw_w = vec_adamw("adamw_dw_w", od_dw_w, m_od_dw_w, v_od_dw_w, dww_all, LANES)
    res_dw_b = vec_adamw("adamw_dw_b", od_dw_b, m_od_dw_b, v_od_dw_b, dwb_all, LANES)
    res_ln_g = vec_adamw("adamw_ln_g", od_ln_g, m_od_ln_g, v_od_ln_g, lng_all, LANES)
    res_ln_b = vec_adamw("adamw_ln_b", od_ln_b, m_od_ln_b, v_od_ln_b, lnb_all, LANES)

    dmod_mine = my_cols(dmod_all, 384).transpose(1, 0, 2)
    dmod_pad = jnp.concatenate([dmod_mine, jnp.zeros((DEPTH, LANES - N_DEV, 384), F32)], axis=1)
    c_all_t = jnp.concatenate([c_all.T, jnp.zeros((D, LANES - N_DEV), F32)], axis=1)
    g_ada_w = _ada_w_grad("ada_w_grad", c_all_t, dmod_pad)
    res_ada_w = [t.reshape(ada_w.shape) for t in
                 _adamw("adamw_ada_w", ada_w.reshape(-1, 384), m_ada_w.reshape(-1, 384), v_ada_w.reshape(-1, 384),
                        g_ada_w.reshape(-1, 384))]

    ev_sh = [[_sum_parts(f"sum_g_ev{i}_{j}", r) for j, r in enumerate(recv_ev[i])] for i in range(2)]
    od_sh = [[_sum_parts(f"sum_g_od{i}_{j}", r) for j, r in enumerate(recv_od[i])] for i in range(2)]

    def mat_adamw(name, w, m, v, g):
        r = _adamw(name, w.reshape(-1, w.shape[-1]), m.reshape(-1, w.shape[-1]), v.reshape(-1, w.shape[-1]),
                   g.reshape(-1, w.shape[-1]))
        return [t.reshape(w.shape) for t in r]

    res_ev_w_in = mat_adamw("adamw_ev_w_in", ev_w_in, m_ev_w_in, v_ev_w_in, jnp.stack([ev_sh[i][0].T for i in range(2)]))
    res_ev_w_uq = mat_adamw("adamw_ev_w_uq", ev_w_uq, m_ev_w_uq, v_ev_w_uq, jnp.stack([ev_sh[i][1].T for i in range(2)]))
    res_ev_w_ukv = mat_adamw("adamw_ev_w_ukv", ev_w_ukv, m_ev_w_ukv, v_ev_w_ukv,
                             jnp.stack([ev_sh[i][2].T for i in range(2)]))
    res_ev_w_out = mat_adamw("adamw_ev_w_out", ev_w_out, m_ev_w_out, v_ev_w_out, jnp.stack([ev_sh[i][3] for i in range(2)]))
    res_od_w_in = mat_adamw("adamw_od_w_in", od_w_in, m_od_w_in, v_od_w_in, jnp.stack([od_sh[i][0].T for i in range(2)]))
    res_od_w_out = mat_adamw("adamw_od_w_out", od_w_out, m_od_w_out, v_od_w_out, jnp.stack([od_sh[i][1] for i in range(2)]))

    per_weight = [res_ada_w] + [[rep_res[t][j] for t in range(4)] for j in range(3)]
    per_weight += [res_ev_w_in, [rep_res[t][3] for t in range(4)], [rep_res[t][4] for t in range(4)],
                   [rep_res[t][5] for t in range(4)], res_ev_w_uq, [rep_res[t][6] for t in range(4)], res_ev_w_ukv,
                   res_ev_w_out, res_od_w_in, res_b_in, res_dw_w, res_dw_b, res_ln_g, res_ln_b, res_od_w_out]
    outs = [loss, grad_x]
    for t in range(4):
        outs += [pw[t] for pw in per_weight]
    return tuple(outs)
```

```python
import functools

import numpy as np
import jax
import jax.numpy as jnp
from jax import lax
from jax.experimental import pallas as pl
from jax.experimental.pallas import tpu as pltpu

F32 = jnp.float32
BF16 = jnp.bfloat16
MESH = pl.DeviceIdType.MESH

N_DEV = 8
D = 1024
DEPTH = 4
EPS = 1e-6
RET_HEADS = 4
MLA_HEADS = 4
RET_SCALE = 64 ** -0.5
MLA_SCALE = 192 ** -0.5
CONV_K = 31
CONV_HALO = 16
ROPE_BASE = 10000.0

ADAM_LR = 0.001
ADAM_B1 = 0.9
ADAM_B2 = 0.999
ADAM_EPS = 1e-08
ADAM_WD = 0.01
ADAM_STEP = 10

LANES = 128
VMEM_LIMIT = 48 * 1024 * 1024

ZL_EV = dict(rv=(0, 512), rg=(512, 512), mg=(1024, 512), cq=(1536, 384), kr=(1920, 128),
             rq=(2048, 512), rk=(2560, 512), ckv=(3072, 256))
ZW_EV = 3328
ZW_OD = 3072


def _cparams(sem, vmem=VMEM_LIMIT):
    return pltpu.CompilerParams(dimension_semantics=sem, vmem_limit_bytes=vmem)


def _pick(n, prefs):
    for p in prefs:
        if n % p == 0:
            return p
    return n


def _sigmoid(x):
    return 0.5 * (jnp.tanh(0.5 * x) + 1.0)


def _silu(x):
    return x * _sigmoid(x)


def _rms(x, g):
    return x * lax.rsqrt(jnp.mean(x * x, axis=-1, keepdims=True) + EPS) * g


def _log_sigmoid(x):
    return jnp.minimum(x, 0.0) - jnp.log(1.0 + jnp.exp(jnp.minimum(x, -x)))


def _tile_lanes(t, width):
    reps = width // t.shape[1]
    return t if reps == 1 else jnp.concatenate([t] * reps, axis=1)


def _rot_half(x):
    w = x.shape[1]
    lane = lax.broadcasted_iota(jnp.int32, x.shape, 1)
    first = jnp.bitwise_and(lane, 63) < 32
    return jnp.where(first, pltpu.roll(x, w - 32, 1), pltpu.roll(x, 32, 1))


def _rope(x, cos, sin):
    w = x.shape[1]
    return x * _tile_lanes(cos, w) + _rot_half(x) * _tile_lanes(sin, w)


def _rope_t(dy, cos, sin):
    w = dy.shape[1]
    return dy * _tile_lanes(cos, w) + _rot_half(dy * _tile_lanes(sin, w))


_DN = {"nn": (((1,), (0,)), ((), ())), "nt": (((1,), (1,)), ((), ())), "tn": (((0,), (0,)), ((), ()))}


def _dot(a, b, mode):
    return lax.dot_general(a.astype(BF16), b.astype(BF16), _DN[mode], preferred_element_type=F32)


@functools.partial(jax.custom_vjp, nondiff_argnums=(2,))
def _bdot(a, b, mode):
    return _dot(a, b, mode)


def _bdot_fwd(a, b, mode):
    return _dot(a, b, mode), (a, b)


def _bdot_bwd(mode, res, g):
    a, b = res
    if mode == "nn":
        return _dot(g, b, "nt"), _dot(a, g, "tn")
    if mode == "nt":
        return _dot(g, b, "nn"), _dot(g, a, "tn")
    return _dot(b, g, "nt"), _dot(a, g, "nn")


_bdot.defvjp(_bdot_fwd, _bdot_bwd)


def _rowwise(name, body, row_ins, vec_ins, row_outs, red_outs=(), tile=512):
    s = row_ins[0][0].shape[0]
    tile = min(tile, s)
    nr, nv, no = len(row_ins), len(vec_ins), len(row_outs)

    def kern(*refs):
        rows = [r[...] for r in refs[:nr]]
        vecs = [r[...] for r in refs[nr:nr + nv]]
        outs, reds = body(rows, vecs)
        for r, o in zip(refs[nr + nv:nr + nv + no], outs):
            r[...] = o.astype(r.dtype)
        red_refs = refs[nr + nv + no:]
        if red_refs:
            @pl.when(pl.program_id(0) == 0)
            def _():
                for r in red_refs:
                    r[...] = jnp.zeros(r.shape, r.dtype)
            for r, v in zip(red_refs, reds):
                r[...] += v

    in_specs = [pl.BlockSpec((tile, w), (lambda i, cb=cb: (i, cb))) for (_, w, cb) in row_ins]
    in_specs += [pl.BlockSpec(v.shape, (lambda i, nd=v.ndim: (0,) * nd)) for v in vec_ins]
    out_specs = [pl.BlockSpec((tile, w), lambda i: (i, 0)) for (w, _) in row_outs]
    out_specs += [pl.BlockSpec(sh, lambda i: (0, 0)) for sh in red_outs]
    out_shape = [jax.ShapeDtypeStruct((s, w), dt) for (w, dt) in row_outs]
    out_shape += [jax.ShapeDtypeStruct(sh, F32) for sh in red_outs]
    res = pl.pallas_call(
        kern, name=name, grid=(s // tile,), in_specs=in_specs, out_specs=out_specs, out_shape=out_shape,
        compiler_params=_cparams(("arbitrary",)),
    )(*[a for (a, _, _) in row_ins], *vec_ins)
    return res[:no], res[no:]


def _mm(name, a, b, mode, out_dtype=F32, bias=None, comm=None):
    if mode == "tn":
        k, m = a.shape
        n = b.shape[1]
        tm = m if m <= 1664 else m // 2
        tk = min(k, 1024)
        nk = k // tk

        def kern(a_ref, b_ref, o_ref, acc_ref):
            kk = pl.program_id(1)
            part = _dot(a_ref[...], b_ref[...], "tn")

            @pl.when(kk == 0)
            def _():
                acc_ref[...] = part

            @pl.when(kk > 0)
            def _():
                acc_ref[...] += part

            @pl.when(kk == nk - 1)
            def _():
                o_ref[...] = acc_ref[...].astype(o_ref.dtype)

        return pl.pallas_call(
            kern, name=name, grid=(m // tm, nk),
            in_specs=[pl.BlockSpec((tk, tm), lambda i, kk: (kk, i)),
                      pl.BlockSpec((tk, n), lambda i, kk: (kk, 0))],
            out_specs=pl.BlockSpec((tm, n), lambda i, kk: (i, 0)),
            out_shape=jax.ShapeDtypeStruct((m, n), out_dtype),
            scratch_shapes=[pltpu.VMEM((tm, n), F32)],
            compiler_params=_cparams(("parallel", "arbitrary")),
        )(a, b)

    m, k = a.shape
    n = b.shape[1] if mode == "nn" else b.shape[0]
    tm = min(m, 1024)
    tn = n if n <= 1664 else n // 2
    has_bias = bias is not None

    def kern(*refs):
        a_ref, b_ref = refs[0], refs[1]
        o_ref = refs[-1]
        r = _dot(a_ref[...], b_ref[...], mode)
        if has_bias:
            r = r + refs[2][...]
        o_ref[...] = r.astype(o_ref.dtype)

    b_spec = (pl.BlockSpec((k, tn), lambda i, j: (0, j)) if mode == "nn"
              else pl.BlockSpec((tn, k), lambda i, j: (j, 0)))
    in_specs = [pl.BlockSpec((tm, k), lambda i, j: (i, 0)), b_spec]
    args = [a, b]
    if has_bias:
        in_specs.append(pl.BlockSpec((1, tn), lambda i, j: (0, j)))
        args.append(bias)
    outs, got = _call(kern, name=name, grid=(m // tm, n // tn), in_specs=in_specs,
                      out_specs=[pl.BlockSpec((tm, tn), lambda i, j: (i, j))],
                      out_shape=[jax.ShapeDtypeStruct((m, n), out_dtype)], args=args, sem=("parallel", "parallel"),
                      comm=comm)
    return (outs[0], got) if comm else outs[0]


def _peers():
    mx, my, mc = lax.axis_index("x"), lax.axis_index("y"), lax.axis_index("c")
    me = 4 * mx + 2 * my + mc
    out = []
    for k in range(1, N_DEV):
        px = 1 - mx if (k >> 2) & 1 else mx
        py = 1 - my if (k >> 1) & 1 else my
        pc = 1 - mc if k & 1 else mc
        out.append((k, (px, py, pc), 4 * px + 2 * py + pc))
    return me, out


def _xchg_copies(x_refs, out_refs, scatter, send_sems, recv_sems, local_sems):
    me, peers = _peers()
    local, out, arrive = [], [], []
    for a, (x, o, sc) in enumerate(zip(x_refs, out_refs, scatter)):
        mine = x.at[me] if sc else x
        local.append(pltpu.make_async_copy(mine, o.at[me], local_sems.at[a]))
        for k, dev, p in peers:
            out.append(pltpu.make_async_remote_copy(
                src_ref=x.at[p] if sc else x, dst_ref=o.at[me],
                send_sem=send_sems.at[a, k - 1], recv_sem=recv_sems.at[a, k - 1],
                device_id=dev, device_id_type=MESH))
            arrive.append(pltpu.make_async_remote_copy(
                src_ref=mine, dst_ref=o.at[p],
                send_sem=send_sems.at[a, k - 1], recv_sem=recv_sems.at[a, k - 1],
                device_id=dev, device_id_type=MESH))
    return local, out, arrive


def _xchg_start(*args):
    local, out, _ = _xchg_copies(*args)
    for cp in local + out:
        cp.start()


def _xchg_wait(*args):
    local, out, arrive = _xchg_copies(*args)
    for cp in out:
        cp.wait_send()
    for cp in arrive:
        cp.wait_recv()
    for cp in local:
        cp.wait()


def _call(kern, *, name, grid, in_specs, out_specs, out_shape, args, sem, scratch_shapes=(), vmem=VMEM_LIMIT,
          comm=None):
    if not comm:
        outs = pl.pallas_call(kern, name=name, grid=grid, in_specs=in_specs, out_specs=out_specs, out_shape=out_shape,
                              scratch_shapes=list(scratch_shapes), compiler_params=_cparams(sem, vmem))(*args)
        return outs, []
    n, ni, no, ns = len(comm), len(in_specs), len(out_specs), len(scratch_shapes)
    xs = [x for x, _ in comm]
    scatter = [sc for _, sc in comm]

    def body(*refs):
        ins, x_refs = refs[:ni], refs[ni:ni + n]
        outs, out_refs = refs[ni + n:ni + n + no], refs[ni + n + no:ni + 2 * n + no]
        scr = refs[ni + 2 * n + no:ni + 2 * n + no + ns]
        sems = refs[ni + 2 * n + no + ns:]
        ids = [pl.program_id(d) for d in range(len(grid))]
        first = functools.reduce(jnp.logical_and, [i == 0 for i in ids])
        last = functools.reduce(jnp.logical_and, [i == g - 1 for i, g in zip(ids, grid)])

        @pl.when(first)
        def _():
            _xchg_start(x_refs, out_refs, scatter, *sems)

        kern(*ins, *outs, *scr)

        @pl.when(last)
        def _():
            _xchg_wait(x_refs, out_refs, scatter, *sems)

    any_spec = pl.BlockSpec(memory_space=pl.ANY)
    res = pl.pallas_call(
        body, name=name, grid=grid,
        in_specs=list(in_specs) + [any_spec] * n, out_specs=list(out_specs) + [any_spec] * n,
        out_shape=list(out_shape) + [jax.ShapeDtypeStruct((N_DEV,) + tuple(x.shape[1:] if sc else x.shape), x.dtype)
                                     for x, sc in comm],
        scratch_shapes=list(scratch_shapes) + [pltpu.SemaphoreType.DMA((n, N_DEV - 1)),
                                               pltpu.SemaphoreType.DMA((n, N_DEV - 1)), pltpu.SemaphoreType.DMA((n,))],
        compiler_params=pltpu.CompilerParams(dimension_semantics=("arbitrary",) * len(grid), vmem_limit_bytes=vmem,
                                             has_side_effects=True),
    )(*args, *xs)
    return res[:no], res[no:]


def _exchange(name, xs, scatter):
    def nothing():
        pass

    return _call(nothing, name=name, grid=(1,), in_specs=[], out_specs=[], out_shape=[], args=[], sem=("arbitrary",),
                 comm=[(x, scatter) for x in xs])[1]


def _sum_parts(name, x):
    p, r, c = x.shape
    tr = r if r * c * p * x.dtype.itemsize <= (8 << 20) else _pick(r, (256, 128, 64, 16))

    def kern(x_ref, o_ref):
        acc = x_ref[0].astype(F32)
        for i in range(1, p):
            acc = acc + x_ref[i].astype(F32)
        o_ref[...] = acc

    return pl.pallas_call(
        kern, name=name, grid=(r // tr,),
        in_specs=[pl.BlockSpec((p, tr, c), lambda i: (0, i, 0))],
        out_specs=pl.BlockSpec((tr, c), lambda i: (i, 0)),
        out_shape=jax.ShapeDtypeStruct((r, c), F32),
        compiler_params=_cparams(("parallel",)),
    )(x)


def _adamw(name, w, m, v, g):
    r, c = w.shape
    parts = g.shape[0] if g.ndim == 3 else 0
    tr = 512 if (r > 512 and r % 512 == 0) else r

    def kern(w_ref, m_ref, v_ref, g_ref, go_ref, d_ref, mo_ref, vo_ref):
        if parts:
            gg = g_ref[0]
            for i in range(1, parts):
                gg = gg + g_ref[i]
        else:
            gg = g_ref[...]
        mm = ADAM_B1 * m_ref[...] + (1.0 - ADAM_B1) * gg
        vv = ADAM_B2 * v_ref[...] + (1.0 - ADAM_B2) * (gg * gg)
        m_hat = mm / (1.0 - ADAM_B1 ** ADAM_STEP)
        v_hat = vv / (1.0 - ADAM_B2 ** ADAM_STEP)
        go_ref[...] = gg
        d_ref[...] = -ADAM_LR * (m_hat / (jnp.sqrt(v_hat) + ADAM_EPS) + ADAM_WD * w_ref[...])
        mo_ref[...] = mm
        vo_ref[...] = vv

    spec = pl.BlockSpec((tr, c), lambda i: (i, 0))
    gspec = pl.BlockSpec((parts, tr, c), lambda i: (0, i, 0)) if parts else spec
    sh = jax.ShapeDtypeStruct((r, c), F32)
    return pl.pallas_call(
        kern, name=name, grid=(r // tr,), in_specs=[spec, spec, spec, gspec],
        out_specs=[spec] * 4, out_shape=[sh] * 4,
        compiler_params=_cparams(("parallel",)),
    )(w, m, v, g)


def _ret_chunk(q, k, v, st, dec_cc, dec_cd, dec_dd, reverse):
    c = q.shape[0]
    row = lax.broadcasted_iota(jnp.int32, (c, c), 0).astype(F32)
    col = lax.broadcasted_iota(jnp.int32, (c, c), 1).astype(F32)
    pos = lax.broadcasted_iota(jnp.int32, (c, LANES), 0).astype(F32)
    if reverse:
        diff, mask = col - row, col > row
        q_exp, k_exp = c - pos, pos
    else:
        diff, mask = row - col, row >= col
        q_exp, k_exp = pos + 1.0, c - 1.0 - pos
    decay = jnp.where(mask, jnp.exp(_log_sigmoid(dec_cc) * jnp.maximum(diff, 0.0)), 0.0)
    lam_cd = _log_sigmoid(dec_cd)
    scores = _bdot(q, k, "nt") * decay
    o = _bdot(scores, v, "nn") + _bdot(q * jnp.exp(lam_cd * q_exp), st, "nn")
    st_new = st * jnp.exp(_log_sigmoid(dec_dd) * float(c)) + _bdot(k * jnp.exp(lam_cd * k_exp), v, "tn")
    return o, st_new


def _ret_dec(dec_ref, h, c):
    d = dec_ref[:, h:h + 1]
    return (jnp.broadcast_to(d, (c, c)), jnp.broadcast_to(d, (c, LANES)), jnp.broadcast_to(d, (LANES, LANES)))


def _ret_fwd(name, q, k, z, dec, reverse, chunk):
    s = q.shape[0]
    chunk = min(chunk, s)
    n = s // chunk
    cmap = (lambda i: (n - 1 - i, 0)) if reverse else (lambda i: (i, 0))
    smap = (lambda i: (n - 1 - i, 0, 0, 0)) if reverse else (lambda i: (i, 0, 0, 0))

    def kern(q_ref, k_ref, v_ref, dec_ref, o_ref, st_out_ref, st_ref):
        @pl.when(pl.program_id(0) == 0)
        def _():
            st_ref[...] = jnp.zeros(st_ref.shape, F32)

        for h in range(RET_HEADS):
            sl = slice(LANES * h, LANES * (h + 1))
            st = st_ref[h]
            st_out_ref[h] = st
            o, st_new = _ret_chunk(q_ref[:, sl].astype(F32), k_ref[:, sl].astype(F32), v_ref[:, sl],
                                   st, *_ret_dec(dec_ref, h, chunk), reverse)
            o_ref[:, sl] = o
            st_ref[h] = st_new

    return pl.pallas_call(
        kern, name=name, grid=(n,),
        in_specs=[pl.BlockSpec((chunk, 512), cmap), pl.BlockSpec((chunk, 512), cmap),
                  pl.BlockSpec((chunk, 512), cmap), pl.BlockSpec((1, RET_HEADS), lambda i: (0, 0))],
        out_specs=[pl.BlockSpec((chunk, 512), cmap), pl.BlockSpec((None, RET_HEADS, LANES, LANES), smap)],
        out_shape=[jax.ShapeDtypeStruct((s, 512), F32), jax.ShapeDtypeStruct((n, RET_HEADS, LANES, LANES), F32)],
        scratch_shapes=[pltpu.VMEM((RET_HEADS, LANES, LANES), F32)],
        compiler_params=_cparams(("arbitrary",)),
    )(q, k, z, dec)


def _ret_bwd(name, q, k, z, dec, states, do, reverse, chunk):
    s = q.shape[0]
    chunk = min(chunk, s)
    n = s // chunk
    cmap = (lambda i: (i, 0)) if reverse else (lambda i: (n - 1 - i, 0))
    smap = (lambda i: (i, 0, 0, 0)) if reverse else (lambda i: (n - 1 - i, 0, 0, 0))

    def kern(q_ref, k_ref, v_ref, dec_ref, st_in_ref, do_ref, dq_ref, dk_ref, dv_ref, ddec_ref, dst_ref):
        @pl.when(pl.program_id(0) == 0)
        def _():
            dst_ref[...] = jnp.zeros(dst_ref.shape, F32)
            ddec_ref[...] = jnp.zeros(ddec_ref.shape, F32)

        lane = lax.broadcasted_iota(jnp.int32, (1, LANES), 1)
        ddec = jnp.zeros((1, LANES), F32)
        for h in range(RET_HEADS):
            sl = slice(LANES * h, LANES * (h + 1))
            fn = functools.partial(_ret_chunk, reverse=reverse)
            _, vjp = jax.vjp(fn, q_ref[:, sl].astype(F32), k_ref[:, sl].astype(F32), v_ref[:, sl],
                             st_in_ref[h], *_ret_dec(dec_ref, h, chunk))
            dq, dk, dv, dst, d_cc, d_cd, d_dd = vjp((do_ref[:, sl], dst_ref[h]))
            dq_ref[:, sl] = dq
            dk_ref[:, sl] = dk
            dv_ref[:, sl] = dv
            dst_ref[h] = dst
            tot = (jnp.sum(jnp.sum(d_cc, axis=1, keepdims=True), axis=0, keepdims=True)
                   + jnp.sum(jnp.sum(d_cd, axis=1, keepdims=True), axis=0, keepdims=True)
                   + jnp.sum(jnp.sum(d_dd, axis=1, keepdims=True), axis=0, keepdims=True))
            ddec = ddec + jnp.where(lane == h, tot, 0.0)
        ddec_ref[...] += ddec

    cspec = pl.BlockSpec((chunk, 512), cmap)
    return pl.pallas_call(
        kern, name=name, grid=(n,),
        in_specs=[cspec, cspec, cspec, pl.BlockSpec((1, RET_HEADS), lambda i: (0, 0)),
                  pl.BlockSpec((None, RET_HEADS, LANES, LANES), smap), cspec],
        out_specs=[cspec, cspec, cspec, pl.BlockSpec((1, LANES), lambda i: (0, 0))],
        out_shape=[jax.ShapeDtypeStruct((s, 512), F32)] * 3 + [jax.ShapeDtypeStruct((1, LANES), F32)],
        scratch_shapes=[pltpu.VMEM((RET_HEADS, LANES, LANES), F32)],
        compiler_params=_cparams(("arbitrary",)),
    )(q, k, z, dec, states, do)


def _fa_fwd(name, q, k, vx, tq, tk, nsub, comm=None):
    s = q.shape[0]
    tq, tk = min(tq, s), min(tk, s)
    nk = s // tk
    sq = tq // nsub

    def kern(q_ref, k_ref, v_ref, o_ref, lse_ref, m_ref, acc_ref):
        j = pl.program_id(2)

        @pl.when(j == 0)
        def _():
            m_ref[...] = jnp.full(m_ref.shape, -jnp.inf, F32)
            acc_ref[...] = jnp.zeros(acc_ref.shape, F32)

        kb, vb = k_ref[...], v_ref[...]
        for c in range(nsub):
            rows = pl.ds(c * sq, sq)
            sc = _dot(q_ref[rows, :], kb, "nt")
            m_prev = m_ref[rows, :]
            m_new = jnp.maximum(m_prev, jnp.max(sc, axis=1, keepdims=True))
            alpha = jnp.exp(m_prev - m_new)
            p = jnp.exp(sc - m_new)
            acc_ref[rows, :] = alpha * acc_ref[rows, :] + _dot(p, vb, "nn")
            m_ref[rows, :] = m_new

        @pl.when(j == nk - 1)
        def _():
            den = acc_ref[:, LANES:]
            o_ref[...] = acc_ref[:, :LANES] / den
            lse_ref[...] = m_ref[...] + jnp.log(den[:, :1])

    (o, lse), got = _call(
        kern, name=name, grid=(MLA_HEADS, s // tq, nk),
        in_specs=[pl.BlockSpec((tq, 256), lambda h, i, j: (i, h)),
                  pl.BlockSpec((tk, 256), lambda h, i, j: (j, h)),
                  pl.BlockSpec((tk, 256), lambda h, i, j: (j, h))],
        out_specs=[pl.BlockSpec((tq, LANES), lambda h, i, j: (i, h)),
                   pl.BlockSpec((None, tq, 1), lambda h, i, j: (h, i, 0))],
        out_shape=[jax.ShapeDtypeStruct((s, 512), F32), jax.ShapeDtypeStruct((MLA_HEADS, s, 1), F32)],
        scratch_shapes=[pltpu.VMEM((tq, 1), F32), pltpu.VMEM((tq, 256), F32)],
        args=[q, k, vx], sem=("parallel", "parallel", "arbitrary"), comm=comm)
    return o, lse, got


def _fa_bwd(name, q, k, kt, vx, do, lse, delta, tq, tk, nsub, comm=None):
    s = q.shape[0]
    tq, tk = min(tq, s), min(tk, s)
    nq = s // tq
    sk = tk // nsub

    def kern(q_ref, k_ref, kt_ref, v_ref, do_ref, lse_ref, dl_ref, dqt_ref, dk_ref, dv_ref):
        j, i = pl.program_id(1), pl.program_id(2)

        @pl.when(i == 0)
        def _():
            dv_ref[...] = jnp.zeros(dv_ref.shape, F32)
            dk_ref[...] = jnp.zeros(dk_ref.shape, F32)

        @pl.when(j == 0)
        def _():
            dqt_ref[i] = jnp.zeros((256, tq), F32)

        qb, dob = q_ref[...], do_ref[...]
        lse_row, dl_row = lse_ref[...], dl_ref[...]
        dqt = dqt_ref[i]
        for c in range(nsub):
            rows = pl.ds(c * sk, sk)
            st = _dot(k_ref[rows, :], qb, "nt")
            pt = jnp.exp(st - lse_row)
            dpt = _dot(v_ref[rows, :], dob, "nt")
            dst = (pt * (dpt - dl_row)).astype(BF16)
            dv_ref[rows, :] += _dot(pt, dob, "nn")
            dk_ref[rows, :] += _dot(dst, qb, "nn")
            dqt = dqt + _dot(kt_ref[:, rows], dst, "nn")
        dqt_ref[i] = dqt

    outs, got = _call(
        kern, name=name, grid=(MLA_HEADS, s // tk, nq),
        in_specs=[pl.BlockSpec((tq, 256), lambda h, j, i: (i, h)),
                  pl.BlockSpec((tk, 256), lambda h, j, i: (j, h)),
                  pl.BlockSpec((None, 256, tk), lambda h, j, i: (h, 0, j)),
                  pl.BlockSpec((tk, LANES), lambda h, j, i: (j, 2 * h)),
                  pl.BlockSpec((tq, LANES), lambda h, j, i: (i, h)),
                  pl.BlockSpec((None, 1, tq), lambda h, j, i: (h, 0, i)),
                  pl.BlockSpec((None, 1, tq), lambda h, j, i: (h, 0, i))],
        out_specs=[pl.BlockSpec((None, nq, 256, tq), lambda h, j, i: (h, 0, 0, 0)),
                   pl.BlockSpec((tk, 256), lambda h, j, i: (j, h)),
                   pl.BlockSpec((tk, LANES), lambda h, j, i: (j, h))],
        out_shape=[jax.ShapeDtypeStruct((MLA_HEADS, nq, 256, tq), F32),
                   jax.ShapeDtypeStruct((s, 1024), F32), jax.ShapeDtypeStruct((s, 512), F32)],
        args=[q, k, kt, vx, do, lse, delta], sem=("parallel", "arbitrary", "arbitrary"), comm=comm)
    return outs[0], outs[1], outs[2], got


def _fill_padded(dst_ref, src_ref, s):
    zeros = jnp.zeros((CONV_HALO, LANES), F32)
    dst_ref[pl.ds(0, CONV_HALO), :] = zeros
    dst_ref[pl.ds(CONV_HALO + s, CONV_HALO), :] = zeros
    dst_ref[pl.ds(CONV_HALO, s), :] = src_ref[...]


def _shifted_windows(win):
    n = win.shape[0]
    return [win] + [pltpu.roll(win, n - b, 0) for b in range(1, 8)]


def _conv_fwd(name, u, w, bias, rc=256):
    s = u.shape[0]
    rc = min(rc, s)

    def kern(u_ref, w_ref, b_ref, o_ref, pad_ref):
        _fill_padded(pad_ref, u_ref, s)
        wv = w_ref[...]
        bv = b_ref[...]

        def chunk(r, carry):
            base = pl.multiple_of(r * rc, rc)
            wins = _shifted_windows(pad_ref[pl.ds(base, rc + 2 * CONV_HALO), :])
            acc = jnp.broadcast_to(bv, (rc, LANES))
            for kk in range(CONV_K):
                a, b = divmod(kk + 1, 8)
                acc = acc + wv[kk:kk + 1, :] * wins[b][8 * a:8 * a + rc]
            o_ref[pl.ds(base, rc), :] = acc
            return carry

        lax.fori_loop(0, s // rc, chunk, 0)

    return pl.pallas_call(
        kern, name=name, grid=(D // LANES,),
        in_specs=[pl.BlockSpec((s, LANES), lambda c: (0, c)), pl.BlockSpec((32, LANES), lambda c: (0, c)),
                  pl.BlockSpec((1, LANES), lambda c: (0, c))],
        out_specs=pl.BlockSpec((s, LANES), lambda c: (0, c)),
        out_shape=jax.ShapeDtypeStruct((s, D), F32),
        scratch_shapes=[pltpu.VMEM((s + 2 * CONV_HALO, LANES), F32)],
        compiler_params=_cparams(("parallel",)),
    )(u, w, bias)


def _conv_bwd(name, u, g, w, rc=256, comm=None):
    s = u.shape[0]
    rc = min(rc, s)

    def kern(u_ref, g_ref, w_ref, du_ref, dw_ref, db_ref, upad_ref, gpad_ref, dwacc_ref):
        _fill_padded(upad_ref, u_ref, s)
        _fill_padded(gpad_ref, g_ref, s)
        dwacc_ref[...] = jnp.zeros(dwacc_ref.shape, F32)
        wv = w_ref[...]

        def chunk(r, carry):
            base = pl.multiple_of(r * rc, rc)
            gwins = _shifted_windows(gpad_ref[pl.ds(base, rc + 2 * CONV_HALO), :])
            uwins = _shifted_windows(upad_ref[pl.ds(base, rc + 2 * CONV_HALO), :])
            gc = g_ref[pl.ds(base, rc), :]
            acc = jnp.zeros((rc, LANES), F32)
            for kk in range(CONV_K):
                a, b = divmod(CONV_K - kk, 8)
                acc = acc + wv[kk:kk + 1, :] * gwins[b][8 * a:8 * a + rc]
                a, b = divmod(kk + 1, 8)
                prod = gc * uwins[b][8 * a:8 * a + rc]
                dwacc_ref[kk] += jnp.sum(prod.reshape(rc // 8, 8, LANES), axis=0)
            dwacc_ref[CONV_K] += jnp.sum(gc.reshape(rc // 8, 8, LANES), axis=0)
            du_ref[pl.ds(base, rc), :] = acc
            return carry

        lax.fori_loop(0, s // rc, chunk, 0)
        tot = jnp.sum(dwacc_ref[...], axis=1)
        lane_row = lax.broadcasted_iota(jnp.int32, (32, LANES), 0)
        dw_ref[...] = jnp.where(lane_row < CONV_K, tot, 0.0)
        db_ref[...] = tot[CONV_K:CONV_K + 1, :]

    cs = pl.BlockSpec((s, LANES), lambda c: (0, c))
    outs, got = _call(
        kern, name=name, grid=(D // LANES,),
        in_specs=[cs, cs, pl.BlockSpec((32, LANES), lambda c: (0, c))],
        out_specs=[cs, pl.BlockSpec((32, LANES), lambda c: (0, c)), pl.BlockSpec((1, LANES), lambda c: (0, c))],
        out_shape=[jax.ShapeDtypeStruct((s, D), F32), jax.ShapeDtypeStruct((32, D), F32),
                   jax.ShapeDtypeStruct((1, D), F32)],
        scratch_shapes=[pltpu.VMEM((s + 2 * CONV_HALO, LANES), F32), pltpu.VMEM((s + 2 * CONV_HALO, LANES), F32),
                        pltpu.VMEM((32, 8, LANES), F32)],
        args=[u, g, w], sem=("parallel",), comm=comm)
    return outs[0], outs[1], outs[2], got


def _mod_local(name, c_all, ada_w):
    def kern(c_ref, w_ref, o_ref):
        o_ref[...] = jnp.dot(_silu(c_ref[...]), w_ref[...], preferred_element_type=F32,
                             precision=lax.Precision.HIGHEST)

    return pl.pallas_call(
        kern, name=name, grid=(DEPTH,),
        in_specs=[pl.BlockSpec((N_DEV, D), lambda l: (0, 0)), pl.BlockSpec((None, D, 384), lambda l: (l, 0, 0))],
        out_specs=pl.BlockSpec((None, N_DEV, 384), lambda l: (l, 0, 0)),
        out_shape=jax.ShapeDtypeStruct((DEPTH, N_DEV, 384), F32),
        compiler_params=_cparams(("parallel",)),
    )(c_all, ada_w)


def _ada_w_grad(name, c_all_t, dmod):
    def kern(c_ref, d_ref, o_ref):
        o_ref[...] = jnp.dot(_silu(c_ref[...]), d_ref[...], preferred_element_type=F32,
                             precision=lax.Precision.HIGHEST)

    return pl.pallas_call(
        kern, name=name, grid=(DEPTH,),
        in_specs=[pl.BlockSpec((D, LANES), lambda l: (0, 0)), pl.BlockSpec((None, LANES, 384), lambda l: (l, 0, 0))],
        out_specs=pl.BlockSpec((None, D, 384), lambda l: (l, 0, 0)),
        out_shape=jax.ShapeDtypeStruct((DEPTH, D, 384), F32),
        compiler_params=_cparams(("parallel",)),
    )(c_all_t, dmod)


def _pre_fn(x, g, scale, shift):
    return _rms(x, g) * (1.0 + scale) + shift


def _post_fn(y, g, gate):
    return gate * _rms(y, g)


def _ev_post_fn(o_heads, rg, a, mg):
    normed = []
    for oh in o_heads:
        mu = jnp.mean(oh, axis=-1, keepdims=True)
        var = jnp.mean(jnp.square(oh - mu), axis=-1, keepdims=True)
        normed.append((oh - mu) * lax.rsqrt(var + EPS))
    return jnp.concatenate([jnp.concatenate(normed, axis=1) * _silu(rg), a * _silu(mg)], axis=1)


def _od_post_fn(u, g, ln_g, ln_b):
    mu = jnp.mean(u, axis=-1, keepdims=True)
    var = jnp.mean(jnp.square(u - mu), axis=-1, keepdims=True)
    y = (u - mu) * lax.rsqrt(var + EPS) * ln_g + ln_b
    return _silu(y) * _silu(g)


def _glu_fn(a, b):
    return a * _sigmoid(b)


def _heads(x):
    return [x[:, LANES * h:LANES * (h + 1)] for h in range(4)]


def _colsum(x):
    return jnp.sum(x, axis=0, keepdims=True)


def _zrows(n, c):
    return jnp.zeros((n, c), BF16)


def _ev_win_layout(wt):
    rq = [p for h in range(4) for p in (wt[64 * h:64 * h + 64], _zrows(64, D))]
    rk = [p for h in range(4) for p in (wt[256 + 64 * h:256 + 64 * h + 64], _zrows(64, D))]
    return jnp.concatenate([wt[512:1024], wt[1024:1536], wt[2240:2752], wt[1536:1920],
                            wt[2176:2240], _zrows(64, D)] + rq + rk + [wt[1920:2176]], axis=0)


def _ev_win_unlayout(g):
    rq = [g[2048 + 128 * h:2048 + 128 * h + 64] for h in range(4)]
    rk = [g[2560 + 128 * h:2560 + 128 * h + 64] for h in range(4)]
    return jnp.concatenate(rq + rk + [g[0:512], g[512:1024], g[1536:1920], g[3072:3328],
                                      g[1920:1984], g[1024:1536]], axis=0)


def _uq_layout(wt):
    return jnp.concatenate([p for h in range(4) for p in (wt[192 * h:192 * h + 192], _zrows(64, 384))], axis=0)


def _uq_unlayout(g):
    return jnp.concatenate([g[256 * h:256 * h + 192] for h in range(4)], axis=0)


def _ukv_layout(wt):
    kpart = [p for h in range(4) for p in (wt[256 * h:256 * h + 128], _zrows(128, 256))]
    vpart = [wt[256 * h + 128:256 * h + 256] for h in range(4)]
    return jnp.concatenate(kpart + vpart, axis=0)


def _ukv_unlayout(g):
    return jnp.concatenate([p for h in range(4) for p in (g[256 * h:256 * h + 128], g[1024 + 128 * h:1024 + 128 * h + 128])],
                           axis=0)


def kernel(x, c, positions, ada_w, ada_b, pre_g, post_g, ev_w_in, ev_dec_f, ev_dec_b, ev_q_norm_g, ev_w_uq, ev_kv_norm_g, ev_w_ukv, ev_w_out, od_w_in, od_b_in, od_dw_w, od_dw_b, od_ln_g, od_ln_b, od_w_out, loss_target, m_ada_w, m_ada_b, m_pre_g, m_post_g, m_ev_w_in, m_ev_dec_f, m_ev_dec_b, m_ev_q_norm_g, m_ev_w_uq, m_ev_kv_norm_g, m_ev_w_ukv, m_ev_w_out, m_od_w_in, m_od_b_in, m_od_dw_w, m_od_dw_b, m_od_ln_g, m_od_ln_b, m_od_w_out, v_ada_w, v_ada_b, v_pre_g, v_post_g, v_ev_w_in, v_ev_dec_f, v_ev_dec_b, v_ev_q_norm_g, v_ev_w_uq, v_ev_kv_norm_g, v_ev_w_ukv, v_ev_w_out, v_od_w_in, v_od_b_in, v_od_dw_w, v_od_dw_b, v_od_ln_g, v_od_ln_b, v_od_w_out):
    s = x.shape[1]
    me = 4 * lax.axis_index("x") + 2 * lax.axis_index("y") + lax.axis_index("c")
    x0 = x.reshape(s, D)
    tgt = loss_target.reshape(s, D)
    ret_chunk = 256
    fa_cfg_f = ((min(1024, s // 2), min(1024, s // 2), 2),) * 2
    fa_cfg_b = ((min(512, s // 2), min(2048, s // 2), 4),) * 2

    start_parts = [c.reshape(-1), od_b_in.reshape(-1), od_dw_w.reshape(-1), od_dw_b.reshape(-1),
                   od_ln_g.reshape(-1), od_ln_b.reshape(-1)]
    start_sizes = [p.shape[0] for p in start_parts]
    start_len = -(-sum(start_sizes) // 1024) * 1024
    start_vec = jnp.concatenate(start_parts + [jnp.zeros((start_len - sum(start_sizes),), F32)])
    start_all = _exchange("gather_start", [start_vec.reshape(-1, LANES)], False)[0].reshape(N_DEV, start_len)
    offs = np.cumsum([0] + start_sizes)
    c_all = start_all[:, offs[0]:offs[1]]
    b_in_all = start_all[:, offs[1]:offs[2]].reshape(N_DEV, 2, 384).transpose(1, 0, 2).reshape(2, 1, ZW_OD)
    dw_w_all = start_all[:, offs[2]:offs[3]].reshape(N_DEV, 2, CONV_K, LANES).transpose(1, 2, 0, 3).reshape(2, CONV_K, D)
    dw_w_all = jnp.concatenate([dw_w_all, jnp.zeros((2, 1, D), F32)], axis=1)
    dw_b_all = start_all[:, offs[3]:offs[4]].reshape(N_DEV, 2, LANES).transpose(1, 0, 2).reshape(2, 1, D)
    ln_g_all = start_all[:, offs[4]:offs[5]].reshape(N_DEV, 2, LANES).transpose(1, 0, 2).reshape(2, 1, D)
    ln_b_all = start_all[:, offs[5]:offs[6]].reshape(N_DEV, 2, LANES).transpose(1, 0, 2).reshape(2, 1, D)

    mod_loc = _mod_local("mod_local", c_all, ada_w)
    mod_all = _exchange("gather_mod", [mod_loc.reshape(DEPTH * N_DEV, 384)], False)[0].reshape(N_DEV, DEPTH, N_DEV, 384)
    mod = lax.dynamic_index_in_dim(mod_all, me, axis=2, keepdims=False)
    mod = mod.transpose(1, 0, 2).reshape(DEPTH, 3 * D) + ada_b
    shift = [mod[l:l + 1, 0:D] for l in range(DEPTH)]
    scale = [mod[l:l + 1, D:2 * D] for l in range(DEPTH)]
    gate = [mod[l:l + 1, 2 * D:3 * D] for l in range(DEPTH)]

    def ev_shards(i):
        return [ev_w_in[i].T.astype(BF16), ev_w_uq[i].T.astype(BF16), ev_w_ukv[i].T.astype(BF16), ev_w_out[i].astype(BF16)]

    def od_shards(i):
        return [od_w_in[i].T.astype(BF16), od_w_out[i].astype(BF16)]

    def full(got):
        return [g.reshape(N_DEV * g.shape[1], g.shape[2]) for g in got]

    def ev_full(got):
        win_t, uq_t, ukv_t, wout = full(got)
        return (_ev_win_layout(win_t), _uq_layout(uq_t), _ukv_layout(ukv_t), wout)

    ev_w = [ev_full(_exchange("gather_w_ev0", ev_shards(0), False)), None]
    od_w = [None, None]
    later_w = [(t, False) for t in od_shards(0) + ev_shards(1) + od_shards(1)]

    inv_freq = ROPE_BASE ** (-jnp.arange(0, 64, 2, dtype=F32) / 64)
    invf = jnp.tile(inv_freq, 4).reshape(1, LANES)
    sgn = jnp.tile(jnp.concatenate([-jnp.ones((32,), F32), jnp.ones((32,), F32)]), 2).reshape(1, LANES)

    def rope_body(rows, vecs):
        ang = rows[0].astype(F32) * vecs[0]
        return [jnp.cos(ang), jnp.sin(ang) * vecs[1]], []

    (cos_t, sin_t), _ = _rowwise("rope_tables", rope_body, [(positions.reshape(s, 1), 1, 0)], [invf, sgn],
                                 [(LANES, F32), (LANES, F32)])

    saved = []
    xl = x0
    for l in range(DEPTH):
        i = l // 2
        sv = dict(x=xl)

        def pre_body(rows, vecs):
            return [_pre_fn(rows[0], *vecs)], []

        (h,), _ = _rowwise(f"pre{l}", pre_body, [(xl, D, 0)], [pre_g[l:l + 1], scale[l], shift[l]], [(D, BF16)])
        sv["h"] = h
        if l % 2 == 0:
            win_t, uq_t, ukv_t, wout = ev_w[i]
            z = _mm(f"ev_in{l}", h, win_t, "nt")
            sv["z"] = z
            dec_f, dec_b = ev_dec_f[i:i + 1], ev_dec_b[i:i + 1]

            def prep_body(rows, vecs):
                rq, rk, cq, ckv, kr, cos, sin = rows
                return [_rope(rq, cos, sin), _rope(rk, cos, sin) * RET_SCALE, _rms(cq, vecs[0]), _rms(ckv, vecs[1]),
                        _rope(kr, cos, sin)], []

            (rq_r, rk_r, qn, kvn, krr), _ = _rowwise(
                f"ev_prep{l}", prep_body,
                [(z, 512, 4), (z, 512, 5), (z, 384, 4), (z, 256, 12), (z, LANES, 15), (cos_t, LANES, 0), (sin_t, LANES, 0)],
                [ev_q_norm_g[i:i + 1], ev_kv_norm_g[i:i + 1]],
                [(512, BF16), (512, BF16), (384, BF16), (256, BF16), (LANES, BF16)])
            sv.update(rq_r=rq_r, rk_r=rk_r, qn=qn, kvn=kvn)
            o_f, st_f = _ret_fwd(f"ret_f{l}", rq_r, rk_r, z, dec_f, False, ret_chunk)
            o_b, st_b = _ret_fwd(f"ret_b{l}", rq_r, rk_r, z, dec_b, True, ret_chunk)
            sv.update(o_f=o_f, o_b=o_b, st_f=st_f, st_b=st_b)
            q_pad = _mm(f"ev_uq{l}", qn, uq_t, "nt")
            kv_pad = _mm(f"ev_ukv{l}", kvn, ukv_t, "nt")

            def mla_prep_body(rows, vecs):
                qp, kk, vv, kr_r, cos, sin = rows
                qs, ks, vs = [], [], []
                ones = jnp.ones((qp.shape[0], LANES), F32)
                for hh in range(4):
                    qs += [qp[:, 256 * hh:256 * hh + 128], _rope(qp[:, 256 * hh + 128:256 * hh + 256], cos, sin)]
                    ks += [kk[:, 256 * hh:256 * hh + 128].astype(BF16), kr_r]
                    vs += [vv[:, LANES * hh:LANES * hh + LANES], ones]
                return [jnp.concatenate(qs, axis=1) * MLA_SCALE, jnp.concatenate(ks, axis=1),
                        jnp.concatenate(vs, axis=1)], []

            (qcat, kcat, v_x), _ = _rowwise(
                f"mla_prep{l}", mla_prep_body,
                [(q_pad, 1024, 0), (kv_pad, 1024, 0), (kv_pad, 512, 2), (krr, LANES, 0), (cos_t, LANES, 0), (sin_t, LANES, 0)],
                [], [(1024, BF16), (1024, BF16), (1024, BF16)])
            a_mla, lse, got = _fa_fwd(f"fa_fwd{l}", qcat, kcat, v_x, *fa_cfg_f[i], comm=later_w if l == 0 else None)
            if l == 0:
                od_w[0], ev_w[1], od_w[1] = tuple(full(got[0:2])), ev_full(got[2:6]), tuple(full(got[6:8]))
            sv.update(qcat=qcat, kcat=kcat, v_x=v_x, a_mla=a_mla, lse=lse)

            def ev_post_body(rows, vecs):
                of, ob, rg, a, mg = rows
                return [_ev_post_fn(_heads(of + ob), rg, a, mg)], []

            (act,), _ = _rowwise(f"ev_post{l}", ev_post_body,
                                 [(o_f, 512, 0), (o_b, 512, 0), (z, 512, 1), (a_mla, 512, 0), (z, 512, 2)], [], [(D, BF16)])
        else:
            win_t, wout = od_w[i]
            z = _mm(f"od_in{l}", h, win_t, "nt", bias=b_in_all[i])
            sv["z"] = z

            def glu_body(rows, vecs):
                return [_glu_fn(rows[0], rows[1])], []

            (u,), _ = _rowwise(f"glu{l}", glu_body, [(z, D, 0), (z, D, 1)], [], [(D, F32)])
            u2 = _conv_fwd(f"conv{l}", u, dw_w_all[i], dw_b_all[i])
            sv.update(u=u, u2=u2)

            def od_post_body(rows, vecs):
                return [_od_post_fn(rows[0], rows[1], vecs[0], vecs[1])], []

            (act,), _ = _rowwise(f"od_post{l}", od_post_body, [(u2, D, 0), (z, D, 2)], [ln_g_all[i], ln_b_all[i]],
                                 [(D, BF16)])
        sv["act"] = act
        y = _mm(f"out{l}", act, wout, "nn")
        sv["y"] = y

        def post_body(rows, vecs):
            return [rows[0] + _post_fn(rows[1], vecs[0], vecs[1])], []

        (xl,), _ = _rowwise(f"post{l}", post_body, [(xl, D, 0), (y, D, 0)], [post_g[l:l + 1], gate[l]], [(D, F32)])
        saved.append(sv)

    def loss_body(rows, vecs):
        diff = rows[0] - rows[1]
        return [diff * (1.0 / D)], [_colsum(diff * diff) * (0.5 / D)]

    (dx,), (loss_lanes,) = _rowwise("loss", loss_body, [(xl, D, 0), (tgt, D, 0)], [], [(D, F32)], [(1, D)])
    loss = lax.psum(jnp.sum(loss_lanes), ("x", "y", "c"))

    g_pre, g_post, g_mod = [None] * DEPTH, [None] * DEPTH, [None] * DEPTH
    g_dec_f, g_dec_b, g_qn, g_kvn = [None] * 2, [None] * 2, [None] * 2, [None] * 2
    g_b_in, g_dw_w, g_dw_b, g_ln_g, g_ln_b = [None] * 2, [None] * 2, [None] * 2, [None] * 2, [None] * 2
    recv_ev, recv_od = [None] * 2, [None] * 2
    pending = []
    for l in reversed(range(DEPTH)):
        i = l // 2
        sv = saved[l]

        def post_bwd_body(rows, vecs):
            yv, dxn = rows
            r, vjp = jax.vjp(_post_fn, yv, vecs[0], vecs[1])
            dy, dg, dgate = vjp(dxn)
            return [dy], [dg, dgate]

        (dy,), (dpost, dgate) = _rowwise(f"post_bwd{l}", post_bwd_body, [(sv["y"], D, 0), (dx, D, 0)],
                                         [post_g[l:l + 1], gate[l]], [(D, BF16)], [(1, D), (1, D)])
        g_post[l] = dpost
        wout = ev_w[i][3] if l % 2 == 0 else od_w[i][1]
        dact = _mm(f"out_dgrad{l}", dy, wout, "nt")
        d_wout = _mm(f"out_wgrad{l}", sv["act"], dy, "tn", out_dtype=BF16)
        z = sv["z"]
        if l % 2 == 0:
            win_t, uq_t, ukv_t, _ = ev_w[i]
            dec_f, dec_b = ev_dec_f[i:i + 1], ev_dec_b[i:i + 1]

            def ev_post_bwd_body(rows, vecs):
                of, ob, rg, a, mg, da = rows
                _, vjp = jax.vjp(_ev_post_fn, _heads(of + ob), rg, a, mg)
                do_heads, drg, d_a, dmg = vjp(da)
                deltas = [jnp.sum(dh_ * ah_, axis=1, keepdims=True) for dh_, ah_ in zip(_heads(d_a), _heads(a))]
                return [jnp.concatenate(do_heads, axis=1), drg, d_a, dmg] + deltas, []

            (do_ret, drg, do_mla, dmg, dl0, dl1, dl2, dl3), _ = _rowwise(
                f"ev_post_bwd{l}", ev_post_bwd_body,
                [(sv["o_f"], 512, 0), (sv["o_b"], 512, 0), (z, 512, 1), (sv["a_mla"], 512, 0), (z, 512, 2), (dact, D, 0)],
                [], [(512, F32), (512, BF16), (512, BF16), (512, BF16)] + [(1, F32)] * 4)
            delta = jnp.stack([dl0, dl1, dl2, dl3]).reshape(MLA_HEADS, 1, s)
            lse = sv["lse"].reshape(MLA_HEADS, 1, s)
            kt = sv["kcat"].reshape(s, MLA_HEADS, 256).transpose(1, 2, 0)
            if l == 0:
                pending = pending + [(d_wout.reshape(N_DEV, 128, D), True)]
            dqt, dkcat, dv, got = _fa_bwd(f"fa_bwd{l}", sv["qcat"], sv["kcat"], kt, sv["v_x"], do_mla, lse, delta,
                                          *fa_cfg_b[i], comm=pending)
            recv_od[i] = got[0:2]
            wout_recv = got[2:]
            dqcat = dqt.transpose(1, 3, 0, 2).reshape(s, 1024)

            def mla_prep_bwd_body(rows, vecs):
                dq, dk, dvv, cos, sin = rows
                qs = []
                dkrr = jnp.zeros((dq.shape[0], LANES), F32)
                for hh in range(4):
                    qs += [dq[:, 256 * hh:256 * hh + 128], _rope_t(dq[:, 256 * hh + 128:256 * hh + 256], cos, sin)]
                    dkrr = dkrr + dk[:, 256 * hh + 128:256 * hh + 256]
                return [jnp.concatenate(qs, axis=1) * MLA_SCALE, jnp.concatenate([dk, dvv], axis=1), dkrr], []

            (dq_pad, dkv_pad, dkrr), _ = _rowwise(
                f"mla_prep_bwd{l}", mla_prep_bwd_body,
                [(dqcat, 1024, 0), (dkcat, 1024, 0), (dv, 512, 0), (cos_t, LANES, 0), (sin_t, LANES, 0)], [],
                [(1024, BF16), (1536, BF16), (LANES, F32)])
            dqn = _mm(f"uq_dgrad{l}", dq_pad, uq_t, "nn")
            d_uq = _mm(f"uq_wgrad{l}", dq_pad, sv["qn"], "tn", out_dtype=BF16)
            dkvn = _mm(f"ukv_dgrad{l}", dkv_pad, ukv_t, "nn")
            d_ukv = _mm(f"ukv_wgrad{l}", dkv_pad, sv["kvn"], "tn", out_dtype=BF16)
            dq_f, dk_f, dv_f, ddec_f = _ret_bwd(f"ret_f_bwd{l}", sv["rq_r"], sv["rk_r"], z, dec_f, sv["st_f"], do_ret,
                                                False, ret_chunk)
            dq_b, dk_b, dv_b, ddec_b = _ret_bwd(f"ret_b_bwd{l}", sv["rq_r"], sv["rk_r"], z, dec_b, sv["st_b"], do_ret,
                                                True, ret_chunk)
            g_dec_f[i], g_dec_b[i] = ddec_f[:, :RET_HEADS], ddec_b[:, :RET_HEADS]

            def prep_bwd_body(rows, vecs):
                cq, ckv, cos, sin, dqf, dqb, dkf, dkb, dvf, dvb, d_qn, d_kvn, d_krr = rows
                _, vjp_q = jax.vjp(_rms, cq, vecs[0])
                dcq, dgq = vjp_q(d_qn)
                _, vjp_kv = jax.vjp(_rms, ckv, vecs[1])
                dckv, dgkv = vjp_kv(d_kvn)
                return [dvf + dvb, dcq, _rope_t(d_krr, cos, sin), _rope_t(dqf + dqb, cos, sin),
                        _rope_t(dkf + dkb, cos, sin) * RET_SCALE, dckv], [dgq, dgkv]

            (drv, dcq, dkr, drq, drk, dckv), (dgq, dgkv) = _rowwise(
                f"ev_prep_bwd{l}", prep_bwd_body,
                [(z, 384, 4), (z, 256, 12), (cos_t, LANES, 0), (sin_t, LANES, 0), (dq_f, 512, 0), (dq_b, 512, 0),
                 (dk_f, 512, 0), (dk_b, 512, 0), (dv_f, 512, 0), (dv_b, 512, 0), (dqn, 384, 0), (dkvn, 256, 0),
                 (dkrr, LANES, 0)],
                [ev_q_norm_g[i:i + 1], ev_kv_norm_g[i:i + 1]],
                [(512, BF16), (384, BF16), (LANES, BF16), (512, BF16), (512, BF16), (256, BF16)], [(1, 384), (1, 256)])
            g_qn[i], g_kvn[i] = dgq, dgkv
            dz = jnp.concatenate([drv, drg, dmg, dcq, dkr, drq, drk, dckv], axis=1)
            d_win = _mm(f"in_wgrad{l}", dz, sv["h"], "tn", out_dtype=BF16)
            pending = [(_ev_win_unlayout(d_win).reshape(N_DEV, 344, D), True), (_uq_unlayout(d_uq).reshape(N_DEV, 96, 384), True),
                       (_ukv_unlayout(d_ukv).reshape(N_DEV, 128, 256), True)]
            if l > 0:
                pending.append((d_wout.reshape(N_DEV, 128, D), True))
        else:
            win_t, _ = od_w[i]

            def od_post_bwd_body(rows, vecs):
                u2, gg, da = rows
                _, vjp = jax.vjp(_od_post_fn, u2, gg, vecs[0], vecs[1])
                du2, dgg, dlg, dlb = vjp(da)
                return [du2, dgg], [dlg, dlb, _colsum(dgg)]

            (du2, dg_gate), (dlg, dlb, dbg) = _rowwise(
                f"od_post_bwd{l}", od_post_bwd_body, [(sv["u2"], D, 0), (z, D, 2), (dact, D, 0)],
                [ln_g_all[i], ln_b_all[i]], [(D, F32), (D, BF16)], [(1, D), (1, D), (1, D)])
            g_ln_g[i], g_ln_b[i] = dlg, dlb
            du, d_dw, d_dwb, got = _conv_bwd(f"conv_bwd{l}", sv["u"], du2, dw_w_all[i], comm=pending)
            if pending:
                recv_ev[i + 1] = got
            g_dw_w[i], g_dw_b[i] = d_dw[:CONV_K], d_dwb

            def glu_bwd_body(rows, vecs):
                a, b, d_u = rows
                _, vjp = jax.vjp(_glu_fn, a, b)
                d_a, d_b = vjp(d_u)
                return [d_a, d_b], [_colsum(d_a), _colsum(d_b)]

            (d_a, d_b), (dba, dbb) = _rowwise(f"glu_bwd{l}", glu_bwd_body, [(z, D, 0), (z, D, 1), (du, D, 0)], [],
                                              [(D, BF16), (D, BF16)], [(1, D), (1, D)])
            g_b_in[i] = jnp.concatenate([dba, dbb, dbg], axis=1)
            dz = jnp.concatenate([d_a, d_b, dg_gate], axis=1)
            d_win = _mm(f"in_wgrad{l}", dz, sv["h"], "tn", out_dtype=BF16)
            pending = [(d_win.reshape(N_DEV, 384, D), True), (d_wout.reshape(N_DEV, 128, D), True)]
        if l == 0:
            dh, got = _mm(f"in_dgrad{l}", dz, win_t, "nn", comm=pending)
            recv_ev[0] = list(got) + list(wout_recv)
        else:
            dh = _mm(f"in_dgrad{l}", dz, win_t, "nn")

        def pre_bwd_body(rows, vecs):
            xv, d_h, dxn = rows
            _, vjp = jax.vjp(_pre_fn, xv, *vecs)
            d_x, dg, dsc, dsh = vjp(d_h)
            return [d_x + dxn], [dg, dsc, dsh]

        (dx,), (dpre, dscale, dshift) = _rowwise(f"pre_bwd{l}", pre_bwd_body, [(sv["x"], D, 0), (dh, D, 0), (dx, D, 0)],
                                                 [pre_g[l:l + 1], scale[l], shift[l]], [(D, F32)], [(1, D)] * 3)
        g_pre[l] = dpre
        g_mod[l] = jnp.concatenate([dshift, dscale, dgate], axis=1)

    grad_x = dx.reshape(1, s, D)

    end_parts = [jnp.concatenate(g_mod, axis=0), jnp.concatenate(g_pre, axis=0), jnp.concatenate(g_post, axis=0),
                 jnp.concatenate(g_dec_f, axis=0), jnp.concatenate(g_dec_b, axis=0), jnp.concatenate(g_qn, axis=0),
                 jnp.concatenate(g_kvn, axis=0), jnp.concatenate(g_b_in, axis=0), jnp.stack(g_dw_w),
                 jnp.concatenate(g_dw_b, axis=0), jnp.concatenate(g_ln_g, axis=0), jnp.concatenate(g_ln_b, axis=0)]
    end_shapes = [p.shape for p in end_parts]
    end_sizes = [int(np.prod(sh)) for sh in end_shapes]
    end_len = -(-sum(end_sizes) // 1024) * 1024
    end_vec = jnp.concatenate([p.reshape(-1) for p in end_parts] + [jnp.zeros((end_len - sum(end_sizes),), F32)])
    end_all = _exchange("gather_end", [end_vec.reshape(-1, LANES)], False)[0].reshape(N_DEV, end_len)
    eo = np.cumsum([0] + end_sizes)
    ends = [end_all[:, eo[j]:eo[j + 1]].reshape((N_DEV,) + tuple(end_shapes[j])) for j in range(len(end_parts))]
    (dmod_all, pre_all, post_all, decf_all, decb_all, qn_all, kvn_all, bin_all, dww_all, dwb_all, lng_all,
     lnb_all) = ends

    def pack_rep(*ts):
        lead = ts[0].ndim - 2
        return jnp.concatenate([t.reshape(t.shape[:lead] + (-1,)) for t in ts], axis=-1)

    rep_sizes = [DEPTH * 3 * D, DEPTH * D, DEPTH * D, 8, 8, 2 * 384, 2 * 256]
    rep_len = -(-sum(rep_sizes) // 1024) * 1024
    rep_pad = rep_len - sum(rep_sizes)

    def rep_rows(flat):
        padz = jnp.zeros(flat.shape[:-1] + (rep_pad,), F32)
        return jnp.concatenate([flat, padz], axis=-1).reshape(flat.shape[:-1] + (rep_len // LANES, LANES))

    rep_w = rep_rows(pack_rep(ada_b, pre_g, post_g, ev_dec_f, ev_dec_b, ev_q_norm_g, ev_kv_norm_g))
    rep_m = rep_rows(pack_rep(m_ada_b, m_pre_g, m_post_g, m_ev_dec_f, m_ev_dec_b, m_ev_q_norm_g, m_ev_kv_norm_g))
    rep_v = rep_rows(pack_rep(v_ada_b, v_pre_g, v_post_g, v_ev_dec_f, v_ev_dec_b, v_ev_q_norm_g, v_ev_kv_norm_g))
    rep_g = rep_rows(pack_rep(dmod_all, pre_all, post_all, decf_all, decb_all, qn_all, kvn_all))
    rep_out = _adamw("adamw_rep", rep_w, rep_m, rep_v, rep_g)
    ro = np.cumsum([0] + rep_sizes)
    rep_shapes = [(DEPTH, 3 * D), (DEPTH, D), (DEPTH, D), (2, 4), (2, 4), (2, 384), (2, 256)]

    def unpack_rep(t):
        flat = t.reshape(-1)
        return [flat[ro[j]:ro[j + 1]].reshape(rep_shapes[j]) for j in range(len(rep_shapes))]

    rep_res = [unpack_rep(t) for t in rep_out]

    def my_cols(t, width):
        return lax.dynamic_slice_in_dim(t, me * width, width, axis=t.ndim - 1)

    def vec_adamw(name, w, m, v, g_all, width):
        g = my_cols(g_all, width)
        r = _adamw(name, w.reshape(-1, width), m.reshape(-1, width), v.reshape(-1, width),
                   g.reshape(N_DEV, -1, width))
        return [t.reshape(w.shape) for t in r]

    res_b_in = vec_adamw("adamw_b_in", od_b_in, m_od_b_in, v_od_b_in, bin_all, 384)
    res_dw_w = vec_adamw("adamw_dw_w", od_dw_w, m_od_dw_w, v_od_dw_w, dww_all, LANES)
    res_dw_b = vec_adamw("adamw_dw_b", od_dw_b, m_od_dw_b, v_od_dw_b, dwb_all, LANES)
    res_ln_g = vec_adamw("adamw_ln_g", od_ln_g, m_od_ln_g, v_od_ln_g, lng_all, LANES)
    res_ln_b = vec_adamw("adamw_ln_b", od_ln_b, m_od_ln_b, v_od_ln_b, lnb_all, LANES)

    dmod_mine = my_cols(dmod_all, 384).transpose(1, 0, 2)
    dmod_pad = jnp.concatenate([dmod_mine, jnp.zeros((DEPTH, LANES - N_DEV, 384), F32)], axis=1)
    c_all_t = jnp.concatenate([c_all.T, jnp.zeros((D, LANES - N_DEV), F32)], axis=1)
    g_ada_w = _ada_w_grad("ada_w_grad", c_all_t, dmod_pad)
    res_ada_w = [t.reshape(ada_w.shape) for t in
                 _adamw("adamw_ada_w", ada_w.reshape(-1, 384), m_ada_w.reshape(-1, 384), v_ada_w.reshape(-1, 384),
                        g_ada_w.reshape(-1, 384))]

    ev_sh = [[_sum_parts(f"sum_g_ev{i}_{j}", r) for j, r in enumerate(recv_ev[i])] for i in range(2)]
    od_sh = [[_sum_parts(f"sum_g_od{i}_{j}", r) for j, r in enumerate(recv_od[i])] for i in range(2)]

    def mat_adamw(name, w, m, v, g):
        r = _adamw(name, w.reshape(-1, w.shape[-1]), m.reshape(-1, w.shape[-1]), v.reshape(-1, w.shape[-1]),
                   g.reshape(-1, w.shape[-1]))
        return [t.reshape(w.shape) for t in r]

    res_ev_w_in = mat_adamw("adamw_ev_w_in", ev_w_in, m_ev_w_in, v_ev_w_in, jnp.stack([ev_sh[i][0].T for i in range(2)]))
    res_ev_w_uq = mat_adamw("adamw_ev_w_uq", ev_w_uq, m_ev_w_uq, v_ev_w_uq, jnp.stack([ev_sh[i][1].T for i in range(2)]))
    res_ev_w_ukv = mat_adamw("adamw_ev_w_ukv", ev_w_ukv, m_ev_w_ukv, v_ev_w_ukv,
                             jnp.stack([ev_sh[i][2].T for i in range(2)]))
    res_ev_w_out = mat_adamw("adamw_ev_w_out", ev_w_out, m_ev_w_out, v_ev_w_out, jnp.stack([ev_sh[i][3] for i in range(2)]))
    res_od_w_in = mat_adamw("adamw_od_w_in", od_w_in, m_od_w_in, v_od_w_in, jnp.stack([od_sh[i][0].T for i in range(2)]))
    res_od_w_out = mat_adamw("adamw_od_w_out", od_w_out, m_od_w_out, v_od_w_out, jnp.stack([od_sh[i][1] for i in range(2)]))

    per_weight = [res_ada_w] + [[rep_res[t][j] for t in range(4)] for j in range(3)]
    per_weight += [res_ev_w_in, [rep_res[t][3] for t in range(4)], [rep_res[t][4] for t in range(4)],
                   [rep_res[t][5] for t in range(4)], res_ev_w_uq, [rep_res[t][6] for t in range(4)], res_ev_w_ukv,
                   res_ev_w_out, res_od_w_in, res_b_in, res_dw_w, res_dw_b, res_ln_g, res_ln_b, res_od_w_out]
    outs = [loss, grad_x]
    for t in range(4):
        outs += [pw[t] for pw in per_weight]
    return tuple(outs)
```

```python
import functools

import numpy as np
import jax
import jax.numpy as jnp
from jax import lax
from jax.experimental import pallas as pl
from jax.experimental.pallas import tpu as pltpu

F32 = jnp.float32
BF16 = jnp.bfloat16
MESH = pl.DeviceIdType.MESH

N_DEV = 8
D = 1024
DEPTH = 4
EPS = 1e-6
RET_HEADS = 4
MLA_HEADS = 4
RET_SCALE = 64 ** -0.5
MLA_SCALE = 192 ** -0.5
CONV_K = 31
CONV_HALO = 16
ROPE_BASE = 10000.0

ADAM_LR = 0.001
ADAM_B1 = 0.9
ADAM_B2 = 0.999
ADAM_EPS = 1e-08
ADAM_WD = 0.01
ADAM_STEP = 10

LANES = 128
VMEM_LIMIT = 48 * 1024 * 1024

ZL_EV = dict(rv=(0, 512), rg=(512, 512), mg=(1024, 512), cq=(1536, 384), kr=(1920, 128),
             rq=(2048, 512), rk=(2560, 512), ckv=(3072, 256))
ZW_EV = 3328
ZW_OD = 3072


def _cparams(sem, vmem=VMEM_LIMIT):
    return pltpu.CompilerParams(dimension_semantics=sem, vmem_limit_bytes=vmem)


def _pick(n, prefs):
    for p in prefs:
        if n % p == 0:
            return p
    return n


def _sigmoid(x):
    return 0.5 * (jnp.tanh(0.5 * x) + 1.0)


def _silu(x):
    return x * _sigmoid(x)


def _rms(x, g):
    return x * lax.rsqrt(jnp.mean(x * x, axis=-1, keepdims=True) + EPS) * g


def _log_sigmoid(x):
    return jnp.minimum(x, 0.0) - jnp.log(1.0 + jnp.exp(jnp.minimum(x, -x)))


def _tile_lanes(t, width):
    reps = width // t.shape[1]
    return t if reps == 1 else jnp.concatenate([t] * reps, axis=1)


def _rot_half(x):
    w = x.shape[1]
    lane = lax.broadcasted_iota(jnp.int32, x.shape, 1)
    first = jnp.bitwise_and(lane, 63) < 32
    return jnp.where(first, pltpu.roll(x, w - 32, 1), pltpu.roll(x, 32, 1))


def _rope(x, cos, sin):
    w = x.shape[1]
    return x * _tile_lanes(cos, w) + _rot_half(x) * _tile_lanes(sin, w)


def _rope_t(dy, cos, sin):
    w = dy.shape[1]
    return dy * _tile_lanes(cos, w) + _rot_half(dy * _tile_lanes(sin, w))


_DN = {"nn": (((1,), (0,)), ((), ())), "nt": (((1,), (1,)), ((), ())), "tn": (((0,), (0,)), ((), ()))}


def _dot(a, b, mode):
    return lax.dot_general(a.astype(BF16), b.astype(BF16), _DN[mode], preferred_element_type=F32)


@functools.partial(jax.custom_vjp, nondiff_argnums=(2,))
def _bdot(a, b, mode):
    return _dot(a, b, mode)


def _bdot_fwd(a, b, mode):
    return _dot(a, b, mode), (a, b)


def _bdot_bwd(mode, res, g):
    a, b = res
    if mode == "nn":
        return _dot(g, b, "nt"), _dot(a, g, "tn")
    if mode == "nt":
        return _dot(g, b, "nn"), _dot(g, a, "tn")
    return _dot(b, g, "nt"), _dot(a, g, "nn")


_bdot.defvjp(_bdot_fwd, _bdot_bwd)


def _rowwise(name, body, row_ins, vec_ins, row_outs, red_outs=(), tile=512):
    s = row_ins[0][0].shape[0]
    tile = min(tile, s)
    nr, nv, no = len(row_ins), len(vec_ins), len(row_outs)

    def kern(*refs):
        rows = [r[...] for r in refs[:nr]]
        vecs = [r[...] for r in refs[nr:nr + nv]]
        outs, reds = body(rows, vecs)
        for r, o in zip(refs[nr + nv:nr + nv + no], outs):
            r[...] = o.astype(r.dtype)
        red_refs = refs[nr + nv + no:]
        if red_refs:
            @pl.when(pl.program_id(0) == 0)
            def _():
                for r in red_refs:
                    r[...] = jnp.zeros(r.shape, r.dtype)
            for r, v in zip(red_refs, reds):
                r[...] += v

    in_specs = [pl.BlockSpec((tile, w), (lambda i, cb=cb: (i, cb))) for (_, w, cb) in row_ins]
    in_specs += [pl.BlockSpec(v.shape, (lambda i, nd=v.ndim: (0,) * nd)) for v in vec_ins]
    out_specs = [pl.BlockSpec((tile, w), lambda i: (i, 0)) for (w, _) in row_outs]
    out_specs += [pl.BlockSpec(sh, lambda i: (0, 0)) for sh in red_outs]
    out_shape = [jax.ShapeDtypeStruct((s, w), dt) for (w, dt) in row_outs]
    out_shape += [jax.ShapeDtypeStruct(sh, F32) for sh in red_outs]
    res = pl.pallas_call(
        kern, name=name, grid=(s // tile,), in_specs=in_specs, out_specs=out_specs, out_shape=out_shape,
        compiler_params=_cparams(("arbitrary",)),
    )(*[a for (a, _, _) in row_ins], *vec_ins)
    return res[:no], res[no:]


def _mm(name, a, b, mode, out_dtype=F32, bias=None, comm=None):
    if mode == "tn":
        k, m = a.shape
        n = b.shape[1]
        tm = m if m <= 1664 else m // 2
        tk = min(k, 1024)
        nk = k // tk

        def kern(a_ref, b_ref, o_ref, acc_ref):
            kk = pl.program_id(1)
            part = _dot(a_ref[...], b_ref[...], "tn")

            @pl.when(kk == 0)
            def _():
                acc_ref[...] = part

            @pl.when(kk > 0)
            def _():
                acc_ref[...] += part

            @pl.when(kk == nk - 1)
            def _():
                o_ref[...] = acc_ref[...].astype(o_ref.dtype)

        return pl.pallas_call(
            kern, name=name, grid=(m // tm, nk),
            in_specs=[pl.BlockSpec((tk, tm), lambda i, kk: (kk, i)),
                      pl.BlockSpec((tk, n), lambda i, kk: (kk, 0))],
            out_specs=pl.BlockSpec((tm, n), lambda i, kk: (i, 0)),
            out_shape=jax.ShapeDtypeStruct((m, n), out_dtype),
            scratch_shapes=[pltpu.VMEM((tm, n), F32)],
            compiler_params=_cparams(("parallel", "arbitrary")),
        )(a, b)

    m, k = a.shape
    n = b.shape[1] if mode == "nn" else b.shape[0]
    tm = min(m, 1024)
    tn = n if n <= 1664 else n // 2
    has_bias = bias is not None

    def kern(*refs):
        a_ref, b_ref = refs[0], refs[1]
        o_ref = refs[-1]
        r = _dot(a_ref[...], b_ref[...], mode)
        if has_bias:
            r = r + refs[2][...]
        o_ref[...] = r.astype(o_ref.dtype)

    b_spec = (pl.BlockSpec((k, tn), lambda i, j: (0, j)) if mode == "nn"
              else pl.BlockSpec((tn, k), lambda i, j: (j, 0)))
    in_specs = [pl.BlockSpec((tm, k), lambda i, j: (i, 0)), b_spec]
    args = [a, b]
    if has_bias:
        in_specs.append(pl.BlockSpec((1, tn), lambda i, j: (0, j)))
        args.append(bias)
    outs, got = _call(kern, name=name, grid=(m // tm, n // tn), in_specs=in_specs,
                      out_specs=[pl.BlockSpec((tm, tn), lambda i, j: (i, j))],
                      out_shape=[jax.ShapeDtypeStruct((m, n), out_dtype)], args=args, sem=("parallel", "parallel"),
                      comm=comm)
    return (outs[0], got) if comm else outs[0]


def _mm_cols_nn(name, pieces, offsets, b, comm=None):
    m = pieces[0].shape[0]
    n = b.shape[1]
    tm = min(m, 1024)
    np_ = len(pieces)

    def kern(*refs):
        acc = _dot(refs[0][...], refs[np_][...], "nn")
        for p in range(1, np_):
            acc = acc + _dot(refs[p][...], refs[np_ + p][...], "nn")
        refs[2 * np_][...] = acc

    in_specs = [pl.BlockSpec((tm, a.shape[1]), lambda i: (i, 0)) for a in pieces]
    in_specs += [pl.BlockSpec((a.shape[1], n), (lambda i, r=off // a.shape[1]: (r, 0))) for a, off in zip(pieces, offsets)]
    outs, got = _call(kern, name=name, grid=(m // tm,), in_specs=in_specs,
                      out_specs=[pl.BlockSpec((tm, n), lambda i: (i, 0))],
                      out_shape=[jax.ShapeDtypeStruct((m, n), F32)], args=list(pieces) + [b] * np_, sem=("parallel",),
                      comm=comm)
    return (outs[0], got) if comm else outs[0]


def _mm_cols_tn(name, pieces, b):
    k, n = b.shape
    tk = min(k, 512)
    nk = k // tk
    np_ = len(pieces)

    def kern(*refs):
        b_ref = refs[np_]
        o_refs, acc_refs = refs[np_ + 1:2 * np_ + 1], refs[2 * np_ + 1:]
        kk = pl.program_id(0)

        @pl.when(kk == 0)
        def _():
            for r in acc_refs:
                r[...] = jnp.zeros(r.shape, F32)

        bb = b_ref[...]
        for p in range(np_):
            acc_refs[p][...] += _dot(refs[p][...], bb, "tn")

        @pl.when(kk == nk - 1)
        def _():
            for o, r in zip(o_refs, acc_refs):
                o[...] = r[...].astype(o.dtype)

    return pl.pallas_call(
        kern, name=name, grid=(nk,),
        in_specs=[pl.BlockSpec((tk, a.shape[1]), lambda kk: (kk, 0)) for a in pieces] + [pl.BlockSpec((tk, n), lambda kk: (kk, 0))],
        out_specs=[pl.BlockSpec((a.shape[1], n), lambda kk: (0, 0)) for a in pieces],
        out_shape=[jax.ShapeDtypeStruct((a.shape[1], n), BF16) for a in pieces],
        scratch_shapes=[pltpu.VMEM((a.shape[1], n), F32) for a in pieces],
        compiler_params=_cparams(("arbitrary",)),
    )(*pieces, b)


def _peers():
    mx, my, mc = lax.axis_index("x"), lax.axis_index("y"), lax.axis_index("c")
    me = 4 * mx + 2 * my + mc
    out = []
    for k in range(1, N_DEV):
        px = 1 - mx if (k >> 2) & 1 else mx
        py = 1 - my if (k >> 1) & 1 else my
        pc = 1 - mc if k & 1 else mc
        out.append((k, (px, py, pc), 4 * px + 2 * py + pc))
    return me, out


def _xchg_copies(x_refs, out_refs, scatter, send_sems, recv_sems, local_sems):
    me, peers = _peers()
    local, out, arrive = [], [], []
    for a, (x, o, sc) in enumerate(zip(x_refs, out_refs, scatter)):
        mine = x.at[me] if sc else x
        local.append(pltpu.make_async_copy(mine, o.at[me], local_sems.at[a]))
        for k, dev, p in peers:
            out.append(pltpu.make_async_remote_copy(
                src_ref=x.at[p] if sc else x, dst_ref=o.at[me],
                send_sem=send_sems.at[a, k - 1], recv_sem=recv_sems.at[a, k - 1],
                device_id=dev, device_id_type=MESH))
            arrive.append(pltpu.make_async_remote_copy(
                src_ref=mine, dst_ref=o.at[p],
                send_sem=send_sems.at[a, k - 1], recv_sem=recv_sems.at[a, k - 1],
                device_id=dev, device_id_type=MESH))
    return local, out, arrive


def _xchg_start(*args):
    local, out, _ = _xchg_copies(*args)
    for cp in local + out:
        cp.start()


def _xchg_wait(*args):
    local, out, arrive = _xchg_copies(*args)
    for cp in out:
        cp.wait_send()
    for cp in arrive:
        cp.wait_recv()
    for cp in local:
        cp.wait()


def _call(kern, *, name, grid, in_specs, out_specs, out_shape, args, sem, scratch_shapes=(), vmem=VMEM_LIMIT,
          comm=None):
    if not comm:
        outs = pl.pallas_call(kern, name=name, grid=grid, in_specs=in_specs, out_specs=out_specs, out_shape=out_shape,
                              scratch_shapes=list(scratch_shapes), compiler_params=_cparams(sem, vmem))(*args)
        return outs, []
    n, ni, no, ns = len(comm), len(in_specs), len(out_specs), len(scratch_shapes)
    xs = [x for x, _ in comm]
    scatter = [sc for _, sc in comm]

    def body(*refs):
        ins, x_refs = refs[:ni], refs[ni:ni + n]
        outs, out_refs = refs[ni + n:ni + n + no], refs[ni + n + no:ni + 2 * n + no]
        scr = refs[ni + 2 * n + no:ni + 2 * n + no + ns]
        sems = refs[ni + 2 * n + no + ns:]
        ids = [pl.program_id(d) for d in range(len(grid))]
        first = functools.reduce(jnp.logical_and, [i == 0 for i in ids])
        last = functools.reduce(jnp.logical_and, [i == g - 1 for i, g in zip(ids, grid)])

        @pl.when(first)
        def _():
            _xchg_start(x_refs, out_refs, scatter, *sems)

        kern(*ins, *outs, *scr)

        @pl.when(last)
        def _():
            _xchg_wait(x_refs, out_refs, scatter, *sems)

    any_spec = pl.BlockSpec(memory_space=pl.ANY)
    res = pl.pallas_call(
        body, name=name, grid=grid,
        in_specs=list(in_specs) + [any_spec] * n, out_specs=list(out_specs) + [any_spec] * n,
        out_shape=list(out_shape) + [jax.ShapeDtypeStruct((N_DEV,) + tuple(x.shape[1:] if sc else x.shape), x.dtype)
                                     for x, sc in comm],
        scratch_shapes=list(scratch_shapes) + [pltpu.SemaphoreType.DMA((n, N_DEV - 1)),
                                               pltpu.SemaphoreType.DMA((n, N_DEV - 1)), pltpu.SemaphoreType.DMA((n,))],
        compiler_params=pltpu.CompilerParams(dimension_semantics=("arbitrary",) * len(grid), vmem_limit_bytes=vmem,
                                             has_side_effects=True),
    )(*args, *xs)
    return res[:no], res[no:]


def _exchange(name, xs, scatter):
    def nothing():
        pass

    return _call(nothing, name=name, grid=(1,), in_specs=[], out_specs=[], out_shape=[], args=[], sem=("arbitrary",),
                 comm=[(x, scatter) for x in xs])[1]


def _sum_parts(name, x):
    p, r, c = x.shape
    tr = r if r * c * p * x.dtype.itemsize <= (8 << 20) else _pick(r, (256, 128, 64, 16))

    def kern(x_ref, o_ref):
        acc = x_ref[0].astype(F32)
        for i in range(1, p):
            acc = acc + x_ref[i].astype(F32)
        o_ref[...] = acc

    return pl.pallas_call(
        kern, name=name, grid=(r // tr,),
        in_specs=[pl.BlockSpec((p, tr, c), lambda i: (0, i, 0))],
        out_specs=pl.BlockSpec((tr, c), lambda i: (i, 0)),
        out_shape=jax.ShapeDtypeStruct((r, c), F32),
        compiler_params=_cparams(("parallel",)),
    )(x)


def _adamw(name, w, m, v, g):
    r, c = w.shape
    parts = g.shape[0] if g.ndim == 3 else 0
    tr = 512 if (r > 512 and r % 512 == 0) else r

    def kern(w_ref, m_ref, v_ref, g_ref, go_ref, d_ref, mo_ref, vo_ref):
        if parts:
            gg = g_ref[0]
            for i in range(1, parts):
                gg = gg + g_ref[i]
        else:
            gg = g_ref[...]
        mm = ADAM_B1 * m_ref[...] + (1.0 - ADAM_B1) * gg
        vv = ADAM_B2 * v_ref[...] + (1.0 - ADAM_B2) * (gg * gg)
        m_hat = mm / (1.0 - ADAM_B1 ** ADAM_STEP)
        v_hat = vv / (1.0 - ADAM_B2 ** ADAM_STEP)
        go_ref[...] = gg
        d_ref[...] = -ADAM_LR * (m_hat / (jnp.sqrt(v_hat) + ADAM_EPS) + ADAM_WD * w_ref[...])
        mo_ref[...] = mm
        vo_ref[...] = vv

    spec = pl.BlockSpec((tr, c), lambda i: (i, 0))
    gspec = pl.BlockSpec((parts, tr, c), lambda i: (0, i, 0)) if parts else spec
    sh = jax.ShapeDtypeStruct((r, c), F32)
    return pl.pallas_call(
        kern, name=name, grid=(r // tr,), in_specs=[spec, spec, spec, gspec],
        out_specs=[spec] * 4, out_shape=[sh] * 4,
        compiler_params=_cparams(("parallel",)),
    )(w, m, v, g)


def _ret_tables(dec_cc, dec_cd, dec_dd, reverse):
    c = dec_cc.shape[0]
    row = lax.broadcasted_iota(jnp.int32, (c, c), 0).astype(F32)
    col = lax.broadcasted_iota(jnp.int32, (c, c), 1).astype(F32)
    pos = lax.broadcasted_iota(jnp.int32, (c, LANES), 0).astype(F32)
    if reverse:
        diff, mask = col - row, col > row
        q_exp, k_exp = c - pos, pos
    else:
        diff, mask = row - col, row >= col
        q_exp, k_exp = pos + 1.0, c - 1.0 - pos
    decay = jnp.where(mask, jnp.exp(_log_sigmoid(dec_cc) * jnp.maximum(diff, 0.0)), 0.0)
    lam_cd = _log_sigmoid(dec_cd)
    return decay, jnp.exp(lam_cd * q_exp), jnp.exp(lam_cd * k_exp), jnp.exp(_log_sigmoid(dec_dd) * float(c))


def _ret_chunk(q, k, v, st, decay, qw, kw, sd):
    scores = _bdot(q, k, "nt") * decay
    o = _bdot(scores, v, "nn") + _bdot(q * qw, st, "nn")
    st_new = st * sd + _bdot(k * kw, v, "tn")
    return o, st_new


def _ret_dec(dec_ref, h, c):
    d = dec_ref[:, h:h + 1]
    return (jnp.broadcast_to(d, (c, c)), jnp.broadcast_to(d, (c, LANES)), jnp.broadcast_to(d, (LANES, LANES)))


def _ret_table_scratch(c):
    return [pltpu.VMEM((RET_HEADS, c, c), F32), pltpu.VMEM((RET_HEADS, c, LANES), F32),
            pltpu.VMEM((RET_HEADS, c, LANES), F32), pltpu.VMEM((RET_HEADS, LANES, LANES), F32)]


def _ret_fwd(name, q, k, z, dec, reverse, chunk):
    s = q.shape[0]
    chunk = min(chunk, s)
    n = s // chunk
    cmap = (lambda i: (n - 1 - i, 0)) if reverse else (lambda i: (i, 0))
    smap = (lambda i: (n - 1 - i, 0, 0, 0)) if reverse else (lambda i: (i, 0, 0, 0))

    def kern(q_ref, k_ref, v_ref, dec_ref, o_ref, st_out_ref, st_ref, *tab_refs):
        @pl.when(pl.program_id(0) == 0)
        def _():
            st_ref[...] = jnp.zeros(st_ref.shape, F32)
            for h in range(RET_HEADS):
                for r, t in zip(tab_refs, _ret_tables(*_ret_dec(dec_ref, h, chunk), reverse)):
                    r[h] = t

        for h in range(RET_HEADS):
            sl = slice(LANES * h, LANES * (h + 1))
            st = st_ref[h]
            st_out_ref[h] = st
            o, st_new = _ret_chunk(q_ref[:, sl].astype(F32), k_ref[:, sl].astype(F32), v_ref[:, sl],
                                   st, *[r[h] for r in tab_refs])
            o_ref[:, sl] = o
            st_ref[h] = st_new

    return pl.pallas_call(
        kern, name=name, grid=(n,),
        in_specs=[pl.BlockSpec((chunk, 512), cmap), pl.BlockSpec((chunk, 512), cmap),
                  pl.BlockSpec((chunk, 512), cmap), pl.BlockSpec((1, RET_HEADS), lambda i: (0, 0))],
        out_specs=[pl.BlockSpec((chunk, 512), cmap), pl.BlockSpec((None, RET_HEADS, LANES, LANES), smap)],
        out_shape=[jax.ShapeDtypeStruct((s, 512), F32), jax.ShapeDtypeStruct((n, RET_HEADS, LANES, LANES), F32)],
        scratch_shapes=[pltpu.VMEM((RET_HEADS, LANES, LANES), F32)] + _ret_table_scratch(chunk),
        compiler_params=_cparams(("arbitrary",)),
    )(q, k, z, dec)


def _ret_bwd(name, q, k, z, dec, states, do, reverse, chunk):
    s = q.shape[0]
    chunk = min(chunk, s)
    n = s // chunk
    cmap = (lambda i: (i, 0)) if reverse else (lambda i: (n - 1 - i, 0))
    smap = (lambda i: (i, 0, 0, 0)) if reverse else (lambda i: (n - 1 - i, 0, 0, 0))

    def kern(q_ref, k_ref, v_ref, dec_ref, st_in_ref, do_ref, dq_ref, dk_ref, dv_ref, ddec_ref, dst_ref, *scr):
        tab_refs, gtab_refs = scr[:4], scr[4:]
        step = pl.program_id(0)

        @pl.when(step == 0)
        def _():
            dst_ref[...] = jnp.zeros(dst_ref.shape, F32)
            for r in gtab_refs:
                r[...] = jnp.zeros(r.shape, F32)
            for h in range(RET_HEADS):
                for r, t in zip(tab_refs, _ret_tables(*_ret_dec(dec_ref, h, chunk), reverse)):
                    r[h] = t

        for h in range(RET_HEADS):
            sl = slice(LANES * h, LANES * (h + 1))
            _, vjp = jax.vjp(_ret_chunk, q_ref[:, sl].astype(F32), k_ref[:, sl].astype(F32), v_ref[:, sl],
                             st_in_ref[h], *[r[h] for r in tab_refs])
            grads = vjp((do_ref[:, sl], dst_ref[h]))
            dq_ref[:, sl] = grads[0]
            dk_ref[:, sl] = grads[1]
            dv_ref[:, sl] = grads[2]
            dst_ref[h] = grads[3]
            for r, g in zip(gtab_refs, grads[4:]):
                r[h] += g

        @pl.when(step == n - 1)
        def _():
            lane = lax.broadcasted_iota(jnp.int32, (1, LANES), 1)
            ddec = jnp.zeros((1, LANES), F32)
            for h in range(RET_HEADS):
                _, vjp_t = jax.vjp(functools.partial(_ret_tables, reverse=reverse), *_ret_dec(dec_ref, h, chunk))
                parts = vjp_t(tuple(r[h] for r in gtab_refs))
                tot = sum(jnp.sum(jnp.sum(p, axis=1, keepdims=True), axis=0, keepdims=True) for p in parts)
                ddec = ddec + jnp.where(lane == h, tot, 0.0)
            ddec_ref[...] = ddec

    cspec = pl.BlockSpec((chunk, 512), cmap)
    return pl.pallas_call(
        kern, name=name, grid=(n,),
        in_specs=[cspec, cspec, cspec, pl.BlockSpec((1, RET_HEADS), lambda i: (0, 0)),
                  pl.BlockSpec((None, RET_HEADS, LANES, LANES), smap), cspec],
        out_specs=[cspec, cspec, cspec, pl.BlockSpec((1, LANES), lambda i: (0, 0))],
        out_shape=[jax.ShapeDtypeStruct((s, 512), F32)] * 3 + [jax.ShapeDtypeStruct((1, LANES), F32)],
        scratch_shapes=[pltpu.VMEM((RET_HEADS, LANES, LANES), F32)] + _ret_table_scratch(chunk) * 2,
        compiler_params=_cparams(("arbitrary",)),
    )(q, k, z, dec, states, do)


def _fa_fwd(name, q, k, vx, tq, tk, nsub, comm=None):
    s = q.shape[0]
    tq, tk = min(tq, s), min(tk, s)
    nk = s // tk
    sq = tq // nsub

    def kern(q_ref, k_ref, v_ref, o_ref, lse_ref, m_ref, acc_ref):
        j = pl.program_id(2)

        @pl.when(j == 0)
        def _():
            m_ref[...] = jnp.full(m_ref.shape, -jnp.inf, F32)
            acc_ref[...] = jnp.zeros(acc_ref.shape, F32)

        kb, vb = k_ref[...], v_ref[...]
        for c in range(nsub):
            rows = pl.ds(c * sq, sq)
            sc = _dot(q_ref[rows, :], kb, "nt")
            m_prev = m_ref[rows, :]
            m_new = jnp.maximum(m_prev, jnp.max(sc, axis=1, keepdims=True))
            alpha = jnp.exp(m_prev - m_new)
            p = jnp.exp(sc - m_new)
            acc_ref[rows, :] = alpha * acc_ref[rows, :] + _dot(p, vb, "nn")
            m_ref[rows, :] = m_new

        @pl.when(j == nk - 1)
        def _():
            den = acc_ref[:, LANES:]
            o_ref[...] = acc_ref[:, :LANES] / den
            lse_ref[...] = m_ref[...] + jnp.log(den[:, :1])

    (o, lse), got = _call(
        kern, name=name, grid=(MLA_HEADS, s // tq, nk),
        in_specs=[pl.BlockSpec((tq, 256), lambda h, i, j: (i, h)),
                  pl.BlockSpec((tk, 256), lambda h, i, j: (j, h)),
                  pl.BlockSpec((tk, 256), lambda h, i, j: (j, h))],
        out_specs=[pl.BlockSpec((tq, LANES), lambda h, i, j: (i, h)),
                   pl.BlockSpec((None, tq, 1), lambda h, i, j: (h, i, 0))],
        out_shape=[jax.ShapeDtypeStruct((s, 512), F32), jax.ShapeDtypeStruct((MLA_HEADS, s, 1), F32)],
        scratch_shapes=[pltpu.VMEM((tq, 1), F32), pltpu.VMEM((tq, 256), F32)],
        args=[q, k, vx], sem=("parallel", "parallel", "arbitrary"), comm=comm)
    return o, lse, got


def _fa_bwd(name, q, k, kt, vx, do, lse, delta, tq, tk, nsub, comm=None):
    s = q.shape[0]
    tq, tk = min(tq, s), min(tk, s)
    nq = s // tq
    sk = tk // nsub

    def kern(q_ref, k_ref, kt_ref, v_ref, do_ref, lse_ref, dl_ref, dqt_ref, dk_ref, dv_ref):
        j, i = pl.program_id(1), pl.program_id(2)

        @pl.when(i == 0)
        def _():
            dv_ref[...] = jnp.zeros(dv_ref.shape, F32)
            dk_ref[...] = jnp.zeros(dk_ref.shape, F32)

        @pl.when(j == 0)
        def _():
            dqt_ref[i] = jnp.zeros((256, tq), F32)

        qb, dob = q_ref[...], do_ref[...]
        lse_row, dl_row = lse_ref[...], dl_ref[...]
        dqt = dqt_ref[i]
        for c in range(nsub):
            rows = pl.ds(c * sk, sk)
            st = _dot(k_ref[rows, :], qb, "nt")
            pt = jnp.exp(st - lse_row)
            dpt = _dot(v_ref[rows, :], dob, "nt")
            dst = (pt * (dpt - dl_row)).astype(BF16)
            dv_ref[rows, :] += _dot(pt, dob, "nn")
            dk_ref[rows, :] += _dot(dst, qb, "nn")
            dqt = dqt + _dot(kt_ref[:, rows], dst, "nn")
        dqt_ref[i] = dqt

    outs, got = _call(
        kern, name=name, grid=(MLA_HEADS, s // tk, nq),
        in_specs=[pl.BlockSpec((tq, 256), lambda h, j, i: (i, h)),
                  pl.BlockSpec((tk, 256), lambda h, j, i: (j, h)),
                  pl.BlockSpec((None, 256, tk), lambda h, j, i: (h, 0, j)),
                  pl.BlockSpec((tk, LANES), lambda h, j, i: (j, 2 * h)),
                  pl.BlockSpec((tq, LANES), lambda h, j, i: (i, h)),
                  pl.BlockSpec((None, 1, tq), lambda h, j, i: (h, 0, i)),
                  pl.BlockSpec((None, 1, tq), lambda h, j, i: (h, 0, i))],
        out_specs=[pl.BlockSpec((None, nq, 256, tq), lambda h, j, i: (h, 0, 0, 0)),
                   pl.BlockSpec((tk, 256), lambda h, j, i: (j, h)),
                   pl.BlockSpec((tk, LANES), lambda h, j, i: (j, h))],
        out_shape=[jax.ShapeDtypeStruct((MLA_HEADS, nq, 256, tq), F32),
                   jax.ShapeDtypeStruct((s, 1024), F32), jax.ShapeDtypeStruct((s, 512), F32)],
        args=[q, k, kt, vx, do, lse, delta], sem=("parallel", "arbitrary", "arbitrary"), comm=comm)
    return outs[0], outs[1], outs[2], got


def _fill_padded(dst_ref, src_ref, s):
    zeros = jnp.zeros((CONV_HALO, LANES), F32)
    dst_ref[pl.ds(0, CONV_HALO), :] = zeros
    dst_ref[pl.ds(CONV_HALO + s, CONV_HALO), :] = zeros
    dst_ref[pl.ds(CONV_HALO, s), :] = src_ref[...]


def _shifted_windows(win):
    n = win.shape[0]
    return [win] + [pltpu.roll(win, n - b, 0) for b in range(1, 8)]


def _conv_fwd(name, u, w, bias, rc=256):
    s = u.shape[0]
    rc = min(rc, s)

    def kern(u_ref, w_ref, b_ref, o_ref, pad_ref):
        _fill_padded(pad_ref, u_ref, s)
        wv = w_ref[...]
        bv = b_ref[...]

        def chunk(r, carry):
            base = pl.multiple_of(r * rc, rc)
            wins = _shifted_windows(pad_ref[pl.ds(base, rc + 2 * CONV_HALO), :])
            acc = jnp.broadcast_to(bv, (rc, LANES))
            for kk in range(CONV_K):
                a, b = divmod(kk + 1, 8)
                acc = acc + wv[kk:kk + 1, :] * wins[b][8 * a:8 * a + rc]
            o_ref[pl.ds(base, rc), :] = acc
            return carry

        lax.fori_loop(0, s // rc, chunk, 0)

    return pl.pallas_call(
        kern, name=name, grid=(D // LANES,),
        in_specs=[pl.BlockSpec((s, LANES), lambda c: (0, c)), pl.BlockSpec((32, LANES), lambda c: (0, c)),
                  pl.BlockSpec((1, LANES), lambda c: (0, c))],
        out_specs=pl.BlockSpec((s, LANES), lambda c: (0, c)),
        out_shape=jax.ShapeDtypeStruct((s, D), F32),
        scratch_shapes=[pltpu.VMEM((s + 2 * CONV_HALO, LANES), F32)],
        compiler_params=_cparams(("parallel",)),
    )(u, w, bias)


def _conv_bwd(name, u, g, w, rc=256, comm=None):
    s = u.shape[0]
    rc = min(rc, s)

    def kern(u_ref, g_ref, w_ref, du_ref, dw_ref, db_ref, upad_ref, gpad_ref, dwacc_ref):
        _fill_padded(upad_ref, u_ref, s)
        _fill_padded(gpad_ref, g_ref, s)
        dwacc_ref[...] = jnp.zeros(dwacc_ref.shape, F32)
        wv = w_ref[...]

        def chunk(r, carry):
            base = pl.multiple_of(r * rc, rc)
            gwins = _shifted_windows(gpad_ref[pl.ds(base, rc + 2 * CONV_HALO), :])
            uwins = _shifted_windows(upad_ref[pl.ds(base, rc + 2 * CONV_HALO), :])
            gc = g_ref[pl.ds(base, rc), :]
            acc = jnp.zeros((rc, LANES), F32)
            for kk in range(CONV_K):
                a, b = divmod(CONV_K - kk, 8)
                acc = acc + wv[kk:kk + 1, :] * gwins[b][8 * a:8 * a + rc]
                a, b = divmod(kk + 1, 8)
                prod = gc * uwins[b][8 * a:8 * a + rc]
                dwacc_ref[kk] += jnp.sum(prod.reshape(rc // 8, 8, LANES), axis=0)
            dwacc_ref[CONV_K] += jnp.sum(gc.reshape(rc // 8, 8, LANES), axis=0)
            du_ref[pl.ds(base, rc), :] = acc
            return carry

        lax.fori_loop(0, s // rc, chunk, 0)
        tot = jnp.sum(dwacc_ref[...], axis=1)
        lane_row = lax.broadcasted_iota(jnp.int32, (32, LANES), 0)
        dw_ref[...] = jnp.where(lane_row < CONV_K, tot, 0.0)
        db_ref[...] = tot[CONV_K:CONV_K + 1, :]

    cs = pl.BlockSpec((s, LANES), lambda c: (0, c))
    outs, got = _call(
        kern, name=name, grid=(D // LANES,),
        in_specs=[cs, cs, pl.BlockSpec((32, LANES), lambda c: (0, c))],
        out_specs=[cs, pl.BlockSpec((32, LANES), lambda c: (0, c)), pl.BlockSpec((1, LANES), lambda c: (0, c))],
        out_shape=[jax.ShapeDtypeStruct((s, D), F32), jax.ShapeDtypeStruct((32, D), F32),
                   jax.ShapeDtypeStruct((1, D), F32)],
        scratch_shapes=[pltpu.VMEM((s + 2 * CONV_HALO, LANES), F32), pltpu.VMEM((s + 2 * CONV_HALO, LANES), F32),
                        pltpu.VMEM((32, 8, LANES), F32)],
        args=[u, g, w], sem=("parallel",), comm=comm)
    return outs[0], outs[1], outs[2], got


def _mod_local(name, c_all, ada_w):
    def kern(c_ref, w_ref, o_ref):
        o_ref[...] = jnp.dot(_silu(c_ref[...]), w_ref[...], preferred_element_type=F32,
                             precision=lax.Precision.HIGHEST)

    return pl.pallas_call(
        kern, name=name, grid=(DEPTH,),
        in_specs=[pl.BlockSpec((N_DEV, D), lambda l: (0, 0)), pl.BlockSpec((None, D, 384), lambda l: (l, 0, 0))],
        out_specs=pl.BlockSpec((None, N_DEV, 384), lambda l: (l, 0, 0)),
        out_shape=jax.ShapeDtypeStruct((DEPTH, N_DEV, 384), F32),
        compiler_params=_cparams(("parallel",)),
    )(c_all, ada_w)


def _ada_w_grad(name, c_all_t, dmod):
    def kern(c_ref, d_ref, o_ref):
        o_ref[...] = jnp.dot(_silu(c_ref[...]), d_ref[...], preferred_element_type=F32,
                             precision=lax.Precision.HIGHEST)

    return pl.pallas_call(
        kern, name=name, grid=(DEPTH,),
        in_specs=[pl.BlockSpec((D, LANES), lambda l: (0, 0)), pl.BlockSpec((None, LANES, 384), lambda l: (l, 0, 0))],
        out_specs=pl.BlockSpec((None, D, 384), lambda l: (l, 0, 0)),
        out_shape=jax.ShapeDtypeStruct((DEPTH, D, 384), F32),
        compiler_params=_cparams(("parallel",)),
    )(c_all_t, dmod)


def _pre_fn(x, g, scale, shift):
    return _rms(x, g) * (1.0 + scale) + shift


def _post_fn(y, g, gate):
    return gate * _rms(y, g)


def _ev_post_fn(o_heads, rg, a, mg):
    normed = []
    for oh in o_heads:
        mu = jnp.mean(oh, axis=-1, keepdims=True)
        var = jnp.mean(jnp.square(oh - mu), axis=-1, keepdims=True)
        normed.append((oh - mu) * lax.rsqrt(var + EPS))
    return jnp.concatenate([jnp.concatenate(normed, axis=1) * _silu(rg), a * _silu(mg)], axis=1)


def _od_post_fn(u, g, ln_g, ln_b):
    mu = jnp.mean(u, axis=-1, keepdims=True)
    var = jnp.mean(jnp.square(u - mu), axis=-1, keepdims=True)
    y = (u - mu) * lax.rsqrt(var + EPS) * ln_g + ln_b
    return _silu(y) * _silu(g)


def _glu_fn(a, b):
    return a * _sigmoid(b)


def _heads(x):
    return [x[:, LANES * h:LANES * (h + 1)] for h in range(4)]


def _colsum(x):
    return jnp.sum(x, axis=0, keepdims=True)


def _zrows(n, c):
    return jnp.zeros((n, c), BF16)


def _ev_win_layout(wt):
    rq = [p for h in range(4) for p in (wt[64 * h:64 * h + 64], _zrows(64, D))]
    rk = [p for h in range(4) for p in (wt[256 + 64 * h:256 + 64 * h + 64], _zrows(64, D))]
    return jnp.concatenate([wt[512:1024], wt[1024:1536], wt[2240:2752], wt[1536:1920],
                            wt[2176:2240], _zrows(64, D)] + rq + rk + [wt[1920:2176]], axis=0)


def _uq_layout(wt):
    return jnp.concatenate([p for h in range(4) for p in (wt[192 * h:192 * h + 192], _zrows(64, 384))], axis=0)


def _uq_unlayout(g):
    return jnp.concatenate([g[256 * h:256 * h + 192] for h in range(4)], axis=0)


def _ukv_layout(wt):
    kpart = [p for h in range(4) for p in (wt[256 * h:256 * h + 128], _zrows(128, 256))]
    vpart = [wt[256 * h + 128:256 * h + 256] for h in range(4)]
    return jnp.concatenate(kpart + vpart, axis=0)


def _ukv_unlayout(g):
    return jnp.concatenate([p for h in range(4) for p in (g[256 * h:256 * h + 128], g[1024 + 128 * h:1024 + 128 * h + 128])],
                           axis=0)


def kernel(x, c, positions, ada_w, ada_b, pre_g, post_g, ev_w_in, ev_dec_f, ev_dec_b, ev_q_norm_g, ev_w_uq, ev_kv_norm_g, ev_w_ukv, ev_w_out, od_w_in, od_b_in, od_dw_w, od_dw_b, od_ln_g, od_ln_b, od_w_out, loss_target, m_ada_w, m_ada_b, m_pre_g, m_post_g, m_ev_w_in, m_ev_dec_f, m_ev_dec_b, m_ev_q_norm_g, m_ev_w_uq, m_ev_kv_norm_g, m_ev_w_ukv, m_ev_w_out, m_od_w_in, m_od_b_in, m_od_dw_w, m_od_dw_b, m_od_ln_g, m_od_ln_b, m_od_w_out, v_ada_w, v_ada_b, v_pre_g, v_post_g, v_ev_w_in, v_ev_dec_f, v_ev_dec_b, v_ev_q_norm_g, v_ev_w_uq, v_ev_kv_norm_g, v_ev_w_ukv, v_ev_w_out, v_od_w_in, v_od_b_in, v_od_dw_w, v_od_dw_b, v_od_ln_g, v_od_ln_b, v_od_w_out):
    s = x.shape[1]
    me = 4 * lax.axis_index("x") + 2 * lax.axis_index("y") + lax.axis_index("c")
    x0 = x.reshape(s, D)
    tgt = loss_target.reshape(s, D)
    ret_chunk = 256
    fa_cfg_f = ((min(4096, s // 2), min(1024, s // 2), min(16, s // 512)),) * 2
    fa_cfg_b = ((min(1024, s // 2), min(4096, s // 2), min(16, s // 512)),) * 2

    start_parts = [c.reshape(-1), od_b_in.reshape(-1), od_dw_w.reshape(-1), od_dw_b.reshape(-1),
                   od_ln_g.reshape(-1), od_ln_b.reshape(-1)]
    start_sizes = [p.shape[0] for p in start_parts]
    start_len = -(-sum(start_sizes) // 1024) * 1024
    start_vec = jnp.concatenate(start_parts + [jnp.zeros((start_len - sum(start_sizes),), F32)])
    start_all = _exchange("gather_start", [start_vec.reshape(-1, LANES)], False)[0].reshape(N_DEV, start_len)
    offs = np.cumsum([0] + start_sizes)
    c_all = start_all[:, offs[0]:offs[1]]
    b_in_all = start_all[:, offs[1]:offs[2]].reshape(N_DEV, 2, 384).transpose(1, 0, 2).reshape(2, 1, ZW_OD)
    dw_w_all = start_all[:, offs[2]:offs[3]].reshape(N_DEV, 2, CONV_K, LANES).transpose(1, 2, 0, 3).reshape(2, CONV_K, D)
    dw_w_all = jnp.concatenate([dw_w_all, jnp.zeros((2, 1, D), F32)], axis=1)
    dw_b_all = start_all[:, offs[3]:offs[4]].reshape(N_DEV, 2, LANES).transpose(1, 0, 2).reshape(2, 1, D)
    ln_g_all = start_all[:, offs[4]:offs[5]].reshape(N_DEV, 2, LANES).transpose(1, 0, 2).reshape(2, 1, D)
    ln_b_all = start_all[:, offs[5]:offs[6]].reshape(N_DEV, 2, LANES).transpose(1, 0, 2).reshape(2, 1, D)

    mod_loc = _mod_local("mod_local", c_all, ada_w)
    mod_all = _exchange("gather_mod", [mod_loc.reshape(DEPTH * N_DEV, 384)], False)[0].reshape(N_DEV, DEPTH, N_DEV, 384)
    mod = lax.dynamic_index_in_dim(mod_all, me, axis=2, keepdims=False)
    mod = mod.transpose(1, 0, 2).reshape(DEPTH, 3 * D) + ada_b
    shift = [mod[l:l + 1, 0:D] for l in range(DEPTH)]
    scale = [mod[l:l + 1, D:2 * D] for l in range(DEPTH)]
    gate = [mod[l:l + 1, 2 * D:3 * D] for l in range(DEPTH)]

    def ev_shards(i):
        return [ev_w_in[i].T.astype(BF16), ev_w_uq[i].T.astype(BF16), ev_w_ukv[i].T.astype(BF16), ev_w_out[i].astype(BF16)]

    def od_shards(i):
        return [od_w_in[i].T.astype(BF16), od_w_out[i].astype(BF16)]

    def full(got):
        return [g.reshape(N_DEV * g.shape[1], g.shape[2]) for g in got]

    def ev_full(got):
        win_t, uq_t, ukv_t, wout = full(got)
        return (_ev_win_layout(win_t), _uq_layout(uq_t), _ukv_layout(ukv_t), wout)

    ev_w = [ev_full(_exchange("gather_w_ev0", ev_shards(0), False)), None]
    od_w = [None, None]
    later_w = [(t, False) for t in od_shards(0) + ev_shards(1) + od_shards(1)]

    inv_freq = ROPE_BASE ** (-jnp.arange(0, 64, 2, dtype=F32) / 64)
    invf = jnp.tile(inv_freq, 4).reshape(1, LANES)
    sgn = jnp.tile(jnp.concatenate([-jnp.ones((32,), F32), jnp.ones((32,), F32)]), 2).reshape(1, LANES)

    def rope_body(rows, vecs):
        ang = rows[0].astype(F32) * vecs[0]
        return [jnp.cos(ang), jnp.sin(ang) * vecs[1]], []

    (cos_t, sin_t), _ = _rowwise("rope_tables", rope_body, [(positions.reshape(s, 1), 1, 0)], [invf, sgn],
                                 [(LANES, F32), (LANES, F32)])

    saved = []
    xl = x0
    for l in range(DEPTH):
        i = l // 2
        sv = dict(x=xl)

        def pre_body(rows, vecs):
            return [_pre_fn(rows[0], *vecs)], []

        (h,), _ = _rowwise(f"pre{l}", pre_body, [(xl, D, 0)], [pre_g[l:l + 1], scale[l], shift[l]], [(D, BF16)])
        sv["h"] = h
        if l % 2 == 0:
            win_t, uq_t, ukv_t, wout = ev_w[i]
            z = _mm(f"ev_in{l}", h, win_t, "nt")
            sv["z"] = z
            dec_f, dec_b = ev_dec_f[i:i + 1], ev_dec_b[i:i + 1]

            def prep_body(rows, vecs):
                rq, rk, cq, ckv, kr, cos, sin = rows
                return [_rope(rq, cos, sin), _rope(rk, cos, sin) * RET_SCALE, _rms(cq, vecs[0]), _rms(ckv, vecs[1]),
                        _rope(kr, cos, sin)], []

            (rq_r, rk_r, qn, kvn, krr), _ = _rowwise(
                f"ev_prep{l}", prep_body,
                [(z, 512, 4), (z, 512, 5), (z, 384, 4), (z, 256, 12), (z, LANES, 15), (cos_t, LANES, 0), (sin_t, LANES, 0)],
                [ev_q_norm_g[i:i + 1], ev_kv_norm_g[i:i + 1]],
                [(512, BF16), (512, BF16), (384, BF16), (256, BF16), (LANES, BF16)])
            sv.update(rq_r=rq_r, rk_r=rk_r, qn=qn, kvn=kvn)
            o_f, st_f = _ret_fwd(f"ret_f{l}", rq_r, rk_r, z, dec_f, False, ret_chunk)
            o_b, st_b = _ret_fwd(f"ret_b{l}", rq_r, rk_r, z, dec_b, True, ret_chunk)
            sv.update(o_f=o_f, o_b=o_b, st_f=st_f, st_b=st_b)
            q_pad = _mm(f"ev_uq{l}", qn, uq_t, "nt")
            kv_pad = _mm(f"ev_ukv{l}", kvn, ukv_t, "nt")

            def mla_prep_body(rows, vecs):
                qp, kk, vv, kr_r, cos, sin = rows
                qs, ks, vs = [], [], []
                ones = jnp.ones((qp.shape[0], LANES), F32)
                for hh in range(4):
                    qs += [qp[:, 256 * hh:256 * hh + 128], _rope(qp[:, 256 * hh + 128:256 * hh + 256], cos, sin)]
                    ks += [kk[:, 256 * hh:256 * hh + 128].astype(BF16), kr_r]
                    vs += [vv[:, LANES * hh:LANES * hh + LANES], ones]
                return [jnp.concatenate(qs, axis=1) * MLA_SCALE, jnp.concatenate(ks, axis=1),
                        jnp.concatenate(vs, axis=1)], []

            (qcat, kcat, v_x), _ = _rowwise(
                f"mla_prep{l}", mla_prep_body,
                [(q_pad, 1024, 0), (kv_pad, 1024, 0), (kv_pad, 512, 2), (krr, LANES, 0), (cos_t, LANES, 0), (sin_t, LANES, 0)],
                [], [(1024, BF16), (1024, BF16), (1024, BF16)])
            a_mla, lse, got = _fa_fwd(f"fa_fwd{l}", qcat, kcat, v_x, *fa_cfg_f[i], comm=later_w if l == 0 else None)
            if l == 0:
                od_w[0], ev_w[1], od_w[1] = tuple(full(got[0:2])), ev_full(got[2:6]), tuple(full(got[6:8]))
            sv.update(qcat=qcat, kcat=kcat, v_x=v_x, a_mla=a_mla, lse=lse)

            def ev_post_body(rows, vecs):
                of, ob, rg, a, mg = rows
                return [_ev_post_fn(_heads(of + ob), rg, a, mg)], []

            (act,), _ = _rowwise(f"ev_post{l}", ev_post_body,
                                 [(o_f, 512, 0), (o_b, 512, 0), (z, 512, 1), (a_mla, 512, 0), (z, 512, 2)], [], [(D, BF16)])
        else:
            win_t, wout = od_w[i]
            z = _mm(f"od_in{l}", h, win_t, "nt", bias=b_in_all[i])
            sv["z"] = z

            def glu_body(rows, vecs):
                return [_glu_fn(rows[0], rows[1])], []

            (u,), _ = _rowwise(f"glu{l}", glu_body, [(z, D, 0), (z, D, 1)], [], [(D, F32)])
            u2 = _conv_fwd(f"conv{l}", u, dw_w_all[i], dw_b_all[i])
            sv.update(u=u, u2=u2)

            def od_post_body(rows, vecs):
                return [_od_post_fn(rows[0], rows[1], vecs[0], vecs[1])], []

            (act,), _ = _rowwise(f"od_post{l}", od_post_body, [(u2, D, 0), (z, D, 2)], [ln_g_all[i], ln_b_all[i]],
                                 [(D, BF16)])
        sv["act"] = act
        y = _mm(f"out{l}", act, wout, "nn")
        sv["y"] = y

        def post_body(rows, vecs):
            return [rows[0] + _post_fn(rows[1], vecs[0], vecs[1])], []

        (xl,), _ = _rowwise(f"post{l}", post_body, [(xl, D, 0), (y, D, 0)], [post_g[l:l + 1], gate[l]], [(D, F32)])
        saved.append(sv)

    def loss_body(rows, vecs):
        diff = rows[0] - rows[1]
        return [diff * (1.0 / D)], [_colsum(diff * diff) * (0.5 / D)]

    (dx,), (loss_lanes,) = _rowwise("loss", loss_body, [(xl, D, 0), (tgt, D, 0)], [], [(D, F32)], [(1, D)])
    loss = lax.psum(jnp.sum(loss_lanes), ("x", "y", "c"))

    g_pre, g_post, g_mod = [None] * DEPTH, [None] * DEPTH, [None] * DEPTH
    g_dec_f, g_dec_b, g_qn, g_kvn = [None] * 2, [None] * 2, [None] * 2, [None] * 2
    g_b_in, g_dw_w, g_dw_b, g_ln_g, g_ln_b = [None] * 2, [None] * 2, [None] * 2, [None] * 2, [None] * 2
    recv_ev, recv_od = [None] * 2, [None] * 2
    pending = []
    for l in reversed(range(DEPTH)):
        i = l // 2
        sv = saved[l]

        def post_bwd_body(rows, vecs):
            yv, dxn = rows
            r, vjp = jax.vjp(_post_fn, yv, vecs[0], vecs[1])
            dy, dg, dgate = vjp(dxn)
            return [dy], [dg, dgate]

        (dy,), (dpost, dgate) = _rowwise(f"post_bwd{l}", post_bwd_body, [(sv["y"], D, 0), (dx, D, 0)],
                                         [post_g[l:l + 1], gate[l]], [(D, BF16)], [(1, D), (1, D)])
        g_post[l] = dpost
        wout = ev_w[i][3] if l % 2 == 0 else od_w[i][1]
        dact = _mm(f"out_dgrad{l}", dy, wout, "nt")
        d_wout = _mm(f"out_wgrad{l}", sv["act"], dy, "tn", out_dtype=BF16)
        z = sv["z"]
        if l % 2 == 0:
            win_t, uq_t, ukv_t, _ = ev_w[i]
            dec_f, dec_b = ev_dec_f[i:i + 1], ev_dec_b[i:i + 1]

            def ev_post_bwd_body(rows, vecs):
                of, ob, rg, a, mg, da = rows
                _, vjp = jax.vjp(_ev_post_fn, _heads(of + ob), rg, a, mg)
                do_heads, drg, d_a, dmg = vjp(da)
                deltas = [jnp.sum(dh_ * ah_, axis=1, keepdims=True) for dh_, ah_ in zip(_heads(d_a), _heads(a))]
                return [jnp.concatenate(do_heads, axis=1), drg, d_a, dmg] + deltas, []

            (do_ret, drg, do_mla, dmg, dl0, dl1, dl2, dl3), _ = _rowwise(
                f"ev_post_bwd{l}", ev_post_bwd_body,
                [(sv["o_f"], 512, 0), (sv["o_b"], 512, 0), (z, 512, 1), (sv["a_mla"], 512, 0), (z, 512, 2), (dact, D, 0)],
                [], [(512, F32), (512, BF16), (512, BF16), (512, BF16)] + [(1, F32)] * 4)
            delta = jnp.stack([dl0, dl1, dl2, dl3]).reshape(MLA_HEADS, 1, s)
            lse = sv["lse"].reshape(MLA_HEADS, 1, s)
            kt = sv["kcat"].reshape(s, MLA_HEADS, 256).transpose(1, 2, 0)
            if l == 0:
                pending = pending + [(d_wout.reshape(N_DEV, 128, D), True)]
            dqt, dkcat, dv, got = _fa_bwd(f"fa_bwd{l}", sv["qcat"], sv["kcat"], kt, sv["v_x"], do_mla, lse, delta,
                                          *fa_cfg_b[i], comm=pending)
            recv_od[i] = got[0:2]
            wout_recv = got[2:]
            dqcat = dqt.transpose(1, 3, 0, 2).reshape(s, 1024)

            def mla_prep_bwd_body(rows, vecs):
                dq, dk, dvv, cos, sin = rows
                qs = []
                dkrr = jnp.zeros((dq.shape[0], LANES), F32)
                for hh in range(4):
                    qs += [dq[:, 256 * hh:256 * hh + 128], _rope_t(dq[:, 256 * hh + 128:256 * hh + 256], cos, sin)]
                    dkrr = dkrr + dk[:, 256 * hh + 128:256 * hh + 256]
                return [jnp.concatenate(qs, axis=1) * MLA_SCALE, jnp.concatenate([dk, dvv], axis=1), dkrr], []

            (dq_pad, dkv_pad, dkrr), _ = _rowwise(
                f"mla_prep_bwd{l}", mla_prep_bwd_body,
                [(dqcat, 1024, 0), (dkcat, 1024, 0), (dv, 512, 0), (cos_t, LANES, 0), (sin_t, LANES, 0)], [],
                [(1024, BF16), (1536, BF16), (LANES, F32)])
            dqn = _mm(f"uq_dgrad{l}", dq_pad, uq_t, "nn")
            d_uq = _mm(f"uq_wgrad{l}", dq_pad, sv["qn"], "tn", out_dtype=BF16)
            dkvn = _mm(f"ukv_dgrad{l}", dkv_pad, ukv_t, "nn")
            d_ukv = _mm(f"ukv_wgrad{l}", dkv_pad, sv["kvn"], "tn", out_dtype=BF16)
            dq_f, dk_f, dv_f, ddec_f = _ret_bwd(f"ret_f_bwd{l}", sv["rq_r"], sv["rk_r"], z, dec_f, sv["st_f"], do_ret,
                                                False, ret_chunk)
            dq_b, dk_b, dv_b, ddec_b = _ret_bwd(f"ret_b_bwd{l}", sv["rq_r"], sv["rk_r"], z, dec_b, sv["st_b"], do_ret,
                                                True, ret_chunk)
            g_dec_f[i], g_dec_b[i] = ddec_f[:, :RET_HEADS], ddec_b[:, :RET_HEADS]

            def prep_bwd_body(rows, vecs):
                cq, ckv, cos, sin, dqf, dqb, dkf, dkb, dvf, dvb, d_qn, d_kvn, d_krr = rows
                _, vjp_q = jax.vjp(_rms, cq, vecs[0])
                dcq, dgq = vjp_q(d_qn)
                _, vjp_kv = jax.vjp(_rms, ckv, vecs[1])
                dckv, dgkv = vjp_kv(d_kvn)
                return [dvf + dvb, dcq, _rope_t(d_krr, cos, sin), _rope_t(dqf + dqb, cos, sin),
                        _rope_t(dkf + dkb, cos, sin) * RET_SCALE, dckv], [dgq, dgkv]

            (drv, dcq, dkr, drq, drk, dckv), (dgq, dgkv) = _rowwise(
                f"ev_prep_bwd{l}", prep_bwd_body,
                [(z, 384, 4), (z, 256, 12), (cos_t, LANES, 0), (sin_t, LANES, 0), (dq_f, 512, 0), (dq_b, 512, 0),
                 (dk_f, 512, 0), (dk_b, 512, 0), (dv_f, 512, 0), (dv_b, 512, 0), (dqn, 384, 0), (dkvn, 256, 0),
                 (dkrr, LANES, 0)],
                [ev_q_norm_g[i:i + 1], ev_kv_norm_g[i:i + 1]],
                [(512, BF16), (384, BF16), (LANES, BF16), (512, BF16), (512, BF16), (256, BF16)], [(1, 384), (1, 256)])
            g_qn[i], g_kvn[i] = dgq, dgkv
            dz = [drv, drg, dmg, dcq, dkr, drq, drk, dckv]
            dz_off = [ZL_EV[nm][0] for nm in ("rv", "rg", "mg", "cq", "kr", "rq", "rk", "ckv")]
            g_rv, g_rg, g_mg, g_cq, g_kr, g_rq, g_rk, g_ckv = _mm_cols_tn(f"in_wgrad{l}", dz, sv["h"])
            d_win = jnp.concatenate([g_rq[128 * hh:128 * hh + 64] for hh in range(4)]
                                    + [g_rk[128 * hh:128 * hh + 64] for hh in range(4)]
                                    + [g_rv, g_rg, g_cq, g_ckv, g_kr[:64], g_mg], axis=0)
            pending = [(d_win.reshape(N_DEV, 344, D), True), (_uq_unlayout(d_uq).reshape(N_DEV, 96, 384), True),
                       (_ukv_unlayout(d_ukv).reshape(N_DEV, 128, 256), True)]
            if l > 0:
                pending.append((d_wout.reshape(N_DEV, 128, D), True))
        else:
            win_t, _ = od_w[i]

            def od_post_bwd_body(rows, vecs):
                u2, gg, da = rows
                _, vjp = jax.vjp(_od_post_fn, u2, gg, vecs[0], vecs[1])
                du2, dgg, dlg, dlb = vjp(da)
                return [du2, dgg], [dlg, dlb, _colsum(dgg)]

            (du2, dg_gate), (dlg, dlb, dbg) = _rowwise(
                f"od_post_bwd{l}", od_post_bwd_body, [(sv["u2"], D, 0), (z, D, 2), (dact, D, 0)],
                [ln_g_all[i], ln_b_all[i]], [(D, F32), (D, BF16)], [(1, D), (1, D), (1, D)])
            g_ln_g[i], g_ln_b[i] = dlg, dlb
            du, d_dw, d_dwb, got = _conv_bwd(f"conv_bwd{l}", sv["u"], du2, dw_w_all[i], comm=pending)
            if pending:
                recv_ev[i + 1] = got
            g_dw_w[i], g_dw_b[i] = d_dw[:CONV_K], d_dwb

            def glu_bwd_body(rows, vecs):
                a, b, d_u = rows
                _, vjp = jax.vjp(_glu_fn, a, b)
                d_a, d_b = vjp(d_u)
                return [d_a, d_b], [_colsum(d_a), _colsum(d_b)]

            (d_a, d_b), (dba, dbb) = _rowwise(f"glu_bwd{l}", glu_bwd_body, [(z, D, 0), (z, D, 1), (du, D, 0)], [],
                                              [(D, BF16), (D, BF16)], [(1, D), (1, D)])
            g_b_in[i] = jnp.concatenate([dba, dbb, dbg], axis=1)
            dz, dz_off = [d_a, d_b, dg_gate], [0, D, 2 * D]
            d_win = jnp.concatenate(_mm_cols_tn(f"in_wgrad{l}", dz, sv["h"]), axis=0)
            pending = [(d_win.reshape(N_DEV, 384, D), True), (d_wout.reshape(N_DEV, 128, D), True)]
        if l == 0:
            dh, got = _mm_cols_nn(f"in_dgrad{l}", dz, dz_off, win_t, comm=pending)
            recv_ev[0] = list(got) + list(wout_recv)
        else:
            dh = _mm_cols_nn(f"in_dgrad{l}", dz, dz_off, win_t)

        def pre_bwd_body(rows, vecs):
            xv, d_h, dxn = rows
            _, vjp = jax.vjp(_pre_fn, xv, *vecs)
            d_x, dg, dsc, dsh = vjp(d_h)
            return [d_x + dxn], [dg, dsc, dsh]

        (dx,), (dpre, dscale, dshift) = _rowwise(f"pre_bwd{l}", pre_bwd_body, [(sv["x"], D, 0), (dh, D, 0), (dx, D, 0)],
                                                 [pre_g[l:l + 1], scale[l], shift[l]], [(D, F32)], [(1, D)] * 3)
        g_pre[l] = dpre
        g_mod[l] = jnp.concatenate([dshift, dscale, dgate], axis=1)

    grad_x = dx.reshape(1, s, D)

    end_parts = [jnp.concatenate(g_mod, axis=0), jnp.concatenate(g_pre, axis=0), jnp.concatenate(g_post, axis=0),
                 jnp.concatenate(g_dec_f, axis=0), jnp.concatenate(g_dec_b, axis=0), jnp.concatenate(g_qn, axis=0),
                 jnp.concatenate(g_kvn, axis=0), jnp.concatenate(g_b_in, axis=0), jnp.stack(g_dw_w),
                 jnp.concatenate(g_dw_b, axis=0), jnp.concatenate(g_ln_g, axis=0), jnp.concatenate(g_ln_b, axis=0)]
    end_shapes = [p.shape for p in end_parts]
    end_sizes = [int(np.prod(sh)) for sh in end_shapes]
    end_len = -(-sum(end_sizes) // 1024) * 1024
    end_vec = jnp.concatenate([p.reshape(-1) for p in end_parts] + [jnp.zeros((end_len - sum(end_sizes),), F32)])
    end_all = _exchange("gather_end", [end_vec.reshape(-1, LANES)], False)[0].reshape(N_DEV, end_len)
    eo = np.cumsum([0] + end_sizes)
    ends = [end_all[:, eo[j]:eo[j + 1]].reshape((N_DEV,) + tuple(end_shapes[j])) for j in range(len(end_parts))]
    (dmod_all, pre_all, post_all, decf_all, decb_all, qn_all, kvn_all, bin_all, dww_all, dwb_all, lng_all,
     lnb_all) = ends

    def pack_rep(*ts):
        lead = ts[0].ndim - 2
        return jnp.concatenate([t.reshape(t.shape[:lead] + (-1,)) for t in ts], axis=-1)

    rep_sizes = [DEPTH * 3 * D, DEPTH * D, DEPTH * D, 8, 8, 2 * 384, 2 * 256]
    rep_len = -(-sum(rep_sizes) // 1024) * 1024
    rep_pad = rep_len - sum(rep_sizes)

    def rep_rows(flat):
        padz = jnp.zeros(flat.shape[:-1] + (rep_pad,), F32)
        return jnp.concatenate([flat, padz], axis=-1).reshape(flat.shape[:-1] + (rep_len // LANES, LANES))

    rep_w = rep_rows(pack_rep(ada_b, pre_g, post_g, ev_dec_f, ev_dec_b, ev_q_norm_g, ev_kv_norm_g))
    rep_m = rep_rows(pack_rep(m_ada_b, m_pre_g, m_post_g, m_ev_dec_f, m_ev_dec_b, m_ev_q_norm_g, m_ev_kv_norm_g))
    rep_v = rep_rows(pack_rep(v_ada_b, v_pre_g, v_post_g, v_ev_dec_f, v_ev_dec_b, v_ev_q_norm_g, v_ev_kv_norm_g))
    rep_g = rep_rows(pack_rep(dmod_all, pre_all, post_all, decf_all, decb_all, qn_all, kvn_all))
    rep_out = _adamw("adamw_rep", rep_w, rep_m, rep_v, rep_g)
    ro = np.cumsum([0] + rep_sizes)
    rep_shapes = [(DEPTH, 3 * D), (DEPTH, D), (DEPTH, D), (2, 4), (2, 4), (2, 384), (2, 256)]

    def unpack_rep(t):
        flat = t.reshape(-1)
        return [flat[ro[j]:ro[j + 1]].reshape(rep_shapes[j]) for j in range(len(rep_shapes))]

    rep_res = [unpack_rep(t) for t in rep_out]

    def my_cols(t, width):
        return lax.dynamic_slice_in_dim(t, me * width, width, axis=t.ndim - 1)

    def vec_adamw(name, w, m, v, g_all, width):
        g = my_cols(g_all, width)
        r = _adamw(name, w.reshape(-1, width), m.reshape(-1, width), v.reshape(-1, width),
                   g.reshape(N_DEV, -1, width))
        return [t.reshape(w.shape) for t in r]

    res_b_in = vec_adamw("adamw_b_in", od_b_in, m_od_b_in, v_od_b_in, bin_all, 384)
    res_dw_w = vec_adamw("adamw_dw_w", od_dw_w, m_od_dw_w, v_od_dw_w, dww_all, LANES)
    res_dw_b = vec_adamw("adamw_dw_b", od_dw_b, m_od_dw_b, v_od_dw_b, dwb_all, LANES)
    res_ln_g = vec_adamw("adamw_ln_g", od_ln_g, m_od_ln_g, v_od_ln_g, lng_all, LANES)
    res_ln_b = vec_adamw("adamw_ln_b", od_ln_b, m_od_ln_b, v_od_ln_b, lnb_all, LANES)

    dmod_mine = my_cols(dmod_all, 384).transpose(1, 0, 2)
    dmod_pad = jnp.concatenate([dmod_mine, jnp.zeros((DEPTH, LANES - N_DEV, 384), F32)], axis=1)
    c_all_t = jnp.concatenate([c_all.T, jnp.zeros((D, LANES - N_DEV), F32)], axis=1)
    g_ada_w = _ada_w_grad("ada_w_grad", c_all_t, dmod_pad)
    res_ada_w = [t.reshape(ada_w.shape) for t in
                 _adamw("adamw_ada_w", ada_w.reshape(-1, 384), m_ada_w.reshape(-1, 384), v_ada_w.reshape(-1, 384),
                        g_ada_w.reshape(-1, 384))]

    ev_sh = [[_sum_parts(f"sum_g_ev{i}_{j}", r) for j, r in enumerate(recv_ev[i])] for i in range(2)]
    od_sh = [[_sum_parts(f"sum_g_od{i}_{j}", r) for j, r in enumerate(recv_od[i])] for i in range(2)]

    def mat_adamw(name, w, m, v, g):
        r = _adamw(name, w.reshape(-1, w.shape[-1]), m.reshape(-1, w.shape[-1]), v.reshape(-1, w.shape[-1]),
                   g.reshape(-1, w.shape[-1]))
        return [t.reshape(w.shape) for t in r]

    res_ev_w_in = mat_adamw("adamw_ev_w_in", ev_w_in, m_ev_w_in, v_ev_w_in, jnp.stack([ev_sh[i][0].T for i in range(2)]))
    res_ev_w_uq = mat_adamw("adamw_ev_w_uq", ev_w_uq, m_ev_w_uq, v_ev_w_uq, jnp.stack([ev_sh[i][1].T for i in range(2)]))
    res_ev_w_ukv = mat_adamw("adamw_ev_w_ukv", ev_w_ukv, m_ev_w_ukv, v_ev_w_ukv,
                             jnp.stack([ev_sh[i][2].T for i in range(2)]))
    res_ev_w_out = mat_adamw("adamw_ev_w_out", ev_w_out, m_ev_w_out, v_ev_w_out, jnp.stack([ev_sh[i][3] for i in range(2)]))
    res_od_w_in = mat_adamw("adamw_od_w_in", od_w_in, m_od_w_in, v_od_w_in, jnp.stack([od_sh[i][0].T for i in range(2)]))
    res_od_w_out = mat_adamw("adamw_od_w_out", od_w_out, m_od_w_out, v_od_w_out, jnp.stack([od_sh[i][1] for i in range(2)]))

    per_weight = [res_ada_w] + [[rep_res[t][j] for t in range(4)] for j in range(3)]
    per_weight += [res_ev_w_in, [rep_res[t][3] for t in range(4)], [rep_res[t][4] for t in range(4)],
                   [rep_res[t][5] for t in range(4)], res_ev_w_uq, [rep_res[t][6] for t in range(4)], res_ev_w_ukv,
                   res_ev_w_out, res_od_w_in, res_b_in, res_dw_w, res_dw_b, res_ln_g, res_ln_b, res_od_w_out]
    outs = [loss, grad_x]
    for t in range(4):
        outs += [pw[t] for pw in per_weight]
    return tuple(outs)
```

```python
import functools

import numpy as np
import jax
import jax.numpy as jnp
from jax import lax
from jax.experimental import pallas as pl
from jax.experimental.pallas import tpu as pltpu

F32 = jnp.float32
BF16 = jnp.bfloat16
MESH = pl.DeviceIdType.MESH

N_DEV = 8
D = 1024
DEPTH = 4
EPS = 1e-6
RET_HEADS = 4
MLA_HEADS = 4
RET_SCALE = 64 ** -0.5
MLA_SCALE = 192 ** -0.5
CONV_K = 31
CONV_HALO = 16
ROPE_BASE = 10000.0

ADAM_LR = 0.001
ADAM_B1 = 0.9
ADAM_B2 = 0.999
ADAM_EPS = 1e-08
ADAM_WD = 0.01
ADAM_STEP = 10

LANES = 128
VMEM_LIMIT = 48 * 1024 * 1024

ZL_EV = dict(rv=(0, 512), rg=(512, 512), mg=(1024, 512), cq=(1536, 384), kr=(1920, 128),
             rq=(2048, 512), rk=(2560, 512), ckv=(3072, 256))
ZW_EV = 3328
ZW_OD = 3072


def _cparams(sem, vmem=VMEM_LIMIT):
    return pltpu.CompilerParams(dimension_semantics=sem, vmem_limit_bytes=vmem)


def _pick(n, prefs):
    for p in prefs:
        if n % p == 0:
            return p
    return n


def _sigmoid(x):
    return 0.5 * (jnp.tanh(0.5 * x) + 1.0)


def _silu(x):
    return x * _sigmoid(x)


def _rms(x, g):
    return x * lax.rsqrt(jnp.mean(x * x, axis=-1, keepdims=True) + EPS) * g


def _log_sigmoid(x):
    return jnp.minimum(x, 0.0) - jnp.log(1.0 + jnp.exp(jnp.minimum(x, -x)))


def _tile_lanes(t, width):
    reps = width // t.shape[1]
    return t if reps == 1 else jnp.concatenate([t] * reps, axis=1)


def _rot_half(x):
    w = x.shape[1]
    lane = lax.broadcasted_iota(jnp.int32, x.shape, 1)
    first = jnp.bitwise_and(lane, 63) < 32
    return jnp.where(first, pltpu.roll(x, w - 32, 1), pltpu.roll(x, 32, 1))


def _rope(x, cos, sin):
    w = x.shape[1]
    return x * _tile_lanes(cos, w) + _rot_half(x) * _tile_lanes(sin, w)


def _rope_t(dy, cos, sin):
    w = dy.shape[1]
    return dy * _tile_lanes(cos, w) + _rot_half(dy * _tile_lanes(sin, w))


_DN = {"nn": (((1,), (0,)), ((), ())), "nt": (((1,), (1,)), ((), ())), "tn": (((0,), (0,)), ((), ()))}


def _dot(a, b, mode):
    return lax.dot_general(a.astype(BF16), b.astype(BF16), _DN[mode], preferred_element_type=F32)


@functools.partial(jax.custom_vjp, nondiff_argnums=(2,))
def _bdot(a, b, mode):
    return _dot(a, b, mode)


def _bdot_fwd(a, b, mode):
    return _dot(a, b, mode), (a, b)


def _bdot_bwd(mode, res, g):
    a, b = res
    if mode == "nn":
        return _dot(g, b, "nt"), _dot(a, g, "tn")
    if mode == "nt":
        return _dot(g, b, "nn"), _dot(g, a, "tn")
    return _dot(b, g, "nt"), _dot(a, g, "nn")


_bdot.defvjp(_bdot_fwd, _bdot_bwd)


def _rowwise(name, body, row_ins, vec_ins, row_outs, red_outs=(), tile=512):
    s = row_ins[0][0].shape[0]
    tile = min(tile, s)
    nr, nv, no = len(row_ins), len(vec_ins), len(row_outs)

    def kern(*refs):
        rows = [r[...].astype(F32) if r.dtype == BF16 else r[...] for r in refs[:nr]]
        vecs = [r[...] for r in refs[nr:nr + nv]]
        outs, reds = body(rows, vecs)
        for r, o in zip(refs[nr + nv:nr + nv + no], outs):
            r[...] = o.astype(r.dtype)
        red_refs = refs[nr + nv + no:]
        if red_refs:
            @pl.when(pl.program_id(0) == 0)
            def _():
                for r in red_refs:
                    r[...] = jnp.zeros(r.shape, r.dtype)
            for r, v in zip(red_refs, reds):
                r[...] += v

    in_specs = [pl.BlockSpec((tile, w), (lambda i, cb=cb: (i, cb))) for (_, w, cb) in row_ins]
    in_specs += [pl.BlockSpec(v.shape, (lambda i, nd=v.ndim: (0,) * nd)) for v in vec_ins]
    out_specs = [pl.BlockSpec((tile, w), lambda i: (i, 0)) for (w, _) in row_outs]
    out_specs += [pl.BlockSpec(sh, lambda i: (0, 0)) for sh in red_outs]
    out_shape = [jax.ShapeDtypeStruct((s, w), dt) for (w, dt) in row_outs]
    out_shape += [jax.ShapeDtypeStruct(sh, F32) for sh in red_outs]
    res = pl.pallas_call(
        kern, name=name, grid=(s // tile,), in_specs=in_specs, out_specs=out_specs, out_shape=out_shape,
        compiler_params=_cparams(("arbitrary",)),
    )(*[a for (a, _, _) in row_ins], *vec_ins)
    return res[:no], res[no:]


def _mm(name, a, b, mode, out_dtype=F32, bias=None, comm=None):
    if mode == "tn":
        k, m = a.shape
        n = b.shape[1]
        tm = m if m <= 1664 else m // 2
        tk = min(k, 1024)
        nk = k // tk

        def kern(a_ref, b_ref, o_ref, acc_ref):
            kk = pl.program_id(1)
            part = _dot(a_ref[...], b_ref[...], "tn")

            @pl.when(kk == 0)
            def _():
                acc_ref[...] = part

            @pl.when(kk > 0)
            def _():
                acc_ref[...] += part

            @pl.when(kk == nk - 1)
            def _():
                o_ref[...] = acc_ref[...].astype(o_ref.dtype)

        return pl.pallas_call(
            kern, name=name, grid=(m // tm, nk),
            in_specs=[pl.BlockSpec((tk, tm), lambda i, kk: (kk, i)),
                      pl.BlockSpec((tk, n), lambda i, kk: (kk, 0))],
            out_specs=pl.BlockSpec((tm, n), lambda i, kk: (i, 0)),
            out_shape=jax.ShapeDtypeStruct((m, n), out_dtype),
            scratch_shapes=[pltpu.VMEM((tm, n), F32)],
            compiler_params=_cparams(("parallel", "arbitrary")),
        )(a, b)

    m, k = a.shape
    n = b.shape[1] if mode == "nn" else b.shape[0]
    tm = min(m, 1024)
    tn = n if n <= 1664 else n // 2
    has_bias = bias is not None

    def kern(*refs):
        a_ref, b_ref = refs[0], refs[1]
        o_ref = refs[-1]
        r = _dot(a_ref[...], b_ref[...], mode)
        if has_bias:
            r = r + refs[2][...]
        o_ref[...] = r.astype(o_ref.dtype)

    b_spec = (pl.BlockSpec((k, tn), lambda i, j: (0, j)) if mode == "nn"
              else pl.BlockSpec((tn, k), lambda i, j: (j, 0)))
    in_specs = [pl.BlockSpec((tm, k), lambda i, j: (i, 0)), b_spec]
    args = [a, b]
    if has_bias:
        in_specs.append(pl.BlockSpec((1, tn), lambda i, j: (0, j)))
        args.append(bias)
    outs, got = _call(kern, name=name, grid=(m // tm, n // tn), in_specs=in_specs,
                      out_specs=[pl.BlockSpec((tm, tn), lambda i, j: (i, j))],
                      out_shape=[jax.ShapeDtypeStruct((m, n), out_dtype)], args=args, sem=("parallel", "parallel"),
                      comm=comm)
    return (outs[0], got) if comm else outs[0]


def _mm_cols_nn(name, pieces, offsets, b, comm=None):
    m = pieces[0].shape[0]
    n = b.shape[1]
    tm = min(m, 1024)
    np_ = len(pieces)

    def kern(*refs):
        acc = _dot(refs[0][...], refs[np_][...], "nn")
        for p in range(1, np_):
            acc = acc + _dot(refs[p][...], refs[np_ + p][...], "nn")
        refs[2 * np_][...] = acc.astype(BF16)

    in_specs = [pl.BlockSpec((tm, a.shape[1]), lambda i: (i, 0)) for a in pieces]
    in_specs += [pl.BlockSpec((a.shape[1], n), (lambda i, r=off // a.shape[1]: (r, 0))) for a, off in zip(pieces, offsets)]
    outs, got = _call(kern, name=name, grid=(m // tm,), in_specs=in_specs,
                      out_specs=[pl.BlockSpec((tm, n), lambda i: (i, 0))],
                      out_shape=[jax.ShapeDtypeStruct((m, n), BF16)], args=list(pieces) + [b] * np_, sem=("parallel",),
                      comm=comm)
    return (outs[0], got) if comm else outs[0]


def _mm_cols_tn(name, pieces, b):
    k, n = b.shape
    tk = min(k, 512)
    nk = k // tk
    np_ = len(pieces)

    def kern(*refs):
        b_ref = refs[np_]
        o_refs, acc_refs = refs[np_ + 1:2 * np_ + 1], refs[2 * np_ + 1:]
        kk = pl.program_id(0)

        @pl.when(kk == 0)
        def _():
            for r in acc_refs:
                r[...] = jnp.zeros(r.shape, F32)

        bb = b_ref[...]
        for p in range(np_):
            acc_refs[p][...] += _dot(refs[p][...], bb, "tn")

        @pl.when(kk == nk - 1)
        def _():
            for o, r in zip(o_refs, acc_refs):
                o[...] = r[...].astype(o.dtype)

    return pl.pallas_call(
        kern, name=name, grid=(nk,),
        in_specs=[pl.BlockSpec((tk, a.shape[1]), lambda kk: (kk, 0)) for a in pieces] + [pl.BlockSpec((tk, n), lambda kk: (kk, 0))],
        out_specs=[pl.BlockSpec((a.shape[1], n), lambda kk: (0, 0)) for a in pieces],
        out_shape=[jax.ShapeDtypeStruct((a.shape[1], n), BF16) for a in pieces],
        scratch_shapes=[pltpu.VMEM((a.shape[1], n), F32) for a in pieces],
        compiler_params=_cparams(("arbitrary",)),
    )(*pieces, b)


def _peers():
    mx, my, mc = lax.axis_index("x"), lax.axis_index("y"), lax.axis_index("c")
    me = 4 * mx + 2 * my + mc
    out = []
    for k in range(1, N_DEV):
        px = 1 - mx if (k >> 2) & 1 else mx
        py = 1 - my if (k >> 1) & 1 else my
        pc = 1 - mc if k & 1 else mc
        out.append((k, (px, py, pc), 4 * px + 2 * py + pc))
    return me, out


def _xchg_copies(x_refs, out_refs, scatter, send_sems, recv_sems, local_sems):
    me, peers = _peers()
    local, out, arrive = [], [], []
    for a, (x, o, sc) in enumerate(zip(x_refs, out_refs, scatter)):
        mine = x.at[me] if sc else x
        local.append(pltpu.make_async_copy(mine, o.at[me], local_sems.at[a]))
        for k, dev, p in peers:
            out.append(pltpu.make_async_remote_copy(
                src_ref=x.at[p] if sc else x, dst_ref=o.at[me],
                send_sem=send_sems.at[a, k - 1], recv_sem=recv_sems.at[a, k - 1],
                device_id=dev, device_id_type=MESH))
            arrive.append(pltpu.make_async_remote_copy(
                src_ref=mine, dst_ref=o.at[p],
                send_sem=send_sems.at[a, k - 1], recv_sem=recv_sems.at[a, k - 1],
                device_id=dev, device_id_type=MESH))
    return local, out, arrive


def _xchg_start(*args):
    local, out, _ = _xchg_copies(*args)
    for cp in local + out:
        cp.start()


def _xchg_wait(*args):
    local, out, arrive = _xchg_copies(*args)
    for cp in out:
        cp.wait_send()
    for cp in arrive:
        cp.wait_recv()
    for cp in local:
        cp.wait()


def _call(kern, *, name, grid, in_specs, out_specs, out_shape, args, sem, scratch_shapes=(), vmem=VMEM_LIMIT,
          comm=None):
    if not comm:
        outs = pl.pallas_call(kern, name=name, grid=grid, in_specs=in_specs, out_specs=out_specs, out_shape=out_shape,
                              scratch_shapes=list(scratch_shapes), compiler_params=_cparams(sem, vmem))(*args)
        return outs, []
    n, ni, no, ns = len(comm), len(in_specs), len(out_specs), len(scratch_shapes)
    xs = [x for x, _ in comm]
    scatter = [sc for _, sc in comm]

    def body(*refs):
        ins, x_refs = refs[:ni], refs[ni:ni + n]
        outs, out_refs = refs[ni + n:ni + n + no], refs[ni + n + no:ni + 2 * n + no]
        scr = refs[ni + 2 * n + no:ni + 2 * n + no + ns]
        sems = refs[ni + 2 * n + no + ns:]
        ids = [pl.program_id(d) for d in range(len(grid))]
        first = functools.reduce(jnp.logical_and, [i == 0 for i in ids])
        last = functools.reduce(jnp.logical_and, [i == g - 1 for i, g in zip(ids, grid)])

        @pl.when(first)
        def _():
            _xchg_start(x_refs, out_refs, scatter, *sems)

        kern(*ins, *outs, *scr)

        @pl.when(last)
        def _():
            _xchg_wait(x_refs, out_refs, scatter, *sems)

    any_spec = pl.BlockSpec(memory_space=pl.ANY)
    res = pl.pallas_call(
        body, name=name, grid=grid,
        in_specs=list(in_specs) + [any_spec] * n, out_specs=list(out_specs) + [any_spec] * n,
        out_shape=list(out_shape) + [jax.ShapeDtypeStruct((N_DEV,) + tuple(x.shape[1:] if sc else x.shape), x.dtype)
                                     for x, sc in comm],
        scratch_shapes=list(scratch_shapes) + [pltpu.SemaphoreType.DMA((n, N_DEV - 1)),
                                               pltpu.SemaphoreType.DMA((n, N_DEV - 1)), pltpu.SemaphoreType.DMA((n,))],
        compiler_params=pltpu.CompilerParams(dimension_semantics=("arbitrary",) * len(grid), vmem_limit_bytes=vmem,
                                             has_side_effects=True),
    )(*args, *xs)
    return res[:no], res[no:]


def _exchange(name, xs, scatter):
    def nothing():
        pass

    return _call(nothing, name=name, grid=(1,), in_specs=[], out_specs=[], out_shape=[], args=[], sem=("arbitrary",),
                 comm=[(x, scatter) for x in xs])[1]


def _sum_parts(name, x):
    p, r, c = x.shape
    tr = r if r * c * p * x.dtype.itemsize <= (8 << 20) else _pick(r, (256, 128, 64, 16))

    def kern(x_ref, o_ref):
        acc = x_ref[0].astype(F32)
        for i in range(1, p):
            acc = acc + x_ref[i].astype(F32)
        o_ref[...] = acc

    return pl.pallas_call(
        kern, name=name, grid=(r // tr,),
        in_specs=[pl.BlockSpec((p, tr, c), lambda i: (0, i, 0))],
        out_specs=pl.BlockSpec((tr, c), lambda i: (i, 0)),
        out_shape=jax.ShapeDtypeStruct((r, c), F32),
        compiler_params=_cparams(("parallel",)),
    )(x)


def _adamw(name, w, m, v, g):
    r, c = w.shape
    parts = g.shape[0] if g.ndim == 3 else 0
    tr = 512 if (r > 512 and r % 512 == 0) else r

    def kern(w_ref, m_ref, v_ref, g_ref, go_ref, d_ref, mo_ref, vo_ref):
        if parts:
            gg = g_ref[0]
            for i in range(1, parts):
                gg = gg + g_ref[i]
        else:
            gg = g_ref[...]
        mm = ADAM_B1 * m_ref[...] + (1.0 - ADAM_B1) * gg
        vv = ADAM_B2 * v_ref[...] + (1.0 - ADAM_B2) * (gg * gg)
        m_hat = mm / (1.0 - ADAM_B1 ** ADAM_STEP)
        v_hat = vv / (1.0 - ADAM_B2 ** ADAM_STEP)
        go_ref[...] = gg
        d_ref[...] = -ADAM_LR * (m_hat / (jnp.sqrt(v_hat) + ADAM_EPS) + ADAM_WD * w_ref[...])
        mo_ref[...] = mm
        vo_ref[...] = vv

    spec = pl.BlockSpec((tr, c), lambda i: (i, 0))
    gspec = pl.BlockSpec((parts, tr, c), lambda i: (0, i, 0)) if parts else spec
    sh = jax.ShapeDtypeStruct((r, c), F32)
    return pl.pallas_call(
        kern, name=name, grid=(r // tr,), in_specs=[spec, spec, spec, gspec],
        out_specs=[spec] * 4, out_shape=[sh] * 4,
        compiler_params=_cparams(("parallel",)),
    )(w, m, v, g)


def _ret_tables(dec_cc, dec_cd, dec_dd, reverse):
    c = dec_cc.shape[0]
    row = lax.broadcasted_iota(jnp.int32, (c, c), 0).astype(F32)
    col = lax.broadcasted_iota(jnp.int32, (c, c), 1).astype(F32)
    pos = lax.broadcasted_iota(jnp.int32, (c, LANES), 0).astype(F32)
    if reverse:
        diff, mask = col - row, col > row
        q_exp, k_exp = c - pos, pos
    else:
        diff, mask = row - col, row >= col
        q_exp, k_exp = pos + 1.0, c - 1.0 - pos
    decay = jnp.where(mask, jnp.exp(_log_sigmoid(dec_cc) * jnp.maximum(diff, 0.0)), 0.0)
    lam_cd = _log_sigmoid(dec_cd)
    return decay, jnp.exp(lam_cd * q_exp), jnp.exp(lam_cd * k_exp), jnp.exp(_log_sigmoid(dec_dd) * float(c))


def _ret_chunk(q, k, v, st, decay, qw, kw, sd):
    scores = _bdot(q, k, "nt") * decay
    o = _bdot(scores, v, "nn") + _bdot(q * qw, st, "nn")
    st_new = st * sd + _bdot(k * kw, v, "tn")
    return o, st_new


def _ret_dec(dec_ref, h, c):
    d = dec_ref[:, h:h + 1]
    return (jnp.broadcast_to(d, (c, c)), jnp.broadcast_to(d, (c, LANES)), jnp.broadcast_to(d, (LANES, LANES)))


def _ret_table_scratch(c):
    return [pltpu.VMEM((RET_HEADS, c, c), F32), pltpu.VMEM((RET_HEADS, c, LANES), F32),
            pltpu.VMEM((RET_HEADS, c, LANES), F32), pltpu.VMEM((RET_HEADS, LANES, LANES), F32)]


def _ret_fwd(name, q, k, z, dec, reverse, chunk):
    s = q.shape[0]
    chunk = min(chunk, s)
    n = s // chunk
    cmap = (lambda i: (n - 1 - i, 0)) if reverse else (lambda i: (i, 0))
    smap = (lambda i: (n - 1 - i, 0, 0, 0)) if reverse else (lambda i: (i, 0, 0, 0))

    def kern(q_ref, k_ref, v_ref, dec_ref, o_ref, st_out_ref, st_ref, *tab_refs):
        @pl.when(pl.program_id(0) == 0)
        def _():
            st_ref[...] = jnp.zeros(st_ref.shape, F32)
            for h in range(RET_HEADS):
                for r, t in zip(tab_refs, _ret_tables(*_ret_dec(dec_ref, h, chunk), reverse)):
                    r[h] = t

        for h in range(RET_HEADS):
            sl = slice(LANES * h, LANES * (h + 1))
            st = st_ref[h]
            st_out_ref[h] = st
            o, st_new = _ret_chunk(q_ref[:, sl].astype(F32), k_ref[:, sl].astype(F32), v_ref[:, sl].astype(F32),
                                   st, *[r[h] for r in tab_refs])
            o_ref[:, sl] = o
            st_ref[h] = st_new

    return pl.pallas_call(
        kern, name=name, grid=(n,),
        in_specs=[pl.BlockSpec((chunk, 512), cmap), pl.BlockSpec((chunk, 512), cmap),
                  pl.BlockSpec((chunk, 512), cmap), pl.BlockSpec((1, RET_HEADS), lambda i: (0, 0))],
        out_specs=[pl.BlockSpec((chunk, 512), cmap), pl.BlockSpec((None, RET_HEADS, LANES, LANES), smap)],
        out_shape=[jax.ShapeDtypeStruct((s, 512), F32), jax.ShapeDtypeStruct((n, RET_HEADS, LANES, LANES), F32)],
        scratch_shapes=[pltpu.VMEM((RET_HEADS, LANES, LANES), F32)] + _ret_table_scratch(chunk),
        compiler_params=_cparams(("arbitrary",)),
    )(q, k, z, dec)


def _ret_bwd(name, q, k, z, dec, states, do, reverse, chunk):
    s = q.shape[0]
    chunk = min(chunk, s)
    n = s // chunk
    cmap = (lambda i: (i, 0)) if reverse else (lambda i: (n - 1 - i, 0))
    smap = (lambda i: (i, 0, 0, 0)) if reverse else (lambda i: (n - 1 - i, 0, 0, 0))

    def kern(q_ref, k_ref, v_ref, dec_ref, st_in_ref, do_ref, dq_ref, dk_ref, dv_ref, ddec_ref, dst_ref, *scr):
        tab_refs, gtab_refs = scr[:4], scr[4:]
        step = pl.program_id(0)

        @pl.when(step == 0)
        def _():
            dst_ref[...] = jnp.zeros(dst_ref.shape, F32)
            for r in gtab_refs:
                r[...] = jnp.zeros(r.shape, F32)
            for h in range(RET_HEADS):
                for r, t in zip(tab_refs, _ret_tables(*_ret_dec(dec_ref, h, chunk), reverse)):
                    r[h] = t

        for h in range(RET_HEADS):
            sl = slice(LANES * h, LANES * (h + 1))
            _, vjp = jax.vjp(_ret_chunk, q_ref[:, sl].astype(F32), k_ref[:, sl].astype(F32), v_ref[:, sl].astype(F32),
                             st_in_ref[h], *[r[h] for r in tab_refs])
            grads = vjp((do_ref[:, sl], dst_ref[h]))
            dq_ref[:, sl] = grads[0]
            dk_ref[:, sl] = grads[1]
            dv_ref[:, sl] = grads[2]
            dst_ref[h] = grads[3]
            for r, g in zip(gtab_refs, grads[4:]):
                r[h] += g

        @pl.when(step == n - 1)
        def _():
            lane = lax.broadcasted_iota(jnp.int32, (1, LANES), 1)
            ddec = jnp.zeros((1, LANES), F32)
            for h in range(RET_HEADS):
                _, vjp_t = jax.vjp(functools.partial(_ret_tables, reverse=reverse), *_ret_dec(dec_ref, h, chunk))
                parts = vjp_t(tuple(r[h] for r in gtab_refs))
                tot = sum(jnp.sum(jnp.sum(p, axis=1, keepdims=True), axis=0, keepdims=True) for p in parts)
                ddec = ddec + jnp.where(lane == h, tot, 0.0)
            ddec_ref[...] = ddec

    cspec = pl.BlockSpec((chunk, 512), cmap)
    return pl.pallas_call(
        kern, name=name, grid=(n,),
        in_specs=[cspec, cspec, cspec, pl.BlockSpec((1, RET_HEADS), lambda i: (0, 0)),
                  pl.BlockSpec((None, RET_HEADS, LANES, LANES), smap), cspec],
        out_specs=[cspec, cspec, cspec, pl.BlockSpec((1, LANES), lambda i: (0, 0))],
        out_shape=[jax.ShapeDtypeStruct((s, 512), F32)] * 3 + [jax.ShapeDtypeStruct((1, LANES), F32)],
        scratch_shapes=[pltpu.VMEM((RET_HEADS, LANES, LANES), F32)] + _ret_table_scratch(chunk) * 2,
        compiler_params=_cparams(("arbitrary",)),
    )(q, k, z, dec, states, do)


def _fa_fwd(name, q, k, vx, tq, tk, nsub, comm=None):
    s = q.shape[0]
    tq, tk = min(tq, s), min(tk, s)
    nk = s // tk
    sq = tq // nsub

    def kern(q_ref, k_ref, v_ref, o_ref, lse_ref, m_ref, acc_ref):
        j = pl.program_id(2)

        @pl.when(j == 0)
        def _():
            m_ref[...] = jnp.full(m_ref.shape, -jnp.inf, F32)
            acc_ref[...] = jnp.zeros(acc_ref.shape, F32)

        kb, vb = k_ref[...], v_ref[...]
        for c in range(nsub):
            rows = pl.ds(c * sq, sq)
            sc = _dot(q_ref[rows, :], kb, "nt")
            m_prev = m_ref[rows, :]
            m_new = jnp.maximum(m_prev, jnp.max(sc, axis=1, keepdims=True))
            alpha = jnp.exp(m_prev - m_new)
            p = jnp.exp(sc - m_new)
            acc_ref[rows, :] = alpha * acc_ref[rows, :] + _dot(p, vb, "nn")
            m_ref[rows, :] = m_new

        @pl.when(j == nk - 1)
        def _():
            den = acc_ref[:, LANES:]
            o_ref[...] = acc_ref[:, :LANES] / den
            lse_ref[...] = m_ref[...] + jnp.log(den[:, :1])

    (o, lse), got = _call(
        kern, name=name, grid=(MLA_HEADS, s // tq, nk),
        in_specs=[pl.BlockSpec((tq, 256), lambda h, i, j: (i, h)),
                  pl.BlockSpec((tk, 256), lambda h, i, j: (j, h)),
                  pl.BlockSpec((tk, 256), lambda h, i, j: (j, h))],
        out_specs=[pl.BlockSpec((tq, LANES), lambda h, i, j: (i, h)),
                   pl.BlockSpec((None, tq, 1), lambda h, i, j: (h, i, 0))],
        out_shape=[jax.ShapeDtypeStruct((s, 512), F32), jax.ShapeDtypeStruct((MLA_HEADS, s, 1), F32)],
        scratch_shapes=[pltpu.VMEM((tq, 1), F32), pltpu.VMEM((tq, 256), F32)],
        args=[q, k, vx], sem=("parallel", "parallel", "arbitrary"), comm=comm)
    return o, lse, got


def _fa_bwd(name, q, k, kt, vx, do, lse, delta, tq, tk, nsub, comm=None):
    s = q.shape[0]
    tq, tk = min(tq, s), min(tk, s)
    nq = s // tq
    sk = tk // nsub

    def kern(q_ref, k_ref, kt_ref, v_ref, do_ref, lse_ref, dl_ref, dqt_ref, dk_ref, dv_ref):
        j, i = pl.program_id(1), pl.program_id(2)

        @pl.when(i == 0)
        def _():
            dv_ref[...] = jnp.zeros(dv_ref.shape, F32)
            dk_ref[...] = jnp.zeros(dk_ref.shape, F32)

        @pl.when(j == 0)
        def _():
            dqt_ref[i] = jnp.zeros((256, tq), F32)

        qb, dob = q_ref[...], do_ref[...]
        lse_row, dl_row = lse_ref[...], dl_ref[...]
        dqt = dqt_ref[i]
        for c in range(nsub):
            rows = pl.ds(c * sk, sk)
            st = _dot(k_ref[rows, :], qb, "nt")
            pt = jnp.exp(st - lse_row)
            dpt = _dot(v_ref[rows, :], dob, "nt")
            dst = (pt * (dpt - dl_row)).astype(BF16)
            dv_ref[rows, :] += _dot(pt, dob, "nn")
            dk_ref[rows, :] += _dot(dst, qb, "nn")
            dqt = dqt + _dot(kt_ref[:, rows], dst, "nn")
        dqt_ref[i] = dqt

    outs, got = _call(
        kern, name=name, grid=(MLA_HEADS, s // tk, nq),
        in_specs=[pl.BlockSpec((tq, 256), lambda h, j, i: (i, h)),
                  pl.BlockSpec((tk, 256), lambda h, j, i: (j, h)),
                  pl.BlockSpec((None, 256, tk), lambda h, j, i: (h, 0, j)),
                  pl.BlockSpec((tk, LANES), lambda h, j, i: (j, 2 * h)),
                  pl.BlockSpec((tq, LANES), lambda h, j, i: (i, h)),
                  pl.BlockSpec((None, 1, tq), lambda h, j, i: (h, 0, i)),
                  pl.BlockSpec((None, 1, tq), lambda h, j, i: (h, 0, i))],
        out_specs=[pl.BlockSpec((None, nq, 256, tq), lambda h, j, i: (h, 0, 0, 0)),
                   pl.BlockSpec((tk, 256), lambda h, j, i: (j, h)),
                   pl.BlockSpec((tk, LANES), lambda h, j, i: (j, h))],
        out_shape=[jax.ShapeDtypeStruct((MLA_HEADS, nq, 256, tq), F32),
                   jax.ShapeDtypeStruct((s, 1024), F32), jax.ShapeDtypeStruct((s, 512), F32)],
        args=[q, k, kt, vx, do, lse, delta], sem=("parallel", "arbitrary", "arbitrary"), comm=comm)
    return outs[0], outs[1], outs[2], got


def _fill_padded(dst_ref, src_ref, s):
    zeros = jnp.zeros((CONV_HALO, LANES), F32)
    dst_ref[pl.ds(0, CONV_HALO), :] = zeros
    dst_ref[pl.ds(CONV_HALO + s, CONV_HALO), :] = zeros
    dst_ref[pl.ds(CONV_HALO, s), :] = src_ref[...]


def _shifted_windows(win):
    n = win.shape[0]
    return [win] + [pltpu.roll(win, n - b, 0) for b in range(1, 8)]


def _conv_fwd(name, u, w, bias, rc=256):
    s = u.shape[0]
    rc = min(rc, s)

    def kern(u_ref, w_ref, b_ref, o_ref, pad_ref):
        _fill_padded(pad_ref, u_ref, s)
        wv = w_ref[...]
        bv = b_ref[...]

        def chunk(r, carry):
            base = pl.multiple_of(r * rc, rc)
            wins = _shifted_windows(pad_ref[pl.ds(base, rc + 2 * CONV_HALO), :])
            acc = jnp.broadcast_to(bv, (rc, LANES))
            for kk in range(CONV_K):
                a, b = divmod(kk + 1, 8)
                acc = acc + wv[kk:kk + 1, :] * wins[b][8 * a:8 * a + rc]
            o_ref[pl.ds(base, rc), :] = acc
            return carry

        lax.fori_loop(0, s // rc, chunk, 0)

    return pl.pallas_call(
        kern, name=name, grid=(D // LANES,),
        in_specs=[pl.BlockSpec((s, LANES), lambda c: (0, c)), pl.BlockSpec((32, LANES), lambda c: (0, c)),
                  pl.BlockSpec((1, LANES), lambda c: (0, c))],
        out_specs=pl.BlockSpec((s, LANES), lambda c: (0, c)),
        out_shape=jax.ShapeDtypeStruct((s, D), F32),
        scratch_shapes=[pltpu.VMEM((s + 2 * CONV_HALO, LANES), F32)],
        compiler_params=_cparams(("parallel",)),
    )(u, w, bias)


def _conv_bwd(name, u, g, w, rc=256, comm=None):
    s = u.shape[0]
    rc = min(rc, s)

    def kern(u_ref, g_ref, w_ref, du_ref, dw_ref, db_ref, upad_ref, gpad_ref, dwacc_ref):
        _fill_padded(upad_ref, u_ref, s)
        _fill_padded(gpad_ref, g_ref, s)
        dwacc_ref[...] = jnp.zeros(dwacc_ref.shape, F32)
        wv = w_ref[...]

        def chunk(r, carry):
            base = pl.multiple_of(r * rc, rc)
            gwins = _shifted_windows(gpad_ref[pl.ds(base, rc + 2 * CONV_HALO), :])
            uwins = _shifted_windows(upad_ref[pl.ds(base, rc + 2 * CONV_HALO), :])
            gc = g_ref[pl.ds(base, rc), :]
            acc = jnp.zeros((rc, LANES), F32)
            for kk in range(CONV_K):
                a, b = divmod(CONV_K - kk, 8)
                acc = acc + wv[kk:kk + 1, :] * gwins[b][8 * a:8 * a + rc]
                a, b = divmod(kk + 1, 8)
                prod = gc * uwins[b][8 * a:8 * a + rc]
                dwacc_ref[kk] += jnp.sum(prod.reshape(rc // 8, 8, LANES), axis=0)
            dwacc_ref[CONV_K] += jnp.sum(gc.reshape(rc // 8, 8, LANES), axis=0)
            du_ref[pl.ds(base, rc), :] = acc
            return carry

        lax.fori_loop(0, s // rc, chunk, 0)
        tot = jnp.sum(dwacc_ref[...], axis=1)
        lane_row = lax.broadcasted_iota(jnp.int32, (32, LANES), 0)
        dw_ref[...] = jnp.where(lane_row < CONV_K, tot, 0.0)
        db_ref[...] = tot[CONV_K:CONV_K + 1, :]

    cs = pl.BlockSpec((s, LANES), lambda c: (0, c))
    outs, got = _call(
        kern, name=name, grid=(D // LANES,),
        in_specs=[cs, cs, pl.BlockSpec((32, LANES), lambda c: (0, c))],
        out_specs=[cs, pl.BlockSpec((32, LANES), lambda c: (0, c)), pl.BlockSpec((1, LANES), lambda c: (0, c))],
        out_shape=[jax.ShapeDtypeStruct((s, D), F32), jax.ShapeDtypeStruct((32, D), F32),
                   jax.ShapeDtypeStruct((1, D), F32)],
        scratch_shapes=[pltpu.VMEM((s + 2 * CONV_HALO, LANES), F32), pltpu.VMEM((s + 2 * CONV_HALO, LANES), F32),
                        pltpu.VMEM((32, 8, LANES), F32)],
        args=[u, g, w], sem=("parallel",), comm=comm)
    return outs[0], outs[1], outs[2], got


def _mod_local(name, c_all, ada_w):
    def kern(c_ref, w_ref, o_ref):
        o_ref[...] = jnp.dot(_silu(c_ref[...]), w_ref[...], preferred_element_type=F32,
                             precision=lax.Precision.HIGHEST)

    return pl.pallas_call(
        kern, name=name, grid=(DEPTH,),
        in_specs=[pl.BlockSpec((N_DEV, D), lambda l: (0, 0)), pl.BlockSpec((None, D, 384), lambda l: (l, 0, 0))],
        out_specs=pl.BlockSpec((None, N_DEV, 384), lambda l: (l, 0, 0)),
        out_shape=jax.ShapeDtypeStruct((DEPTH, N_DEV, 384), F32),
        compiler_params=_cparams(("parallel",)),
    )(c_all, ada_w)


def _ada_w_grad(name, c_all_t, dmod):
    def kern(c_ref, d_ref, o_ref):
        o_ref[...] = jnp.dot(_silu(c_ref[...]), d_ref[...], preferred_element_type=F32,
                             precision=lax.Precision.HIGHEST)

    return pl.pallas_call(
        kern, name=name, grid=(DEPTH,),
        in_specs=[pl.BlockSpec((D, LANES), lambda l: (0, 0)), pl.BlockSpec((None, LANES, 384), lambda l: (l, 0, 0))],
        out_specs=pl.BlockSpec((None, D, 384), lambda l: (l, 0, 0)),
        out_shape=jax.ShapeDtypeStruct((DEPTH, D, 384), F32),
        compiler_params=_cparams(("parallel",)),
    )(c_all_t, dmod)


def _pre_fn(x, g, scale, shift):
    return _rms(x, g) * (1.0 + scale) + shift


def _post_fn(y, g, gate):
    return gate * _rms(y, g)


def _ev_post_fn(o_heads, rg, a, mg):
    normed = []
    for oh in o_heads:
        mu = jnp.mean(oh, axis=-1, keepdims=True)
        var = jnp.mean(jnp.square(oh - mu), axis=-1, keepdims=True)
        normed.append((oh - mu) * lax.rsqrt(var + EPS))
    return jnp.concatenate([jnp.concatenate(normed, axis=1) * _silu(rg), a * _silu(mg)], axis=1)


def _od_post_fn(u, g, ln_g, ln_b):
    mu = jnp.mean(u, axis=-1, keepdims=True)
    var = jnp.mean(jnp.square(u - mu), axis=-1, keepdims=True)
    y = (u - mu) * lax.rsqrt(var + EPS) * ln_g + ln_b
    return _silu(y) * _silu(g)


def _glu_fn(a, b):
    return a * _sigmoid(b)


def _heads(x):
    return [x[:, LANES * h:LANES * (h + 1)] for h in range(4)]


def _colsum(x):
    return jnp.sum(x, axis=0, keepdims=True)


def _zrows(n, c):
    return jnp.zeros((n, c), BF16)


def _ev_win_layout(wt):
    rq = [p for h in range(4) for p in (wt[64 * h:64 * h + 64], _zrows(64, D))]
    rk = [p for h in range(4) for p in (wt[256 + 64 * h:256 + 64 * h + 64], _zrows(64, D))]
    return jnp.concatenate([wt[512:1024], wt[1024:1536], wt[2240:2752], wt[1536:1920],
                            wt[2176:2240], _zrows(64, D)] + rq + rk + [wt[1920:2176]], axis=0)


def _uq_layout(wt):
    return jnp.concatenate([p for h in range(4) for p in (wt[192 * h:192 * h + 192], _zrows(64, 384))], axis=0)


def _uq_unlayout(g):
    return jnp.concatenate([g[256 * h:256 * h + 192] for h in range(4)], axis=0)


def _ukv_layout(wt):
    kpart = [p for h in range(4) for p in (wt[256 * h:256 * h + 128], _zrows(128, 256))]
    vpart = [wt[256 * h + 128:256 * h + 256] for h in range(4)]
    return jnp.concatenate(kpart + vpart, axis=0)


def _ukv_unlayout(g):
    return jnp.concatenate([p for h in range(4) for p in (g[256 * h:256 * h + 128], g[1024 + 128 * h:1024 + 128 * h + 128])],
                           axis=0)


def kernel(x, c, positions, ada_w, ada_b, pre_g, post_g, ev_w_in, ev_dec_f, ev_dec_b, ev_q_norm_g, ev_w_uq, ev_kv_norm_g, ev_w_ukv, ev_w_out, od_w_in, od_b_in, od_dw_w, od_dw_b, od_ln_g, od_ln_b, od_w_out, loss_target, m_ada_w, m_ada_b, m_pre_g, m_post_g, m_ev_w_in, m_ev_dec_f, m_ev_dec_b, m_ev_q_norm_g, m_ev_w_uq, m_ev_kv_norm_g, m_ev_w_ukv, m_ev_w_out, m_od_w_in, m_od_b_in, m_od_dw_w, m_od_dw_b, m_od_ln_g, m_od_ln_b, m_od_w_out, v_ada_w, v_ada_b, v_pre_g, v_post_g, v_ev_w_in, v_ev_dec_f, v_ev_dec_b, v_ev_q_norm_g, v_ev_w_uq, v_ev_kv_norm_g, v_ev_w_ukv, v_ev_w_out, v_od_w_in, v_od_b_in, v_od_dw_w, v_od_dw_b, v_od_ln_g, v_od_ln_b, v_od_w_out):
    s = x.shape[1]
    me = 4 * lax.axis_index("x") + 2 * lax.axis_index("y") + lax.axis_index("c")
    x0 = x.reshape(s, D)
    tgt = loss_target.reshape(s, D)
    ret_chunk = 256
    fa_cfg_f = ((min(4096, s // 2), min(1024, s // 2), min(16, s // 512)),) * 2
    fa_cfg_b = ((min(1024, s // 2), min(4096, s // 2), min(16, s // 512)),) * 2

    start_parts = [c.reshape(-1), od_b_in.reshape(-1), od_dw_w.reshape(-1), od_dw_b.reshape(-1),
                   od_ln_g.reshape(-1), od_ln_b.reshape(-1)]
    start_sizes = [p.shape[0] for p in start_parts]
    start_len = -(-sum(start_sizes) // 1024) * 1024
    start_vec = jnp.concatenate(start_parts + [jnp.zeros((start_len - sum(start_sizes),), F32)])
    start_all = _exchange("gather_start", [start_vec.reshape(-1, LANES)], False)[0].reshape(N_DEV, start_len)
    offs = np.cumsum([0] + start_sizes)
    c_all = start_all[:, offs[0]:offs[1]]
    b_in_all = start_all[:, offs[1]:offs[2]].reshape(N_DEV, 2, 384).transpose(1, 0, 2).reshape(2, 1, ZW_OD)
    dw_w_all = start_all[:, offs[2]:offs[3]].reshape(N_DEV, 2, CONV_K, LANES).transpose(1, 2, 0, 3).reshape(2, CONV_K, D)
    dw_w_all = jnp.concatenate([dw_w_all, jnp.zeros((2, 1, D), F32)], axis=1)
    dw_b_all = start_all[:, offs[3]:offs[4]].reshape(N_DEV, 2, LANES).transpose(1, 0, 2).reshape(2, 1, D)
    ln_g_all = start_all[:, offs[4]:offs[5]].reshape(N_DEV, 2, LANES).transpose(1, 0, 2).reshape(2, 1, D)
    ln_b_all = start_all[:, offs[5]:offs[6]].reshape(N_DEV, 2, LANES).transpose(1, 0, 2).reshape(2, 1, D)

    mod_loc = _mod_local("mod_local", c_all, ada_w)
    mod_all = _exchange("gather_mod", [mod_loc.reshape(DEPTH * N_DEV, 384)], False)[0].reshape(N_DEV, DEPTH, N_DEV, 384)
    mod = lax.dynamic_index_in_dim(mod_all, me, axis=2, keepdims=False)
    mod = mod.transpose(1, 0, 2).reshape(DEPTH, 3 * D) + ada_b
    shift = [mod[l:l + 1, 0:D] for l in range(DEPTH)]
    scale = [mod[l:l + 1, D:2 * D] for l in range(DEPTH)]
    gate = [mod[l:l + 1, 2 * D:3 * D] for l in range(DEPTH)]

    def ev_shards(i):
        return [ev_w_in[i].T.astype(BF16), ev_w_uq[i].T.astype(BF16), ev_w_ukv[i].T.astype(BF16), ev_w_out[i].astype(BF16)]

    def od_shards(i):
        return [od_w_in[i].T.astype(BF16), od_w_out[i].astype(BF16)]

    def full(got):
        return [g.reshape(N_DEV * g.shape[1], g.shape[2]) for g in got]

    def ev_full(got):
        win_t, uq_t, ukv_t, wout = full(got)
        return (_ev_win_layout(win_t), _uq_layout(uq_t), _ukv_layout(ukv_t), wout)

    ev_w = [ev_full(_exchange("gather_w_ev0", ev_shards(0), False)), None]
    od_w = [None, None]
    later_w = [(t, False) for t in od_shards(0) + ev_shards(1) + od_shards(1)]

    inv_freq = ROPE_BASE ** (-jnp.arange(0, 64, 2, dtype=F32) / 64)
    invf = jnp.tile(inv_freq, 4).reshape(1, LANES)
    sgn = jnp.tile(jnp.concatenate([-jnp.ones((32,), F32), jnp.ones((32,), F32)]), 2).reshape(1, LANES)

    def rope_body(rows, vecs):
        ang = rows[0].astype(F32) * vecs[0]
        return [jnp.cos(ang), jnp.sin(ang) * vecs[1]], []

    (cos_t, sin_t), _ = _rowwise("rope_tables", rope_body, [(positions.reshape(s, 1), 1, 0)], [invf, sgn],
                                 [(LANES, F32), (LANES, F32)])

    saved = []
    xl = x0
    for l in range(DEPTH):
        i = l // 2
        sv = dict(x=xl)

        if l == 0:
            def pre_body(rows, vecs):
                return [_pre_fn(rows[0], *vecs)], []

            (h,), _ = _rowwise("pre0", pre_body, [(xl, D, 0)], [pre_g[0:1], scale[0], shift[0]], [(D, BF16)])
        sv["h"] = h
        if l % 2 == 0:
            win_t, uq_t, ukv_t, wout = ev_w[i]
            z = _mm(f"ev_in{l}", h, win_t, "nt", out_dtype=BF16)
            sv["z"] = z
            dec_f, dec_b = ev_dec_f[i:i + 1], ev_dec_b[i:i + 1]

            def prep_body(rows, vecs):
                rq, rk, cq, ckv, kr, cos, sin = rows
                return [_rope(rq, cos, sin), _rope(rk, cos, sin) * RET_SCALE, _rms(cq, vecs[0]), _rms(ckv, vecs[1]),
                        _rope(kr, cos, sin)], []

            (rq_r, rk_r, qn, kvn, krr), _ = _rowwise(
                f"ev_prep{l}", prep_body,
                [(z, 512, 4), (z, 512, 5), (z, 384, 4), (z, 256, 12), (z, LANES, 15), (cos_t, LANES, 0), (sin_t, LANES, 0)],
                [ev_q_norm_g[i:i + 1], ev_kv_norm_g[i:i + 1]],
                [(512, BF16), (512, BF16), (384, BF16), (256, BF16), (LANES, BF16)])
            sv.update(rq_r=rq_r, rk_r=rk_r, qn=qn, kvn=kvn)
            o_f, st_f = _ret_fwd(f"ret_f{l}", rq_r, rk_r, z, dec_f, False, ret_chunk)
            o_b, st_b = _ret_fwd(f"ret_b{l}", rq_r, rk_r, z, dec_b, True, ret_chunk)
            sv.update(o_f=o_f, o_b=o_b, st_f=st_f, st_b=st_b)
            q_pad = _mm(f"ev_uq{l}", qn, uq_t, "nt", out_dtype=BF16)
            kv_pad = _mm(f"ev_ukv{l}", kvn, ukv_t, "nt", out_dtype=BF16)

            def mla_prep_body(rows, vecs):
                qp, kk, vv, kr_r, cos, sin = rows
                qs, ks, vs = [], [], []
                ones = jnp.ones((qp.shape[0], LANES), F32)
                for hh in range(4):
                    qs += [qp[:, 256 * hh:256 * hh + 128], _rope(qp[:, 256 * hh + 128:256 * hh + 256], cos, sin)]
                    ks += [kk[:, 256 * hh:256 * hh + 128], kr_r]
                    vs += [vv[:, LANES * hh:LANES * hh + LANES], ones]
                return [jnp.concatenate(qs, axis=1) * MLA_SCALE, jnp.concatenate(ks, axis=1),
                        jnp.concatenate(vs, axis=1)], []

            (qcat, kcat, v_x), _ = _rowwise(
                f"mla_prep{l}", mla_prep_body,
                [(q_pad, 1024, 0), (kv_pad, 1024, 0), (kv_pad, 512, 2), (krr, LANES, 0), (cos_t, LANES, 0), (sin_t, LANES, 0)],
                [], [(1024, BF16), (1024, BF16), (1024, BF16)])
            a_mla, lse, got = _fa_fwd(f"fa_fwd{l}", qcat, kcat, v_x, *fa_cfg_f[i], comm=later_w if l == 0 else None)
            if l == 0:
                od_w[0], ev_w[1], od_w[1] = tuple(full(got[0:2])), ev_full(got[2:6]), tuple(full(got[6:8]))
            sv.update(qcat=qcat, kcat=kcat, v_x=v_x, a_mla=a_mla, lse=lse)

            def ev_post_body(rows, vecs):
                of, ob, rg, a, mg = rows
                return [_ev_post_fn(_heads(of + ob), rg, a, mg)], []

            (act,), _ = _rowwise(f"ev_post{l}", ev_post_body,
                                 [(o_f, 512, 0), (o_b, 512, 0), (z, 512, 1), (a_mla, 512, 0), (z, 512, 2)], [], [(D, BF16)])
        else:
            win_t, wout = od_w[i]
            z = _mm(f"od_in{l}", h, win_t, "nt", out_dtype=BF16, bias=b_in_all[i])
            sv["z"] = z

            def glu_body(rows, vecs):
                return [_glu_fn(rows[0], rows[1])], []

            (u,), _ = _rowwise(f"glu{l}", glu_body, [(z, D, 0), (z, D, 1)], [], [(D, F32)])
            u2 = _conv_fwd(f"conv{l}", u, dw_w_all[i], dw_b_all[i])
            sv.update(u=u, u2=u2)

            def od_post_body(rows, vecs):
                return [_od_post_fn(rows[0], rows[1], vecs[0], vecs[1])], []

            (act,), _ = _rowwise(f"od_post{l}", od_post_body, [(u2, D, 0), (z, D, 2)], [ln_g_all[i], ln_b_all[i]],
                                 [(D, BF16)])
        sv["act"] = act
        y = _mm(f"out{l}", act, wout, "nn")
        sv["y"] = y

        saved.append(sv)
        if l < DEPTH - 1:
            def post_body(rows, vecs):
                xn = rows[0] + _post_fn(rows[1], vecs[0], vecs[1])
                return [xn, _pre_fn(xn, vecs[2], vecs[3], vecs[4])], []

            (xl, h), _ = _rowwise(f"post{l}", post_body, [(xl, D, 0), (y, D, 0)],
                                  [post_g[l:l + 1], gate[l], pre_g[l + 1:l + 2], scale[l + 1], shift[l + 1]],
                                  [(D, F32), (D, BF16)])
        else:
            def post_body(rows, vecs):
                diff = rows[0] + _post_fn(rows[1], vecs[0], vecs[1]) - rows[2]
                return [diff * (1.0 / D)], [_colsum(diff * diff) * (0.5 / D)]

            (dx,), (loss_lanes,) = _rowwise(f"post{l}", post_body, [(xl, D, 0), (y, D, 0), (tgt, D, 0)],
                                            [post_g[l:l + 1], gate[l]], [(D, F32)], [(1, D)])
    loss = lax.psum(jnp.sum(loss_lanes), ("x", "y", "c"))

    g_pre, g_post, g_mod = [None] * DEPTH, [None] * DEPTH, [None] * DEPTH
    g_dec_f, g_dec_b, g_qn, g_kvn = [None] * 2, [None] * 2, [None] * 2, [None] * 2
    g_b_in, g_dw_w, g_dw_b, g_ln_g, g_ln_b = [None] * 2, [None] * 2, [None] * 2, [None] * 2, [None] * 2
    recv_ev, recv_od = [None] * 2, [None] * 2
    pending = []
    for l in reversed(range(DEPTH)):
        i = l // 2
        sv = saved[l]

        if l == DEPTH - 1:
            def post_bwd_body(rows, vecs):
                yv, dxn = rows
                _, vjp = jax.vjp(_post_fn, yv, vecs[0], vecs[1])
                d_y, dg, d_gate = vjp(dxn)
                return [d_y], [dg, d_gate]

            (dy,), (dpost, dgate) = _rowwise(f"post_bwd{l}", post_bwd_body, [(sv["y"], D, 0), (dx, D, 0)],
                                             [post_g[l:l + 1], gate[l]], [(D, BF16)], [(1, D), (1, D)])
        g_post[l] = dpost
        dgate_l = dgate
        wout = ev_w[i][3] if l % 2 == 0 else od_w[i][1]
        dact = _mm(f"out_dgrad{l}", dy, wout, "nt", out_dtype=BF16)
        d_wout = _mm(f"out_wgrad{l}", sv["act"], dy, "tn", out_dtype=BF16)
        z = sv["z"]
        if l % 2 == 0:
            win_t, uq_t, ukv_t, _ = ev_w[i]
            dec_f, dec_b = ev_dec_f[i:i + 1], ev_dec_b[i:i + 1]

            def ev_post_bwd_body(rows, vecs):
                of, ob, rg, a, mg, da = rows
                _, vjp = jax.vjp(_ev_post_fn, _heads(of + ob), rg, a, mg)
                do_heads, drg, d_a, dmg = vjp(da)
                deltas = [jnp.sum(dh_ * ah_, axis=1, keepdims=True) for dh_, ah_ in zip(_heads(d_a), _heads(a))]
                return [jnp.concatenate(do_heads, axis=1), drg, d_a, dmg] + deltas, []

            (do_ret, drg, do_mla, dmg, dl0, dl1, dl2, dl3), _ = _rowwise(
                f"ev_post_bwd{l}", ev_post_bwd_body,
                [(sv["o_f"], 512, 0), (sv["o_b"], 512, 0), (z, 512, 1), (sv["a_mla"], 512, 0), (z, 512, 2), (dact, D, 0)],
                [], [(512, F32), (512, BF16), (512, BF16), (512, BF16)] + [(1, F32)] * 4)
            delta = jnp.stack([dl0, dl1, dl2, dl3]).reshape(MLA_HEADS, 1, s)
            lse = sv["lse"].reshape(MLA_HEADS, 1, s)
            kt = sv["kcat"].reshape(s, MLA_HEADS, 256).transpose(1, 2, 0)
            if l == 0:
                pending = pending + [(d_wout.reshape(N_DEV, 128, D), True)]
            dqt, dkcat, dv, got = _fa_bwd(f"fa_bwd{l}", sv["qcat"], sv["kcat"], kt, sv["v_x"], do_mla, lse, delta,
                                          *fa_cfg_b[i], comm=pending)
            recv_od[i] = got[0:2]
            wout_recv = got[2:]
            dqcat = dqt.transpose(1, 3, 0, 2).reshape(s, 1024)

            def mla_prep_bwd_body(rows, vecs):
                dq, dk, dvv, cos, sin = rows
                qs = []
                dkrr = jnp.zeros((dq.shape[0], LANES), F32)
                for hh in range(4):
                    qs += [dq[:, 256 * hh:256 * hh + 128], _rope_t(dq[:, 256 * hh + 128:256 * hh + 256], cos, sin)]
                    dkrr = dkrr + dk[:, 256 * hh + 128:256 * hh + 256]
                return [jnp.concatenate(qs, axis=1) * MLA_SCALE, jnp.concatenate([dk, dvv], axis=1), dkrr], []

            (dq_pad, dkv_pad, dkrr), _ = _rowwise(
                f"mla_prep_bwd{l}", mla_prep_bwd_body,
                [(dqcat, 1024, 0), (dkcat, 1024, 0), (dv, 512, 0), (cos_t, LANES, 0), (sin_t, LANES, 0)], [],
                [(1024, BF16), (1536, BF16), (LANES, F32)])
            dqn = _mm(f"uq_dgrad{l}", dq_pad, uq_t, "nn")
            d_uq = _mm(f"uq_wgrad{l}", dq_pad, sv["qn"], "tn", out_dtype=BF16)
            dkvn = _mm(f"ukv_dgrad{l}", dkv_pad, ukv_t, "nn")
            d_ukv = _mm(f"ukv_wgrad{l}", dkv_pad, sv["kvn"], "tn", out_dtype=BF16)
            dq_f, dk_f, dv_f, ddec_f = _ret_bwd(f"ret_f_bwd{l}", sv["rq_r"], sv["rk_r"], z, dec_f, sv["st_f"], do_ret,
                                                False, ret_chunk)
            dq_b, dk_b, dv_b, ddec_b = _ret_bwd(f"ret_b_bwd{l}", sv["rq_r"], sv["rk_r"], z, dec_b, sv["st_b"], do_ret,
                                                True, ret_chunk)
            g_dec_f[i], g_dec_b[i] = ddec_f[:, :RET_HEADS], ddec_b[:, :RET_HEADS]

            def prep_bwd_body(rows, vecs):
                cq, ckv, cos, sin, dqf, dqb, dkf, dkb, dvf, dvb, d_qn, d_kvn, d_krr = rows
                _, vjp_q = jax.vjp(_rms, cq, vecs[0])
                dcq, dgq = vjp_q(d_qn)
                _, vjp_kv = jax.vjp(_rms, ckv, vecs[1])
                dckv, dgkv = vjp_kv(d_kvn)
                return [dvf + dvb, dcq, _rope_t(d_krr, cos, sin), _rope_t(dqf + dqb, cos, sin),
                        _rope_t(dkf + dkb, cos, sin) * RET_SCALE, dckv], [dgq, dgkv]

            (drv, dcq, dkr, drq, drk, dckv), (dgq, dgkv) = _rowwise(
                f"ev_prep_bwd{l}", prep_bwd_body,
                [(z, 384, 4), (z, 256, 12), (cos_t, LANES, 0), (sin_t, LANES, 0), (dq_f, 512, 0), (dq_b, 512, 0),
                 (dk_f, 512, 0), (dk_b, 512, 0), (dv_f, 512, 0), (dv_b, 512, 0), (dqn, 384, 0), (dkvn, 256, 0),
                 (dkrr, LANES, 0)],
                [ev_q_norm_g[i:i + 1], ev_kv_norm_g[i:i + 1]],
                [(512, BF16), (384, BF16), (LANES, BF16), (512, BF16), (512, BF16), (256, BF16)], [(1, 384), (1, 256)])
            g_qn[i], g_kvn[i] = dgq, dgkv
            dz = [drv, drg, dmg, dcq, dkr, drq, drk, dckv]
            dz_off = [ZL_EV[nm][0] for nm in ("rv", "rg", "mg", "cq", "kr", "rq", "rk", "ckv")]
            g_rv, g_rg, g_mg, g_cq, g_kr, g_rq, g_rk, g_ckv = _mm_cols_tn(f"in_wgrad{l}", dz, sv["h"])
            d_win = jnp.concatenate([g_rq[128 * hh:128 * hh + 64] for hh in range(4)]
                                    + [g_rk[128 * hh:128 * hh + 64] for hh in range(4)]
                                    + [g_rv, g_rg, g_cq, g_ckv, g_kr[:64], g_mg], axis=0)
            pending = [(d_win.reshape(N_DEV, 344, D), True), (_uq_unlayout(d_uq).reshape(N_DEV, 96, 384), True),
                       (_ukv_unlayout(d_ukv).reshape(N_DEV, 128, 256), True)]
            if l > 0:
                pending.append((d_wout.reshape(N_DEV, 128, D), True))
        else:
            win_t, _ = od_w[i]

            def od_post_bwd_body(rows, vecs):
                u2, gg, da = rows
                _, vjp = jax.vjp(_od_post_fn, u2, gg, vecs[0], vecs[1])
                du2, dgg, dlg, dlb = vjp(da)
                return [du2, dgg], [dlg, dlb, _colsum(dgg)]

            (du2, dg_gate), (dlg, dlb, dbg) = _rowwise(
                f"od_post_bwd{l}", od_post_bwd_body, [(sv["u2"], D, 0), (z, D, 2), (dact, D, 0)],
                [ln_g_all[i], ln_b_all[i]], [(D, F32), (D, BF16)], [(1, D), (1, D), (1, D)])
            g_ln_g[i], g_ln_b[i] = dlg, dlb
            du, d_dw, d_dwb, got = _conv_bwd(f"conv_bwd{l}", sv["u"], du2, dw_w_all[i], comm=pending)
            if pending:
                recv_ev[i + 1] = got
            g_dw_w[i], g_dw_b[i] = d_dw[:CONV_K], d_dwb

            def glu_bwd_body(rows, vecs):
                a, b, d_u = rows
                _, vjp = jax.vjp(_glu_fn, a, b)
                d_a, d_b = vjp(d_u)
                return [d_a, d_b], [_colsum(d_a), _colsum(d_b)]

            (d_a, d_b), (dba, dbb) = _rowwise(f"glu_bwd{l}", glu_bwd_body, [(z, D, 0), (z, D, 1), (du, D, 0)], [],
                                              [(D, BF16), (D, BF16)], [(1, D), (1, D)])
            g_b_in[i] = jnp.concatenate([dba, dbb, dbg], axis=1)
            dz, dz_off = [d_a, d_b, dg_gate], [0, D, 2 * D]
            d_win = jnp.concatenate(_mm_cols_tn(f"in_wgrad{l}", dz, sv["h"]), axis=0)
            pending = [(d_win.reshape(N_DEV, 384, D), True), (d_wout.reshape(N_DEV, 128, D), True)]
        if l == 0:
            dh, got = _mm_cols_nn(f"in_dgrad{l}", dz, dz_off, win_t, comm=pending)
            recv_ev[0] = list(got) + list(wout_recv)
        else:
            dh = _mm_cols_nn(f"in_dgrad{l}", dz, dz_off, win_t)

        if l > 0:
            def pre_bwd_body(rows, vecs):
                xv, d_h, dxn, yv = rows
                _, vjp = jax.vjp(_pre_fn, xv, vecs[0], vecs[1], vecs[2])
                d_x, dg, dsc, dsh = vjp(d_h)
                d_x = d_x + dxn
                _, vjp_p = jax.vjp(_post_fn, yv, vecs[3], vecs[4])
                d_y, dgp, d_gate = vjp_p(d_x)
                return [d_x, d_y], [dg, dsc, dsh, dgp, d_gate]

            (dx, dy), (dpre, dscale, dshift, dpost, dgate) = _rowwise(
                f"pre_bwd{l}", pre_bwd_body, [(sv["x"], D, 0), (dh, D, 0), (dx, D, 0), (saved[l - 1]["y"], D, 0)],
                [pre_g[l:l + 1], scale[l], shift[l], post_g[l - 1:l], gate[l - 1]], [(D, F32), (D, BF16)], [(1, D)] * 5)
        else:
            def pre_bwd_body(rows, vecs):
                xv, d_h, dxn = rows
                _, vjp = jax.vjp(_pre_fn, xv, *vecs)
                d_x, dg, dsc, dsh = vjp(d_h)
                return [d_x + dxn], [dg, dsc, dsh]

            (dx,), (dpre, dscale, dshift) = _rowwise(f"pre_bwd{l}", pre_bwd_body, [(sv["x"], D, 0), (dh, D, 0), (dx, D, 0)],
                                                     [pre_g[l:l + 1], scale[l], shift[l]], [(D, F32)], [(1, D)] * 3)
        g_pre[l] = dpre
        g_mod[l] = jnp.concatenate([dshift, dscale, dgate_l], axis=1)

    grad_x = dx.reshape(1, s, D)

    end_parts = [jnp.concatenate(g_mod, axis=0), jnp.concatenate(g_pre, axis=0), jnp.concatenate(g_post, axis=0),
                 jnp.concatenate(g_dec_f, axis=0), jnp.concatenate(g_dec_b, axis=0), jnp.concatenate(g_qn, axis=0),
                 jnp.concatenate(g_kvn, axis=0), jnp.concatenate(g_b_in, axis=0), jnp.stack(g_dw_w),
                 jnp.concatenate(g_dw_b, axis=0), jnp.concatenate(g_ln_g, axis=0), jnp.concatenate(g_ln_b, axis=0)]
    end_shapes = [p.shape for p in end_parts]
    end_sizes = [int(np.prod(sh)) for sh in end_shapes]
    end_len = -(-sum(end_sizes) // 1024) * 1024
    end_vec = jnp.concatenate([p.reshape(-1) for p in end_parts] + [jnp.zeros((end_len - sum(end_sizes),), F32)])
    end_all = _exchange("gather_end", [end_vec.reshape(-1, LANES)], False)[0].reshape(N_DEV, end_len)
    eo = np.cumsum([0] + end_sizes)
    ends = [end_all[:, eo[j]:eo[j + 1]].reshape((N_DEV,) + tuple(end_shapes[j])) for j in range(len(end_parts))]
    (dmod_all, pre_all, post_all, decf_all, decb_all, qn_all, kvn_all, bin_all, dww_all, dwb_all, lng_all,
     lnb_all) = ends

    def pack_rep(*ts):
        lead = ts[0].ndim - 2
        return jnp.concatenate([t.reshape(t.shape[:lead] + (-1,)) for t in ts], axis=-1)

    rep_sizes = [DEPTH * 3 * D, DEPTH * D, DEPTH * D, 8, 8, 2 * 384, 2 * 256]
    rep_len = -(-sum(rep_sizes) // 1024) * 1024
    rep_pad = rep_len - sum(rep_sizes)

    def rep_rows(flat):
        padz = jnp.zeros(flat.shape[:-1] + (rep_pad,), F32)
        return jnp.concatenate([flat, padz], axis=-1).reshape(flat.shape[:-1] + (rep_len // LANES, LANES))

    rep_w = rep_rows(pack_rep(ada_b, pre_g, post_g, ev_dec_f, ev_dec_b, ev_q_norm_g, ev_kv_norm_g))
    rep_m = rep_rows(pack_rep(m_ada_b, m_pre_g, m_post_g, m_ev_dec_f, m_ev_dec_b, m_ev_q_norm_g, m_ev_kv_norm_g))
    rep_v = rep_rows(pack_rep(v_ada_b, v_pre_g, v_post_g, v_ev_dec_f, v_ev_dec_b, v_ev_q_norm_g, v_ev_kv_norm_g))
    rep_g = rep_rows(pack_rep(dmod_all, pre_all, post_all, decf_all, decb_all, qn_all, kvn_all))
    rep_out = _adamw("adamw_rep", rep_w, rep_m, rep_v, rep_g)
    ro = np.cumsum([0] + rep_sizes)
    rep_shapes = [(DEPTH, 3 * D), (DEPTH, D), (DEPTH, D), (2, 4), (2, 4), (2, 384), (2, 256)]

    def unpack_rep(t):
        flat = t.reshape(-1)
        return [flat[ro[j]:ro[j + 1]].reshape(rep_shapes[j]) for j in range(len(rep_shapes))]

    rep_res = [unpack_rep(t) for t in rep_out]

    def my_cols(t, width):
        return lax.dynamic_slice_in_dim(t, me * width, width, axis=t.ndim - 1)

    def vec_adamw(name, w, m, v, g_all, width):
        g = my_cols(g_all, width)
        r = _adamw(name, w.reshape(-1, width), m.reshape(-1, width), v.reshape(-1, width),
                   g.reshape(N_DEV, -1, width))
        return [t.reshape(w.shape) for t in r]

    res_b_in = vec_adamw("adamw_b_in", od_b_in, m_od_b_in, v_od_b_in, bin_all, 384)
    res_dw_w = vec_adamw("adamw_dw_w", od_dw_w, m_od_dw_w, v_od_dw_w, dww_all, LANES)
    res_dw_b = vec_adamw("adamw_dw_b", od_dw_b, m_od_dw_b, v_od_dw_b, dwb_all, LANES)
    res_ln_g = vec_adamw("adamw_ln_g", od_ln_g, m_od_ln_g, v_od_ln_g, lng_all, LANES)
    res_ln_b = vec_adamw("adamw_ln_b", od_ln_b, m_od_ln_b, v_od_ln_b, lnb_all, LANES)

    dmod_mine = my_cols(dmod_all, 384).transpose(1, 0, 2)
    dmod_pad = jnp.concatenate([dmod_mine, jnp.zeros((DEPTH, LANES - N_DEV, 384), F32)], axis=1)
    c_all_t = jnp.concatenate([c_all.T, jnp.zeros((D, LANES - N_DEV), F32)], axis=1)
    g_ada_w = _ada_w_grad("ada_w_grad", c_all_t, dmod_pad)
    res_ada_w = [t.reshape(ada_w.shape) for t in
                 _adamw("adamw_ada_w", ada_w.reshape(-1, 384), m_ada_w.reshape(-1, 384), v_ada_w.reshape(-1, 384),
                        g_ada_w.reshape(-1, 384))]

    ev_sh = [[_sum_parts(f"sum_g_ev{i}_{j}", r) for j, r in enumerate(recv_ev[i])] for i in range(2)]
    od_sh = [[_sum_parts(f"sum_g_od{i}_{j}", r) for j, r in enumerate(recv_od[i])] for i in range(2)]

    def mat_adamw(name, w, m, v, g):
        r = _adamw(name, w.reshape(-1, w.shape[-1]), m.reshape(-1, w.shape[-1]), v.reshape(-1, w.shape[-1]),
                   g.reshape(-1, w.shape[-1]))
        return [t.reshape(w.shape) for t in r]

    res_ev_w_in = mat_adamw("adamw_ev_w_in", ev_w_in, m_ev_w_in, v_ev_w_in, jnp.stack([ev_sh[i][0].T for i in range(2)]))
    res_ev_w_uq = mat_adamw("adamw_ev_w_uq", ev_w_uq, m_ev_w_uq, v_ev_w_uq, jnp.stack([ev_sh[i][1].T for i in range(2)]))
    res_ev_w_ukv = mat_adamw("adamw_ev_w_ukv", ev_w_ukv, m_ev_w_ukv, v_ev_w_ukv,
                             jnp.stack([ev_sh[i][2].T for i in range(2)]))
    res_ev_w_out = mat_adamw("adamw_ev_w_out", ev_w_out, m_ev_w_out, v_ev_w_out, jnp.stack([ev_sh[i][3] for i in range(2)]))
    res_od_w_in = mat_adamw("adamw_od_w_in", od_w_in, m_od_w_in, v_od_w_in, jnp.stack([od_sh[i][0].T for i in range(2)]))
    res_od_w_out = mat_adamw("adamw_od_w_out", od_w_out, m_od_w_out, v_od_w_out, jnp.stack([od_sh[i][1] for i in range(2)]))

    per_weight = [res_ada_w] + [[rep_res[t][j] for t in range(4)] for j in range(3)]
    per_weight += [res_ev_w_in, [rep_res[t][3] for t in range(4)], [rep_res[t][4] for t in range(4)],
                   [rep_res[t][5] for t in range(4)], res_ev_w_uq, [rep_res[t][6] for t in range(4)], res_ev_w_ukv,
                   res_ev_w_out, res_od_w_in, res_b_in, res_dw_w, res_dw_b, res_ln_g, res_ln_b, res_od_w_out]
    outs = [loss, grad_x]
    for t in range(4):
        outs += [pw[t] for pw in per_weight]
    return tuple(outs)
```

```python
import functools

import numpy as np
import jax
import jax.numpy as jnp
from jax import lax
from jax.experimental import pallas as pl
from jax.experimental.pallas import tpu as pltpu

F32 = jnp.float32
BF16 = jnp.bfloat16
MESH = pl.DeviceIdType.MESH

N_DEV = 8
D = 1024
DEPTH = 4
EPS = 1e-6
RET_HEADS = 4
MLA_HEADS = 4
RET_SCALE = 64 ** -0.5
MLA_SCALE = 192 ** -0.5
CONV_K = 31
CONV_HALO = 16
ROPE_BASE = 10000.0

ADAM_LR = 0.001
ADAM_B1 = 0.9
ADAM_B2 = 0.999
ADAM_EPS = 1e-08
ADAM_WD = 0.01
ADAM_STEP = 10

LANES = 128
VMEM_LIMIT = 48 * 1024 * 1024

ZL_EV = dict(rv=(0, 512), rg=(512, 512), mg=(1024, 512), cq=(1536, 384), kr=(1920, 128),
             rq=(2048, 512), rk=(2560, 512), ckv=(3072, 256))
ZW_EV = 3328
ZW_OD = 3072


def _cparams(sem, vmem=VMEM_LIMIT):
    return pltpu.CompilerParams(dimension_semantics=sem, vmem_limit_bytes=vmem)


def _pick(n, prefs):
    for p in prefs:
        if n % p == 0:
            return p
    return n


def _sigmoid(x):
    return 0.5 * (jnp.tanh(0.5 * x) + 1.0)


def _silu(x):
    return x * _sigmoid(x)


def _rms(x, g):
    return x * lax.rsqrt(jnp.mean(x * x, axis=-1, keepdims=True) + EPS) * g


def _log_sigmoid(x):
    return jnp.minimum(x, 0.0) - jnp.log(1.0 + jnp.exp(jnp.minimum(x, -x)))


def _tile_lanes(t, width):
    reps = width // t.shape[1]
    return t if reps == 1 else jnp.concatenate([t] * reps, axis=1)


def _rot_half(x):
    w = x.shape[1]
    lane = lax.broadcasted_iota(jnp.int32, x.shape, 1)
    first = jnp.bitwise_and(lane, 63) < 32
    return jnp.where(first, pltpu.roll(x, w - 32, 1), pltpu.roll(x, 32, 1))


def _rope(x, cos, sin):
    w = x.shape[1]
    return x * _tile_lanes(cos, w) + _rot_half(x) * _tile_lanes(sin, w)


def _rope_t(dy, cos, sin):
    w = dy.shape[1]
    return dy * _tile_lanes(cos, w) + _rot_half(dy * _tile_lanes(sin, w))


_DN = {"nn": (((1,), (0,)), ((), ())), "nt": (((1,), (1,)), ((), ())), "tn": (((0,), (0,)), ((), ()))}


def _dot(a, b, mode):
    return lax.dot_general(a.astype(BF16), b.astype(BF16), _DN[mode], preferred_element_type=F32)


@functools.partial(jax.custom_vjp, nondiff_argnums=(2,))
def _bdot(a, b, mode):
    return _dot(a, b, mode)


def _bdot_fwd(a, b, mode):
    return _dot(a, b, mode), (a, b)


def _bdot_bwd(mode, res, g):
    a, b = res
    if mode == "nn":
        return _dot(g, b, "nt"), _dot(a, g, "tn")
    if mode == "nt":
        return _dot(g, b, "nn"), _dot(g, a, "tn")
    return _dot(b, g, "nt"), _dot(a, g, "nn")


_bdot.defvjp(_bdot_fwd, _bdot_bwd)


def _rowwise(name, body, row_ins, vec_ins, row_outs, red_outs=(), tile=512):
    s = row_ins[0][0].shape[0]
    tile = min(tile, s)
    nr, nv, no = len(row_ins), len(vec_ins), len(row_outs)

    def kern(*refs):
        rows = [r[...].astype(F32) if r.dtype == BF16 else r[...] for r in refs[:nr]]
        vecs = [r[...] for r in refs[nr:nr + nv]]
        outs, reds = body(rows, vecs)
        for r, o in zip(refs[nr + nv:nr + nv + no], outs):
            r[...] = o.astype(r.dtype)
        red_refs = refs[nr + nv + no:]
        if red_refs:
            @pl.when(pl.program_id(0) == 0)
            def _():
                for r in red_refs:
                    r[...] = jnp.zeros(r.shape, r.dtype)
            for r, v in zip(red_refs, reds):
                r[...] += v

    in_specs = [pl.BlockSpec((tile, w), (lambda i, cb=cb: (i, cb))) for (_, w, cb) in row_ins]
    in_specs += [pl.BlockSpec(v.shape, (lambda i, nd=v.ndim: (0,) * nd)) for v in vec_ins]
    out_specs = [pl.BlockSpec((tile, w), lambda i: (i, 0)) for (w, _) in row_outs]
    out_specs += [pl.BlockSpec(sh, lambda i: (0, 0)) for sh in red_outs]
    out_shape = [jax.ShapeDtypeStruct((s, w), dt) for (w, dt) in row_outs]
    out_shape += [jax.ShapeDtypeStruct(sh, F32) for sh in red_outs]
    res = pl.pallas_call(
        kern, name=name, grid=(s // tile,), in_specs=in_specs, out_specs=out_specs, out_shape=out_shape,
        compiler_params=_cparams(("arbitrary",)),
    )(*[a for (a, _, _) in row_ins], *vec_ins)
    return res[:no], res[no:]


def _mm(name, a, b, mode, out_dtype=F32, bias=None, comm=None):
    if mode == "tn":
        k, m = a.shape
        n = b.shape[1]
        tm = m if m <= 1664 else m // 2
        tk = min(k, 1024)
        nk = k // tk

        def kern(a_ref, b_ref, o_ref, acc_ref):
            kk = pl.program_id(1)
            part = _dot(a_ref[...], b_ref[...], "tn")

            @pl.when(kk == 0)
            def _():
                acc_ref[...] = part

            @pl.when(kk > 0)
            def _():
                acc_ref[...] += part

            @pl.when(kk == nk - 1)
            def _():
                o_ref[...] = acc_ref[...].astype(o_ref.dtype)

        return pl.pallas_call(
            kern, name=name, grid=(m // tm, nk),
            in_specs=[pl.BlockSpec((tk, tm), lambda i, kk: (kk, i)),
                      pl.BlockSpec((tk, n), lambda i, kk: (kk, 0))],
            out_specs=pl.BlockSpec((tm, n), lambda i, kk: (i, 0)),
            out_shape=jax.ShapeDtypeStruct((m, n), out_dtype),
            scratch_shapes=[pltpu.VMEM((tm, n), F32)],
            compiler_params=_cparams(("parallel", "arbitrary")),
        )(a, b)

    m, k = a.shape
    n = b.shape[1] if mode == "nn" else b.shape[0]
    tm = min(m, 1024)
    tn = n if n <= 1664 else n // 2
    has_bias = bias is not None

    def kern(*refs):
        a_ref, b_ref = refs[0], refs[1]
        o_ref = refs[-1]
        r = _dot(a_ref[...], b_ref[...], mode)
        if has_bias:
            r = r + refs[2][...]
        o_ref[...] = r.astype(o_ref.dtype)

    b_spec = (pl.BlockSpec((k, tn), lambda i, j: (0, j)) if mode == "nn"
              else pl.BlockSpec((tn, k), lambda i, j: (j, 0)))
    in_specs = [pl.BlockSpec((tm, k), lambda i, j: (i, 0)), b_spec]
    args = [a, b]
    if has_bias:
        in_specs.append(pl.BlockSpec((1, tn), lambda i, j: (0, j)))
        args.append(bias)
    outs, got = _call(kern, name=name, grid=(m // tm, n // tn), in_specs=in_specs,
                      out_specs=[pl.BlockSpec((tm, tn), lambda i, j: (i, j))],
                      out_shape=[jax.ShapeDtypeStruct((m, n), out_dtype)], args=args, sem=("parallel", "parallel"),
                      comm=comm)
    return (outs[0], got) if comm else outs[0]


def _mm_cols_nn(name, pieces, offsets, b, comm=None):
    m = pieces[0].shape[0]
    n = b.shape[1]
    tm = min(m, 1024)
    np_ = len(pieces)

    def kern(*refs):
        acc = _dot(refs[0][...], refs[np_][...], "nn")
        for p in range(1, np_):
            acc = acc + _dot(refs[p][...], refs[np_ + p][...], "nn")
        refs[2 * np_][...] = acc.astype(BF16)

    in_specs = [pl.BlockSpec((tm, a.shape[1]), lambda i: (i, 0)) for a in pieces]
    in_specs += [pl.BlockSpec((a.shape[1], n), (lambda i, r=off // a.shape[1]: (r, 0))) for a, off in zip(pieces, offsets)]
    outs, got = _call(kern, name=name, grid=(m // tm,), in_specs=in_specs,
                      out_specs=[pl.BlockSpec((tm, n), lambda i: (i, 0))],
                      out_shape=[jax.ShapeDtypeStruct((m, n), BF16)], args=list(pieces) + [b] * np_, sem=("parallel",),
                      comm=comm)
    return (outs[0], got) if comm else outs[0]


def _mm_cols_tn(name, pieces, b):
    k, n = b.shape
    tk = min(k, 512)
    nk = k // tk
    np_ = len(pieces)

    def kern(*refs):
        b_ref = refs[np_]
        o_refs, acc_refs = refs[np_ + 1:2 * np_ + 1], refs[2 * np_ + 1:]
        kk = pl.program_id(0)

        @pl.when(kk == 0)
        def _():
            for r in acc_refs:
                r[...] = jnp.zeros(r.shape, F32)

        bb = b_ref[...]
        for p in range(np_):
            acc_refs[p][...] += _dot(refs[p][...], bb, "tn")

        @pl.when(kk == nk - 1)
        def _():
            for o, r in zip(o_refs, acc_refs):
                o[...] = r[...].astype(o.dtype)

    return pl.pallas_call(
        kern, name=name, grid=(nk,),
        in_specs=[pl.BlockSpec((tk, a.shape[1]), lambda kk: (kk, 0)) for a in pieces] + [pl.BlockSpec((tk, n), lambda kk: (kk, 0))],
        out_specs=[pl.BlockSpec((a.shape[1], n), lambda kk: (0, 0)) for a in pieces],
        out_shape=[jax.ShapeDtypeStruct((a.shape[1], n), BF16) for a in pieces],
        scratch_shapes=[pltpu.VMEM((a.shape[1], n), F32) for a in pieces],
        compiler_params=_cparams(("arbitrary",)),
    )(*pieces, b)


def _mla_proj(name, qn, uq_t, kvn, ukv_t, krr, cos, sin):
    s = qn.shape[0]
    tm = min(s, 1024)

    def kern(qn_ref, uq_ref, kvn_ref, ukv_ref, kr_ref, cos_ref, sin_ref, q_out, k_out, v_out):
        qp = _dot(qn_ref[...], uq_ref[...], "nt")
        kv = _dot(kvn_ref[...], ukv_ref[...], "nt")
        c, sn = cos_ref[...], sin_ref[...]
        kr_r = kr_ref[...].astype(F32)
        ones = jnp.ones((tm, LANES), F32)
        qs, ks, vs = [], [], []
        for hh in range(MLA_HEADS):
            qs += [qp[:, 256 * hh:256 * hh + 128], _rope(qp[:, 256 * hh + 128:256 * hh + 256], c, sn)]
            ks += [kv[:, 256 * hh:256 * hh + 128], kr_r]
            vs += [kv[:, 1024 + LANES * hh:1024 + LANES * hh + LANES], ones]
        q_out[...] = (jnp.concatenate(qs, axis=1) * MLA_SCALE).astype(BF16)
        k_out[...] = jnp.concatenate(ks, axis=1).astype(BF16)
        v_out[...] = jnp.concatenate(vs, axis=1).astype(BF16)

    row = lambda w: pl.BlockSpec((tm, w), lambda i: (i, 0))
    whole = lambda a: pl.BlockSpec(a.shape, lambda i: (0, 0))
    return pl.pallas_call(
        kern, name=name, grid=(s // tm,),
        in_specs=[row(qn.shape[1]), whole(uq_t), row(kvn.shape[1]), whole(ukv_t), row(LANES), row(LANES), row(LANES)],
        out_specs=[row(1024)] * 3, out_shape=[jax.ShapeDtypeStruct((s, 1024), BF16)] * 3,
        compiler_params=_cparams(("parallel",)),
    )(qn, uq_t, kvn, ukv_t, krr, cos, sin)


def _peers():
    mx, my, mc = lax.axis_index("x"), lax.axis_index("y"), lax.axis_index("c")
    me = 4 * mx + 2 * my + mc
    out = []
    for k in range(1, N_DEV):
        px = 1 - mx if (k >> 2) & 1 else mx
        py = 1 - my if (k >> 1) & 1 else my
        pc = 1 - mc if k & 1 else mc
        out.append((k, (px, py, pc), 4 * px + 2 * py + pc))
    return me, out


def _xchg_copies(x_refs, out_refs, scatter, send_sems, recv_sems, local_sems):
    me, peers = _peers()
    local, out, arrive = [], [], []
    for a, (x, o, sc) in enumerate(zip(x_refs, out_refs, scatter)):
        mine = x.at[me] if sc else x
        local.append(pltpu.make_async_copy(mine, o.at[me], local_sems.at[a]))
        for k, dev, p in peers:
            out.append(pltpu.make_async_remote_copy(
                src_ref=x.at[p] if sc else x, dst_ref=o.at[me],
                send_sem=send_sems.at[a, k - 1], recv_sem=recv_sems.at[a, k - 1],
                device_id=dev, device_id_type=MESH))
            arrive.append(pltpu.make_async_remote_copy(
                src_ref=mine, dst_ref=o.at[p],
                send_sem=send_sems.at[a, k - 1], recv_sem=recv_sems.at[a, k - 1],
                device_id=dev, device_id_type=MESH))
    return local, out, arrive


def _xchg_start(*args):
    local, out, _ = _xchg_copies(*args)
    for cp in local + out:
        cp.start()


def _xchg_wait(*args):
    local, out, arrive = _xchg_copies(*args)
    for cp in out:
        cp.wait_send()
    for cp in arrive:
        cp.wait_recv()
    for cp in local:
        cp.wait()


def _call(kern, *, name, grid, in_specs, out_specs, out_shape, args, sem, scratch_shapes=(), vmem=VMEM_LIMIT,
          comm=None):
    if not comm:
        outs = pl.pallas_call(kern, name=name, grid=grid, in_specs=in_specs, out_specs=out_specs, out_shape=out_shape,
                              scratch_shapes=list(scratch_shapes), compiler_params=_cparams(sem, vmem))(*args)
        return outs, []
    n, ni, no, ns = len(comm), len(in_specs), len(out_specs), len(scratch_shapes)
    xs = [x for x, _ in comm]
    scatter = [sc for _, sc in comm]

    def body(*refs):
        ins, x_refs = refs[:ni], refs[ni:ni + n]
        outs, out_refs = refs[ni + n:ni + n + no], refs[ni + n + no:ni + 2 * n + no]
        scr = refs[ni + 2 * n + no:ni + 2 * n + no + ns]
        sems = refs[ni + 2 * n + no + ns:]
        ids = [pl.program_id(d) for d in range(len(grid))]
        first = functools.reduce(jnp.logical_and, [i == 0 for i in ids])
        last = functools.reduce(jnp.logical_and, [i == g - 1 for i, g in zip(ids, grid)])

        @pl.when(first)
        def _():
            _xchg_start(x_refs, out_refs, scatter, *sems)

        kern(*ins, *outs, *scr)

        @pl.when(last)
        def _():
            _xchg_wait(x_refs, out_refs, scatter, *sems)

    any_spec = pl.BlockSpec(memory_space=pl.ANY)
    res = pl.pallas_call(
        body, name=name, grid=grid,
        in_specs=list(in_specs) + [any_spec] * n, out_specs=list(out_specs) + [any_spec] * n,
        out_shape=list(out_shape) + [jax.ShapeDtypeStruct((N_DEV,) + tuple(x.shape[1:] if sc else x.shape), x.dtype)
                                     for x, sc in comm],
        scratch_shapes=list(scratch_shapes) + [pltpu.SemaphoreType.DMA((n, N_DEV - 1)),
                                               pltpu.SemaphoreType.DMA((n, N_DEV - 1)), pltpu.SemaphoreType.DMA((n,))],
        compiler_params=pltpu.CompilerParams(dimension_semantics=("arbitrary",) * len(grid), vmem_limit_bytes=vmem,
                                             has_side_effects=True),
    )(*args, *xs)
    return res[:no], res[no:]


def _exchange(name, xs, scatter):
    def nothing():
        pass

    return _call(nothing, name=name, grid=(1,), in_specs=[], out_specs=[], out_shape=[], args=[], sem=("arbitrary",),
                 comm=[(x, scatter) for x in xs])[1]


def _sum_parts(name, x):
    p, r, c = x.shape
    tr = r if r * c * p * x.dtype.itemsize <= (8 << 20) else _pick(r, (256, 128, 64, 16))

    def kern(x_ref, o_ref):
        acc = x_ref[0].astype(F32)
        for i in range(1, p):
            acc = acc + x_ref[i].astype(F32)
        o_ref[...] = acc

    return pl.pallas_call(
        kern, name=name, grid=(r // tr,),
        in_specs=[pl.BlockSpec((p, tr, c), lambda i: (0, i, 0))],
        out_specs=pl.BlockSpec((tr, c), lambda i: (i, 0)),
        out_shape=jax.ShapeDtypeStruct((r, c), F32),
        compiler_params=_cparams(("parallel",)),
    )(x)


def _adamw(name, w, m, v, g):
    r, c = w.shape
    parts = g.shape[0] if g.ndim == 3 else 0
    tr = 512 if (r > 512 and r % 512 == 0) else r

    def kern(w_ref, m_ref, v_ref, g_ref, go_ref, d_ref, mo_ref, vo_ref):
        if parts:
            gg = g_ref[0]
            for i in range(1, parts):
                gg = gg + g_ref[i]
        else:
            gg = g_ref[...]
        mm = ADAM_B1 * m_ref[...] + (1.0 - ADAM_B1) * gg
        vv = ADAM_B2 * v_ref[...] + (1.0 - ADAM_B2) * (gg * gg)
        m_hat = mm / (1.0 - ADAM_B1 ** ADAM_STEP)
        v_hat = vv / (1.0 - ADAM_B2 ** ADAM_STEP)
        go_ref[...] = gg
        d_ref[...] = -ADAM_LR * (m_hat / (jnp.sqrt(v_hat) + ADAM_EPS) + ADAM_WD * w_ref[...])
        mo_ref[...] = mm
        vo_ref[...] = vv

    spec = pl.BlockSpec((tr, c), lambda i: (i, 0))
    gspec = pl.BlockSpec((parts, tr, c), lambda i: (0, i, 0)) if parts else spec
    sh = jax.ShapeDtypeStruct((r, c), F32)
    return pl.pallas_call(
        kern, name=name, grid=(r // tr,), in_specs=[spec, spec, spec, gspec],
        out_specs=[spec] * 4, out_shape=[sh] * 4,
        compiler_params=_cparams(("parallel",)),
    )(w, m, v, g)


def _ret_tables(dec_cc, dec_cd, dec_dd, reverse):
    c = dec_cc.shape[0]
    row = lax.broadcasted_iota(jnp.int32, (c, c), 0).astype(F32)
    col = lax.broadcasted_iota(jnp.int32, (c, c), 1).astype(F32)
    pos = lax.broadcasted_iota(jnp.int32, (c, LANES), 0).astype(F32)
    if reverse:
        diff, mask = col - row, col > row
        q_exp, k_exp = c - pos, pos
    else:
        diff, mask = row - col, row >= col
        q_exp, k_exp = pos + 1.0, c - 1.0 - pos
    decay = jnp.where(mask, jnp.exp(_log_sigmoid(dec_cc) * jnp.maximum(diff, 0.0)), 0.0)
    lam_cd = _log_sigmoid(dec_cd)
    return decay, jnp.exp(lam_cd * q_exp), jnp.exp(lam_cd * k_exp), jnp.exp(_log_sigmoid(dec_dd) * float(c))


def _ret_chunk(q, k, v, st, decay, qw, kw, sd):
    scores = _bdot(q, k, "nt") * decay
    o = _bdot(scores, v, "nn") + _bdot(q * qw, st, "nn")
    st_new = st * sd + _bdot(k * kw, v, "tn")
    return o, st_new


def _ret_dec(dec_ref, h, c):
    d = dec_ref[:, h:h + 1]
    return (jnp.broadcast_to(d, (c, c)), jnp.broadcast_to(d, (c, LANES)), jnp.broadcast_to(d, (LANES, LANES)))


def _ret_table_scratch(c):
    return [pltpu.VMEM((RET_HEADS, c, c), F32), pltpu.VMEM((RET_HEADS, c, LANES), F32),
            pltpu.VMEM((RET_HEADS, c, LANES), F32), pltpu.VMEM((RET_HEADS, LANES, LANES), F32)]


def _ret_fwd(name, q, k, z, dec, reverse, chunk):
    s = q.shape[0]
    chunk = min(chunk, s)
    n = s // chunk
    cmap = (lambda i: (n - 1 - i, 0)) if reverse else (lambda i: (i, 0))
    smap = (lambda i: (n - 1 - i, 0, 0, 0)) if reverse else (lambda i: (i, 0, 0, 0))

    def kern(q_ref, k_ref, v_ref, dec_ref, o_ref, st_out_ref, st_ref, *tab_refs):
        @pl.when(pl.program_id(0) == 0)
        def _():
            st_ref[...] = jnp.zeros(st_ref.shape, F32)
            for h in range(RET_HEADS):
                for r, t in zip(tab_refs, _ret_tables(*_ret_dec(dec_ref, h, chunk), reverse)):
                    r[h] = t

        for h in range(RET_HEADS):
            sl = slice(LANES * h, LANES * (h + 1))
            st = st_ref[h]
            st_out_ref[h] = st
            o, st_new = _ret_chunk(q_ref[:, sl].astype(F32), k_ref[:, sl].astype(F32), v_ref[:, sl].astype(F32),
                                   st, *[r[h] for r in tab_refs])
            o_ref[:, sl] = o
            st_ref[h] = st_new

    return pl.pallas_call(
        kern, name=name, grid=(n,),
        in_specs=[pl.BlockSpec((chunk, 512), cmap), pl.BlockSpec((chunk, 512), cmap),
                  pl.BlockSpec((chunk, 512), cmap), pl.BlockSpec((1, RET_HEADS), lambda i: (0, 0))],
        out_specs=[pl.BlockSpec((chunk, 512), cmap), pl.BlockSpec((None, RET_HEADS, LANES, LANES), smap)],
        out_shape=[jax.ShapeDtypeStruct((s, 512), F32), jax.ShapeDtypeStruct((n, RET_HEADS, LANES, LANES), F32)],
        scratch_shapes=[pltpu.VMEM((RET_HEADS, LANES, LANES), F32)] + _ret_table_scratch(chunk),
        compiler_params=_cparams(("arbitrary",)),
    )(q, k, z, dec)


def _ret_bwd(name, q, k, z, dec, states, do, reverse, chunk):
    s = q.shape[0]
    chunk = min(chunk, s)
    n = s // chunk
    cmap = (lambda i: (i, 0)) if reverse else (lambda i: (n - 1 - i, 0))
    smap = (lambda i: (i, 0, 0, 0)) if reverse else (lambda i: (n - 1 - i, 0, 0, 0))

    def kern(q_ref, k_ref, v_ref, dec_ref, st_in_ref, do_ref, dq_ref, dk_ref, dv_ref, ddec_ref, dst_ref, *scr):
        tab_refs, gtab_refs = scr[:4], scr[4:]
        step = pl.program_id(0)

        @pl.when(step == 0)
        def _():
            dst_ref[...] = jnp.zeros(dst_ref.shape, F32)
            for r in gtab_refs:
                r[...] = jnp.zeros(r.shape, F32)
            for h in range(RET_HEADS):
                for r, t in zip(tab_refs, _ret_tables(*_ret_dec(dec_ref, h, chunk), reverse)):
                    r[h] = t

        for h in range(RET_HEADS):
            sl = slice(LANES * h, LANES * (h + 1))
            _, vjp = jax.vjp(_ret_chunk, q_ref[:, sl].astype(F32), k_ref[:, sl].astype(F32), v_ref[:, sl].astype(F32),
                             st_in_ref[h], *[r[h] for r in tab_refs])
            grads = vjp((do_ref[:, sl], dst_ref[h]))
            dq_ref[:, sl] = grads[0]
            dk_ref[:, sl] = grads[1]
            dv_ref[:, sl] = grads[2]
            dst_ref[h] = grads[3]
            for r, g in zip(gtab_refs, grads[4:]):
                r[h] += g

        @pl.when(step == n - 1)
        def _():
            lane = lax.broadcasted_iota(jnp.int32, (1, LANES), 1)
            ddec = jnp.zeros((1, LANES), F32)
            for h in range(RET_HEADS):
                _, vjp_t = jax.vjp(functools.partial(_ret_tables, reverse=reverse), *_ret_dec(dec_ref, h, chunk))
                parts = vjp_t(tuple(r[h] for r in gtab_refs))
                tot = sum(jnp.sum(jnp.sum(p, axis=1, keepdims=True), axis=0, keepdims=True) for p in parts)
                ddec = ddec + jnp.where(lane == h, tot, 0.0)
            ddec_ref[...] = ddec

    cspec = pl.BlockSpec((chunk, 512), cmap)
    return pl.pallas_call(
        kern, name=name, grid=(n,),
        in_specs=[cspec, cspec, cspec, pl.BlockSpec((1, RET_HEADS), lambda i: (0, 0)),
                  pl.BlockSpec((None, RET_HEADS, LANES, LANES), smap), cspec],
        out_specs=[cspec, cspec, cspec, pl.BlockSpec((1, LANES), lambda i: (0, 0))],
        out_shape=[jax.ShapeDtypeStruct((s, 512), F32)] * 3 + [jax.ShapeDtypeStruct((1, LANES), F32)],
        scratch_shapes=[pltpu.VMEM((RET_HEADS, LANES, LANES), F32)] + _ret_table_scratch(chunk) * 2,
        compiler_params=_cparams(("arbitrary",)),
    )(q, k, z, dec, states, do)


def _fa_fwd(name, q, k, vx, tq, tk, nsub, comm=None):
    s = q.shape[0]
    tq, tk = min(tq, s), min(tk, s)
    nk = s // tk
    sq = tq // nsub

    def kern(q_ref, k_ref, v_ref, o_ref, lse_ref, m_ref, acc_ref):
        j = pl.program_id(2)

        @pl.when(j == 0)
        def _():
            m_ref[...] = jnp.full(m_ref.shape, -jnp.inf, F32)
            acc_ref[...] = jnp.zeros(acc_ref.shape, F32)

        kb, vb = k_ref[...], v_ref[...]
        for c in range(nsub):
            rows = pl.ds(c * sq, sq)
            sc = _dot(q_ref[rows, :], kb, "nt")
            m_prev = m_ref[rows, :]
            m_new = jnp.maximum(m_prev, jnp.max(sc, axis=1, keepdims=True))
            alpha = jnp.exp(m_prev - m_new)
            p = jnp.exp(sc - m_new)
            acc_ref[rows, :] = alpha * acc_ref[rows, :] + _dot(p, vb, "nn")
            m_ref[rows, :] = m_new

        @pl.when(j == nk - 1)
        def _():
            den = acc_ref[:, LANES:]
            o_ref[...] = acc_ref[:, :LANES] / den
            lse_ref[...] = m_ref[...] + jnp.log(den[:, :1])

    (o, lse), got = _call(
        kern, name=name, grid=(MLA_HEADS, s // tq, nk),
        in_specs=[pl.BlockSpec((tq, 256), lambda h, i, j: (i, h)),
                  pl.BlockSpec((tk, 256), lambda h, i, j: (j, h)),
                  pl.BlockSpec((tk, 256), lambda h, i, j: (j, h))],
        out_specs=[pl.BlockSpec((tq, LANES), lambda h, i, j: (i, h)),
                   pl.BlockSpec((None, tq, 1), lambda h, i, j: (h, i, 0))],
        out_shape=[jax.ShapeDtypeStruct((s, 512), F32), jax.ShapeDtypeStruct((MLA_HEADS, s, 1), F32)],
        scratch_shapes=[pltpu.VMEM((tq, 1), F32), pltpu.VMEM((tq, 256), F32)],
        args=[q, k, vx], sem=("parallel", "parallel", "arbitrary"), comm=comm)
    return o, lse, got


def _fa_bwd(name, q, k, kt, vx, do, lse, delta, tq, tk, nsub, comm=None):
    s = q.shape[0]
    tq, tk = min(tq, s), min(tk, s)
    nq = s // tq
    sk = tk // nsub

    def kern(q_ref, k_ref, kt_ref, v_ref, do_ref, lse_ref, dl_ref, dqt_ref, dk_ref, dv_ref):
        j, i = pl.program_id(1), pl.program_id(2)

        @pl.when(i == 0)
        def _():
            dv_ref[...] = jnp.zeros(dv_ref.shape, F32)
            dk_ref[...] = jnp.zeros(dk_ref.shape, F32)

        @pl.when(j == 0)
        def _():
            dqt_ref[i] = jnp.zeros((256, tq), F32)

        qb, dob = q_ref[...], do_ref[...]
        lse_row, dl_row = lse_ref[...], dl_ref[...]
        dqt = dqt_ref[i]
        for c in range(nsub):
            rows = pl.ds(c * sk, sk)
            st = _dot(k_ref[rows, :], qb, "nt")
            pt = jnp.exp(st - lse_row)
            dpt = _dot(v_ref[rows, :], dob, "nt")
            dst = (pt * (dpt - dl_row)).astype(BF16)
            dv_ref[rows, :] += _dot(pt, dob, "nn")
            dk_ref[rows, :] += _dot(dst, qb, "nn")
            dqt = dqt + _dot(kt_ref[:, rows], dst, "nn")
        dqt_ref[i] = dqt

    outs, got = _call(
        kern, name=name, grid=(MLA_HEADS, s // tk, nq),
        in_specs=[pl.BlockSpec((tq, 256), lambda h, j, i: (i, h)),
                  pl.BlockSpec((tk, 256), lambda h, j, i: (j, h)),
                  pl.BlockSpec((None, 256, tk), lambda h, j, i: (h, 0, j)),
                  pl.BlockSpec((tk, LANES), lambda h, j, i: (j, 2 * h)),
                  pl.BlockSpec((tq, LANES), lambda h, j, i: (i, h)),
                  pl.BlockSpec((None, 1, tq), lambda h, j, i: (h, 0, i)),
                  pl.BlockSpec((None, 1, tq), lambda h, j, i: (h, 0, i))],
        out_specs=[pl.BlockSpec((None, nq, 256, tq), lambda h, j, i: (h, 0, 0, 0)),
                   pl.BlockSpec((tk, 256), lambda h, j, i: (j, h)),
                   pl.BlockSpec((tk, LANES), lambda h, j, i: (j, h))],
        out_shape=[jax.ShapeDtypeStruct((MLA_HEADS, nq, 256, tq), F32),
                   jax.ShapeDtypeStruct((s, 1024), F32), jax.ShapeDtypeStruct((s, 512), F32)],
        args=[q, k, kt, vx, do, lse, delta], sem=("parallel", "arbitrary", "arbitrary"), comm=comm)
    return outs[0], outs[1], outs[2], got


def _fill_padded(dst_ref, val, s):
    zeros = jnp.zeros((CONV_HALO, LANES), F32)
    dst_ref[pl.ds(0, CONV_HALO), :] = zeros
    dst_ref[pl.ds(CONV_HALO + s, CONV_HALO), :] = zeros
    dst_ref[pl.ds(CONV_HALO, s), :] = val


def _shifted_windows(win):
    n = win.shape[0]
    return [win] + [pltpu.roll(win, n - b, 0) for b in range(1, 8)]


def _conv_fwd(name, z, w, bias, rc=256):
    s = z.shape[0]
    rc = min(rc, s)

    def kern(a_ref, g_ref, w_ref, b_ref, o_ref, pad_ref):
        _fill_padded(pad_ref, _glu_fn(a_ref[...].astype(F32), g_ref[...].astype(F32)), s)
        wv = w_ref[...]
        bv = b_ref[...]

        def chunk(r, carry):
            base = pl.multiple_of(r * rc, rc)
            wins = _shifted_windows(pad_ref[pl.ds(base, rc + 2 * CONV_HALO), :])
            acc = jnp.broadcast_to(bv, (rc, LANES))
            for kk in range(CONV_K):
                a, b = divmod(kk + 1, 8)
                acc = acc + wv[kk:kk + 1, :] * wins[b][8 * a:8 * a + rc]
            o_ref[pl.ds(base, rc), :] = acc
            return carry

        lax.fori_loop(0, s // rc, chunk, 0)

    nblk = D // LANES
    return pl.pallas_call(
        kern, name=name, grid=(nblk,),
        in_specs=[pl.BlockSpec((s, LANES), lambda c: (0, c)), pl.BlockSpec((s, LANES), lambda c: (0, nblk + c)),
                  pl.BlockSpec((32, LANES), lambda c: (0, c)), pl.BlockSpec((1, LANES), lambda c: (0, c))],
        out_specs=pl.BlockSpec((s, LANES), lambda c: (0, c)),
        out_shape=jax.ShapeDtypeStruct((s, D), F32),
        scratch_shapes=[pltpu.VMEM((s + 2 * CONV_HALO, LANES), F32)],
        compiler_params=_cparams(("parallel",)),
    )(z, z, w, bias)


def _conv_bwd(name, z, g, w, rc=256, comm=None):
    s = z.shape[0]
    rc = min(rc, s)

    def kern(a_ref, b_ref, g_ref, w_ref, da_ref, db_ref, dw_ref, dbias_ref, sa_ref, sb_ref, upad_ref, gpad_ref,
             dwacc_ref):
        _fill_padded(upad_ref, _glu_fn(a_ref[...].astype(F32), b_ref[...].astype(F32)), s)
        _fill_padded(gpad_ref, g_ref[...], s)
        dwacc_ref[...] = jnp.zeros(dwacc_ref.shape, F32)
        wv = w_ref[...]

        def chunk(r, carry):
            sum_a, sum_b = carry
            base = pl.multiple_of(r * rc, rc)
            gwins = _shifted_windows(gpad_ref[pl.ds(base, rc + 2 * CONV_HALO), :])
            uwins = _shifted_windows(upad_ref[pl.ds(base, rc + 2 * CONV_HALO), :])
            gc = g_ref[pl.ds(base, rc), :]
            acc = jnp.zeros((rc, LANES), F32)
            for kk in range(CONV_K):
                a, b = divmod(CONV_K - kk, 8)
                acc = acc + wv[kk:kk + 1, :] * gwins[b][8 * a:8 * a + rc]
                a, b = divmod(kk + 1, 8)
                prod = gc * uwins[b][8 * a:8 * a + rc]
                dwacc_ref[kk] += jnp.sum(prod.reshape(rc // 8, 8, LANES), axis=0)
            dwacc_ref[CONV_K] += jnp.sum(gc.reshape(rc // 8, 8, LANES), axis=0)
            av = a_ref[pl.ds(base, rc), :].astype(F32)
            sg = _sigmoid(b_ref[pl.ds(base, rc), :].astype(F32))
            d_a = acc * sg
            d_b = acc * av * sg * (1.0 - sg)
            da_ref[pl.ds(base, rc), :] = d_a.astype(BF16)
            db_ref[pl.ds(base, rc), :] = d_b.astype(BF16)
            return (sum_a + jnp.sum(d_a.reshape(rc // 8, 8, LANES), axis=0),
                    sum_b + jnp.sum(d_b.reshape(rc // 8, 8, LANES), axis=0))

        zero = jnp.zeros((8, LANES), F32)
        sum_a, sum_b = lax.fori_loop(0, s // rc, chunk, (zero, zero))
        sa_ref[...] = jnp.sum(sum_a, axis=0, keepdims=True)
        sb_ref[...] = jnp.sum(sum_b, axis=0, keepdims=True)
        tot = jnp.sum(dwacc_ref[...], axis=1)
        lane_row = lax.broadcasted_iota(jnp.int32, (32, LANES), 0)
        dw_ref[...] = jnp.where(lane_row < CONV_K, tot, 0.0)
        dbias_ref[...] = tot[CONV_K:CONV_K + 1, :]

    nblk = D // LANES
    cs = pl.BlockSpec((s, LANES), lambda c: (0, c))
    vec = pl.BlockSpec((1, LANES), lambda c: (0, c))
    outs, got = _call(
        kern, name=name, grid=(nblk,),
        in_specs=[cs, pl.BlockSpec((s, LANES), lambda c: (0, nblk + c)), cs, pl.BlockSpec((32, LANES), lambda c: (0, c))],
        out_specs=[cs, cs, pl.BlockSpec((32, LANES), lambda c: (0, c)), vec, vec, vec],
        out_shape=[jax.ShapeDtypeStruct((s, D), BF16), jax.ShapeDtypeStruct((s, D), BF16),
                   jax.ShapeDtypeStruct((32, D), F32)] + [jax.ShapeDtypeStruct((1, D), F32)] * 3,
        scratch_shapes=[pltpu.VMEM((s + 2 * CONV_HALO, LANES), F32), pltpu.VMEM((s + 2 * CONV_HALO, LANES), F32),
                        pltpu.VMEM((32, 8, LANES), F32)],
        args=[z, z, g, w], sem=("parallel",), comm=comm)
    return outs, got


def _mod_local(name, c_all, ada_w):
    def kern(c_ref, w_ref, o_ref):
        o_ref[...] = jnp.dot(_silu(c_ref[...]), w_ref[...], preferred_element_type=F32,
                             precision=lax.Precision.HIGHEST)

    return pl.pallas_call(
        kern, name=name, grid=(DEPTH,),
        in_specs=[pl.BlockSpec((N_DEV, D), lambda l: (0, 0)), pl.BlockSpec((None, D, 384), lambda l: (l, 0, 0))],
        out_specs=pl.BlockSpec((None, N_DEV, 384), lambda l: (l, 0, 0)),
        out_shape=jax.ShapeDtypeStruct((DEPTH, N_DEV, 384), F32),
        compiler_params=_cparams(("parallel",)),
    )(c_all, ada_w)


def _ada_w_grad(name, c_all_t, dmod):
    def kern(c_ref, d_ref, o_ref):
        o_ref[...] = jnp.dot(_silu(c_ref[...]), d_ref[...], preferred_element_type=F32,
                             precision=lax.Precision.HIGHEST)

    return pl.pallas_call(
        kern, name=name, grid=(DEPTH,),
        in_specs=[pl.BlockSpec((D, LANES), lambda l: (0, 0)), pl.BlockSpec((None, LANES, 384), lambda l: (l, 0, 0))],
        out_specs=pl.BlockSpec((None, D, 384), lambda l: (l, 0, 0)),
        out_shape=jax.ShapeDtypeStruct((DEPTH, D, 384), F32),
        compiler_params=_cparams(("parallel",)),
    )(c_all_t, dmod)


def _pre_fn(x, g, scale, shift):
    return _rms(x, g) * (1.0 + scale) + shift


def _post_fn(y, g, gate):
    return gate * _rms(y, g)


def _ev_post_fn(o_heads, rg, a, mg):
    normed = []
    for oh in o_heads:
        mu = jnp.mean(oh, axis=-1, keepdims=True)
        var = jnp.mean(jnp.square(oh - mu), axis=-1, keepdims=True)
        normed.append((oh - mu) * lax.rsqrt(var + EPS))
    return jnp.concatenate([jnp.concatenate(normed, axis=1) * _silu(rg), a * _silu(mg)], axis=1)


def _od_post_fn(u, g, ln_g, ln_b):
    mu = jnp.mean(u, axis=-1, keepdims=True)
    var = jnp.mean(jnp.square(u - mu), axis=-1, keepdims=True)
    y = (u - mu) * lax.rsqrt(var + EPS) * ln_g + ln_b
    return _silu(y) * _silu(g)


def _glu_fn(a, b):
    return a * _sigmoid(b)


def _heads(x):
    return [x[:, LANES * h:LANES * (h + 1)] for h in range(4)]


def _colsum(x):
    return jnp.sum(x, axis=0, keepdims=True)


def _zrows(n, c):
    return jnp.zeros((n, c), BF16)


def _ev_win_layout(wt):
    rq = [p for h in range(4) for p in (wt[64 * h:64 * h + 64], _zrows(64, D))]
    rk = [p for h in range(4) for p in (wt[256 + 64 * h:256 + 64 * h + 64], _zrows(64, D))]
    return jnp.concatenate([wt[512:1024], wt[1024:1536], wt[2240:2752], wt[1536:1920],
                            wt[2176:2240], _zrows(64, D)] + rq + rk + [wt[1920:2176]], axis=0)


def _uq_layout(wt):
    return jnp.concatenate([p for h in range(4) for p in (wt[192 * h:192 * h + 192], _zrows(64, 384))], axis=0)


def _uq_unlayout(g):
    return jnp.concatenate([g[256 * h:256 * h + 192] for h in range(4)], axis=0)


def _ukv_layout(wt):
    kpart = [p for h in range(4) for p in (wt[256 * h:256 * h + 128], _zrows(128, 256))]
    vpart = [wt[256 * h + 128:256 * h + 256] for h in range(4)]
    return jnp.concatenate(kpart + vpart, axis=0)


def _ukv_unlayout(g):
    return jnp.concatenate([p for h in range(4) for p in (g[256 * h:256 * h + 128], g[1024 + 128 * h:1024 + 128 * h + 128])],
                           axis=0)


def kernel(x, c, positions, ada_w, ada_b, pre_g, post_g, ev_w_in, ev_dec_f, ev_dec_b, ev_q_norm_g, ev_w_uq, ev_kv_norm_g, ev_w_ukv, ev_w_out, od_w_in, od_b_in, od_dw_w, od_dw_b, od_ln_g, od_ln_b, od_w_out, loss_target, m_ada_w, m_ada_b, m_pre_g, m_post_g, m_ev_w_in, m_ev_dec_f, m_ev_dec_b, m_ev_q_norm_g, m_ev_w_uq, m_ev_kv_norm_g, m_ev_w_ukv, m_ev_w_out, m_od_w_in, m_od_b_in, m_od_dw_w, m_od_dw_b, m_od_ln_g, m_od_ln_b, m_od_w_out, v_ada_w, v_ada_b, v_pre_g, v_post_g, v_ev_w_in, v_ev_dec_f, v_ev_dec_b, v_ev_q_norm_g, v_ev_w_uq, v_ev_kv_norm_g, v_ev_w_ukv, v_ev_w_out, v_od_w_in, v_od_b_in, v_od_dw_w, v_od_dw_b, v_od_ln_g, v_od_ln_b, v_od_w_out):
    s = x.shape[1]
    me = 4 * lax.axis_index("x") + 2 * lax.axis_index("y") + lax.axis_index("c")
    x0 = x.reshape(s, D)
    tgt = loss_target.reshape(s, D)
    ret_chunk = 256
    fa_cfg_f = ((min(4096, s // 2), min(2048, s // 2), min(16, s // 512)),) * 2
    fa_cfg_b = ((min(2048, s // 2), min(4096, s // 2), min(16, s // 512)),) * 2

    start_parts = [c.reshape(-1), od_b_in.reshape(-1), od_dw_w.reshape(-1), od_dw_b.reshape(-1),
                   od_ln_g.reshape(-1), od_ln_b.reshape(-1)]
    start_sizes = [p.shape[0] for p in start_parts]
    start_len = -(-sum(start_sizes) // 1024) * 1024
    start_vec = jnp.concatenate(start_parts + [jnp.zeros((start_len - sum(start_sizes),), F32)])
    start_all = _exchange("gather_start", [start_vec.reshape(-1, LANES)], False)[0].reshape(N_DEV, start_len)
    offs = np.cumsum([0] + start_sizes)
    c_all = start_all[:, offs[0]:offs[1]]
    b_in_all = start_all[:, offs[1]:offs[2]].reshape(N_DEV, 2, 384).transpose(1, 0, 2).reshape(2, 1, ZW_OD)
    dw_w_all = start_all[:, offs[2]:offs[3]].reshape(N_DEV, 2, CONV_K, LANES).transpose(1, 2, 0, 3).reshape(2, CONV_K, D)
    dw_w_all = jnp.concatenate([dw_w_all, jnp.zeros((2, 1, D), F32)], axis=1)
    dw_b_all = start_all[:, offs[3]:offs[4]].reshape(N_DEV, 2, LANES).transpose(1, 0, 2).reshape(2, 1, D)
    ln_g_all = start_all[:, offs[4]:offs[5]].reshape(N_DEV, 2, LANES).transpose(1, 0, 2).reshape(2, 1, D)
    ln_b_all = start_all[:, offs[5]:offs[6]].reshape(N_DEV, 2, LANES).transpose(1, 0, 2).reshape(2, 1, D)

    mod_loc = _mod_local("mod_local", c_all, ada_w)
    mod_all = _exchange("gather_mod", [mod_loc.reshape(DEPTH * N_DEV, 384)], False)[0].reshape(N_DEV, DEPTH, N_DEV, 384)
    mod = lax.dynamic_index_in_dim(mod_all, me, axis=2, keepdims=False)
    mod = mod.transpose(1, 0, 2).reshape(DEPTH, 3 * D) + ada_b
    shift = [mod[l:l + 1, 0:D] for l in range(DEPTH)]
    scale = [mod[l:l + 1, D:2 * D] for l in range(DEPTH)]
    gate = [mod[l:l + 1, 2 * D:3 * D] for l in range(DEPTH)]

    def ev_shards(i):
        return [ev_w_in[i].T.astype(BF16), ev_w_uq[i].T.astype(BF16), ev_w_ukv[i].T.astype(BF16), ev_w_out[i].astype(BF16)]

    def od_shards(i):
        return [od_w_in[i].T.astype(BF16), od_w_out[i].astype(BF16)]

    def full(got):
        return [g.reshape(N_DEV * g.shape[1], g.shape[2]) for g in got]

    def ev_full(got):
        win_t, uq_t, ukv_t, wout = full(got)
        return (_ev_win_layout(win_t), _uq_layout(uq_t), _ukv_layout(ukv_t), wout)

    win0_t = _ev_win_layout(full(_exchange("gather_w_ev0", ev_shards(0)[:1], False))[0])
    ev_w = [None, None]
    od_w = [None, None]
    later_w = [(t, False) for t in od_shards(0) + ev_shards(1) + od_shards(1)]

    inv_freq = ROPE_BASE ** (-jnp.arange(0, 64, 2, dtype=F32) / 64)
    invf = jnp.tile(inv_freq, 4).reshape(1, LANES)
    sgn = jnp.tile(jnp.concatenate([-jnp.ones((32,), F32), jnp.ones((32,), F32)]), 2).reshape(1, LANES)

    def rope_body(rows, vecs):
        ang = rows[0].astype(F32) * vecs[0]
        return [jnp.cos(ang), jnp.sin(ang) * vecs[1]], []

    (cos_t, sin_t), _ = _rowwise("rope_tables", rope_body, [(positions.reshape(s, 1), 1, 0)], [invf, sgn],
                                 [(LANES, F32), (LANES, F32)])

    saved = []
    xl = x0
    for l in range(DEPTH):
        i = l // 2
        sv = dict(x=xl)

        if l == 0:
            def pre_body(rows, vecs):
                return [_pre_fn(rows[0], *vecs)], []

            (h,), _ = _rowwise("pre0", pre_body, [(xl, D, 0)], [pre_g[0:1], scale[0], shift[0]], [(D, BF16)])
        sv["h"] = h
        if l % 2 == 0:
            if l == 0:
                z, got = _mm("ev_in0", h, win0_t, "nt", out_dtype=BF16, comm=[(t, False) for t in ev_shards(0)[1:]])
                uq0_t, ukv0_t, wout0 = full(got)
                ev_w[0] = (win0_t, _uq_layout(uq0_t), _ukv_layout(ukv0_t), wout0)
                win_t, uq_t, ukv_t, wout = ev_w[0]
            else:
                win_t, uq_t, ukv_t, wout = ev_w[i]
                z = _mm(f"ev_in{l}", h, win_t, "nt", out_dtype=BF16)
            sv["z"] = z
            dec_f, dec_b = ev_dec_f[i:i + 1], ev_dec_b[i:i + 1]

            def prep_body(rows, vecs):
                rq, rk, cq, ckv, kr, cos, sin = rows
                return [_rope(rq, cos, sin), _rope(rk, cos, sin) * RET_SCALE, _rms(cq, vecs[0]), _rms(ckv, vecs[1]),
                        _rope(kr, cos, sin)], []

            (rq_r, rk_r, qn, kvn, krr), _ = _rowwise(
                f"ev_prep{l}", prep_body,
                [(z, 512, 4), (z, 512, 5), (z, 384, 4), (z, 256, 12), (z, LANES, 15), (cos_t, LANES, 0), (sin_t, LANES, 0)],
                [ev_q_norm_g[i:i + 1], ev_kv_norm_g[i:i + 1]],
                [(512, BF16), (512, BF16), (384, BF16), (256, BF16), (LANES, BF16)])
            sv.update(rq_r=rq_r, rk_r=rk_r, qn=qn, kvn=kvn)
            o_f, st_f = _ret_fwd(f"ret_f{l}", rq_r, rk_r, z, dec_f, False, ret_chunk)
            o_b, st_b = _ret_fwd(f"ret_b{l}", rq_r, rk_r, z, dec_b, True, ret_chunk)
            sv.update(o_f=o_f, o_b=o_b, st_f=st_f, st_b=st_b)
            qcat, kcat, v_x = _mla_proj(f"mla_proj{l}", qn, uq_t, kvn, ukv_t, krr, cos_t, sin_t)
            a_mla, lse, got = _fa_fwd(f"fa_fwd{l}", qcat, kcat, v_x, *fa_cfg_f[i], comm=later_w if l == 0 else None)
            if l == 0:
                od_w[0], ev_w[1], od_w[1] = tuple(full(got[0:2])), ev_full(got[2:6]), tuple(full(got[6:8]))
            sv.update(qcat=qcat, kcat=kcat, v_x=v_x, a_mla=a_mla, lse=lse)

            def ev_post_body(rows, vecs):
                of, ob, rg, a, mg = rows
                return [_ev_post_fn(_heads(of + ob), rg, a, mg)], []

            (act,), _ = _rowwise(f"ev_post{l}", ev_post_body,
                                 [(o_f, 512, 0), (o_b, 512, 0), (z, 512, 1), (a_mla, 512, 0), (z, 512, 2)], [], [(D, BF16)])
        else:
            win_t, wout = od_w[i]
            z = _mm(f"od_in{l}", h, win_t, "nt", out_dtype=BF16, bias=b_in_all[i])
            sv["z"] = z

            u2 = _conv_fwd(f"conv{l}", z, dw_w_all[i], dw_b_all[i])
            sv.update(u2=u2)

            def od_post_body(rows, vecs):
                return [_od_post_fn(rows[0], rows[1], vecs[0], vecs[1])], []

            (act,), _ = _rowwise(f"od_post{l}", od_post_body, [(u2, D, 0), (z, D, 2)], [ln_g_all[i], ln_b_all[i]],
                                 [(D, BF16)])
        sv["act"] = act
        y = _mm(f"out{l}", act, wout, "nn")
        sv["y"] = y

        saved.append(sv)
        if l < DEPTH - 1:
            def post_body(rows, vecs):
                xn = rows[0] + _post_fn(rows[1], vecs[0], vecs[1])
                return [xn, _pre_fn(xn, vecs[2], vecs[3], vecs[4])], []

            (xl, h), _ = _rowwise(f"post{l}", post_body, [(xl, D, 0), (y, D, 0)],
                                  [post_g[l:l + 1], gate[l], pre_g[l + 1:l + 2], scale[l + 1], shift[l + 1]],
                                  [(D, F32), (D, BF16)])
        else:
            def post_body(rows, vecs):
                diff = rows[0] + _post_fn(rows[1], vecs[0], vecs[1]) - rows[2]
                return [diff * (1.0 / D)], [_colsum(diff * diff) * (0.5 / D)]

            (dx,), (loss_lanes,) = _rowwise(f"post{l}", post_body, [(xl, D, 0), (y, D, 0), (tgt, D, 0)],
                                            [post_g[l:l + 1], gate[l]], [(D, F32)], [(1, D)])
    loss = lax.psum(jnp.sum(loss_lanes), ("x", "y", "c"))

    g_pre, g_post, g_mod = [None] * DEPTH, [None] * DEPTH, [None] * DEPTH
    g_dec_f, g_dec_b, g_qn, g_kvn = [None] * 2, [None] * 2, [None] * 2, [None] * 2
    g_b_in, g_dw_w, g_dw_b, g_ln_g, g_ln_b = [None] * 2, [None] * 2, [None] * 2, [None] * 2, [None] * 2
    recv_ev, recv_od = [None] * 2, [None] * 2
    pending = []
    for l in reversed(range(DEPTH)):
        i = l // 2
        sv = saved[l]

        if l == DEPTH - 1:
            def post_bwd_body(rows, vecs):
                yv, dxn = rows
                _, vjp = jax.vjp(_post_fn, yv, vecs[0], vecs[1])
                d_y, dg, d_gate = vjp(dxn)
                return [d_y], [dg, d_gate]

            (dy,), (dpost, dgate) = _rowwise(f"post_bwd{l}", post_bwd_body, [(sv["y"], D, 0), (dx, D, 0)],
                                             [post_g[l:l + 1], gate[l]], [(D, BF16)], [(1, D), (1, D)])
        g_post[l] = dpost
        dgate_l = dgate
        wout = ev_w[i][3] if l % 2 == 0 else od_w[i][1]
        dact = _mm(f"out_dgrad{l}", dy, wout, "nt", out_dtype=BF16)
        d_wout = _mm(f"out_wgrad{l}", sv["act"], dy, "tn", out_dtype=BF16)
        z = sv["z"]
        if l % 2 == 0:
            win_t, uq_t, ukv_t, _ = ev_w[i]
            dec_f, dec_b = ev_dec_f[i:i + 1], ev_dec_b[i:i + 1]

            def ev_post_bwd_body(rows, vecs):
                of, ob, rg, a, mg, da = rows
                _, vjp = jax.vjp(_ev_post_fn, _heads(of + ob), rg, a, mg)
                do_heads, drg, d_a, dmg = vjp(da)
                deltas = [jnp.sum(dh_ * ah_, axis=1, keepdims=True) for dh_, ah_ in zip(_heads(d_a), _heads(a))]
                return [jnp.concatenate(do_heads, axis=1), drg, d_a, dmg] + deltas, []

            (do_ret, drg, do_mla, dmg, dl0, dl1, dl2, dl3), _ = _rowwise(
                f"ev_post_bwd{l}", ev_post_bwd_body,
                [(sv["o_f"], 512, 0), (sv["o_b"], 512, 0), (z, 512, 1), (sv["a_mla"], 512, 0), (z, 512, 2), (dact, D, 0)],
                [], [(512, F32), (512, BF16), (512, BF16), (512, BF16)] + [(1, F32)] * 4)
            delta = jnp.stack([dl0, dl1, dl2, dl3]).reshape(MLA_HEADS, 1, s)
            lse = sv["lse"].reshape(MLA_HEADS, 1, s)
            kt = sv["kcat"].reshape(s, MLA_HEADS, 256).transpose(1, 2, 0)
            if l == 0:
                pending = pending + [(d_wout.reshape(N_DEV, 128, D), True)]
            dqt, dkcat, dv, got = _fa_bwd(f"fa_bwd{l}", sv["qcat"], sv["kcat"], kt, sv["v_x"], do_mla, lse, delta,
                                          *fa_cfg_b[i], comm=pending)
            recv_od[i] = got[0:2]
            wout_recv = got[2:]
            dqcat = dqt.transpose(1, 3, 0, 2).reshape(s, 1024)

            def mla_prep_bwd_body(rows, vecs):
                dq, dk, dvv, cos, sin = rows
                qs = []
                dkrr = jnp.zeros((dq.shape[0], LANES), F32)
                for hh in range(4):
                    qs += [dq[:, 256 * hh:256 * hh + 128], _rope_t(dq[:, 256 * hh + 128:256 * hh + 256], cos, sin)]
                    dkrr = dkrr + dk[:, 256 * hh + 128:256 * hh + 256]
                return [jnp.concatenate(qs, axis=1) * MLA_SCALE, jnp.concatenate([dk, dvv], axis=1), dkrr], []

            (dq_pad, dkv_pad, dkrr), _ = _rowwise(
                f"mla_prep_bwd{l}", mla_prep_bwd_body,
                [(dqcat, 1024, 0), (dkcat, 1024, 0), (dv, 512, 0), (cos_t, LANES, 0), (sin_t, LANES, 0)], [],
                [(1024, BF16), (1536, BF16), (LANES, F32)])
            dqn = _mm(f"uq_dgrad{l}", dq_pad, uq_t, "nn")
            d_uq = _mm(f"uq_wgrad{l}", dq_pad, sv["qn"], "tn", out_dtype=BF16)
            dkvn = _mm(f"ukv_dgrad{l}", dkv_pad, ukv_t, "nn")
            d_ukv = _mm(f"ukv_wgrad{l}", dkv_pad, sv["kvn"], "tn", out_dtype=BF16)
            dq_f, dk_f, dv_f, ddec_f = _ret_bwd(f"ret_f_bwd{l}", sv["rq_r"], sv["rk_r"], z, dec_f, sv["st_f"], do_ret,
                                                False, ret_chunk)
            dq_b, dk_b, dv_b, ddec_b = _ret_bwd(f"ret_b_bwd{l}", sv["rq_r"], sv["rk_r"], z, dec_b, sv["st_b"], do_ret,
                                                True, ret_chunk)
            g_dec_f[i], g_dec_b[i] = ddec_f[:, :RET_HEADS], ddec_b[:, :RET_HEADS]

            def prep_bwd_body(rows, vecs):
                cq, ckv, cos, sin, dqf, dqb, dkf, dkb, dvf, dvb, d_qn, d_kvn, d_krr = rows
                _, vjp_q = jax.vjp(_rms, cq, vecs[0])
                dcq, dgq = vjp_q(d_qn)
                _, vjp_kv = jax.vjp(_rms, ckv, vecs[1])
                dckv, dgkv = vjp_kv(d_kvn)
                return [dvf + dvb, dcq, _rope_t(d_krr, cos, sin), _rope_t(dqf + dqb, cos, sin),
                        _rope_t(dkf + dkb, cos, sin) * RET_SCALE, dckv], [dgq, dgkv]

            (drv, dcq, dkr, drq, drk, dckv), (dgq, dgkv) = _rowwise(
                f"ev_prep_bwd{l}", prep_bwd_body,
                [(z, 384, 4), (z, 256, 12), (cos_t, LANES, 0), (sin_t, LANES, 0), (dq_f, 512, 0), (dq_b, 512, 0),
                 (dk_f, 512, 0), (dk_b, 512, 0), (dv_f, 512, 0), (dv_b, 512, 0), (dqn, 384, 0), (dkvn, 256, 0),
                 (dkrr, LANES, 0)],
                [ev_q_norm_g[i:i + 1], ev_kv_norm_g[i:i + 1]],
                [(512, BF16), (384, BF16), (LANES, BF16), (512, BF16), (512, BF16), (256, BF16)], [(1, 384), (1, 256)])
            g_qn[i], g_kvn[i] = dgq, dgkv
            dz = [drv, drg, dmg, dcq, dkr, drq, drk, dckv]
            dz_off = [ZL_EV[nm][0] for nm in ("rv", "rg", "mg", "cq", "kr", "rq", "rk", "ckv")]
            g_rv, g_rg, g_mg, g_cq, g_kr, g_rq, g_rk, g_ckv = _mm_cols_tn(f"in_wgrad{l}", dz, sv["h"])
            d_win = jnp.concatenate([g_rq[128 * hh:128 * hh + 64] for hh in range(4)]
                                    + [g_rk[128 * hh:128 * hh + 64] for hh in range(4)]
                                    + [g_rv, g_rg, g_cq, g_ckv, g_kr[:64], g_mg], axis=0)
            pending = [(d_win.reshape(N_DEV, 344, D), True), (_uq_unlayout(d_uq).reshape(N_DEV, 96, 384), True),
                       (_ukv_unlayout(d_ukv).reshape(N_DEV, 128, 256), True)]
            if l > 0:
                pending.append((d_wout.reshape(N_DEV, 128, D), True))
        else:
            win_t, _ = od_w[i]

            def od_post_bwd_body(rows, vecs):
                u2, gg, da = rows
                _, vjp = jax.vjp(_od_post_fn, u2, gg, vecs[0], vecs[1])
                du2, dgg, dlg, dlb = vjp(da)
                return [du2, dgg], [dlg, dlb, _colsum(dgg)]

            (du2, dg_gate), (dlg, dlb, dbg) = _rowwise(
                f"od_post_bwd{l}", od_post_bwd_body, [(sv["u2"], D, 0), (z, D, 2), (dact, D, 0)],
                [ln_g_all[i], ln_b_all[i]], [(D, F32), (D, BF16)], [(1, D), (1, D), (1, D)])
            g_ln_g[i], g_ln_b[i] = dlg, dlb
            (d_a, d_b, d_dw, d_dwb, dba, dbb), got = _conv_bwd(f"conv_bwd{l}", z, du2, dw_w_all[i], comm=pending)
            if pending:
                recv_ev[i + 1] = got
            g_dw_w[i], g_dw_b[i] = d_dw[:CONV_K], d_dwb
            g_b_in[i] = jnp.concatenate([dba, dbb, dbg], axis=1)
            dz, dz_off = [d_a, d_b, dg_gate], [0, D, 2 * D]
            d_win = jnp.concatenate(_mm_cols_tn(f"in_wgrad{l}", dz, sv["h"]), axis=0)
            pending = [(d_win.reshape(N_DEV, 384, D), True), (d_wout.reshape(N_DEV, 128, D), True)]
        if l == 0:
            dh, got = _mm_cols_nn(f"in_dgrad{l}", dz, dz_off, win_t, comm=pending)
            recv_ev[0] = list(got) + list(wout_recv)
        else:
            dh = _mm_cols_nn(f"in_dgrad{l}", dz, dz_off, win_t)

        if l > 0:
            def pre_bwd_body(rows, vecs):
                xv, d_h, dxn, yv = rows
                _, vjp = jax.vjp(_pre_fn, xv, vecs[0], vecs[1], vecs[2])
                d_x, dg, dsc, dsh = vjp(d_h)
                d_x = d_x + dxn
                _, vjp_p = jax.vjp(_post_fn, yv, vecs[3], vecs[4])
                d_y, dgp, d_gate = vjp_p(d_x)
                return [d_x, d_y], [dg, dsc, dsh, dgp, d_gate]

            (dx, dy), (dpre, dscale, dshift, dpost, dgate) = _rowwise(
                f"pre_bwd{l}", pre_bwd_body, [(sv["x"], D, 0), (dh, D, 0), (dx, D, 0), (saved[l - 1]["y"], D, 0)],
                [pre_g[l:l + 1], scale[l], shift[l], post_g[l - 1:l], gate[l - 1]], [(D, F32), (D, BF16)], [(1, D)] * 5)
        else:
            def pre_bwd_body(rows, vecs):
                xv, d_h, dxn = rows
                _, vjp = jax.vjp(_pre_fn, xv, *vecs)
                d_x, dg, dsc, dsh = vjp(d_h)
                return [d_x + dxn], [dg, dsc, dsh]

            (dx,), (dpre, dscale, dshift) = _rowwise(f"pre_bwd{l}", pre_bwd_body, [(sv["x"], D, 0), (dh, D, 0), (dx, D, 0)],
                                                     [pre_g[l:l + 1], scale[l], shift[l]], [(D, F32)], [(1, D)] * 3)
        g_pre[l] = dpre
        g_mod[l] = jnp.concatenate([dshift, dscale, dgate_l], axis=1)

    grad_x = dx.reshape(1, s, D)

    end_parts = [jnp.concatenate(g_mod, axis=0), jnp.concatenate(g_pre, axis=0), jnp.concatenate(g_post, axis=0),
                 jnp.concatenate(g_dec_f, axis=0), jnp.concatenate(g_dec_b, axis=0), jnp.concatenate(g_qn, axis=0),
                 jnp.concatenate(g_kvn, axis=0), jnp.concatenate(g_b_in, axis=0), jnp.stack(g_dw_w),
                 jnp.concatenate(g_dw_b, axis=0), jnp.concatenate(g_ln_g, axis=0), jnp.concatenate(g_ln_b, axis=0)]
    end_shapes = [p.shape for p in end_parts]
    end_sizes = [int(np.prod(sh)) for sh in end_shapes]
    end_len = -(-sum(end_sizes) // 1024) * 1024
    end_vec = jnp.concatenate([p.reshape(-1) for p in end_parts] + [jnp.zeros((end_len - sum(end_sizes),), F32)])
    end_all = _exchange("gather_end", [end_vec.reshape(-1, LANES)], False)[0].reshape(N_DEV, end_len)
    eo = np.cumsum([0] + end_sizes)
    ends = [end_all[:, eo[j]:eo[j + 1]].reshape((N_DEV,) + tuple(end_shapes[j])) for j in range(len(end_parts))]
    (dmod_all, pre_all, post_all, decf_all, decb_all, qn_all, kvn_all, bin_all, dww_all, dwb_all, lng_all,
     lnb_all) = ends

    def pack_rep(*ts):
        lead = ts[0].ndim - 2
        return jnp.concatenate([t.reshape(t.shape[:lead] + (-1,)) for t in ts], axis=-1)

    rep_sizes = [DEPTH * 3 * D, DEPTH * D, DEPTH * D, 8, 8, 2 * 384, 2 * 256]
    rep_len = -(-sum(rep_sizes) // 1024) * 1024
    rep_pad = rep_len - sum(rep_sizes)

    def rep_rows(flat):
        padz = jnp.zeros(flat.shape[:-1] + (rep_pad,), F32)
        return jnp.concatenate([flat, padz], axis=-1).reshape(flat.shape[:-1] + (rep_len // LANES, LANES))

    rep_w = rep_rows(pack_rep(ada_b, pre_g, post_g, ev_dec_f, ev_dec_b, ev_q_norm_g, ev_kv_norm_g))
    rep_m = rep_rows(pack_rep(m_ada_b, m_pre_g, m_post_g, m_ev_dec_f, m_ev_dec_b, m_ev_q_norm_g, m_ev_kv_norm_g))
    rep_v = rep_rows(pack_rep(v_ada_b, v_pre_g, v_post_g, v_ev_dec_f, v_ev_dec_b, v_ev_q_norm_g, v_ev_kv_norm_g))
    rep_g = rep_rows(pack_rep(dmod_all, pre_all, post_all, decf_all, decb_all, qn_all, kvn_all))
    rep_out = _adamw("adamw_rep", rep_w, rep_m, rep_v, rep_g)
    ro = np.cumsum([0] + rep_sizes)
    rep_shapes = [(DEPTH, 3 * D), (DEPTH, D), (DEPTH, D), (2, 4), (2, 4), (2, 384), (2, 256)]

    def unpack_rep(t):
        flat = t.reshape(-1)
        return [flat[ro[j]:ro[j + 1]].reshape(rep_shapes[j]) for j in range(len(rep_shapes))]

    rep_res = [unpack_rep(t) for t in rep_out]

    def my_cols(t, width):
        return lax.dynamic_slice_in_dim(t, me * width, width, axis=t.ndim - 1)

    def vec_adamw(name, w, m, v, g_all, width):
        g = my_cols(g_all, width)
        r = _adamw(name, w.reshape(-1, width), m.reshape(-1, width), v.reshape(-1, width),
                   g.reshape(N_DEV, -1, width))
        return [t.reshape(w.shape) for t in r]

    res_b_in = vec_adamw("adamw_b_in", od_b_in, m_od_b_in, v_od_b_in, bin_all, 384)
    res_dw_w = vec_adamw("adamw_dw_w", od_dw_w, m_od_dw_w, v_od_dw_w, dww_all, LANES)
    res_dw_b = vec_adamw("adamw_dw_b", od_dw_b, m_od_dw_b, v_od_dw_b, dwb_all, LANES)
    res_ln_g = vec_adamw("adamw_ln_g", od_ln_g, m_od_ln_g, v_od_ln_g, lng_all, LANES)
    res_ln_b = vec_adamw("adamw_ln_b", od_ln_b, m_od_ln_b, v_od_ln_b, lnb_all, LANES)

    dmod_mine = my_cols(dmod_all, 384).transpose(1, 0, 2)
    dmod_pad = jnp.concatenate([dmod_mine, jnp.zeros((DEPTH, LANES - N_DEV, 384), F32)], axis=1)
    c_all_t = jnp.concatenate([c_all.T, jnp.zeros((D, LANES - N_DEV), F32)], axis=1)
    g_ada_w = _ada_w_grad("ada_w_grad", c_all_t, dmod_pad)
    res_ada_w = [t.reshape(ada_w.shape) for t in
                 _adamw("adamw_ada_w", ada_w.reshape(-1, 384), m_ada_w.reshape(-1, 384), v_ada_w.reshape(-1, 384),
                        g_ada_w.reshape(-1, 384))]

    ev_sh = [[_sum_parts(f"sum_g_ev{i}_{j}", r) for j, r in enumerate(recv_ev[i])] for i in range(2)]
    od_sh = [[_sum_parts(f"sum_g_od{i}_{j}", r) for j, r in enumerate(recv_od[i])] for i in range(2)]

    def mat_adamw(name, w, m, v, g):
        r = _adamw(name, w.reshape(-1, w.shape[-1]), m.reshape(-1, w.shape[-1]), v.reshape(-1, w.shape[-1]),
                   g.reshape(-1, w.shape[-1]))
        return [t.reshape(w.shape) for t in r]

    res_ev_w_in = mat_adamw("adamw_ev_w_in", ev_w_in, m_ev_w_in, v_ev_w_in, jnp.stack([ev_sh[i][0].T for i in range(2)]))
    res_ev_w_uq = mat_adamw("adamw_ev_w_uq", ev_w_uq, m_ev_w_uq, v_ev_w_uq, jnp.stack([ev_sh[i][1].T for i in range(2)]))
    res_ev_w_ukv = mat_adamw("adamw_ev_w_ukv", ev_w_ukv, m_ev_w_ukv, v_ev_w_ukv,
                             jnp.stack([ev_sh[i][2].T for i in range(2)]))
    res_ev_w_out = mat_adamw("adamw_ev_w_out", ev_w_out, m_ev_w_out, v_ev_w_out, jnp.stack([ev_sh[i][3] for i in range(2)]))
    res_od_w_in = mat_adamw("adamw_od_w_in", od_w_in, m_od_w_in, v_od_w_in, jnp.stack([od_sh[i][0].T for i in range(2)]))
    res_od_w_out = mat_adamw("adamw_od_w_out", od_w_out, m_od_w_out, v_od_w_out, jnp.stack([od_sh[i][1] for i in range(2)]))

    per_weight = [res_ada_w] + [[rep_res[t][j] for t in range(4)] for j in range(3)]
    per_weight += [res_ev_w_in, [rep_res[t][3] for t in range(4)], [rep_res[t][4] for t in range(4)],
                   [rep_res[t][5] for t in range(4)], res_ev_w_uq, [rep_res[t][6] for t in range(4)], res_ev_w_ukv,
                   res_ev_w_out, res_od_w_in, res_b_in, res_dw_w, res_dw_b, res_ln_g, res_ln_b, res_od_w_out]
    outs = [loss, grad_x]
    for t in range(4):
        outs += [pw[t] for pw in per_weight]
    return tuple(outs)
```

```python
import functools

import numpy as np
import jax
import jax.numpy as jnp
from jax import lax
from jax.experimental import pallas as pl
from jax.experimental.pallas import tpu as pltpu

F32 = jnp.float32
BF16 = jnp.bfloat16
MESH = pl.DeviceIdType.MESH

N_DEV = 8
D = 1024
DEPTH = 4
EPS = 1e-6
RET_HEADS = 4
MLA_HEADS = 4
RET_SCALE = 64 ** -0.5
MLA_SCALE = 192 ** -0.5
CONV_K = 31
CONV_HALO = 16
ROPE_BASE = 10000.0

ADAM_LR = 0.001
ADAM_B1 = 0.9
ADAM_B2 = 0.999
ADAM_EPS = 1e-08
ADAM_WD = 0.01
ADAM_STEP = 10

LANES = 128
VMEM_LIMIT = 48 * 1024 * 1024

ZL_EV = dict(rv=(0, 512), rg=(512, 512), mg=(1024, 512), cq=(1536, 384), kr=(1920, 128),
             rq=(2048, 512), rk=(2560, 512), ckv=(3072, 256))
ZW_EV = 3328
ZW_OD = 3072


def _cparams(sem, vmem=VMEM_LIMIT):
    return pltpu.CompilerParams(dimension_semantics=sem, vmem_limit_bytes=vmem)


def _pick(n, prefs):
    for p in prefs:
        if n % p == 0:
            return p
    return n


def _sigmoid(x):
    return 0.5 * (jnp.tanh(0.5 * x) + 1.0)


def _silu(x):
    return x * _sigmoid(x)


def _rms(x, g):
    return x * lax.rsqrt(jnp.mean(x * x, axis=-1, keepdims=True) + EPS) * g


def _log_sigmoid(x):
    return jnp.minimum(x, 0.0) - jnp.log(1.0 + jnp.exp(jnp.minimum(x, -x)))


def _tile_lanes(t, width):
    reps = width // t.shape[1]
    return t if reps == 1 else jnp.concatenate([t] * reps, axis=1)


def _rot_half(x):
    w = x.shape[1]
    lane = lax.broadcasted_iota(jnp.int32, x.shape, 1)
    first = jnp.bitwise_and(lane, 63) < 32
    return jnp.where(first, pltpu.roll(x, w - 32, 1), pltpu.roll(x, 32, 1))


def _rope(x, cos, sin):
    w = x.shape[1]
    return x * _tile_lanes(cos, w) + _rot_half(x) * _tile_lanes(sin, w)


def _rope_t(dy, cos, sin):
    w = dy.shape[1]
    return dy * _tile_lanes(cos, w) + _rot_half(dy * _tile_lanes(sin, w))


_DN = {"nn": (((1,), (0,)), ((), ())), "nt": (((1,), (1,)), ((), ())), "tn": (((0,), (0,)), ((), ()))}


def _dot(a, b, mode):
    return lax.dot_general(a.astype(BF16), b.astype(BF16), _DN[mode], preferred_element_type=F32)


@functools.partial(jax.custom_vjp, nondiff_argnums=(2,))
def _bdot(a, b, mode):
    return _dot(a, b, mode)


def _bdot_fwd(a, b, mode):
    return _dot(a, b, mode), (a, b)


def _bdot_bwd(mode, res, g):
    a, b = res
    if mode == "nn":
        return _dot(g, b, "nt"), _dot(a, g, "tn")
    if mode == "nt":
        return _dot(g, b, "nn"), _dot(g, a, "tn")
    return _dot(b, g, "nt"), _dot(a, g, "nn")


_bdot.defvjp(_bdot_fwd, _bdot_bwd)


def _rowwise(name, body, row_ins, vec_ins, row_outs, red_outs=(), tile=512):
    s = row_ins[0][0].shape[0]
    tile = min(tile, s)
    nr, nv, no = len(row_ins), len(vec_ins), len(row_outs)

    def kern(*refs):
        rows = [r[...].astype(F32) if r.dtype == BF16 else r[...] for r in refs[:nr]]
        vecs = [r[...] for r in refs[nr:nr + nv]]
        outs, reds = body(rows, vecs)
        for r, o in zip(refs[nr + nv:nr + nv + no], outs):
            r[...] = o.astype(r.dtype)
        red_refs = refs[nr + nv + no:]
        if red_refs:
            @pl.when(pl.program_id(0) == 0)
            def _():
                for r in red_refs:
                    r[...] = jnp.zeros(r.shape, r.dtype)
            for r, v in zip(red_refs, reds):
                r[...] += v

    in_specs = [pl.BlockSpec((tile, w), (lambda i, cb=cb: (i, cb))) for (_, w, cb) in row_ins]
    in_specs += [pl.BlockSpec(v.shape, (lambda i, nd=v.ndim: (0,) * nd)) for v in vec_ins]
    out_specs = [pl.BlockSpec((tile, w), lambda i: (i, 0)) for (w, _) in row_outs]
    out_specs += [pl.BlockSpec(sh, lambda i: (0, 0)) for sh in red_outs]
    out_shape = [jax.ShapeDtypeStruct((s, w), dt) for (w, dt) in row_outs]
    out_shape += [jax.ShapeDtypeStruct(sh, F32) for sh in red_outs]
    res = pl.pallas_call(
        kern, name=name, grid=(s // tile,), in_specs=in_specs, out_specs=out_specs, out_shape=out_shape,
        compiler_params=_cparams(("arbitrary",)),
    )(*[a for (a, _, _) in row_ins], *vec_ins)
    return res[:no], res[no:]


def _mm(name, a, b, mode, out_dtype=F32, bias=None, comm=None):
    if mode == "tn":
        k, m = a.shape
        n = b.shape[1]
        tm = m if m <= 1664 else m // 2
        tk = min(k, 1024)
        nk = k // tk

        def kern(a_ref, b_ref, o_ref, acc_ref):
            kk = pl.program_id(1)
            part = _dot(a_ref[...], b_ref[...], "tn")

            @pl.when(kk == 0)
            def _():
                acc_ref[...] = part

            @pl.when(kk > 0)
            def _():
                acc_ref[...] += part

            @pl.when(kk == nk - 1)
            def _():
                o_ref[...] = acc_ref[...].astype(o_ref.dtype)

        return pl.pallas_call(
            kern, name=name, grid=(m // tm, nk),
            in_specs=[pl.BlockSpec((tk, tm), lambda i, kk: (kk, i)),
                      pl.BlockSpec((tk, n), lambda i, kk: (kk, 0))],
            out_specs=pl.BlockSpec((tm, n), lambda i, kk: (i, 0)),
            out_shape=jax.ShapeDtypeStruct((m, n), out_dtype),
            scratch_shapes=[pltpu.VMEM((tm, n), F32)],
            compiler_params=_cparams(("parallel", "arbitrary")),
        )(a, b)

    m, k = a.shape
    n = b.shape[1] if mode == "nn" else b.shape[0]
    tm = min(m, 1024)
    tn = n if n <= 1664 else n // 2
    has_bias = bias is not None

    def kern(*refs):
        a_ref, b_ref = refs[0], refs[1]
        o_ref = refs[-1]
        r = _dot(a_ref[...], b_ref[...], mode)
        if has_bias:
            r = r + refs[2][...]
        o_ref[...] = r.astype(o_ref.dtype)

    b_spec = (pl.BlockSpec((k, tn), lambda i, j: (0, j)) if mode == "nn"
              else pl.BlockSpec((tn, k), lambda i, j: (j, 0)))
    in_specs = [pl.BlockSpec((tm, k), lambda i, j: (i, 0)), b_spec]
    args = [a, b]
    if has_bias:
        in_specs.append(pl.BlockSpec((1, tn), lambda i, j: (0, j)))
        args.append(bias)
    outs, got = _call(kern, name=name, grid=(m // tm, n // tn), in_specs=in_specs,
                      out_specs=[pl.BlockSpec((tm, tn), lambda i, j: (i, j))],
                      out_shape=[jax.ShapeDtypeStruct((m, n), out_dtype)], args=args, sem=("parallel", "parallel"),
                      comm=comm)
    return (outs[0], got) if comm else outs[0]


def _mm_cols_nn(name, pieces, offsets, b, comm=None):
    m = pieces[0].shape[0]
    n = b.shape[1]
    tm = min(m, 1024)
    np_ = len(pieces)

    def kern(*refs):
        acc = _dot(refs[0][...], refs[np_][...], "nn")
        for p in range(1, np_):
            acc = acc + _dot(refs[p][...], refs[np_ + p][...], "nn")
        refs[2 * np_][...] = acc.astype(BF16)

    in_specs = [pl.BlockSpec((tm, a.shape[1]), lambda i: (i, 0)) for a in pieces]
    in_specs += [pl.BlockSpec((a.shape[1], n), (lambda i, r=off // a.shape[1]: (r, 0))) for a, off in zip(pieces, offsets)]
    outs, got = _call(kern, name=name, grid=(m // tm,), in_specs=in_specs,
                      out_specs=[pl.BlockSpec((tm, n), lambda i: (i, 0))],
                      out_shape=[jax.ShapeDtypeStruct((m, n), BF16)], args=list(pieces) + [b] * np_, sem=("parallel",),
                      comm=comm)
    return (outs[0], got) if comm else outs[0]


def _mm_cols_tn(name, pieces, b):
    k, n = b.shape
    tk = min(k, 512)
    nk = k // tk
    np_ = len(pieces)

    def kern(*refs):
        b_ref = refs[np_]
        o_refs, acc_refs = refs[np_ + 1:2 * np_ + 1], refs[2 * np_ + 1:]
        kk = pl.program_id(0)

        @pl.when(kk == 0)
        def _():
            for r in acc_refs:
                r[...] = jnp.zeros(r.shape, F32)

        bb = b_ref[...]
        for p in range(np_):
            acc_refs[p][...] += _dot(refs[p][...], bb, "tn")

        @pl.when(kk == nk - 1)
        def _():
            for o, r in zip(o_refs, acc_refs):
                o[...] = r[...].astype(o.dtype)

    return pl.pallas_call(
        kern, name=name, grid=(nk,),
        in_specs=[pl.BlockSpec((tk, a.shape[1]), lambda kk: (kk, 0)) for a in pieces] + [pl.BlockSpec((tk, n), lambda kk: (kk, 0))],
        out_specs=[pl.BlockSpec((a.shape[1], n), lambda kk: (0, 0)) for a in pieces],
        out_shape=[jax.ShapeDtypeStruct((a.shape[1], n), BF16) for a in pieces],
        scratch_shapes=[pltpu.VMEM((a.shape[1], n), F32) for a in pieces],
        compiler_params=_cparams(("arbitrary",)),
    )(*pieces, b)


def _mla_proj(name, qn, uq_t, kvn, ukv_t, krr, cos, sin):
    s = qn.shape[0]
    tm = min(s, 1024)

    def kern(qn_ref, uq_ref, kvn_ref, ukv_ref, kr_ref, cos_ref, sin_ref, q_out, k_out, v_out):
        qp = _dot(qn_ref[...], uq_ref[...], "nt")
        kv = _dot(kvn_ref[...], ukv_ref[...], "nt")
        c, sn = cos_ref[...], sin_ref[...]
        kr_r = kr_ref[...].astype(F32)
        ones = jnp.ones((tm, LANES), F32)
        qs, ks, vs = [], [], []
        for hh in range(MLA_HEADS):
            qs += [qp[:, 256 * hh:256 * hh + 128], _rope(qp[:, 256 * hh + 128:256 * hh + 256], c, sn)]
            ks += [kv[:, 256 * hh:256 * hh + 128], kr_r]
            vs += [kv[:, 1024 + LANES * hh:1024 + LANES * hh + LANES], ones]
        q_out[...] = (jnp.concatenate(qs, axis=1) * MLA_SCALE).astype(BF16)
        k_out[...] = jnp.concatenate(ks, axis=1).astype(BF16)
        v_out[...] = jnp.concatenate(vs, axis=1).astype(BF16)

    row = lambda w: pl.BlockSpec((tm, w), lambda i: (i, 0))
    whole = lambda a: pl.BlockSpec(a.shape, lambda i: (0, 0))
    return pl.pallas_call(
        kern, name=name, grid=(s // tm,),
        in_specs=[row(qn.shape[1]), whole(uq_t), row(kvn.shape[1]), whole(ukv_t), row(LANES), row(LANES), row(LANES)],
        out_specs=[row(1024)] * 3, out_shape=[jax.ShapeDtypeStruct((s, 1024), BF16)] * 3,
        compiler_params=_cparams(("parallel",)),
    )(qn, uq_t, kvn, ukv_t, krr, cos, sin)


def _peers():
    mx, my, mc = lax.axis_index("x"), lax.axis_index("y"), lax.axis_index("c")
    me = 4 * mx + 2 * my + mc
    out = []
    for k in range(1, N_DEV):
        px = 1 - mx if (k >> 2) & 1 else mx
        py = 1 - my if (k >> 1) & 1 else my
        pc = 1 - mc if k & 1 else mc
        out.append((k, (px, py, pc), 4 * px + 2 * py + pc))
    return me, out


def _xchg_copies(x_refs, out_refs, scatter, send_sems, recv_sems, local_sems):
    me, peers = _peers()
    local, out, arrive = [], [], []
    for a, (x, o, sc) in enumerate(zip(x_refs, out_refs, scatter)):
        mine = x.at[me] if sc else x
        local.append(pltpu.make_async_copy(mine, o.at[me], local_sems.at[a]))
        for k, dev, p in peers:
            out.append(pltpu.make_async_remote_copy(
                src_ref=x.at[p] if sc else x, dst_ref=o.at[me],
                send_sem=send_sems.at[a, k - 1], recv_sem=recv_sems.at[a, k - 1],
                device_id=dev, device_id_type=MESH))
            arrive.append(pltpu.make_async_remote_copy(
                src_ref=mine, dst_ref=o.at[p],
                send_sem=send_sems.at[a, k - 1], recv_sem=recv_sems.at[a, k - 1],
                device_id=dev, device_id_type=MESH))
    return local, out, arrive


def _xchg_start(*args):
    local, out, _ = _xchg_copies(*args)
    for cp in local + out:
        cp.start()


def _xchg_wait(*args):
    local, out, arrive = _xchg_copies(*args)
    for cp in out:
        cp.wait_send()
    for cp in arrive:
        cp.wait_recv()
    for cp in local:
        cp.wait()


def _call(kern, *, name, grid, in_specs, out_specs, out_shape, args, sem, scratch_shapes=(), vmem=VMEM_LIMIT,
          comm=None):
    if not comm:
        outs = pl.pallas_call(kern, name=name, grid=grid, in_specs=in_specs, out_specs=out_specs, out_shape=out_shape,
                              scratch_shapes=list(scratch_shapes), compiler_params=_cparams(sem, vmem))(*args)
        return outs, []
    n, ni, no, ns = len(comm), len(in_specs), len(out_specs), len(scratch_shapes)
    xs = [x for x, _ in comm]
    scatter = [sc for _, sc in comm]

    def body(*refs):
        ins, x_refs = refs[:ni], refs[ni:ni + n]
        outs, out_refs = refs[ni + n:ni + n + no], refs[ni + n + no:ni + 2 * n + no]
        scr = refs[ni + 2 * n + no:ni + 2 * n + no + ns]
        sems = refs[ni + 2 * n + no + ns:]
        ids = [pl.program_id(d) for d in range(len(grid))]
        first = functools.reduce(jnp.logical_and, [i == 0 for i in ids])
        last = functools.reduce(jnp.logical_and, [i == g - 1 for i, g in zip(ids, grid)])

        @pl.when(first)
        def _():
            _xchg_start(x_refs, out_refs, scatter, *sems)

        kern(*ins, *outs, *scr)

        @pl.when(last)
        def _():
            _xchg_wait(x_refs, out_refs, scatter, *sems)

    any_spec = pl.BlockSpec(memory_space=pl.ANY)
    res = pl.pallas_call(
        body, name=name, grid=grid,
        in_specs=list(in_specs) + [any_spec] * n, out_specs=list(out_specs) + [any_spec] * n,
        out_shape=list(out_shape) + [jax.ShapeDtypeStruct((N_DEV,) + tuple(x.shape[1:] if sc else x.shape), x.dtype)
                                     for x, sc in comm],
        scratch_shapes=list(scratch_shapes) + [pltpu.SemaphoreType.DMA((n, N_DEV - 1)),
                                               pltpu.SemaphoreType.DMA((n, N_DEV - 1)), pltpu.SemaphoreType.DMA((n,))],
        compiler_params=pltpu.CompilerParams(dimension_semantics=("arbitrary",) * len(grid), vmem_limit_bytes=vmem,
                                             has_side_effects=True),
    )(*args, *xs)
    return res[:no], res[no:]


def _exchange(name, xs, scatter):
    def nothing():
        pass

    return _call(nothing, name=name, grid=(1,), in_specs=[], out_specs=[], out_shape=[], args=[], sem=("arbitrary",),
                 comm=[(x, scatter) for x in xs])[1]


def _sum_parts(name, x):
    p, r, c = x.shape
    tr = r if r * c * p * x.dtype.itemsize <= (8 << 20) else _pick(r, (256, 128, 64, 16))

    def kern(x_ref, o_ref):
        acc = x_ref[0].astype(F32)
        for i in range(1, p):
            acc = acc + x_ref[i].astype(F32)
        o_ref[...] = acc

    return pl.pallas_call(
        kern, name=name, grid=(r // tr,),
        in_specs=[pl.BlockSpec((p, tr, c), lambda i: (0, i, 0))],
        out_specs=pl.BlockSpec((tr, c), lambda i: (i, 0)),
        out_shape=jax.ShapeDtypeStruct((r, c), F32),
        compiler_params=_cparams(("parallel",)),
    )(x)


def _adamw(name, w, m, v, g):
    r, c = w.shape
    parts = g.shape[0] if g.ndim == 3 else 0
    tr = 512 if (r > 512 and r % 512 == 0) else r

    def kern(w_ref, m_ref, v_ref, g_ref, go_ref, d_ref, mo_ref, vo_ref):
        if parts:
            gg = g_ref[0]
            for i in range(1, parts):
                gg = gg + g_ref[i]
        else:
            gg = g_ref[...]
        mm = ADAM_B1 * m_ref[...] + (1.0 - ADAM_B1) * gg
        vv = ADAM_B2 * v_ref[...] + (1.0 - ADAM_B2) * (gg * gg)
        m_hat = mm / (1.0 - ADAM_B1 ** ADAM_STEP)
        v_hat = vv / (1.0 - ADAM_B2 ** ADAM_STEP)
        go_ref[...] = gg
        d_ref[...] = -ADAM_LR * (m_hat / (jnp.sqrt(v_hat) + ADAM_EPS) + ADAM_WD * w_ref[...])
        mo_ref[...] = mm
        vo_ref[...] = vv

    spec = pl.BlockSpec((tr, c), lambda i: (i, 0))
    gspec = pl.BlockSpec((parts, tr, c), lambda i: (0, i, 0)) if parts else spec
    sh = jax.ShapeDtypeStruct((r, c), F32)
    return pl.pallas_call(
        kern, name=name, grid=(r // tr,), in_specs=[spec, spec, spec, gspec],
        out_specs=[spec] * 4, out_shape=[sh] * 4,
        compiler_params=_cparams(("parallel",)),
    )(w, m, v, g)


def _ret_tables(dec_cc, dec_cd, dec_dd, reverse):
    c = dec_cc.shape[0]
    row = lax.broadcasted_iota(jnp.int32, (c, c), 0).astype(F32)
    col = lax.broadcasted_iota(jnp.int32, (c, c), 1).astype(F32)
    pos = lax.broadcasted_iota(jnp.int32, (c, LANES), 0).astype(F32)
    if reverse:
        diff, mask = col - row, col > row
        q_exp, k_exp = c - pos, pos
    else:
        diff, mask = row - col, row >= col
        q_exp, k_exp = pos + 1.0, c - 1.0 - pos
    decay = jnp.where(mask, jnp.exp(_log_sigmoid(dec_cc) * jnp.maximum(diff, 0.0)), 0.0)
    lam_cd = _log_sigmoid(dec_cd)
    return decay, jnp.exp(lam_cd * q_exp), jnp.exp(lam_cd * k_exp), jnp.exp(_log_sigmoid(dec_dd) * float(c))


def _ret_chunk(q, k, v, st, decay, qw, kw, sd):
    scores = _bdot(q, k, "nt") * decay
    o = _bdot(scores, v, "nn") + _bdot(q * qw, st, "nn")
    st_new = st * sd + _bdot(k * kw, v, "tn")
    return o, st_new


def _ret_dec(dec_ref, h, c):
    d = dec_ref[:, h:h + 1]
    return (jnp.broadcast_to(d, (c, c)), jnp.broadcast_to(d, (c, LANES)), jnp.broadcast_to(d, (LANES, LANES)))


def _ret_table_scratch(c):
    return [pltpu.VMEM((RET_HEADS, c, c), F32), pltpu.VMEM((RET_HEADS, c, LANES), F32),
            pltpu.VMEM((RET_HEADS, c, LANES), F32), pltpu.VMEM((RET_HEADS, LANES, LANES), F32)]


def _ret_fwd(name, q, k, z, dec, reverse, chunk):
    s = q.shape[0]
    chunk = min(chunk, s)
    n = s // chunk
    cmap = (lambda i: (n - 1 - i, 0)) if reverse else (lambda i: (i, 0))
    smap = (lambda i: (n - 1 - i, 0, 0, 0)) if reverse else (lambda i: (i, 0, 0, 0))

    def kern(q_ref, k_ref, v_ref, dec_ref, o_ref, st_out_ref, st_ref, *tab_refs):
        @pl.when(pl.program_id(0) == 0)
        def _():
            st_ref[...] = jnp.zeros(st_ref.shape, F32)
            for h in range(RET_HEADS):
                for r, t in zip(tab_refs, _ret_tables(*_ret_dec(dec_ref, h, chunk), reverse)):
                    r[h] = t

        for h in range(RET_HEADS):
            sl = slice(LANES * h, LANES * (h + 1))
            st = st_ref[h]
            st_out_ref[h] = st
            o, st_new = _ret_chunk(q_ref[:, sl].astype(F32), k_ref[:, sl].astype(F32), v_ref[:, sl].astype(F32),
                                   st, *[r[h] for r in tab_refs])
            o_ref[:, sl] = o.astype(BF16)
            st_ref[h] = st_new

    return pl.pallas_call(
        kern, name=name, grid=(n,),
        in_specs=[pl.BlockSpec((chunk, 512), cmap), pl.BlockSpec((chunk, 512), cmap),
                  pl.BlockSpec((chunk, 512), cmap), pl.BlockSpec((1, RET_HEADS), lambda i: (0, 0))],
        out_specs=[pl.BlockSpec((chunk, 512), cmap), pl.BlockSpec((None, RET_HEADS, LANES, LANES), smap)],
        out_shape=[jax.ShapeDtypeStruct((s, 512), BF16), jax.ShapeDtypeStruct((n, RET_HEADS, LANES, LANES), F32)],
        scratch_shapes=[pltpu.VMEM((RET_HEADS, LANES, LANES), F32)] + _ret_table_scratch(chunk),
        compiler_params=_cparams(("arbitrary",)),
    )(q, k, z, dec)


def _ret_bwd(name, q, k, z, dec, states, do, reverse, chunk):
    s = q.shape[0]
    chunk = min(chunk, s)
    n = s // chunk
    cmap = (lambda i: (i, 0)) if reverse else (lambda i: (n - 1 - i, 0))
    smap = (lambda i: (i, 0, 0, 0)) if reverse else (lambda i: (n - 1 - i, 0, 0, 0))

    def kern(q_ref, k_ref, v_ref, dec_ref, st_in_ref, do_ref, dq_ref, dk_ref, dv_ref, ddec_ref, dst_ref, *scr):
        tab_refs, gtab_refs = scr[:4], scr[4:]
        step = pl.program_id(0)

        @pl.when(step == 0)
        def _():
            dst_ref[...] = jnp.zeros(dst_ref.shape, F32)
            for r in gtab_refs:
                r[...] = jnp.zeros(r.shape, F32)
            for h in range(RET_HEADS):
                for r, t in zip(tab_refs, _ret_tables(*_ret_dec(dec_ref, h, chunk), reverse)):
                    r[h] = t

        for h in range(RET_HEADS):
            sl = slice(LANES * h, LANES * (h + 1))
            _, vjp = jax.vjp(_ret_chunk, q_ref[:, sl].astype(F32), k_ref[:, sl].astype(F32), v_ref[:, sl].astype(F32),
                             st_in_ref[h], *[r[h] for r in tab_refs])
            grads = vjp((do_ref[:, sl].astype(F32), dst_ref[h]))
            dq_ref[:, sl] = grads[0].astype(BF16)
            dk_ref[:, sl] = grads[1].astype(BF16)
            dv_ref[:, sl] = grads[2].astype(BF16)
            dst_ref[h] = grads[3]
            for r, g in zip(gtab_refs, grads[4:]):
                r[h] += g

        @pl.when(step == n - 1)
        def _():
            lane = lax.broadcasted_iota(jnp.int32, (1, LANES), 1)
            ddec = jnp.zeros((1, LANES), F32)
            for h in range(RET_HEADS):
                _, vjp_t = jax.vjp(functools.partial(_ret_tables, reverse=reverse), *_ret_dec(dec_ref, h, chunk))
                parts = vjp_t(tuple(r[h] for r in gtab_refs))
                tot = sum(jnp.sum(jnp.sum(p, axis=1, keepdims=True), axis=0, keepdims=True) for p in parts)
                ddec = ddec + jnp.where(lane == h, tot, 0.0)
            ddec_ref[...] = ddec

    cspec = pl.BlockSpec((chunk, 512), cmap)
    return pl.pallas_call(
        kern, name=name, grid=(n,),
        in_specs=[cspec, cspec, cspec, pl.BlockSpec((1, RET_HEADS), lambda i: (0, 0)),
                  pl.BlockSpec((None, RET_HEADS, LANES, LANES), smap), cspec],
        out_specs=[cspec, cspec, cspec, pl.BlockSpec((1, LANES), lambda i: (0, 0))],
        out_shape=[jax.ShapeDtypeStruct((s, 512), BF16)] * 3 + [jax.ShapeDtypeStruct((1, LANES), F32)],
        scratch_shapes=[pltpu.VMEM((RET_HEADS, LANES, LANES), F32)] + _ret_table_scratch(chunk) * 2,
        compiler_params=_cparams(("arbitrary",)),
    )(q, k, z, dec, states, do)


def _fa_fwd(name, q, k, vx, tq, tk, nsub, comm=None):
    s = q.shape[0]
    tq, tk = min(tq, s), min(tk, s)
    nk = s // tk
    sq = tq // nsub

    def kern(q_ref, k_ref, v_ref, o_ref, lse_ref, m_ref, acc_ref):
        j = pl.program_id(2)

        @pl.when(j == 0)
        def _():
            m_ref[...] = jnp.full(m_ref.shape, -jnp.inf, F32)
            acc_ref[...] = jnp.zeros(acc_ref.shape, F32)

        kb, vb = k_ref[...], v_ref[...]
        for c in range(nsub):
            rows = pl.ds(c * sq, sq)
            sc = _dot(q_ref[rows, :], kb, "nt")
            m_prev = m_ref[rows, :]
            m_new = jnp.maximum(m_prev, jnp.max(sc, axis=1, keepdims=True))
            alpha = jnp.exp(m_prev - m_new)
            p = jnp.exp(sc - m_new)
            acc_ref[rows, :] = alpha * acc_ref[rows, :] + _dot(p, vb, "nn")
            m_ref[rows, :] = m_new

        @pl.when(j == nk - 1)
        def _():
            den = acc_ref[:, LANES:]
            o_ref[...] = (acc_ref[:, :LANES] / den).astype(BF16)
            lse_ref[...] = m_ref[...] + jnp.log(den[:, :1])

    (o, lse), got = _call(
        kern, name=name, grid=(MLA_HEADS, s // tq, nk),
        in_specs=[pl.BlockSpec((tq, 256), lambda h, i, j: (i, h)),
                  pl.BlockSpec((tk, 256), lambda h, i, j: (j, h)),
                  pl.BlockSpec((tk, 256), lambda h, i, j: (j, h))],
        out_specs=[pl.BlockSpec((tq, LANES), lambda h, i, j: (i, h)),
                   pl.BlockSpec((None, tq, 1), lambda h, i, j: (h, i, 0))],
        out_shape=[jax.ShapeDtypeStruct((s, 512), BF16), jax.ShapeDtypeStruct((MLA_HEADS, s, 1), F32)],
        scratch_shapes=[pltpu.VMEM((tq, 1), F32), pltpu.VMEM((tq, 256), F32)],
        args=[q, k, vx], sem=("parallel", "parallel", "arbitrary"), comm=comm)
    return o, lse, got


def _fa_bwd(name, q, k, kt, vx, do, lse, delta, tq, tk, nsub, comm=None):
    s = q.shape[0]
    tq, tk = min(tq, s), min(tk, s)
    nq = s // tq
    sk = tk // nsub

    nkb = s // tk

    def kern(q_ref, k_ref, kt_ref, v_ref, do_ref, lse_ref, dl_ref, dqt_ref, dk_ref, dv_ref, dqt_acc, dk_acc, dv_acc):
        j, i = pl.program_id(1), pl.program_id(2)

        @pl.when(i == 0)
        def _():
            dv_acc[...] = jnp.zeros(dv_acc.shape, F32)
            dk_acc[...] = jnp.zeros(dk_acc.shape, F32)

        @pl.when(j == 0)
        def _():
            dqt_acc[i] = jnp.zeros((256, tq), F32)

        qb, dob = q_ref[...], do_ref[...]
        lse_row, dl_row = lse_ref[...], dl_ref[...]
        dqt = dqt_acc[i]
        for c in range(nsub):
            rows = pl.ds(c * sk, sk)
            st = _dot(k_ref[rows, :], qb, "nt")
            pt = jnp.exp(st - lse_row)
            dpt = _dot(v_ref[rows, :], dob, "nt")
            dst = (pt * (dpt - dl_row)).astype(BF16)
            dv_acc[rows, :] += _dot(pt, dob, "nn")
            dk_acc[rows, :] += _dot(dst, qb, "nn")
            dqt = dqt + _dot(kt_ref[:, rows], dst, "nn")
        dqt_acc[i] = dqt

        @pl.when(i == nq - 1)
        def _():
            dv_ref[...] = dv_acc[...].astype(BF16)
            dk_ref[...] = dk_acc[...].astype(BF16)

        @pl.when(j == nkb - 1)
        def _():
            dqt_ref[i] = dqt.astype(BF16)

    outs, got = _call(
        kern, name=name, grid=(MLA_HEADS, s // tk, nq),
        in_specs=[pl.BlockSpec((tq, 256), lambda h, j, i: (i, h)),
                  pl.BlockSpec((tk, 256), lambda h, j, i: (j, h)),
                  pl.BlockSpec((None, 256, tk), lambda h, j, i: (h, 0, j)),
                  pl.BlockSpec((tk, LANES), lambda h, j, i: (j, 2 * h)),
                  pl.BlockSpec((tq, LANES), lambda h, j, i: (i, h)),
                  pl.BlockSpec((None, 1, tq), lambda h, j, i: (h, 0, i)),
                  pl.BlockSpec((None, 1, tq), lambda h, j, i: (h, 0, i))],
        out_specs=[pl.BlockSpec((None, nq, 256, tq), lambda h, j, i: (h, 0, 0, 0)),
                   pl.BlockSpec((tk, 256), lambda h, j, i: (j, h)),
                   pl.BlockSpec((tk, LANES), lambda h, j, i: (j, h))],
        out_shape=[jax.ShapeDtypeStruct((MLA_HEADS, nq, 256, tq), BF16),
                   jax.ShapeDtypeStruct((s, 1024), BF16), jax.ShapeDtypeStruct((s, 512), BF16)],
        scratch_shapes=[pltpu.VMEM((nq, 256, tq), F32), pltpu.VMEM((tk, 256), F32), pltpu.VMEM((tk, LANES), F32)],
        args=[q, k, kt, vx, do, lse, delta], sem=("parallel", "arbitrary", "arbitrary"), comm=comm)
    return outs[0], outs[1], outs[2], got


def _fill_padded(dst_ref, val, s):
    zeros = jnp.zeros((CONV_HALO, LANES), F32)
    dst_ref[pl.ds(0, CONV_HALO), :] = zeros
    dst_ref[pl.ds(CONV_HALO + s, CONV_HALO), :] = zeros
    dst_ref[pl.ds(CONV_HALO, s), :] = val


def _shifted_windows(win):
    n = win.shape[0]
    return [win] + [pltpu.roll(win, n - b, 0) for b in range(1, 8)]


def _conv_fwd(name, z, w, bias, rc=256):
    s = z.shape[0]
    rc = min(rc, s)

    def kern(a_ref, g_ref, w_ref, b_ref, o_ref, pad_ref):
        _fill_padded(pad_ref, _glu_fn(a_ref[...].astype(F32), g_ref[...].astype(F32)), s)
        wv = w_ref[...]
        bv = b_ref[...]

        def chunk(r, carry):
            base = pl.multiple_of(r * rc, rc)
            wins = _shifted_windows(pad_ref[pl.ds(base, rc + 2 * CONV_HALO), :])
            acc = jnp.broadcast_to(bv, (rc, LANES))
            for kk in range(CONV_K):
                a, b = divmod(kk + 1, 8)
                acc = acc + wv[kk:kk + 1, :] * wins[b][8 * a:8 * a + rc]
            o_ref[pl.ds(base, rc), :] = acc.astype(BF16)
            return carry

        lax.fori_loop(0, s // rc, chunk, 0)

    nblk = D // LANES
    return pl.pallas_call(
        kern, name=name, grid=(nblk,),
        in_specs=[pl.BlockSpec((s, LANES), lambda c: (0, c)), pl.BlockSpec((s, LANES), lambda c: (0, nblk + c)),
                  pl.BlockSpec((32, LANES), lambda c: (0, c)), pl.BlockSpec((1, LANES), lambda c: (0, c))],
        out_specs=pl.BlockSpec((s, LANES), lambda c: (0, c)),
        out_shape=jax.ShapeDtypeStruct((s, D), BF16),
        scratch_shapes=[pltpu.VMEM((s + 2 * CONV_HALO, LANES), F32)],
        compiler_params=_cparams(("parallel",)),
    )(z, z, w, bias)


def _conv_bwd(name, z, g, w, rc=256, comm=None):
    s = z.shape[0]
    rc = min(rc, s)

    def kern(a_ref, b_ref, g_ref, w_ref, da_ref, db_ref, dw_ref, dbias_ref, sa_ref, sb_ref, upad_ref, gpad_ref,
             dwacc_ref):
        _fill_padded(upad_ref, _glu_fn(a_ref[...].astype(F32), b_ref[...].astype(F32)), s)
        _fill_padded(gpad_ref, g_ref[...].astype(F32), s)
        dwacc_ref[...] = jnp.zeros(dwacc_ref.shape, F32)
        wv = w_ref[...]

        def chunk(r, carry):
            sum_a, sum_b = carry
            base = pl.multiple_of(r * rc, rc)
            gwins = _shifted_windows(gpad_ref[pl.ds(base, rc + 2 * CONV_HALO), :])
            uwins = _shifted_windows(upad_ref[pl.ds(base, rc + 2 * CONV_HALO), :])
            gc = g_ref[pl.ds(base, rc), :].astype(F32)
            acc = jnp.zeros((rc, LANES), F32)
            for kk in range(CONV_K):
                a, b = divmod(CONV_K - kk, 8)
                acc = acc + wv[kk:kk + 1, :] * gwins[b][8 * a:8 * a + rc]
                a, b = divmod(kk + 1, 8)
                prod = gc * uwins[b][8 * a:8 * a + rc]
                dwacc_ref[kk] += jnp.sum(prod.reshape(rc // 8, 8, LANES), axis=0)
            dwacc_ref[CONV_K] += jnp.sum(gc.reshape(rc // 8, 8, LANES), axis=0)
            av = a_ref[pl.ds(base, rc), :].astype(F32)
            sg = _sigmoid(b_ref[pl.ds(base, rc), :].astype(F32))
            d_a = acc * sg
            d_b = acc * av * sg * (1.0 - sg)
            da_ref[pl.ds(base, rc), :] = d_a.astype(BF16)
            db_ref[pl.ds(base, rc), :] = d_b.astype(BF16)
            return (sum_a + jnp.sum(d_a.reshape(rc // 8, 8, LANES), axis=0),
                    sum_b + jnp.sum(d_b.reshape(rc // 8, 8, LANES), axis=0))

        zero = jnp.zeros((8, LANES), F32)
        sum_a, sum_b = lax.fori_loop(0, s // rc, chunk, (zero, zero))
        sa_ref[...] = jnp.sum(sum_a, axis=0, keepdims=True)
        sb_ref[...] = jnp.sum(sum_b, axis=0, keepdims=True)
        tot = jnp.sum(dwacc_ref[...], axis=1)
        lane_row = lax.broadcasted_iota(jnp.int32, (32, LANES), 0)
        dw_ref[...] = jnp.where(lane_row < CONV_K, tot, 0.0)
        dbias_ref[...] = tot[CONV_K:CONV_K + 1, :]

    nblk = D // LANES
    cs = pl.BlockSpec((s, LANES), lambda c: (0, c))
    vec = pl.BlockSpec((1, LANES), lambda c: (0, c))
    outs, got = _call(
        kern, name=name, grid=(nblk,),
        in_specs=[cs, pl.BlockSpec((s, LANES), lambda c: (0, nblk + c)), cs, pl.BlockSpec((32, LANES), lambda c: (0, c))],
        out_specs=[cs, cs, pl.BlockSpec((32, LANES), lambda c: (0, c)), vec, vec, vec],
        out_shape=[jax.ShapeDtypeStruct((s, D), BF16), jax.ShapeDtypeStruct((s, D), BF16),
                   jax.ShapeDtypeStruct((32, D), F32)] + [jax.ShapeDtypeStruct((1, D), F32)] * 3,
        scratch_shapes=[pltpu.VMEM((s + 2 * CONV_HALO, LANES), F32), pltpu.VMEM((s + 2 * CONV_HALO, LANES), F32),
                        pltpu.VMEM((32, 8, LANES), F32)],
        args=[z, z, g, w], sem=("parallel",), comm=comm)
    return outs, got


def _mod_local(name, c_all, ada_w):
    def kern(c_ref, w_ref, o_ref):
        o_ref[...] = jnp.dot(_silu(c_ref[...]), w_ref[...], preferred_element_type=F32,
                             precision=lax.Precision.HIGHEST)

    return pl.pallas_call(
        kern, name=name, grid=(DEPTH,),
        in_specs=[pl.BlockSpec((N_DEV, D), lambda l: (0, 0)), pl.BlockSpec((None, D, 384), lambda l: (l, 0, 0))],
        out_specs=pl.BlockSpec((None, N_DEV, 384), lambda l: (l, 0, 0)),
        out_shape=jax.ShapeDtypeStruct((DEPTH, N_DEV, 384), F32),
        compiler_params=_cparams(("parallel",)),
    )(c_all, ada_w)


def _ada_w_grad(name, c_all_t, dmod):
    def kern(c_ref, d_ref, o_ref):
        o_ref[...] = jnp.dot(_silu(c_ref[...]), d_ref[...], preferred_element_type=F32,
                             precision=lax.Precision.HIGHEST)

    return pl.pallas_call(
        kern, name=name, grid=(DEPTH,),
        in_specs=[pl.BlockSpec((D, LANES), lambda l: (0, 0)), pl.BlockSpec((None, LANES, 384), lambda l: (l, 0, 0))],
        out_specs=pl.BlockSpec((None, D, 384), lambda l: (l, 0, 0)),
        out_shape=jax.ShapeDtypeStruct((DEPTH, D, 384), F32),
        compiler_params=_cparams(("parallel",)),
    )(c_all_t, dmod)


def _pre_fn(x, g, scale, shift):
    return _rms(x, g) * (1.0 + scale) + shift


def _post_fn(y, g, gate):
    return gate * _rms(y, g)


def _ev_post_fn(o_heads, rg, a, mg):
    normed = []
    for oh in o_heads:
        mu = jnp.mean(oh, axis=-1, keepdims=True)
        var = jnp.mean(jnp.square(oh - mu), axis=-1, keepdims=True)
        normed.append((oh - mu) * lax.rsqrt(var + EPS))
    return jnp.concatenate([jnp.concatenate(normed, axis=1) * _silu(rg), a * _silu(mg)], axis=1)


def _od_post_fn(u, g, ln_g, ln_b):
    mu = jnp.mean(u, axis=-1, keepdims=True)
    var = jnp.mean(jnp.square(u - mu), axis=-1, keepdims=True)
    y = (u - mu) * lax.rsqrt(var + EPS) * ln_g + ln_b
    return _silu(y) * _silu(g)


def _od_post_bwd(u, g, da, ln_g, ln_b):
    xc = u - jnp.mean(u, axis=-1, keepdims=True)
    r = lax.rsqrt(jnp.mean(xc * xc, axis=-1, keepdims=True) + EPS)
    xh = xc * r
    y = xh * ln_g + ln_b
    s1, s2 = _sigmoid(y), _sigmoid(g)
    dg = da * (y * s1) * (s2 * (1.0 + g * (1.0 - s2)))
    dy = da * (g * s2) * (s1 * (1.0 + y * (1.0 - s1)))
    dxh = dy * ln_g
    m1 = jnp.mean(dxh, axis=-1, keepdims=True)
    m2 = jnp.mean(dxh * xh, axis=-1, keepdims=True)
    return r * (dxh - m1 - xh * m2), dg, _colsum(dy * xh), _colsum(dy)


def _glu_fn(a, b):
    return a * _sigmoid(b)


def _heads(x):
    return [x[:, LANES * h:LANES * (h + 1)] for h in range(4)]


def _colsum(x):
    return jnp.sum(x, axis=0, keepdims=True)


def _zrows(n, c):
    return jnp.zeros((n, c), BF16)


def _ev_win_layout(wt):
    rq = [p for h in range(4) for p in (wt[64 * h:64 * h + 64], _zrows(64, D))]
    rk = [p for h in range(4) for p in (wt[256 + 64 * h:256 + 64 * h + 64], _zrows(64, D))]
    return jnp.concatenate([wt[512:1024], wt[1024:1536], wt[2240:2752], wt[1536:1920],
                            wt[2176:2240], _zrows(64, D)] + rq + rk + [wt[1920:2176]], axis=0)


def _uq_layout(wt):
    return jnp.concatenate([p for h in range(4) for p in (wt[192 * h:192 * h + 192], _zrows(64, 384))], axis=0)


def _uq_unlayout(g):
    return jnp.concatenate([g[256 * h:256 * h + 192] for h in range(4)], axis=0)


def _ukv_layout(wt):
    kpart = [p for h in range(4) for p in (wt[256 * h:256 * h + 128], _zrows(128, 256))]
    vpart = [wt[256 * h + 128:256 * h + 256] for h in range(4)]
    return jnp.concatenate(kpart + vpart, axis=0)


def _ukv_unlayout(g):
    return jnp.concatenate([p for h in range(4) for p in (g[256 * h:256 * h + 128], g[1024 + 128 * h:1024 + 128 * h + 128])],
                           axis=0)


def kernel(x, c, positions, ada_w, ada_b, pre_g, post_g, ev_w_in, ev_dec_f, ev_dec_b, ev_q_norm_g, ev_w_uq, ev_kv_norm_g, ev_w_ukv, ev_w_out, od_w_in, od_b_in, od_dw_w, od_dw_b, od_ln_g, od_ln_b, od_w_out, loss_target, m_ada_w, m_ada_b, m_pre_g, m_post_g, m_ev_w_in, m_ev_dec_f, m_ev_dec_b, m_ev_q_norm_g, m_ev_w_uq, m_ev_kv_norm_g, m_ev_w_ukv, m_ev_w_out, m_od_w_in, m_od_b_in, m_od_dw_w, m_od_dw_b, m_od_ln_g, m_od_ln_b, m_od_w_out, v_ada_w, v_ada_b, v_pre_g, v_post_g, v_ev_w_in, v_ev_dec_f, v_ev_dec_b, v_ev_q_norm_g, v_ev_w_uq, v_ev_kv_norm_g, v_ev_w_ukv, v_ev_w_out, v_od_w_in, v_od_b_in, v_od_dw_w, v_od_dw_b, v_od_ln_g, v_od_ln_b, v_od_w_out):
    s = x.shape[1]
    me = 4 * lax.axis_index("x") + 2 * lax.axis_index("y") + lax.axis_index("c")
    x0 = x.reshape(s, D)
    tgt = loss_target.reshape(s, D)
    ret_chunk = 256
    fa_cfg_f = ((min(4096, s // 2), min(2048, s // 2), min(16, s // 512)),) * 2
    fa_cfg_b = ((min(2048, s // 2), min(4096, s // 2), min(16, s // 512)),) * 2

    start_parts = [c.reshape(-1), od_b_in.reshape(-1), od_dw_w.reshape(-1), od_dw_b.reshape(-1),
                   od_ln_g.reshape(-1), od_ln_b.reshape(-1)]
    start_sizes = [p.shape[0] for p in start_parts]
    start_len = -(-sum(start_sizes) // 1024) * 1024
    start_vec = jnp.concatenate(start_parts + [jnp.zeros((start_len - sum(start_sizes),), F32)])
    start_all = _exchange("gather_start", [start_vec.reshape(-1, LANES)], False)[0].reshape(N_DEV, start_len)
    offs = np.cumsum([0] + start_sizes)
    c_all = start_all[:, offs[0]:offs[1]]
    b_in_all = start_all[:, offs[1]:offs[2]].reshape(N_DEV, 2, 384).transpose(1, 0, 2).reshape(2, 1, ZW_OD)
    dw_w_all = start_all[:, offs[2]:offs[3]].reshape(N_DEV, 2, CONV_K, LANES).transpose(1, 2, 0, 3).reshape(2, CONV_K, D)
    dw_w_all = jnp.concatenate([dw_w_all, jnp.zeros((2, 1, D), F32)], axis=1)
    dw_b_all = start_all[:, offs[3]:offs[4]].reshape(N_DEV, 2, LANES).transpose(1, 0, 2).reshape(2, 1, D)
    ln_g_all = start_all[:, offs[4]:offs[5]].reshape(N_DEV, 2, LANES).transpose(1, 0, 2).reshape(2, 1, D)
    ln_b_all = start_all[:, offs[5]:offs[6]].reshape(N_DEV, 2, LANES).transpose(1, 0, 2).reshape(2, 1, D)

    mod_loc = _mod_local("mod_local", c_all, ada_w)
    mod_all = _exchange("gather_mod", [mod_loc.reshape(DEPTH * N_DEV, 384)], False)[0].reshape(N_DEV, DEPTH, N_DEV, 384)
    mod = lax.dynamic_index_in_dim(mod_all, me, axis=2, keepdims=False)
    mod = mod.transpose(1, 0, 2).reshape(DEPTH, 3 * D) + ada_b
    shift = [mod[l:l + 1, 0:D] for l in range(DEPTH)]
    scale = [mod[l:l + 1, D:2 * D] for l in range(DEPTH)]
    gate = [mod[l:l + 1, 2 * D:3 * D] for l in range(DEPTH)]

    def ev_shards(i):
        return [ev_w_in[i].T.astype(BF16), ev_w_uq[i].T.astype(BF16), ev_w_ukv[i].T.astype(BF16), ev_w_out[i].astype(BF16)]

    def od_shards(i):
        return [od_w_in[i].T.astype(BF16), od_w_out[i].astype(BF16)]

    def full(got):
        return [g.reshape(N_DEV * g.shape[1], g.shape[2]) for g in got]

    def ev_full(got):
        win_t, uq_t, ukv_t, wout = full(got)
        return (_ev_win_layout(win_t), _uq_layout(uq_t), _ukv_layout(ukv_t), wout)

    win0_t = _ev_win_layout(full(_exchange("gather_w_ev0", ev_shards(0)[:1], False))[0])
    ev_w = [None, None]
    od_w = [None, None]
    later_w = [(t, False) for t in od_shards(0) + ev_shards(1) + od_shards(1)]

    inv_freq = ROPE_BASE ** (-jnp.arange(0, 64, 2, dtype=F32) / 64)
    invf = jnp.tile(inv_freq, 4).reshape(1, LANES)
    sgn = jnp.tile(jnp.concatenate([-jnp.ones((32,), F32), jnp.ones((32,), F32)]), 2).reshape(1, LANES)

    def rope_body(rows, vecs):
        ang = rows[0].astype(F32) * vecs[0]
        return [jnp.cos(ang), jnp.sin(ang) * vecs[1]], []

    (cos_t, sin_t), _ = _rowwise("rope_tables", rope_body, [(positions.reshape(s, 1), 1, 0)], [invf, sgn],
                                 [(LANES, F32), (LANES, F32)])

    saved = []
    xl = x0
    for l in range(DEPTH):
        i = l // 2
        sv = dict(x=xl)

        if l == 0:
            def pre_body(rows, vecs):
                return [_pre_fn(rows[0], *vecs)], []

            (h,), _ = _rowwise("pre0", pre_body, [(xl, D, 0)], [pre_g[0:1], scale[0], shift[0]], [(D, BF16)])
        sv["h"] = h
        if l % 2 == 0:
            if l == 0:
                z, got = _mm("ev_in0", h, win0_t, "nt", out_dtype=BF16, comm=[(t, False) for t in ev_shards(0)[1:]])
                uq0_t, ukv0_t, wout0 = full(got)
                ev_w[0] = (win0_t, _uq_layout(uq0_t), _ukv_layout(ukv0_t), wout0)
                win_t, uq_t, ukv_t, wout = ev_w[0]
            else:
                win_t, uq_t, ukv_t, wout = ev_w[i]
                z = _mm(f"ev_in{l}", h, win_t, "nt", out_dtype=BF16)
            sv["z"] = z
            dec_f, dec_b = ev_dec_f[i:i + 1], ev_dec_b[i:i + 1]

            def prep_body(rows, vecs):
                rq, rk, cq, ckv, kr, cos, sin = rows
                return [_rope(rq, cos, sin), _rope(rk, cos, sin) * RET_SCALE, _rms(cq, vecs[0]), _rms(ckv, vecs[1]),
                        _rope(kr, cos, sin)], []

            (rq_r, rk_r, qn, kvn, krr), _ = _rowwise(
                f"ev_prep{l}", prep_body,
                [(z, 512, 4), (z, 512, 5), (z, 384, 4), (z, 256, 12), (z, LANES, 15), (cos_t, LANES, 0), (sin_t, LANES, 0)],
                [ev_q_norm_g[i:i + 1], ev_kv_norm_g[i:i + 1]],
                [(512, BF16), (512, BF16), (384, BF16), (256, BF16), (LANES, BF16)])
            sv.update(rq_r=rq_r, rk_r=rk_r, qn=qn, kvn=kvn)
            o_f, st_f = _ret_fwd(f"ret_f{l}", rq_r, rk_r, z, dec_f, False, ret_chunk)
            o_b, st_b = _ret_fwd(f"ret_b{l}", rq_r, rk_r, z, dec_b, True, ret_chunk)
            sv.update(o_f=o_f, o_b=o_b, st_f=st_f, st_b=st_b)
            qcat, kcat, v_x = _mla_proj(f"mla_proj{l}", qn, uq_t, kvn, ukv_t, krr, cos_t, sin_t)
            a_mla, lse, got = _fa_fwd(f"fa_fwd{l}", qcat, kcat, v_x, *fa_cfg_f[i], comm=later_w if l == 0 else None)
            if l == 0:
                od_w[0], ev_w[1], od_w[1] = tuple(full(got[0:2])), ev_full(got[2:6]), tuple(full(got[6:8]))
            sv.update(qcat=qcat, kcat=kcat, v_x=v_x, a_mla=a_mla, lse=lse)

            def ev_post_body(rows, vecs):
                of, ob, rg, a, mg = rows
                return [_ev_post_fn(_heads(of + ob), rg, a, mg)], []

            (act,), _ = _rowwise(f"ev_post{l}", ev_post_body,
                                 [(o_f, 512, 0), (o_b, 512, 0), (z, 512, 1), (a_mla, 512, 0), (z, 512, 2)], [], [(D, BF16)])
        else:
            win_t, wout = od_w[i]
            z = _mm(f"od_in{l}", h, win_t, "nt", out_dtype=BF16, bias=b_in_all[i])
            sv["z"] = z

            u2 = _conv_fwd(f"conv{l}", z, dw_w_all[i], dw_b_all[i])
            sv.update(u2=u2)

            def od_post_body(rows, vecs):
                return [_od_post_fn(rows[0], rows[1], vecs[0], vecs[1])], []

            (act,), _ = _rowwise(f"od_post{l}", od_post_body, [(u2, D, 0), (z, D, 2)], [ln_g_all[i], ln_b_all[i]],
                                 [(D, BF16)])
        sv["act"] = act
        y = _mm(f"out{l}", act, wout, "nn")
        sv["y"] = y

        saved.append(sv)
        if l < DEPTH - 1:
            def post_body(rows, vecs):
                xn = rows[0] + _post_fn(rows[1], vecs[0], vecs[1])
                return [xn, _pre_fn(xn, vecs[2], vecs[3], vecs[4])], []

            (xl, h), _ = _rowwise(f"post{l}", post_body, [(xl, D, 0), (y, D, 0)],
                                  [post_g[l:l + 1], gate[l], pre_g[l + 1:l + 2], scale[l + 1], shift[l + 1]],
                                  [(D, F32), (D, BF16)])
        else:
            def post_body(rows, vecs):
                diff = rows[0] + _post_fn(rows[1], vecs[0], vecs[1]) - rows[2]
                return [diff * (1.0 / D)], [_colsum(diff * diff) * (0.5 / D)]

            (dx,), (loss_lanes,) = _rowwise(f"post{l}", post_body, [(xl, D, 0), (y, D, 0), (tgt, D, 0)],
                                            [post_g[l:l + 1], gate[l]], [(D, F32)], [(1, D)])
    loss = lax.psum(jnp.sum(loss_lanes), ("x", "y", "c"))

    g_pre, g_post, g_mod = [None] * DEPTH, [None] * DEPTH, [None] * DEPTH
    g_dec_f, g_dec_b, g_qn, g_kvn = [None] * 2, [None] * 2, [None] * 2, [None] * 2
    g_b_in, g_dw_w, g_dw_b, g_ln_g, g_ln_b = [None] * 2, [None] * 2, [None] * 2, [None] * 2, [None] * 2
    recv_ev, recv_od = [None] * 2, [None] * 2
    pending = []
    for l in reversed(range(DEPTH)):
        i = l // 2
        sv = saved[l]

        if l == DEPTH - 1:
            def post_bwd_body(rows, vecs):
                yv, dxn = rows
                _, vjp = jax.vjp(_post_fn, yv, vecs[0], vecs[1])
                d_y, dg, d_gate = vjp(dxn)
                return [d_y], [dg, d_gate]

            (dy,), (dpost, dgate) = _rowwise(f"post_bwd{l}", post_bwd_body, [(sv["y"], D, 0), (dx, D, 0)],
                                             [post_g[l:l + 1], gate[l]], [(D, BF16)], [(1, D), (1, D)])
        g_post[l] = dpost
        dgate_l = dgate
        wout = ev_w[i][3] if l % 2 == 0 else od_w[i][1]
        dact = _mm(f"out_dgrad{l}", dy, wout, "nt", out_dtype=BF16)
        d_wout = _mm(f"out_wgrad{l}", sv["act"], dy, "tn", out_dtype=BF16)
        z = sv["z"]
        if l % 2 == 0:
            win_t, uq_t, ukv_t, _ = ev_w[i]
            dec_f, dec_b = ev_dec_f[i:i + 1], ev_dec_b[i:i + 1]

            def ev_post_bwd_body(rows, vecs):
                of, ob, rg, a, mg, da = rows
                _, vjp = jax.vjp(_ev_post_fn, _heads(of + ob), rg, a, mg)
                do_heads, drg, d_a, dmg = vjp(da)
                deltas = [jnp.sum(dh_ * ah_, axis=1, keepdims=True) for dh_, ah_ in zip(_heads(d_a), _heads(a))]
                return [jnp.concatenate(do_heads, axis=1), drg, d_a, dmg] + deltas, []

            (do_ret, drg, do_mla, dmg, dl0, dl1, dl2, dl3), _ = _rowwise(
                f"ev_post_bwd{l}", ev_post_bwd_body,
                [(sv["o_f"], 512, 0), (sv["o_b"], 512, 0), (z, 512, 1), (sv["a_mla"], 512, 0), (z, 512, 2), (dact, D, 0)],
                [], [(512, BF16), (512, BF16), (512, BF16), (512, BF16)] + [(1, F32)] * 4)
            delta = jnp.stack([dl0, dl1, dl2, dl3]).reshape(MLA_HEADS, 1, s)
            lse = sv["lse"].reshape(MLA_HEADS, 1, s)
            kt = sv["kcat"].reshape(s, MLA_HEADS, 256).transpose(1, 2, 0)
            if l == 0:
                pending = pending + [(d_wout.reshape(N_DEV, 128, D), True)]
            dqt, dkcat, dv, got = _fa_bwd(f"fa_bwd{l}", sv["qcat"], sv["kcat"], kt, sv["v_x"], do_mla, lse, delta,
                                          *fa_cfg_b[i], comm=pending)
            recv_od[i] = got[0:2]
            wout_recv = got[2:]
            dqcat = dqt.transpose(1, 3, 0, 2).reshape(s, 1024)

            def mla_prep_bwd_body(rows, vecs):
                dq, dk, dvv, cos, sin = rows
                qs = []
                dkrr = jnp.zeros((dq.shape[0], LANES), F32)
                for hh in range(4):
                    qs += [dq[:, 256 * hh:256 * hh + 128], _rope_t(dq[:, 256 * hh + 128:256 * hh + 256], cos, sin)]
                    dkrr = dkrr + dk[:, 256 * hh + 128:256 * hh + 256]
                return [jnp.concatenate(qs, axis=1) * MLA_SCALE, jnp.concatenate([dk, dvv], axis=1), dkrr], []

            (dq_pad, dkv_pad, dkrr), _ = _rowwise(
                f"mla_prep_bwd{l}", mla_prep_bwd_body,
                [(dqcat, 1024, 0), (dkcat, 1024, 0), (dv, 512, 0), (cos_t, LANES, 0), (sin_t, LANES, 0)], [],
                [(1024, BF16), (1536, BF16), (LANES, F32)])
            dqn = _mm(f"uq_dgrad{l}", dq_pad, uq_t, "nn")
            d_uq = _mm(f"uq_wgrad{l}", dq_pad, sv["qn"], "tn", out_dtype=BF16)
            dkvn = _mm(f"ukv_dgrad{l}", dkv_pad, ukv_t, "nn")
            d_ukv = _mm(f"ukv_wgrad{l}", dkv_pad, sv["kvn"], "tn", out_dtype=BF16)
            dq_f, dk_f, dv_f, ddec_f = _ret_bwd(f"ret_f_bwd{l}", sv["rq_r"], sv["rk_r"], z, dec_f, sv["st_f"], do_ret,
                                                False, ret_chunk)
            dq_b, dk_b, dv_b, ddec_b = _ret_bwd(f"ret_b_bwd{l}", sv["rq_r"], sv["rk_r"], z, dec_b, sv["st_b"], do_ret,
                                                True, ret_chunk)
            g_dec_f[i], g_dec_b[i] = ddec_f[:, :RET_HEADS], ddec_b[:, :RET_HEADS]

            def prep_bwd_body(rows, vecs):
                cq, ckv, cos, sin, dqf, dqb, dkf, dkb, dvf, dvb, d_qn, d_kvn, d_krr = rows
                _, vjp_q = jax.vjp(_rms, cq, vecs[0])
                dcq, dgq = vjp_q(d_qn)
                _, vjp_kv = jax.vjp(_rms, ckv, vecs[1])
                dckv, dgkv = vjp_kv(d_kvn)
                return [dvf + dvb, dcq, _rope_t(d_krr, cos, sin), _rope_t(dqf + dqb, cos, sin),
                        _rope_t(dkf + dkb, cos, sin) * RET_SCALE, dckv], [dgq, dgkv]

            (drv, dcq, dkr, drq, drk, dckv), (dgq, dgkv) = _rowwise(
                f"ev_prep_bwd{l}", prep_bwd_body,
                [(z, 384, 4), (z, 256, 12), (cos_t, LANES, 0), (sin_t, LANES, 0), (dq_f, 512, 0), (dq_b, 512, 0),
                 (dk_f, 512, 0), (dk_b, 512, 0), (dv_f, 512, 0), (dv_b, 512, 0), (dqn, 384, 0), (dkvn, 256, 0),
                 (dkrr, LANES, 0)],
                [ev_q_norm_g[i:i + 1], ev_kv_norm_g[i:i + 1]],
                [(512, BF16), (384, BF16), (LANES, BF16), (512, BF16), (512, BF16), (256, BF16)], [(1, 384), (1, 256)])
            g_qn[i], g_kvn[i] = dgq, dgkv
            dz = [drv, drg, dmg, dcq, dkr, drq, drk, dckv]
            dz_off = [ZL_EV[nm][0] for nm in ("rv", "rg", "mg", "cq", "kr", "rq", "rk", "ckv")]
            g_rv, g_rg, g_mg, g_cq, g_kr, g_rq, g_rk, g_ckv = _mm_cols_tn(f"in_wgrad{l}", dz, sv["h"])
            d_win = jnp.concatenate([g_rq[128 * hh:128 * hh + 64] for hh in range(4)]
                                    + [g_rk[128 * hh:128 * hh + 64] for hh in range(4)]
                                    + [g_rv, g_rg, g_cq, g_ckv, g_kr[:64], g_mg], axis=0)
            pending = [(d_win.reshape(N_DEV, 344, D), True), (_uq_unlayout(d_uq).reshape(N_DEV, 96, 384), True),
                       (_ukv_unlayout(d_ukv).reshape(N_DEV, 128, 256), True)]
            if l > 0:
                pending.append((d_wout.reshape(N_DEV, 128, D), True))
        else:
            win_t, _ = od_w[i]

            def od_post_bwd_body(rows, vecs):
                u2, gg, da = rows
                du2, dgg, dlg, dlb = _od_post_bwd(u2, gg, da, vecs[0], vecs[1])
                return [du2, dgg], [dlg, dlb, _colsum(dgg)]

            (du2, dg_gate), (dlg, dlb, dbg) = _rowwise(
                f"od_post_bwd{l}", od_post_bwd_body, [(sv["u2"], D, 0), (z, D, 2), (dact, D, 0)],
                [ln_g_all[i], ln_b_all[i]], [(D, BF16), (D, BF16)], [(1, D), (1, D), (1, D)])
            g_ln_g[i], g_ln_b[i] = dlg, dlb
            (d_a, d_b, d_dw, d_dwb, dba, dbb), got = _conv_bwd(f"conv_bwd{l}", z, du2, dw_w_all[i], comm=pending)
            if pending:
                recv_ev[i + 1] = got
            g_dw_w[i], g_dw_b[i] = d_dw[:CONV_K], d_dwb
            g_b_in[i] = jnp.concatenate([dba, dbb, dbg], axis=1)
            dz, dz_off = [d_a, d_b, dg_gate], [0, D, 2 * D]
            d_win = jnp.concatenate(_mm_cols_tn(f"in_wgrad{l}", dz, sv["h"]), axis=0)
            pending = [(d_win.reshape(N_DEV, 384, D), True), (d_wout.reshape(N_DEV, 128, D), True)]
        if l == 0:
            dh, got = _mm_cols_nn(f"in_dgrad{l}", dz, dz_off, win_t, comm=pending)
            recv_ev[0] = list(got) + list(wout_recv)
        else:
            dh = _mm_cols_nn(f"in_dgrad{l}", dz, dz_off, win_t)

        if l > 0:
            def pre_bwd_body(rows, vecs):
                xv, d_h, dxn, yv = rows
                _, vjp = jax.vjp(_pre_fn, xv, vecs[0], vecs[1], vecs[2])
                d_x, dg, dsc, dsh = vjp(d_h)
                d_x = d_x + dxn
                _, vjp_p = jax.vjp(_post_fn, yv, vecs[3], vecs[4])
                d_y, dgp, d_gate = vjp_p(d_x)
                return [d_x, d_y], [dg, dsc, dsh, dgp, d_gate]

            (dx, dy), (dpre, dscale, dshift, dpost, dgate) = _rowwise(
                f"pre_bwd{l}", pre_bwd_body, [(sv["x"], D, 0), (dh, D, 0), (dx, D, 0), (saved[l - 1]["y"], D, 0)],
                [pre_g[l:l + 1], scale[l], shift[l], post_g[l - 1:l], gate[l - 1]], [(D, F32), (D, BF16)], [(1, D)] * 5)
        else:
            def pre_bwd_body(rows, vecs):
                xv, d_h, dxn = rows
                _, vjp = jax.vjp(_pre_fn, xv, *vecs)
                d_x, dg, dsc, dsh = vjp(d_h)
                return [d_x + dxn], [dg, dsc, dsh]

            (dx,), (dpre, dscale, dshift) = _rowwise(f"pre_bwd{l}", pre_bwd_body, [(sv["x"], D, 0), (dh, D, 0), (dx, D, 0)],
                                                     [pre_g[l:l + 1], scale[l], shift[l]], [(D, F32)], [(1, D)] * 3)
        g_pre[l] = dpre
        g_mod[l] = jnp.concatenate([dshift, dscale, dgate_l], axis=1)

    grad_x = dx.reshape(1, s, D)

    end_parts = [jnp.concatenate(g_mod, axis=0), jnp.concatenate(g_pre, axis=0), jnp.concatenate(g_post, axis=0),
                 jnp.concatenate(g_dec_f, axis=0), jnp.concatenate(g_dec_b, axis=0), jnp.concatenate(g_qn, axis=0),
                 jnp.concatenate(g_kvn, axis=0), jnp.concatenate(g_b_in, axis=0), jnp.stack(g_dw_w),
                 jnp.concatenate(g_dw_b, axis=0), jnp.concatenate(g_ln_g, axis=0), jnp.concatenate(g_ln_b, axis=0)]
    end_shapes = [p.shape for p in end_parts]
    end_sizes = [int(np.prod(sh)) for sh in end_shapes]
    end_len = -(-sum(end_sizes) // 1024) * 1024
    end_vec = jnp.concatenate([p.reshape(-1) for p in end_parts] + [jnp.zeros((end_len - sum(end_sizes),), F32)])
    end_all = _exchange("gather_end", [end_vec.reshape(-1, LANES)], False)[0].reshape(N_DEV, end_len)
    eo = np.cumsum([0] + end_sizes)
    ends = [end_all[:, eo[j]:eo[j + 1]].reshape((N_DEV,) + tuple(end_shapes[j])) for j in range(len(end_parts))]
    (dmod_all, pre_all, post_all, decf_all, decb_all, qn_all, kvn_all, bin_all, dww_all, dwb_all, lng_all,
     lnb_all) = ends

    def pack_rep(*ts):
        lead = ts[0].ndim - 2
        return jnp.concatenate([t.reshape(t.shape[:lead] + (-1,)) for t in ts], axis=-1)

    rep_sizes = [DEPTH * 3 * D, DEPTH * D, DEPTH * D, 8, 8, 2 * 384, 2 * 256]
    rep_len = -(-sum(rep_sizes) // 1024) * 1024
    rep_pad = rep_len - sum(rep_sizes)

    def rep_rows(flat):
        padz = jnp.zeros(flat.shape[:-1] + (rep_pad,), F32)
        return jnp.concatenate([flat, padz], axis=-1).reshape(flat.shape[:-1] + (rep_len // LANES, LANES))

    rep_w = rep_rows(pack_rep(ada_b, pre_g, post_g, ev_dec_f, ev_dec_b, ev_q_norm_g, ev_kv_norm_g))
    rep_m = rep_rows(pack_rep(m_ada_b, m_pre_g, m_post_g, m_ev_dec_f, m_ev_dec_b, m_ev_q_norm_g, m_ev_kv_norm_g))
    rep_v = rep_rows(pack_rep(v_ada_b, v_pre_g, v_post_g, v_ev_dec_f, v_ev_dec_b, v_ev_q_norm_g, v_ev_kv_norm_g))
    rep_g = rep_rows(pack_rep(dmod_all, pre_all, post_all, decf_all, decb_all, qn_all, kvn_all))
    rep_out = _adamw("adamw_rep", rep_w, rep_m, rep_v, rep_g)
    ro = np.cumsum([0] + rep_sizes)
    rep_shapes = [(DEPTH, 3 * D), (DEPTH, D), (DEPTH, D), (2, 4), (2, 4), (2, 384), (2, 256)]

    def unpack_rep(t):
        flat = t.reshape(-1)
        return [flat[ro[j]:ro[j + 1]].reshape(rep_shapes[j]) for j in range(len(rep_shapes))]

    rep_res = [unpack_rep(t) for t in rep_out]

    def my_cols(t, width):
        return lax.dynamic_slice_in_dim(t, me * width, width, axis=t.ndim - 1)

    def vec_adamw(name, w, m, v, g_all, width):
        g = my_cols(g_all, width)
        r = _adamw(name, w.reshape(-1, width), m.reshape(-1, width), v.reshape(-1, width),
                   g.reshape(N_DEV, -1, width))
        return [t.reshape(w.shape) for t in r]

    res_b_in = vec_adamw("adamw_b_in", od_b_in, m_od_b_in, v_od_b_in, bin_all, 384)
    res_dw_w = vec_adamw("adamw_dw_w", od_dw_w, m_od_dw_w, v_od_dw_w, dww_all, LANES)
    res_dw_b = vec_adamw("adamw_dw_b", od_dw_b, m_od_dw_b, v_od_dw_b, dwb_all, LANES)
    res_ln_g = vec_adamw("adamw_ln_g", od_ln_g, m_od_ln_g, v_od_ln_g, lng_all, LANES)
    res_ln_b = vec_adamw("adamw_ln_b", od_ln_b, m_od_ln_b, v_od_ln_b, lnb_all, LANES)

    dmod_mine = my_cols(dmod_all, 384).transpose(1, 0, 2)
    dmod_pad = jnp.concatenate([dmod_mine, jnp.zeros((DEPTH, LANES - N_DEV, 384), F32)], axis=1)
    c_all_t = jnp.concatenate([c_all.T, jnp.zeros((D, LANES - N_DEV), F32)], axis=1)
    g_ada_w = _ada_w_grad("ada_w_grad", c_all_t, dmod_pad)
    res_ada_w = [t.reshape(ada_w.shape) for t in
                 _adamw("adamw_ada_w", ada_w.reshape(-1, 384), m_ada_w.reshape(-1, 384), v_ada_w.reshape(-1, 384),
                        g_ada_w.reshape(-1, 384))]

    ev_sh = [[_sum_parts(f"sum_g_ev{i}_{j}", r) for j, r in enumerate(recv_ev[i])] for i in range(2)]
    od_sh = [[_sum_parts(f"sum_g_od{i}_{j}", r) for j, r in enumerate(recv_od[i])] for i in range(2)]

    def mat_adamw(name, w, m, v, g):
        r = _adamw(name, w.reshape(-1, w.shape[-1]), m.reshape(-1, w.shape[-1]), v.reshape(-1, w.shape[-1]),
                   g.reshape(-1, w.shape[-1]))
        return [t.reshape(w.shape) for t in r]

    res_ev_w_in = mat_adamw("adamw_ev_w_in", ev_w_in, m_ev_w_in, v_ev_w_in, jnp.stack([ev_sh[i][0].T for i in range(2)]))
    res_ev_w_uq = mat_adamw("adamw_ev_w_uq", ev_w_uq, m_ev_w_uq, v_ev_w_uq, jnp.stack([ev_sh[i][1].T for i in range(2)]))
    res_ev_w_ukv = mat_adamw("adamw_ev_w_ukv", ev_w_ukv, m_ev_w_ukv, v_ev_w_ukv,
                             jnp.stack([ev_sh[i][2].T for i in range(2)]))
    res_ev_w_out = mat_adamw("adamw_ev_w_out", ev_w_out, m_ev_w_out, v_ev_w_out, jnp.stack([ev_sh[i][3] for i in range(2)]))
    res_od_w_in = mat_adamw("adamw_od_w_in", od_w_in, m_od_w_in, v_od_w_in, jnp.stack([od_sh[i][0].T for i in range(2)]))
    res_od_w_out = mat_adamw("adamw_od_w_out", od_w_out, m_od_w_out, v_od_w_out, jnp.stack([od_sh[i][1] for i in range(2)]))

    per_weight = [res_ada_w] + [[rep_res[t][j] for t in range(4)] for j in range(3)]
    per_weight += [res_ev_w_in, [rep_res[t][3] for t in range(4)], [rep_res[t][4] for t in range(4)],
                   [rep_res[t][5] for t in range(4)], res_ev_w_uq, [rep_res[t][6] for t in range(4)], res_ev_w_ukv,
                   res_ev_w_out, res_od_w_in, res_b_in, res_dw_w, res_dw_b, res_ln_g, res_ln_b, res_od_w_out]
    outs = [loss, grad_x]
    for t in range(4):
        outs += [pw[t] for pw in per_weight]
    return tuple(outs)
```

```python
import functools

import numpy as np
import jax
import jax.numpy as jnp
from jax import lax
from jax.experimental import pallas as pl
from jax.experimental.pallas import tpu as pltpu

F32 = jnp.float32
BF16 = jnp.bfloat16
MESH = pl.DeviceIdType.MESH

N_DEV = 8
D = 1024
DEPTH = 4
EPS = 1e-6
RET_HEADS = 4
MLA_HEADS = 4
RET_SCALE = 64 ** -0.5
MLA_SCALE = 192 ** -0.5
CONV_K = 31
CONV_HALO = 16
ROPE_BASE = 10000.0

ADAM_LR = 0.001
ADAM_B1 = 0.9
ADAM_B2 = 0.999
ADAM_EPS = 1e-08
ADAM_WD = 0.01
ADAM_STEP = 10

LANES = 128
VMEM_LIMIT = 48 * 1024 * 1024

ZL_EV = dict(rv=(0, 512), rg=(512, 512), mg=(1024, 512), cq=(1536, 384), kr=(1920, 128),
             rq=(2048, 512), rk=(2560, 512), ckv=(3072, 256))
ZW_EV = 3328
ZW_OD = 3072


def _cparams(sem, vmem=VMEM_LIMIT):
    return pltpu.CompilerParams(dimension_semantics=sem, vmem_limit_bytes=vmem)


def _pick(n, prefs):
    for p in prefs:
        if n % p == 0:
            return p
    return n


def _sigmoid(x):
    return 0.5 * (jnp.tanh(0.5 * x) + 1.0)


def _silu(x):
    return x * _sigmoid(x)


def _rms(x, g):
    return x * lax.rsqrt(jnp.mean(x * x, axis=-1, keepdims=True) + EPS) * g


def _log_sigmoid(x):
    return jnp.minimum(x, 0.0) - jnp.log(1.0 + jnp.exp(jnp.minimum(x, -x)))


def _tile_lanes(t, width):
    reps = width // t.shape[1]
    return t if reps == 1 else jnp.concatenate([t] * reps, axis=1)


def _rot_half(x):
    w = x.shape[1]
    lane = lax.broadcasted_iota(jnp.int32, x.shape, 1)
    first = jnp.bitwise_and(lane, 63) < 32
    return jnp.where(first, pltpu.roll(x, w - 32, 1), pltpu.roll(x, 32, 1))


def _rope(x, cos, sin):
    w = x.shape[1]
    return x * _tile_lanes(cos, w) + _rot_half(x) * _tile_lanes(sin, w)


def _rope_t(dy, cos, sin):
    w = dy.shape[1]
    return dy * _tile_lanes(cos, w) + _rot_half(dy * _tile_lanes(sin, w))


_DN = {"nn": (((1,), (0,)), ((), ())), "nt": (((1,), (1,)), ((), ())), "tn": (((0,), (0,)), ((), ()))}


def _dot(a, b, mode):
    return lax.dot_general(a.astype(BF16), b.astype(BF16), _DN[mode], preferred_element_type=F32)


@functools.partial(jax.custom_vjp, nondiff_argnums=(2,))
def _bdot(a, b, mode):
    return _dot(a, b, mode)


def _bdot_fwd(a, b, mode):
    return _dot(a, b, mode), (a, b)


def _bdot_bwd(mode, res, g):
    a, b = res
    if mode == "nn":
        return _dot(g, b, "nt"), _dot(a, g, "tn")
    if mode == "nt":
        return _dot(g, b, "nn"), _dot(g, a, "tn")
    return _dot(b, g, "nt"), _dot(a, g, "nn")


_bdot.defvjp(_bdot_fwd, _bdot_bwd)


def _rowwise(name, body, row_ins, vec_ins, row_outs, red_outs=(), tile=512):
    s = row_ins[0][0].shape[0]
    tile = min(tile, s)
    nr, nv, no = len(row_ins), len(vec_ins), len(row_outs)

    def kern(*refs):
        rows = [r[...].astype(F32) if r.dtype == BF16 else r[...] for r in refs[:nr]]
        vecs = [r[...] for r in refs[nr:nr + nv]]
        outs, reds = body(rows, vecs)
        for r, o in zip(refs[nr + nv:nr + nv + no], outs):
            r[...] = o.astype(r.dtype)
        red_refs = refs[nr + nv + no:]
        if red_refs:
            @pl.when(pl.program_id(0) == 0)
            def _():
                for r in red_refs:
                    r[...] = jnp.zeros(r.shape, r.dtype)
            for r, v in zip(red_refs, reds):
                r[...] += v

    in_specs = [pl.BlockSpec((tile, w), (lambda i, cb=cb: (i, cb))) for (_, w, cb) in row_ins]
    in_specs += [pl.BlockSpec(v.shape, (lambda i, nd=v.ndim: (0,) * nd)) for v in vec_ins]
    out_specs = [pl.BlockSpec((tile, w), lambda i: (i, 0)) for (w, _) in row_outs]
    out_specs += [pl.BlockSpec(sh, lambda i: (0, 0)) for sh in red_outs]
    out_shape = [jax.ShapeDtypeStruct((s, w), dt) for (w, dt) in row_outs]
    out_shape += [jax.ShapeDtypeStruct(sh, F32) for sh in red_outs]
    res = pl.pallas_call(
        kern, name=name, grid=(s // tile,), in_specs=in_specs, out_specs=out_specs, out_shape=out_shape,
        compiler_params=_cparams(("arbitrary",)),
    )(*[a for (a, _, _) in row_ins], *vec_ins)
    return res[:no], res[no:]


def _mm(name, a, b, mode, out_dtype=F32, bias=None, comm=None):
    if mode == "tn":
        k, m = a.shape
        n = b.shape[1]
        tm = m if m <= 1664 else m // 2
        tk = min(k, 1024)
        nk = k // tk

        def kern(a_ref, b_ref, o_ref, acc_ref):
            kk = pl.program_id(1)
            part = _dot(a_ref[...], b_ref[...], "tn")

            @pl.when(kk == 0)
            def _():
                acc_ref[...] = part

            @pl.when(kk > 0)
            def _():
                acc_ref[...] += part

            @pl.when(kk == nk - 1)
            def _():
                o_ref[...] = acc_ref[...].astype(o_ref.dtype)

        return pl.pallas_call(
            kern, name=name, grid=(m // tm, nk),
            in_specs=[pl.BlockSpec((tk, tm), lambda i, kk: (kk, i)),
                      pl.BlockSpec((tk, n), lambda i, kk: (kk, 0))],
            out_specs=pl.BlockSpec((tm, n), lambda i, kk: (i, 0)),
            out_shape=jax.ShapeDtypeStruct((m, n), out_dtype),
            scratch_shapes=[pltpu.VMEM((tm, n), F32)],
            compiler_params=_cparams(("parallel", "arbitrary")),
        )(a, b)

    m, k = a.shape
    n = b.shape[1] if mode == "nn" else b.shape[0]
    tm = min(m, 1024)
    tn = n if n <= 1664 else n // 2
    has_bias = bias is not None

    def kern(*refs):
        a_ref, b_ref = refs[0], refs[1]
        o_ref = refs[-1]
        r = _dot(a_ref[...], b_ref[...], mode)
        if has_bias:
            r = r + refs[2][...]
        o_ref[...] = r.astype(o_ref.dtype)

    b_spec = (pl.BlockSpec((k, tn), lambda i, j: (0, j)) if mode == "nn"
              else pl.BlockSpec((tn, k), lambda i, j: (j, 0)))
    in_specs = [pl.BlockSpec((tm, k), lambda i, j: (i, 0)), b_spec]
    args = [a, b]
    if has_bias:
        in_specs.append(pl.BlockSpec((1, tn), lambda i, j: (0, j)))
        args.append(bias)
    outs, got = _call(kern, name=name, grid=(m // tm, n // tn), in_specs=in_specs,
                      out_specs=[pl.BlockSpec((tm, tn), lambda i, j: (i, j))],
                      out_shape=[jax.ShapeDtypeStruct((m, n), out_dtype)], args=args, sem=("parallel", "parallel"),
                      comm=comm)
    return (outs[0], got) if comm else outs[0]


def _mm_cols_nn(name, pieces, offsets, b, comm=None):
    m = pieces[0].shape[0]
    n = b.shape[1]
    tm = min(m, 1024)
    np_ = len(pieces)

    def kern(*refs):
        acc = _dot(refs[0][...], refs[np_][...], "nn")
        for p in range(1, np_):
            acc = acc + _dot(refs[p][...], refs[np_ + p][...], "nn")
        refs[2 * np_][...] = acc.astype(BF16)

    in_specs = [pl.BlockSpec((tm, a.shape[1]), lambda i: (i, 0)) for a in pieces]
    in_specs += [pl.BlockSpec((a.shape[1], n), (lambda i, r=off // a.shape[1]: (r, 0))) for a, off in zip(pieces, offsets)]
    outs, got = _call(kern, name=name, grid=(m // tm,), in_specs=in_specs,
                      out_specs=[pl.BlockSpec((tm, n), lambda i: (i, 0))],
                      out_shape=[jax.ShapeDtypeStruct((m, n), BF16)], args=list(pieces) + [b] * np_, sem=("parallel",),
                      comm=comm)
    return (outs[0], got) if comm else outs[0]


def _mm_cols_tn(name, pieces, b):
    k, n = b.shape
    tk = min(k, 512)
    nk = k // tk
    np_ = len(pieces)

    def kern(*refs):
        b_ref = refs[np_]
        o_refs, acc_refs = refs[np_ + 1:2 * np_ + 1], refs[2 * np_ + 1:]
        kk = pl.program_id(0)

        @pl.when(kk == 0)
        def _():
            for r in acc_refs:
                r[...] = jnp.zeros(r.shape, F32)

        bb = b_ref[...]
        for p in range(np_):
            acc_refs[p][...] += _dot(refs[p][...], bb, "tn")

        @pl.when(kk == nk - 1)
        def _():
            for o, r in zip(o_refs, acc_refs):
                o[...] = r[...].astype(o.dtype)

    return pl.pallas_call(
        kern, name=name, grid=(nk,),
        in_specs=[pl.BlockSpec((tk, a.shape[1]), lambda kk: (kk, 0)) for a in pieces] + [pl.BlockSpec((tk, n), lambda kk: (kk, 0))],
        out_specs=[pl.BlockSpec((a.shape[1], n), lambda kk: (0, 0)) for a in pieces],
        out_shape=[jax.ShapeDtypeStruct((a.shape[1], n), BF16) for a in pieces],
        scratch_shapes=[pltpu.VMEM((a.shape[1], n), F32) for a in pieces],
        compiler_params=_cparams(("arbitrary",)),
    )(*pieces, b)


def _mla_proj(name, qn, uq_t, kvn, ukv_t, krr, cos, sin):
    s = qn.shape[0]
    tm = min(s, 1024)

    def kern(qn_ref, uq_ref, kvn_ref, ukv_ref, kr_ref, cos_ref, sin_ref, q_out, k_out, v_out, kt_out):
        qp = _dot(qn_ref[...], uq_ref[...], "nt")
        kv = _dot(kvn_ref[...], ukv_ref[...], "nt")
        c, sn = cos_ref[...], sin_ref[...]
        kr_r = kr_ref[...].astype(F32)
        ones = jnp.ones((tm, LANES), F32)
        qs, ks, vs = [], [], []
        for hh in range(MLA_HEADS):
            qs += [qp[:, 256 * hh:256 * hh + 128], _rope(qp[:, 256 * hh + 128:256 * hh + 256], c, sn)]
            ks += [kv[:, 256 * hh:256 * hh + 128], kr_r]
            vs += [kv[:, 1024 + LANES * hh:1024 + LANES * hh + LANES], ones]
        q_out[...] = (jnp.concatenate(qs, axis=1) * MLA_SCALE).astype(BF16)
        k_out[...] = jnp.concatenate(ks, axis=1).astype(BF16)
        v_out[...] = jnp.concatenate(vs, axis=1).astype(BF16)
        for hh in range(MLA_HEADS):
            kt_out[hh] = jnp.concatenate([ks[2 * hh], ks[2 * hh + 1]], axis=1).T.astype(BF16)

    row = lambda w: pl.BlockSpec((tm, w), lambda i: (i, 0))
    whole = lambda a: pl.BlockSpec(a.shape, lambda i: (0, 0))
    return pl.pallas_call(
        kern, name=name, grid=(s // tm,),
        in_specs=[row(qn.shape[1]), whole(uq_t), row(kvn.shape[1]), whole(ukv_t), row(LANES), row(LANES), row(LANES)],
        out_specs=[row(1024)] * 3 + [pl.BlockSpec((MLA_HEADS, 256, tm), lambda i: (0, 0, i))],
        out_shape=[jax.ShapeDtypeStruct((s, 1024), BF16)] * 3 + [jax.ShapeDtypeStruct((MLA_HEADS, 256, s), BF16)],
        compiler_params=_cparams(("parallel",)),
    )(qn, uq_t, kvn, ukv_t, krr, cos, sin)


def _peers():
    mx, my, mc = lax.axis_index("x"), lax.axis_index("y"), lax.axis_index("c")
    me = 4 * mx + 2 * my + mc
    out = []
    for k in range(1, N_DEV):
        px = 1 - mx if (k >> 2) & 1 else mx
        py = 1 - my if (k >> 1) & 1 else my
        pc = 1 - mc if k & 1 else mc
        out.append((k, (px, py, pc), 4 * px + 2 * py + pc))
    return me, out


def _xchg_copies(x_refs, out_refs, scatter, send_sems, recv_sems, local_sems):
    me, peers = _peers()
    local, out, arrive = [], [], []
    for a, (x, o, sc) in enumerate(zip(x_refs, out_refs, scatter)):
        mine = x.at[me] if sc else x
        local.append(pltpu.make_async_copy(mine, o.at[me], local_sems.at[a]))
        for k, dev, p in peers:
            out.append(pltpu.make_async_remote_copy(
                src_ref=x.at[p] if sc else x, dst_ref=o.at[me],
                send_sem=send_sems.at[a, k - 1], recv_sem=recv_sems.at[a, k - 1],
                device_id=dev, device_id_type=MESH))
            arrive.append(pltpu.make_async_remote_copy(
                src_ref=mine, dst_ref=o.at[p],
                send_sem=send_sems.at[a, k - 1], recv_sem=recv_sems.at[a, k - 1],
                device_id=dev, device_id_type=MESH))
    return local, out, arrive


def _xchg_start(*args):
    local, out, _ = _xchg_copies(*args)
    for cp in local + out:
        cp.start()


def _xchg_wait(*args):
    local, out, arrive = _xchg_copies(*args)
    for cp in out:
        cp.wait_send()
    for cp in arrive:
        cp.wait_recv()
    for cp in local:
        cp.wait()


def _call(kern, *, name, grid, in_specs, out_specs, out_shape, args, sem, scratch_shapes=(), vmem=VMEM_LIMIT,
          comm=None):
    if not comm:
        outs = pl.pallas_call(kern, name=name, grid=grid, in_specs=in_specs, out_specs=out_specs, out_shape=out_shape,
                              scratch_shapes=list(scratch_shapes), compiler_params=_cparams(sem, vmem))(*args)
        return outs, []
    n, ni, no, ns = len(comm), len(in_specs), len(out_specs), len(scratch_shapes)
    xs = [x for x, _ in comm]
    scatter = [sc for _, sc in comm]

    def body(*refs):
        ins, x_refs = refs[:ni], refs[ni:ni + n]
        outs, out_refs = refs[ni + n:ni + n + no], refs[ni + n + no:ni + 2 * n + no]
        scr = refs[ni + 2 * n + no:ni + 2 * n + no + ns]
        sems = refs[ni + 2 * n + no + ns:]
        ids = [pl.program_id(d) for d in range(len(grid))]
        first = functools.reduce(jnp.logical_and, [i == 0 for i in ids])
        last = functools.reduce(jnp.logical_and, [i == g - 1 for i, g in zip(ids, grid)])

        @pl.when(first)
        def _():
            _xchg_start(x_refs, out_refs, scatter, *sems)

        kern(*ins, *outs, *scr)

        @pl.when(last)
        def _():
            _xchg_wait(x_refs, out_refs, scatter, *sems)

    any_spec = pl.BlockSpec(memory_space=pl.ANY)
    res = pl.pallas_call(
        body, name=name, grid=grid,
        in_specs=list(in_specs) + [any_spec] * n, out_specs=list(out_specs) + [any_spec] * n,
        out_shape=list(out_shape) + [jax.ShapeDtypeStruct((N_DEV,) + tuple(x.shape[1:] if sc else x.shape), x.dtype)
                                     for x, sc in comm],
        scratch_shapes=list(scratch_shapes) + [pltpu.SemaphoreType.DMA((n, N_DEV - 1)),
                                               pltpu.SemaphoreType.DMA((n, N_DEV - 1)), pltpu.SemaphoreType.DMA((n,))],
        compiler_params=pltpu.CompilerParams(dimension_semantics=("arbitrary",) * len(grid), vmem_limit_bytes=vmem,
                                             has_side_effects=True),
    )(*args, *xs)
    return res[:no], res[no:]


def _exchange(name, xs, scatter):
    def nothing():
        pass

    return _call(nothing, name=name, grid=(1,), in_specs=[], out_specs=[], out_shape=[], args=[], sem=("arbitrary",),
                 comm=[(x, scatter) for x in xs])[1]


def _sum_parts(name, x):
    p, r, c = x.shape
    tr = r if r * c * p * x.dtype.itemsize <= (8 << 20) else _pick(r, (256, 128, 64, 16))

    def kern(x_ref, o_ref):
        acc = x_ref[0].astype(F32)
        for i in range(1, p):
            acc = acc + x_ref[i].astype(F32)
        o_ref[...] = acc

    return pl.pallas_call(
        kern, name=name, grid=(r // tr,),
        in_specs=[pl.BlockSpec((p, tr, c), lambda i: (0, i, 0))],
        out_specs=pl.BlockSpec((tr, c), lambda i: (i, 0)),
        out_shape=jax.ShapeDtypeStruct((r, c), F32),
        compiler_params=_cparams(("parallel",)),
    )(x)


def _adamw(name, w, m, v, g):
    r, c = w.shape
    parts = g.shape[0] if g.ndim == 3 else 0
    tr = 512 if (r > 512 and r % 512 == 0) else r

    def kern(w_ref, m_ref, v_ref, g_ref, go_ref, d_ref, mo_ref, vo_ref):
        if parts:
            gg = g_ref[0]
            for i in range(1, parts):
                gg = gg + g_ref[i]
        else:
            gg = g_ref[...]
        mm = ADAM_B1 * m_ref[...] + (1.0 - ADAM_B1) * gg
        vv = ADAM_B2 * v_ref[...] + (1.0 - ADAM_B2) * (gg * gg)
        m_hat = mm / (1.0 - ADAM_B1 ** ADAM_STEP)
        v_hat = vv / (1.0 - ADAM_B2 ** ADAM_STEP)
        go_ref[...] = gg
        d_ref[...] = -ADAM_LR * (m_hat / (jnp.sqrt(v_hat) + ADAM_EPS) + ADAM_WD * w_ref[...])
        mo_ref[...] = mm
        vo_ref[...] = vv

    spec = pl.BlockSpec((tr, c), lambda i: (i, 0))
    gspec = pl.BlockSpec((parts, tr, c), lambda i: (0, i, 0)) if parts else spec
    sh = jax.ShapeDtypeStruct((r, c), F32)
    return pl.pallas_call(
        kern, name=name, grid=(r // tr,), in_specs=[spec, spec, spec, gspec],
        out_specs=[spec] * 4, out_shape=[sh] * 4,
        compiler_params=_cparams(("parallel",)),
    )(w, m, v, g)


def _ret_tables(dec_cc, dec_cd, dec_dd, reverse):
    c = dec_cc.shape[0]
    row = lax.broadcasted_iota(jnp.int32, (c, c), 0).astype(F32)
    col = lax.broadcasted_iota(jnp.int32, (c, c), 1).astype(F32)
    pos = lax.broadcasted_iota(jnp.int32, (c, LANES), 0).astype(F32)
    if reverse:
        diff, mask = col - row, col > row
        q_exp, k_exp = c - pos, pos
    else:
        diff, mask = row - col, row >= col
        q_exp, k_exp = pos + 1.0, c - 1.0 - pos
    decay = jnp.where(mask, jnp.exp(_log_sigmoid(dec_cc) * jnp.maximum(diff, 0.0)), 0.0)
    lam_cd = _log_sigmoid(dec_cd)
    return decay, jnp.exp(lam_cd * q_exp), jnp.exp(lam_cd * k_exp), jnp.exp(_log_sigmoid(dec_dd) * float(c))


def _ret_chunk(q, k, v, st, decay, qw, kw, sd):
    scores = _bdot(q, k, "nt") * decay
    o = _bdot(scores, v, "nn") + _bdot(q * qw, st, "nn")
    st_new = st * sd + _bdot(k * kw, v, "tn")
    return o, st_new


def _ret_dec(dec_ref, h, c):
    d = dec_ref[:, h:h + 1]
    return (jnp.broadcast_to(d, (c, c)), jnp.broadcast_to(d, (c, LANES)), jnp.broadcast_to(d, (LANES, LANES)))


def _ret_per_step(n_chunks):
    return 2 if (n_chunks % 2 == 0 and n_chunks >= 4) else 1


def _ret_table_scratch(c):
    return [pltpu.VMEM((RET_HEADS, c, c), F32), pltpu.VMEM((RET_HEADS, c, LANES), F32),
            pltpu.VMEM((RET_HEADS, c, LANES), F32), pltpu.VMEM((RET_HEADS, LANES, LANES), F32)]


def _ret_fwd(name, q, k, z, dec, reverse, chunk):
    s = q.shape[0]
    chunk = min(chunk, s)
    per = _ret_per_step(s // chunk)
    n = s // (chunk * per)
    cmap = (lambda i: (n - 1 - i, 0)) if reverse else (lambda i: (i, 0))
    smap = (lambda i: (n - 1 - i, 0, 0, 0)) if reverse else (lambda i: (i, 0, 0, 0))

    def kern(q_ref, k_ref, v_ref, dec_ref, o_ref, st_out_ref, st_ref, *tab_refs):
        @pl.when(pl.program_id(0) == 0)
        def _():
            st_ref[...] = jnp.zeros(st_ref.shape, F32)
            for h in range(RET_HEADS):
                for r, t in zip(tab_refs, _ret_tables(*_ret_dec(dec_ref, h, chunk), reverse)):
                    r[h] = t

        for c2 in (range(per - 1, -1, -1) if reverse else range(per)):
            rows = pl.ds(c2 * chunk, chunk)
            for h in range(RET_HEADS):
                sl = slice(LANES * h, LANES * (h + 1))
                st = st_ref[h]
                st_out_ref[c2, h] = st
                o, st_new = _ret_chunk(q_ref[rows, sl].astype(F32), k_ref[rows, sl].astype(F32),
                                       v_ref[rows, sl].astype(F32), st, *[r[h] for r in tab_refs])
                o_ref[rows, sl] = o.astype(BF16)
                st_ref[h] = st_new

    blk = chunk * per
    return pl.pallas_call(
        kern, name=name, grid=(n,),
        in_specs=[pl.BlockSpec((blk, 512), cmap), pl.BlockSpec((blk, 512), cmap),
                  pl.BlockSpec((blk, 512), cmap), pl.BlockSpec((1, RET_HEADS), lambda i: (0, 0))],
        out_specs=[pl.BlockSpec((blk, 512), cmap), pl.BlockSpec((per, RET_HEADS, LANES, LANES), smap)],
        out_shape=[jax.ShapeDtypeStruct((s, 512), BF16), jax.ShapeDtypeStruct((n * per, RET_HEADS, LANES, LANES), F32)],
        scratch_shapes=[pltpu.VMEM((RET_HEADS, LANES, LANES), F32)] + _ret_table_scratch(chunk),
        compiler_params=_cparams(("arbitrary",)),
    )(q, k, z, dec)


def _ret_bwd(name, q, k, z, dec, states, do, reverse, chunk):
    s = q.shape[0]
    chunk = min(chunk, s)
    per = _ret_per_step(s // chunk)
    n = s // (chunk * per)
    cmap = (lambda i: (i, 0)) if reverse else (lambda i: (n - 1 - i, 0))
    smap = (lambda i: (i, 0, 0, 0)) if reverse else (lambda i: (n - 1 - i, 0, 0, 0))

    def kern(q_ref, k_ref, v_ref, dec_ref, st_in_ref, do_ref, dq_ref, dk_ref, dv_ref, ddec_ref, dst_ref, *scr):
        tab_refs, gtab_refs = scr[:4], scr[4:]
        step = pl.program_id(0)

        @pl.when(step == 0)
        def _():
            dst_ref[...] = jnp.zeros(dst_ref.shape, F32)
            for r in gtab_refs:
                r[...] = jnp.zeros(r.shape, F32)
            for h in range(RET_HEADS):
                for r, t in zip(tab_refs, _ret_tables(*_ret_dec(dec_ref, h, chunk), reverse)):
                    r[h] = t

        for c2 in (range(per) if reverse else range(per - 1, -1, -1)):
            rows = pl.ds(c2 * chunk, chunk)
            for h in range(RET_HEADS):
                sl = slice(LANES * h, LANES * (h + 1))
                _, vjp = jax.vjp(_ret_chunk, q_ref[rows, sl].astype(F32), k_ref[rows, sl].astype(F32),
                                 v_ref[rows, sl].astype(F32), st_in_ref[c2, h], *[r[h] for r in tab_refs])
                grads = vjp((do_ref[rows, sl].astype(F32), dst_ref[h]))
                dq_ref[rows, sl] = grads[0].astype(BF16)
                dk_ref[rows, sl] = grads[1].astype(BF16)
                dv_ref[rows, sl] = grads[2].astype(BF16)
                dst_ref[h] = grads[3]
                for r, g in zip(gtab_refs, grads[4:]):
                    r[h] += g

        @pl.when(step == n - 1)
        def _():
            lane = lax.broadcasted_iota(jnp.int32, (1, LANES), 1)
            ddec = jnp.zeros((1, LANES), F32)
            for h in range(RET_HEADS):
                _, vjp_t = jax.vjp(functools.partial(_ret_tables, reverse=reverse), *_ret_dec(dec_ref, h, chunk))
                parts = vjp_t(tuple(r[h] for r in gtab_refs))
                tot = sum(jnp.sum(jnp.sum(p, axis=1, keepdims=True), axis=0, keepdims=True) for p in parts)
                ddec = ddec + jnp.where(lane == h, tot, 0.0)
            ddec_ref[...] = ddec

    cspec = pl.BlockSpec((chunk * per, 512), cmap)
    return pl.pallas_call(
        kern, name=name, grid=(n,),
        in_specs=[cspec, cspec, cspec, pl.BlockSpec((1, RET_HEADS), lambda i: (0, 0)),
                  pl.BlockSpec((per, RET_HEADS, LANES, LANES), smap), cspec],
        out_specs=[cspec, cspec, cspec, pl.BlockSpec((1, LANES), lambda i: (0, 0))],
        out_shape=[jax.ShapeDtypeStruct((s, 512), BF16)] * 3 + [jax.ShapeDtypeStruct((1, LANES), F32)],
        scratch_shapes=[pltpu.VMEM((RET_HEADS, LANES, LANES), F32)] + _ret_table_scratch(chunk) * 2,
        compiler_params=_cparams(("arbitrary",)),
    )(q, k, z, dec, states, do)


def _fa_fwd(name, q, k, vx, tq, tk, nsub, comm=None):
    s = q.shape[0]
    tq, tk = min(tq, s), min(tk, s)
    nk = s // tk
    sq = tq // nsub

    def kern(q_ref, k_ref, v_ref, o_ref, lse_ref, m_ref, acc_ref):
        j = pl.program_id(2)

        @pl.when(j == 0)
        def _():
            m_ref[...] = jnp.full(m_ref.shape, -jnp.inf, F32)
            acc_ref[...] = jnp.zeros(acc_ref.shape, F32)

        kb, vb = k_ref[...], v_ref[...]
        for c in range(nsub):
            rows = pl.ds(c * sq, sq)
            sc = _dot(q_ref[rows, :], kb, "nt")
            m_prev = m_ref[rows, :]
            m_new = jnp.maximum(m_prev, jnp.max(sc, axis=1, keepdims=True))
            alpha = jnp.exp(m_prev - m_new)
            p = jnp.exp(sc - m_new)
            acc_ref[rows, :] = alpha * acc_ref[rows, :] + _dot(p, vb, "nn")
            m_ref[rows, :] = m_new

        @pl.when(j == nk - 1)
        def _():
            den = acc_ref[:, LANES:]
            o_ref[...] = (acc_ref[:, :LANES] / den).astype(BF16)
            lse_ref[...] = m_ref[...] + jnp.log(den[:, :1])

    (o, lse), got = _call(
        kern, name=name, grid=(MLA_HEADS, s // tq, nk),
        in_specs=[pl.BlockSpec((tq, 256), lambda h, i, j: (i, h)),
                  pl.BlockSpec((tk, 256), lambda h, i, j: (j, h)),
                  pl.BlockSpec((tk, 256), lambda h, i, j: (j, h))],
        out_specs=[pl.BlockSpec((tq, LANES), lambda h, i, j: (i, h)),
                   pl.BlockSpec((None, tq, 1), lambda h, i, j: (h, i, 0))],
        out_shape=[jax.ShapeDtypeStruct((s, 512), BF16), jax.ShapeDtypeStruct((MLA_HEADS, s, 1), F32)],
        scratch_shapes=[pltpu.VMEM((tq, 1), F32), pltpu.VMEM((tq, 256), F32)],
        args=[q, k, vx], sem=("parallel", "parallel", "arbitrary"), comm=comm)
    return o, lse, got


def _fa_bwd(name, q, k, kt, vx, do, lse, delta, tq, tk, nsub, comm=None):
    s = q.shape[0]
    tq, tk = min(tq, s), min(tk, s)
    nq = s // tq
    sk = tk // nsub

    nkb = s // tk

    def kern(q_ref, k_ref, kt_ref, v_ref, do_ref, lse_ref, dl_ref, dqt_ref, dk_ref, dv_ref, dqt_acc, dk_acc, dv_acc):
        j, i = pl.program_id(1), pl.program_id(2)

        @pl.when(i == 0)
        def _():
            dv_acc[...] = jnp.zeros(dv_acc.shape, F32)
            dk_acc[...] = jnp.zeros(dk_acc.shape, F32)

        @pl.when(j == 0)
        def _():
            dqt_acc[i] = jnp.zeros((256, tq), F32)

        qb, dob = q_ref[...], do_ref[...]
        lse_row, dl_row = lse_ref[...], dl_ref[...]
        dqt = dqt_acc[i]
        for c in range(nsub):
            rows = pl.ds(c * sk, sk)
            st = _dot(k_ref[rows, :], qb, "nt")
            pt = jnp.exp(st - lse_row)
            dpt = _dot(v_ref[rows, :], dob, "nt")
            dst = (pt * (dpt - dl_row)).astype(BF16)
            dv_acc[rows, :] += _dot(pt, dob, "nn")
            dk_acc[rows, :] += _dot(dst, qb, "nn")
            dqt = dqt + _dot(kt_ref[:, rows], dst, "nn")
        dqt_acc[i] = dqt

        @pl.when(i == nq - 1)
        def _():
            dv_ref[...] = dv_acc[...].astype(BF16)
            dk_ref[...] = dk_acc[...].astype(BF16)

        @pl.when(j == nkb - 1)
        def _():
            dqt_ref[i] = dqt.astype(BF16)

    outs, got = _call(
        kern, name=name, grid=(MLA_HEADS, s // tk, nq),
        in_specs=[pl.BlockSpec((tq, 256), lambda h, j, i: (i, h)),
                  pl.BlockSpec((tk, 256), lambda h, j, i: (j, h)),
                  pl.BlockSpec((None, 256, tk), lambda h, j, i: (h, 0, j)),
                  pl.BlockSpec((tk, LANES), lambda h, j, i: (j, 2 * h)),
                  pl.BlockSpec((tq, LANES), lambda h, j, i: (i, h)),
                  pl.BlockSpec((None, 1, tq), lambda h, j, i: (h, 0, i)),
                  pl.BlockSpec((None, 1, tq), lambda h, j, i: (h, 0, i))],
        out_specs=[pl.BlockSpec((None, nq, 256, tq), lambda h, j, i: (h, 0, 0, 0)),
                   pl.BlockSpec((tk, 256), lambda h, j, i: (j, h)),
                   pl.BlockSpec((tk, LANES), lambda h, j, i: (j, h))],
        out_shape=[jax.ShapeDtypeStruct((MLA_HEADS, nq, 256, tq), BF16),
                   jax.ShapeDtypeStruct((s, 1024), BF16), jax.ShapeDtypeStruct((s, 512), BF16)],
        scratch_shapes=[pltpu.VMEM((nq, 256, tq), F32), pltpu.VMEM((tk, 256), F32), pltpu.VMEM((tk, LANES), F32)],
        args=[q, k, kt, vx, do, lse, delta], sem=("parallel", "arbitrary", "arbitrary"), comm=comm)
    return outs[0], outs[1], outs[2], got


def _fill_padded(dst_ref, val, s):
    zeros = jnp.zeros((CONV_HALO, LANES), F32)
    dst_ref[pl.ds(0, CONV_HALO), :] = zeros
    dst_ref[pl.ds(CONV_HALO + s, CONV_HALO), :] = zeros
    dst_ref[pl.ds(CONV_HALO, s), :] = val


def _shifted_windows(win):
    n = win.shape[0]
    return [win] + [pltpu.roll(win, n - b, 0) for b in range(1, 8)]


def _conv_fwd(name, z, w, bias, rc=256):
    s = z.shape[0]
    rc = min(rc, s)

    def kern(a_ref, g_ref, w_ref, b_ref, o_ref, pad_ref):
        _fill_padded(pad_ref, _glu_fn(a_ref[...].astype(F32), g_ref[...].astype(F32)), s)
        wv = w_ref[...]
        bv = b_ref[...]

        def chunk(r, carry):
            base = pl.multiple_of(r * rc, rc)
            wins = _shifted_windows(pad_ref[pl.ds(base, rc + 2 * CONV_HALO), :])
            acc = jnp.broadcast_to(bv, (rc, LANES))
            for kk in range(CONV_K):
                a, b = divmod(kk + 1, 8)
                acc = acc + wv[kk:kk + 1, :] * wins[b][8 * a:8 * a + rc]
            o_ref[pl.ds(base, rc), :] = acc.astype(BF16)
            return carry

        lax.fori_loop(0, s // rc, chunk, 0)

    nblk = D // LANES
    return pl.pallas_call(
        kern, name=name, grid=(nblk,),
        in_specs=[pl.BlockSpec((s, LANES), lambda c: (0, c)), pl.BlockSpec((s, LANES), lambda c: (0, nblk + c)),
                  pl.BlockSpec((32, LANES), lambda c: (0, c)), pl.BlockSpec((1, LANES), lambda c: (0, c))],
        out_specs=pl.BlockSpec((s, LANES), lambda c: (0, c)),
        out_shape=jax.ShapeDtypeStruct((s, D), BF16),
        scratch_shapes=[pltpu.VMEM((s + 2 * CONV_HALO, LANES), F32)],
        compiler_params=_cparams(("parallel",)),
    )(z, z, w, bias)


def _conv_bwd(name, z, g, w, rc=256, comm=None):
    s = z.shape[0]
    rc = min(rc, s)

    def kern(a_ref, b_ref, g_ref, w_ref, da_ref, db_ref, dw_ref, dbias_ref, sa_ref, sb_ref, upad_ref, gpad_ref,
             dwacc_ref):
        _fill_padded(upad_ref, _glu_fn(a_ref[...].astype(F32), b_ref[...].astype(F32)), s)
        _fill_padded(gpad_ref, g_ref[...].astype(F32), s)
        dwacc_ref[...] = jnp.zeros(dwacc_ref.shape, F32)
        wv = w_ref[...]

        def chunk(r, carry):
            sum_a, sum_b = carry
            base = pl.multiple_of(r * rc, rc)
            gwins = _shifted_windows(gpad_ref[pl.ds(base, rc + 2 * CONV_HALO), :])
            uwins = _shifted_windows(upad_ref[pl.ds(base, rc + 2 * CONV_HALO), :])
            gc = g_ref[pl.ds(base, rc), :].astype(F32)
            acc = jnp.zeros((rc, LANES), F32)
            for kk in range(CONV_K):
                a, b = divmod(CONV_K - kk, 8)
                acc = acc + wv[kk:kk + 1, :] * gwins[b][8 * a:8 * a + rc]
                a, b = divmod(kk + 1, 8)
                prod = gc * uwins[b][8 * a:8 * a + rc]
                dwacc_ref[kk] += jnp.sum(prod.reshape(rc // 8, 8, LANES), axis=0)
            dwacc_ref[CONV_K] += jnp.sum(gc.reshape(rc // 8, 8, LANES), axis=0)
            av = a_ref[pl.ds(base, rc), :].astype(F32)
            sg = _sigmoid(b_ref[pl.ds(base, rc), :].astype(F32))
            d_a = acc * sg
            d_b = acc * av * sg * (1.0 - sg)
            da_ref[pl.ds(base, rc), :] = d_a.astype(BF16)
            db_ref[pl.ds(base, rc), :] = d_b.astype(BF16)
            return (sum_a + jnp.sum(d_a.reshape(rc // 8, 8, LANES), axis=0),
                    sum_b + jnp.sum(d_b.reshape(rc // 8, 8, LANES), axis=0))

        zero = jnp.zeros((8, LANES), F32)
        sum_a, sum_b = lax.fori_loop(0, s // rc, chunk, (zero, zero))
        sa_ref[...] = jnp.sum(sum_a, axis=0, keepdims=True)
        sb_ref[...] = jnp.sum(sum_b, axis=0, keepdims=True)
        tot = jnp.sum(dwacc_ref[...], axis=1)
        lane_row = lax.broadcasted_iota(jnp.int32, (32, LANES), 0)
        dw_ref[...] = jnp.where(lane_row < CONV_K, tot, 0.0)
        dbias_ref[...] = tot[CONV_K:CONV_K + 1, :]

    nblk = D // LANES
    cs = pl.BlockSpec((s, LANES), lambda c: (0, c))
    vec = pl.BlockSpec((1, LANES), lambda c: (0, c))
    outs, got = _call(
        kern, name=name, grid=(nblk,),
        in_specs=[cs, pl.BlockSpec((s, LANES), lambda c: (0, nblk + c)), cs, pl.BlockSpec((32, LANES), lambda c: (0, c))],
        out_specs=[cs, cs, pl.BlockSpec((32, LANES), lambda c: (0, c)), vec, vec, vec],
        out_shape=[jax.ShapeDtypeStruct((s, D), BF16), jax.ShapeDtypeStruct((s, D), BF16),
                   jax.ShapeDtypeStruct((32, D), F32)] + [jax.ShapeDtypeStruct((1, D), F32)] * 3,
        scratch_shapes=[pltpu.VMEM((s + 2 * CONV_HALO, LANES), F32), pltpu.VMEM((s + 2 * CONV_HALO, LANES), F32),
                        pltpu.VMEM((32, 8, LANES), F32)],
        args=[z, z, g, w], sem=("parallel",), comm=comm)
    return outs, got


def _mod_local(name, c_all, ada_w):
    def kern(c_ref, w_ref, o_ref):
        o_ref[...] = jnp.dot(_silu(c_ref[...]), w_ref[...], preferred_element_type=F32,
                             precision=lax.Precision.HIGHEST)

    return pl.pallas_call(
        kern, name=name, grid=(DEPTH,),
        in_specs=[pl.BlockSpec((N_DEV, D), lambda l: (0, 0)), pl.BlockSpec((None, D, 384), lambda l: (l, 0, 0))],
        out_specs=pl.BlockSpec((None, N_DEV, 384), lambda l: (l, 0, 0)),
        out_shape=jax.ShapeDtypeStruct((DEPTH, N_DEV, 384), F32),
        compiler_params=_cparams(("parallel",)),
    )(c_all, ada_w)


def _ada_w_grad(name, c_all_t, dmod):
    def kern(c_ref, d_ref, o_ref):
        o_ref[...] = jnp.dot(_silu(c_ref[...]), d_ref[...], preferred_element_type=F32,
                             precision=lax.Precision.HIGHEST)

    return pl.pallas_call(
        kern, name=name, grid=(DEPTH,),
        in_specs=[pl.BlockSpec((D, LANES), lambda l: (0, 0)), pl.BlockSpec((None, LANES, 384), lambda l: (l, 0, 0))],
        out_specs=pl.BlockSpec((None, D, 384), lambda l: (l, 0, 0)),
        out_shape=jax.ShapeDtypeStruct((DEPTH, D, 384), F32),
        compiler_params=_cparams(("parallel",)),
    )(c_all_t, dmod)


def _pre_fn(x, g, scale, shift):
    return _rms(x, g) * (1.0 + scale) + shift


def _post_fn(y, g, gate):
    return gate * _rms(y, g)


def _ev_post_fn(o_heads, rg, a, mg):
    normed = []
    for oh in o_heads:
        mu = jnp.mean(oh, axis=-1, keepdims=True)
        var = jnp.mean(jnp.square(oh - mu), axis=-1, keepdims=True)
        normed.append((oh - mu) * lax.rsqrt(var + EPS))
    return jnp.concatenate([jnp.concatenate(normed, axis=1) * _silu(rg), a * _silu(mg)], axis=1)


def _od_post_fn(u, g, ln_g, ln_b):
    mu = jnp.mean(u, axis=-1, keepdims=True)
    var = jnp.mean(jnp.square(u - mu), axis=-1, keepdims=True)
    y = (u - mu) * lax.rsqrt(var + EPS) * ln_g + ln_b
    return _silu(y) * _silu(g)


def _dsilu(x, s):
    return s * (1.0 + x * (1.0 - s))


def _ev_post_bwd(o_heads, rg, a, mg, da):
    d_ret, d_mla = da[:, :512], da[:, 512:]
    s_rg, s_mg = _sigmoid(rg), _sigmoid(mg)
    d_on = _heads(d_ret * (rg * s_rg))
    normed, do_heads = [], []
    for oh, dn in zip(o_heads, d_on):
        xc = oh - jnp.mean(oh, axis=-1, keepdims=True)
        r = lax.rsqrt(jnp.mean(xc * xc, axis=-1, keepdims=True) + EPS)
        xh = xc * r
        normed.append(xh)
        m1 = jnp.mean(dn, axis=-1, keepdims=True)
        m2 = jnp.mean(dn * xh, axis=-1, keepdims=True)
        do_heads.append(r * (dn - m1 - xh * m2))
    drg = d_ret * jnp.concatenate(normed, axis=1) * _dsilu(rg, s_rg)
    return do_heads, drg, d_mla * (mg * s_mg), d_mla * a * _dsilu(mg, s_mg)


def _od_post_bwd(u, g, da, ln_g, ln_b):
    xc = u - jnp.mean(u, axis=-1, keepdims=True)
    r = lax.rsqrt(jnp.mean(xc * xc, axis=-1, keepdims=True) + EPS)
    xh = xc * r
    y = xh * ln_g + ln_b
    s1, s2 = _sigmoid(y), _sigmoid(g)
    dg = da * (y * s1) * (s2 * (1.0 + g * (1.0 - s2)))
    dy = da * (g * s2) * (s1 * (1.0 + y * (1.0 - s1)))
    dxh = dy * ln_g
    m1 = jnp.mean(dxh, axis=-1, keepdims=True)
    m2 = jnp.mean(dxh * xh, axis=-1, keepdims=True)
    return r * (dxh - m1 - xh * m2), dg, _colsum(dy * xh), _colsum(dy)


def _glu_fn(a, b):
    return a * _sigmoid(b)


def _heads(x):
    return [x[:, LANES * h:LANES * (h + 1)] for h in range(4)]


def _colsum(x):
    return jnp.sum(x, axis=0, keepdims=True)


def _zrows(n, c):
    return jnp.zeros((n, c), BF16)


def _ev_win_layout(wt):
    rq = [p for h in range(4) for p in (wt[64 * h:64 * h + 64], _zrows(64, D))]
    rk = [p for h in range(4) for p in (wt[256 + 64 * h:256 + 64 * h + 64], _zrows(64, D))]
    return jnp.concatenate([wt[512:1024], wt[1024:1536], wt[2240:2752], wt[1536:1920],
                            wt[2176:2240], _zrows(64, D)] + rq + rk + [wt[1920:2176]], axis=0)


def _uq_layout(wt):
    return jnp.concatenate([p for h in range(4) for p in (wt[192 * h:192 * h + 192], _zrows(64, 384))], axis=0)


def _uq_unlayout(g):
    return jnp.concatenate([g[256 * h:256 * h + 192] for h in range(4)], axis=0)


def _ukv_layout(wt):
    kpart = [p for h in range(4) for p in (wt[256 * h:256 * h + 128], _zrows(128, 256))]
    vpart = [wt[256 * h + 128:256 * h + 256] for h in range(4)]
    return jnp.concatenate(kpart + vpart, axis=0)


def _ukv_unlayout(g):
    return jnp.concatenate([p for h in range(4) for p in (g[256 * h:256 * h + 128], g[1024 + 128 * h:1024 + 128 * h + 128])],
                           axis=0)


def kernel(x, c, positions, ada_w, ada_b, pre_g, post_g, ev_w_in, ev_dec_f, ev_dec_b, ev_q_norm_g, ev_w_uq, ev_kv_norm_g, ev_w_ukv, ev_w_out, od_w_in, od_b_in, od_dw_w, od_dw_b, od_ln_g, od_ln_b, od_w_out, loss_target, m_ada_w, m_ada_b, m_pre_g, m_post_g, m_ev_w_in, m_ev_dec_f, m_ev_dec_b, m_ev_q_norm_g, m_ev_w_uq, m_ev_kv_norm_g, m_ev_w_ukv, m_ev_w_out, m_od_w_in, m_od_b_in, m_od_dw_w, m_od_dw_b, m_od_ln_g, m_od_ln_b, m_od_w_out, v_ada_w, v_ada_b, v_pre_g, v_post_g, v_ev_w_in, v_ev_dec_f, v_ev_dec_b, v_ev_q_norm_g, v_ev_w_uq, v_ev_kv_norm_g, v_ev_w_ukv, v_ev_w_out, v_od_w_in, v_od_b_in, v_od_dw_w, v_od_dw_b, v_od_ln_g, v_od_ln_b, v_od_w_out):
    s = x.shape[1]
    me = 4 * lax.axis_index("x") + 2 * lax.axis_index("y") + lax.axis_index("c")
    x0 = x.reshape(s, D)
    tgt = loss_target.reshape(s, D)
    ret_chunk = 256
    fa_cfg_f = ((min(4096, s // 2), min(2048, s // 2), min(16, s // 512)),) * 2
    fa_cfg_b = ((min(2048, s // 2), min(4096, s // 2), min(16, s // 512)),) * 2

    start_parts = [c.reshape(-1), od_b_in.reshape(-1), od_dw_w.reshape(-1), od_dw_b.reshape(-1),
                   od_ln_g.reshape(-1), od_ln_b.reshape(-1)]
    start_sizes = [p.shape[0] for p in start_parts]
    start_len = -(-sum(start_sizes) // 1024) * 1024
    start_vec = jnp.concatenate(start_parts + [jnp.zeros((start_len - sum(start_sizes),), F32)])
    start_all = _exchange("gather_start", [start_vec.reshape(-1, LANES)], False)[0].reshape(N_DEV, start_len)
    offs = np.cumsum([0] + start_sizes)
    c_all = start_all[:, offs[0]:offs[1]]
    b_in_all = start_all[:, offs[1]:offs[2]].reshape(N_DEV, 2, 384).transpose(1, 0, 2).reshape(2, 1, ZW_OD)
    dw_w_all = start_all[:, offs[2]:offs[3]].reshape(N_DEV, 2, CONV_K, LANES).transpose(1, 2, 0, 3).reshape(2, CONV_K, D)
    dw_w_all = jnp.concatenate([dw_w_all, jnp.zeros((2, 1, D), F32)], axis=1)
    dw_b_all = start_all[:, offs[3]:offs[4]].reshape(N_DEV, 2, LANES).transpose(1, 0, 2).reshape(2, 1, D)
    ln_g_all = start_all[:, offs[4]:offs[5]].reshape(N_DEV, 2, LANES).transpose(1, 0, 2).reshape(2, 1, D)
    ln_b_all = start_all[:, offs[5]:offs[6]].reshape(N_DEV, 2, LANES).transpose(1, 0, 2).reshape(2, 1, D)

    mod_loc = _mod_local("mod_local", c_all, ada_w)
    mod_all = _exchange("gather_mod", [mod_loc.reshape(DEPTH * N_DEV, 384)], False)[0].reshape(N_DEV, DEPTH, N_DEV, 384)
    mod = lax.dynamic_index_in_dim(mod_all, me, axis=2, keepdims=False)
    mod = mod.transpose(1, 0, 2).reshape(DEPTH, 3 * D) + ada_b
    shift = [mod[l:l + 1, 0:D] for l in range(DEPTH)]
    scale = [mod[l:l + 1, D:2 * D] for l in range(DEPTH)]
    gate = [mod[l:l + 1, 2 * D:3 * D] for l in range(DEPTH)]

    def ev_shards(i):
        return [ev_w_in[i].T.astype(BF16), ev_w_uq[i].T.astype(BF16), ev_w_ukv[i].T.astype(BF16), ev_w_out[i].astype(BF16)]

    def od_shards(i):
        return [od_w_in[i].T.astype(BF16), od_w_out[i].astype(BF16)]

    def full(got):
        return [g.reshape(N_DEV * g.shape[1], g.shape[2]) for g in got]

    def ev_full(got):
        win_t, uq_t, ukv_t, wout = full(got)
        return (_ev_win_layout(win_t), _uq_layout(uq_t), _ukv_layout(ukv_t), wout)

    win0_t = _ev_win_layout(full(_exchange("gather_w_ev0", ev_shards(0)[:1], False))[0])
    ev_w = [None, None]
    od_w = [None, None]
    later_w = [(t, False) for t in od_shards(0) + ev_shards(1) + od_shards(1)]

    inv_freq = ROPE_BASE ** (-jnp.arange(0, 64, 2, dtype=F32) / 64)
    invf = jnp.tile(inv_freq, 4).reshape(1, LANES)
    sgn = jnp.tile(jnp.concatenate([-jnp.ones((32,), F32), jnp.ones((32,), F32)]), 2).reshape(1, LANES)

    def rope_body(rows, vecs):
        ang = rows[0].astype(F32) * vecs[0]
        return [jnp.cos(ang), jnp.sin(ang) * vecs[1]], []

    (cos_t, sin_t), _ = _rowwise("rope_tables", rope_body, [(positions.reshape(s, 1), 1, 0)], [invf, sgn],
                                 [(LANES, F32), (LANES, F32)])

    saved = []
    xl = x0
    for l in range(DEPTH):
        i = l // 2
        sv = dict(x=xl)

        if l == 0:
            def pre_body(rows, vecs):
                return [_pre_fn(rows[0], *vecs)], []

            (h,), _ = _rowwise("pre0", pre_body, [(xl, D, 0)], [pre_g[0:1], scale[0], shift[0]], [(D, BF16)])
        sv["h"] = h
        if l % 2 == 0:
            if l == 0:
                z, got = _mm("ev_in0", h, win0_t, "nt", out_dtype=BF16, comm=[(t, False) for t in ev_shards(0)[1:]])
                uq0_t, ukv0_t, wout0 = full(got)
                ev_w[0] = (win0_t, _uq_layout(uq0_t), _ukv_layout(ukv0_t), wout0)
                win_t, uq_t, ukv_t, wout = ev_w[0]
            else:
                win_t, uq_t, ukv_t, wout = ev_w[i]
                z = _mm(f"ev_in{l}", h, win_t, "nt", out_dtype=BF16)
            sv["z"] = z
            dec_f, dec_b = ev_dec_f[i:i + 1], ev_dec_b[i:i + 1]

            def prep_body(rows, vecs):
                rq, rk, cq, ckv, kr, cos, sin = rows
                return [_rope(rq, cos, sin), _rope(rk, cos, sin) * RET_SCALE, _rms(cq, vecs[0]), _rms(ckv, vecs[1]),
                        _rope(kr, cos, sin)], []

            (rq_r, rk_r, qn, kvn, krr), _ = _rowwise(
                f"ev_prep{l}", prep_body,
                [(z, 512, 4), (z, 512, 5), (z, 384, 4), (z, 256, 12), (z, LANES, 15), (cos_t, LANES, 0), (sin_t, LANES, 0)],
                [ev_q_norm_g[i:i + 1], ev_kv_norm_g[i:i + 1]],
                [(512, BF16), (512, BF16), (384, BF16), (256, BF16), (LANES, BF16)])
            sv.update(rq_r=rq_r, rk_r=rk_r, qn=qn, kvn=kvn)
            o_f, st_f = _ret_fwd(f"ret_f{l}", rq_r, rk_r, z, dec_f, False, ret_chunk)
            o_b, st_b = _ret_fwd(f"ret_b{l}", rq_r, rk_r, z, dec_b, True, ret_chunk)
            sv.update(o_f=o_f, o_b=o_b, st_f=st_f, st_b=st_b)
            qcat, kcat, v_x, kcat_t = _mla_proj(f"mla_proj{l}", qn, uq_t, kvn, ukv_t, krr, cos_t, sin_t)
            sv["kcat_t"] = kcat_t
            a_mla, lse, got = _fa_fwd(f"fa_fwd{l}", qcat, kcat, v_x, *fa_cfg_f[i], comm=later_w if l == 0 else None)
            if l == 0:
                od_w[0], ev_w[1], od_w[1] = tuple(full(got[0:2])), ev_full(got[2:6]), tuple(full(got[6:8]))
            sv.update(qcat=qcat, kcat=kcat, v_x=v_x, a_mla=a_mla, lse=lse)

            def ev_post_body(rows, vecs):
                of, ob, rg, a, mg = rows
                return [_ev_post_fn(_heads(of + ob), rg, a, mg)], []

            (act,), _ = _rowwise(f"ev_post{l}", ev_post_body,
                                 [(o_f, 512, 0), (o_b, 512, 0), (z, 512, 1), (a_mla, 512, 0), (z, 512, 2)], [], [(D, BF16)])
        else:
            win_t, wout = od_w[i]
            z = _mm(f"od_in{l}", h, win_t, "nt", out_dtype=BF16, bias=b_in_all[i])
            sv["z"] = z

            u2 = _conv_fwd(f"conv{l}", z, dw_w_all[i], dw_b_all[i])
            sv.update(u2=u2)

            def od_post_body(rows, vecs):
                return [_od_post_fn(rows[0], rows[1], vecs[0], vecs[1])], []

            (act,), _ = _rowwise(f"od_post{l}", od_post_body, [(u2, D, 0), (z, D, 2)], [ln_g_all[i], ln_b_all[i]],
                                 [(D, BF16)])
        sv["act"] = act
        y = _mm(f"out{l}", act, wout, "nn", out_dtype=BF16)
        sv["y"] = y

        saved.append(sv)
        if l < DEPTH - 1:
            def post_body(rows, vecs):
                xn = rows[0] + _post_fn(rows[1], vecs[0], vecs[1])
                return [xn, _pre_fn(xn, vecs[2], vecs[3], vecs[4])], []

            (xl, h), _ = _rowwise(f"post{l}", post_body, [(xl, D, 0), (y, D, 0)],
                                  [post_g[l:l + 1], gate[l], pre_g[l + 1:l + 2], scale[l + 1], shift[l + 1]],
                                  [(D, F32), (D, BF16)])
        else:
            def post_body(rows, vecs):
                diff = rows[0] + _post_fn(rows[1], vecs[0], vecs[1]) - rows[2]
                return [diff * (1.0 / D)], [_colsum(diff * diff) * (0.5 / D)]

            (dx,), (loss_lanes,) = _rowwise(f"post{l}", post_body, [(xl, D, 0), (y, D, 0), (tgt, D, 0)],
                                            [post_g[l:l + 1], gate[l]], [(D, F32)], [(1, D)])
    loss = lax.psum(jnp.sum(loss_lanes), ("x", "y", "c"))

    g_pre, g_post, g_mod = [None] * DEPTH, [None] * DEPTH, [None] * DEPTH
    g_dec_f, g_dec_b, g_qn, g_kvn = [None] * 2, [None] * 2, [None] * 2, [None] * 2
    g_b_in, g_dw_w, g_dw_b, g_ln_g, g_ln_b = [None] * 2, [None] * 2, [None] * 2, [None] * 2, [None] * 2
    recv_ev, recv_od = [None] * 2, [None] * 2
    pending = []
    for l in reversed(range(DEPTH)):
        i = l // 2
        sv = saved[l]

        if l == DEPTH - 1:
            def post_bwd_body(rows, vecs):
                yv, dxn = rows
                _, vjp = jax.vjp(_post_fn, yv, vecs[0], vecs[1])
                d_y, dg, d_gate = vjp(dxn)
                return [d_y], [dg, d_gate]

            (dy,), (dpost, dgate) = _rowwise(f"post_bwd{l}", post_bwd_body, [(sv["y"], D, 0), (dx, D, 0)],
                                             [post_g[l:l + 1], gate[l]], [(D, BF16)], [(1, D), (1, D)])
        g_post[l] = dpost
        dgate_l = dgate
        wout = ev_w[i][3] if l % 2 == 0 else od_w[i][1]
        dact = _mm(f"out_dgrad{l}", dy, wout, "nt", out_dtype=BF16)
        d_wout = _mm(f"out_wgrad{l}", sv["act"], dy, "tn", out_dtype=BF16)
        z = sv["z"]
        if l % 2 == 0:
            win_t, uq_t, ukv_t, _ = ev_w[i]
            dec_f, dec_b = ev_dec_f[i:i + 1], ev_dec_b[i:i + 1]

            def ev_post_bwd_body(rows, vecs):
                of, ob, rg, a, mg, da = rows
                do_heads, drg, d_a, dmg = _ev_post_bwd(_heads(of + ob), rg, a, mg, da)
                deltas = [jnp.sum(dh_ * ah_, axis=1, keepdims=True) for dh_, ah_ in zip(_heads(d_a), _heads(a))]
                return [jnp.concatenate(do_heads, axis=1), drg, d_a, dmg] + deltas, []

            (do_ret, drg, do_mla, dmg, dl0, dl1, dl2, dl3), _ = _rowwise(
                f"ev_post_bwd{l}", ev_post_bwd_body,
                [(sv["o_f"], 512, 0), (sv["o_b"], 512, 0), (z, 512, 1), (sv["a_mla"], 512, 0), (z, 512, 2), (dact, D, 0)],
                [], [(512, BF16), (512, BF16), (512, BF16), (512, BF16)] + [(1, F32)] * 4)
            delta = jnp.stack([dl0, dl1, dl2, dl3]).reshape(MLA_HEADS, 1, s)
            lse = sv["lse"].reshape(MLA_HEADS, 1, s)
            kt = sv["kcat_t"]
            if l == 0:
                pending = pending + [(d_wout.reshape(N_DEV, 128, D), True)]
            dqt, dkcat, dv, got = _fa_bwd(f"fa_bwd{l}", sv["qcat"], sv["kcat"], kt, sv["v_x"], do_mla, lse, delta,
                                          *fa_cfg_b[i], comm=pending)
            recv_od[i] = got[0:2]
            wout_recv = got[2:]
            dqcat = dqt.transpose(1, 3, 0, 2).reshape(s, 1024)

            def mla_prep_bwd_body(rows, vecs):
                dq, dk, dvv, cos, sin = rows
                qs = []
                dkrr = jnp.zeros((dq.shape[0], LANES), F32)
                for hh in range(4):
                    qs += [dq[:, 256 * hh:256 * hh + 128], _rope_t(dq[:, 256 * hh + 128:256 * hh + 256], cos, sin)]
                    dkrr = dkrr + dk[:, 256 * hh + 128:256 * hh + 256]
                return [jnp.concatenate(qs, axis=1) * MLA_SCALE, jnp.concatenate([dk, dvv], axis=1), dkrr], []

            (dq_pad, dkv_pad, dkrr), _ = _rowwise(
                f"mla_prep_bwd{l}", mla_prep_bwd_body,
                [(dqcat, 1024, 0), (dkcat, 1024, 0), (dv, 512, 0), (cos_t, LANES, 0), (sin_t, LANES, 0)], [],
                [(1024, BF16), (1536, BF16), (LANES, F32)])
            dqn = _mm(f"uq_dgrad{l}", dq_pad, uq_t, "nn")
            d_uq = _mm(f"uq_wgrad{l}", dq_pad, sv["qn"], "tn", out_dtype=BF16)
            dkvn = _mm(f"ukv_dgrad{l}", dkv_pad, ukv_t, "nn")
            d_ukv = _mm(f"ukv_wgrad{l}", dkv_pad, sv["kvn"], "tn", out_dtype=BF16)
            dq_f, dk_f, dv_f, ddec_f = _ret_bwd(f"ret_f_bwd{l}", sv["rq_r"], sv["rk_r"], z, dec_f, sv["st_f"], do_ret,
                                                False, ret_chunk)
            dq_b, dk_b, dv_b, ddec_b = _ret_bwd(f"ret_b_bwd{l}", sv["rq_r"], sv["rk_r"], z, dec_b, sv["st_b"], do_ret,
                                                True, ret_chunk)
            g_dec_f[i], g_dec_b[i] = ddec_f[:, :RET_HEADS], ddec_b[:, :RET_HEADS]

            def prep_bwd_body(rows, vecs):
                cq, ckv, cos, sin, dqf, dqb, dkf, dkb, dvf, dvb, d_qn, d_kvn, d_krr = rows
                _, vjp_q = jax.vjp(_rms, cq, vecs[0])
                dcq, dgq = vjp_q(d_qn)
                _, vjp_kv = jax.vjp(_rms, ckv, vecs[1])
                dckv, dgkv = vjp_kv(d_kvn)
                return [dvf + dvb, dcq, _rope_t(d_krr, cos, sin), _rope_t(dqf + dqb, cos, sin),
                        _rope_t(dkf + dkb, cos, sin) * RET_SCALE, dckv], [dgq, dgkv]

            (drv, dcq, dkr, drq, drk, dckv), (dgq, dgkv) = _rowwise(
                f"ev_prep_bwd{l}", prep_bwd_body,
                [(z, 384, 4), (z, 256, 12), (cos_t, LANES, 0), (sin_t, LANES, 0), (dq_f, 512, 0), (dq_b, 512, 0),
                 (dk_f, 512, 0), (dk_b, 512, 0), (dv_f, 512, 0), (dv_b, 512, 0), (dqn, 384, 0), (dkvn, 256, 0),
                 (dkrr, LANES, 0)],
                [ev_q_norm_g[i:i + 1], ev_kv_norm_g[i:i + 1]],
                [(512, BF16), (384, BF16), (LANES, BF16), (512, BF16), (512, BF16), (256, BF16)], [(1, 384), (1, 256)])
            g_qn[i], g_kvn[i] = dgq, dgkv
            dz = [drv, drg, dmg, dcq, dkr, drq, drk, dckv]
            dz_off = [ZL_EV[nm][0] for nm in ("rv", "rg", "mg", "cq", "kr", "rq", "rk", "ckv")]
            g_rv, g_rg, g_mg, g_cq, g_kr, g_rq, g_rk, g_ckv = _mm_cols_tn(f"in_wgrad{l}", dz, sv["h"])
            d_win = jnp.concatenate([g_rq[128 * hh:128 * hh + 64] for hh in range(4)]
                                    + [g_rk[128 * hh:128 * hh + 64] for hh in range(4)]
                                    + [g_rv, g_rg, g_cq, g_ckv, g_kr[:64], g_mg], axis=0)
            pending = [(d_win.reshape(N_DEV, 344, D), True), (_uq_unlayout(d_uq).reshape(N_DEV, 96, 384), True),
                       (_ukv_unlayout(d_ukv).reshape(N_DEV, 128, 256), True)]
            if l > 0:
                pending.append((d_wout.reshape(N_DEV, 128, D), True))
        else:
            win_t, _ = od_w[i]

            def od_post_bwd_body(rows, vecs):
                u2, gg, da = rows
                du2, dgg, dlg, dlb = _od_post_bwd(u2, gg, da, vecs[0], vecs[1])
                return [du2, dgg], [dlg, dlb, _colsum(dgg)]

            (du2, dg_gate), (dlg, dlb, dbg) = _rowwise(
                f"od_post_bwd{l}", od_post_bwd_body, [(sv["u2"], D, 0), (z, D, 2), (dact, D, 0)],
                [ln_g_all[i], ln_b_all[i]], [(D, BF16), (D, BF16)], [(1, D), (1, D), (1, D)])
            g_ln_g[i], g_ln_b[i] = dlg, dlb
            (d_a, d_b, d_dw, d_dwb, dba, dbb), got = _conv_bwd(f"conv_bwd{l}", z, du2, dw_w_all[i], comm=pending)
            if pending:
                recv_ev[i + 1] = got
            g_dw_w[i], g_dw_b[i] = d_dw[:CONV_K], d_dwb
            g_b_in[i] = jnp.concatenate([dba, dbb, dbg], axis=1)
            dz, dz_off = [d_a, d_b, dg_gate], [0, D, 2 * D]
            d_win = jnp.concatenate(_mm_cols_tn(f"in_wgrad{l}", dz, sv["h"]), axis=0)
            pending = [(d_win.reshape(N_DEV, 384, D), True), (d_wout.reshape(N_DEV, 128, D), True)]
        if l == 0:
            dh, got = _mm_cols_nn(f"in_dgrad{l}", dz, dz_off, win_t, comm=pending)
            recv_ev[0] = list(got) + list(wout_recv)
        else:
            dh = _mm_cols_nn(f"in_dgrad{l}", dz, dz_off, win_t)

        if l > 0:
            def pre_bwd_body(rows, vecs):
                xv, d_h, dxn, yv = rows
                _, vjp = jax.vjp(_pre_fn, xv, vecs[0], vecs[1], vecs[2])
                d_x, dg, dsc, dsh = vjp(d_h)
                d_x = d_x + dxn
                _, vjp_p = jax.vjp(_post_fn, yv, vecs[3], vecs[4])
                d_y, dgp, d_gate = vjp_p(d_x)
                return [d_x, d_y], [dg, dsc, dsh, dgp, d_gate]

            (dx, dy), (dpre, dscale, dshift, dpost, dgate) = _rowwise(
                f"pre_bwd{l}", pre_bwd_body, [(sv["x"], D, 0), (dh, D, 0), (dx, D, 0), (saved[l - 1]["y"], D, 0)],
                [pre_g[l:l + 1], scale[l], shift[l], post_g[l - 1:l], gate[l - 1]], [(D, F32), (D, BF16)], [(1, D)] * 5)
        else:
            def pre_bwd_body(rows, vecs):
                xv, d_h, dxn = rows
                _, vjp = jax.vjp(_pre_fn, xv, *vecs)
                d_x, dg, dsc, dsh = vjp(d_h)
                return [d_x + dxn], [dg, dsc, dsh]

            (dx,), (dpre, dscale, dshift) = _rowwise(f"pre_bwd{l}", pre_bwd_body, [(sv["x"], D, 0), (dh, D, 0), (dx, D, 0)],
                                                     [pre_g[l:l + 1], scale[l], shift[l]], [(D, F32)], [(1, D)] * 3)
        g_pre[l] = dpre
        g_mod[l] = jnp.concatenate([dshift, dscale, dgate_l], axis=1)

    grad_x = dx.reshape(1, s, D)

    end_parts = [jnp.concatenate(g_mod, axis=0), jnp.concatenate(g_pre, axis=0), jnp.concatenate(g_post, axis=0),
                 jnp.concatenate(g_dec_f, axis=0), jnp.concatenate(g_dec_b, axis=0), jnp.concatenate(g_qn, axis=0),
                 jnp.concatenate(g_kvn, axis=0), jnp.concatenate(g_b_in, axis=0), jnp.stack(g_dw_w),
                 jnp.concatenate(g_dw_b, axis=0), jnp.concatenate(g_ln_g, axis=0), jnp.concatenate(g_ln_b, axis=0)]
    end_shapes = [p.shape for p in end_parts]
    end_sizes = [int(np.prod(sh)) for sh in end_shapes]
    end_len = -(-sum(end_sizes) // 1024) * 1024
    end_vec = jnp.concatenate([p.reshape(-1) for p in end_parts] + [jnp.zeros((end_len - sum(end_sizes),), F32)])
    end_all = _exchange("gather_end", [end_vec.reshape(-1, LANES)], False)[0].reshape(N_DEV, end_len)
    eo = np.cumsum([0] + end_sizes)
    ends = [end_all[:, eo[j]:eo[j + 1]].reshape((N_DEV,) + tuple(end_shapes[j])) for j in range(len(end_parts))]
    (dmod_all, pre_all, post_all, decf_all, decb_all, qn_all, kvn_all, bin_all, dww_all, dwb_all, lng_all,
     lnb_all) = ends

    def pack_rep(*ts):
        lead = ts[0].ndim - 2
        return jnp.concatenate([t.reshape(t.shape[:lead] + (-1,)) for t in ts], axis=-1)

    rep_sizes = [DEPTH * 3 * D, DEPTH * D, DEPTH * D, 8, 8, 2 * 384, 2 * 256]
    rep_len = -(-sum(rep_sizes) // 1024) * 1024
    rep_pad = rep_len - sum(rep_sizes)

    def rep_rows(flat):
        padz = jnp.zeros(flat.shape[:-1] + (rep_pad,), F32)
        return jnp.concatenate([flat, padz], axis=-1).reshape(flat.shape[:-1] + (rep_len // LANES, LANES))

    rep_w = rep_rows(pack_rep(ada_b, pre_g, post_g, ev_dec_f, ev_dec_b, ev_q_norm_g, ev_kv_norm_g))
    rep_m = rep_rows(pack_rep(m_ada_b, m_pre_g, m_post_g, m_ev_dec_f, m_ev_dec_b, m_ev_q_norm_g, m_ev_kv_norm_g))
    rep_v = rep_rows(pack_rep(v_ada_b, v_pre_g, v_post_g, v_ev_dec_f, v_ev_dec_b, v_ev_q_norm_g, v_ev_kv_norm_g))
    rep_g = rep_rows(pack_rep(dmod_all, pre_all, post_all, decf_all, decb_all, qn_all, kvn_all))
    rep_out = _adamw("adamw_rep", rep_w, rep_m, rep_v, rep_g)
    ro = np.cumsum([0] + rep_sizes)
    rep_shapes = [(DEPTH, 3 * D), (DEPTH, D), (DEPTH, D), (2, 4), (2, 4), (2, 384), (2, 256)]

    def unpack_rep(t):
        flat = t.reshape(-1)
        return [flat[ro[j]:ro[j + 1]].reshape(rep_shapes[j]) for j in range(len(rep_shapes))]

    rep_res = [unpack_rep(t) for t in rep_out]

    def my_cols(t, width):
        return lax.dynamic_slice_in_dim(t, me * width, width, axis=t.ndim - 1)

    def vec_adamw(name, w, m, v, g_all, width):
        g = my_cols(g_all, width)
        r = _adamw(name, w.reshape(-1, width), m.reshape(-1, width), v.reshape(-1, width),
                   g.reshape(N_DEV, -1, width))
        return [t.reshape(w.shape) for t in r]

    res_b_in = vec_adamw("adamw_b_in", od_b_in, m_od_b_in, v_od_b_in, bin_all, 384)
    res_dw_w = vec_adamw("adamw_dw_w", od_dw_w, m_od_dw_w, v_od_dw_w, dww_all, LANES)
    res_dw_b = vec_adamw("adamw_dw_b", od_dw_b, m_od_dw_b, v_od_dw_b, dwb_all, LANES)
    res_ln_g = vec_adamw("adamw_ln_g", od_ln_g, m_od_ln_g, v_od_ln_g, lng_all, LANES)
    res_ln_b = vec_adamw("adamw_ln_b", od_ln_b, m_od_ln_b, v_od_ln_b, lnb_all, LANES)

    dmod_mine = my_cols(dmod_all, 384).transpose(1, 0, 2)
    dmod_pad = jnp.concatenate([dmod_mine, jnp.zeros((DEPTH, LANES - N_DEV, 384), F32)], axis=1)
    c_all_t = jnp.concatenate([c_all.T, jnp.zeros((D, LANES - N_DEV), F32)], axis=1)
    g_ada_w = _ada_w_grad("ada_w_grad", c_all_t, dmod_pad)
    res_ada_w = [t.reshape(ada_w.shape) for t in
                 _adamw("adamw_ada_w", ada_w.reshape(-1, 384), m_ada_w.reshape(-1, 384), v_ada_w.reshape(-1, 384),
                        g_ada_w.reshape(-1, 384))]

    ev_sh = [[_sum_parts(f"sum_g_ev{i}_{j}", r) for j, r in enumerate(recv_ev[i])] for i in range(2)]
    od_sh = [[_sum_parts(f"sum_g_od{i}_{j}", r) for j, r in enumerate(recv_od[i])] for i in range(2)]

    def mat_adamw(name, w, m, v, g):
        r = _adamw(name, w.reshape(-1, w.shape[-1]), m.reshape(-1, w.shape[-1]), v.reshape(-1, w.shape[-1]),
                   g.reshape(-1, w.shape[-1]))
        return [t.reshape(w.shape) for t in r]

    res_ev_w_in = mat_adamw("adamw_ev_w_in", ev_w_in, m_ev_w_in, v_ev_w_in, jnp.stack([ev_sh[i][0].T for i in range(2)]))
    res_ev_w_uq = mat_adamw("adamw_ev_w_uq", ev_w_uq, m_ev_w_uq, v_ev_w_uq, jnp.stack([ev_sh[i][1].T for i in range(2)]))
    res_ev_w_ukv = mat_adamw("adamw_ev_w_ukv", ev_w_ukv, m_ev_w_ukv, v_ev_w_ukv,
                             jnp.stack([ev_sh[i][2].T for i in range(2)]))
    res_ev_w_out = mat_adamw("adamw_ev_w_out", ev_w_out, m_ev_w_out, v_ev_w_out, jnp.stack([ev_sh[i][3] for i in range(2)]))
    res_od_w_in = mat_adamw("adamw_od_w_in", od_w_in, m_od_w_in, v_od_w_in, jnp.stack([od_sh[i][0].T for i in range(2)]))
    res_od_w_out = mat_adamw("adamw_od_w_out", od_w_out, m_od_w_out, v_od_w_out, jnp.stack([od_sh[i][1] for i in range(2)]))

    per_weight = [res_ada_w] + [[rep_res[t][j] for t in range(4)] for j in range(3)]
    per_weight += [res_ev_w_in, [rep_res[t][3] for t in range(4)], [rep_res[t][4] for t in range(4)],
                   [rep_res[t][5] for t in range(4)], res_ev_w_uq, [rep_res[t][6] for t in range(4)], res_ev_w_ukv,
                   res_ev_w_out, res_od_w_in, res_b_in, res_dw_w, res_dw_b, res_ln_g, res_ln_b, res_od_w_out]
    outs = [loss, grad_x]
    for t in range(4):
        outs += [pw[t] for pw in per_weight]
    return tuple(outs)
```

```python
import functools

import numpy as np
import jax
import jax.numpy as jnp
from jax import lax
from jax.experimental import pallas as pl
from jax.experimental.pallas import tpu as pltpu

F32 = jnp.float32
BF16 = jnp.bfloat16
MESH = pl.DeviceIdType.MESH

N_DEV = 8
D = 1024
DEPTH = 4
EPS = 1e-6
RET_HEADS = 4
MLA_HEADS = 4
RET_SCALE = 64 ** -0.5
MLA_SCALE = 192 ** -0.5
CONV_K = 31
CONV_HALO = 16
ROPE_BASE = 10000.0

ADAM_LR = 0.001
ADAM_B1 = 0.9
ADAM_B2 = 0.999
ADAM_EPS = 1e-08
ADAM_WD = 0.01
ADAM_STEP = 10

LANES = 128
VMEM_LIMIT = 48 * 1024 * 1024

ZL_EV = dict(rv=(0, 512), rg=(512, 512), mg=(1024, 512), cq=(1536, 384), kr=(1920, 128),
             rq=(2048, 512), rk=(2560, 512), ckv=(3072, 256))
ZW_EV = 3328
ZW_OD = 3072


def _cparams(sem, vmem=VMEM_LIMIT):
    return pltpu.CompilerParams(dimension_semantics=sem, vmem_limit_bytes=vmem)


def _pick(n, prefs):
    for p in prefs:
        if n % p == 0:
            return p
    return n


def _sigmoid(x):
    return 0.5 * (jnp.tanh(0.5 * x) + 1.0)


def _silu(x):
    return x * _sigmoid(x)


def _rms(x, g):
    return x * lax.rsqrt(jnp.mean(x * x, axis=-1, keepdims=True) + EPS) * g


def _log_sigmoid(x):
    return jnp.minimum(x, 0.0) - jnp.log(1.0 + jnp.exp(jnp.minimum(x, -x)))


def _tile_lanes(t, width):
    reps = width // t.shape[1]
    return t if reps == 1 else jnp.concatenate([t] * reps, axis=1)


def _rot_half(x):
    w = x.shape[1]
    lane = lax.broadcasted_iota(jnp.int32, x.shape, 1)
    first = jnp.bitwise_and(lane, 63) < 32
    return jnp.where(first, pltpu.roll(x, w - 32, 1), pltpu.roll(x, 32, 1))


def _rope(x, cos, sin):
    w = x.shape[1]
    return x * _tile_lanes(cos, w) + _rot_half(x) * _tile_lanes(sin, w)


def _rope_t(dy, cos, sin):
    w = dy.shape[1]
    return dy * _tile_lanes(cos, w) + _rot_half(dy * _tile_lanes(sin, w))


_DN = {"nn": (((1,), (0,)), ((), ())), "nt": (((1,), (1,)), ((), ())), "tn": (((0,), (0,)), ((), ()))}


def _dot(a, b, mode):
    return lax.dot_general(a.astype(BF16), b.astype(BF16), _DN[mode], preferred_element_type=F32)


@functools.partial(jax.custom_vjp, nondiff_argnums=(2,))
def _bdot(a, b, mode):
    return _dot(a, b, mode)


def _bdot_fwd(a, b, mode):
    return _dot(a, b, mode), (a, b)


def _bdot_bwd(mode, res, g):
    a, b = res
    if mode == "nn":
        return _dot(g, b, "nt"), _dot(a, g, "tn")
    if mode == "nt":
        return _dot(g, b, "nn"), _dot(g, a, "tn")
    return _dot(b, g, "nt"), _dot(a, g, "nn")


_bdot.defvjp(_bdot_fwd, _bdot_bwd)


def _rowwise(name, body, row_ins, vec_ins, row_outs, red_outs=(), tile=512):
    s = row_ins[0][0].shape[0]
    tile = min(tile, s)
    nr, nv, no = len(row_ins), len(vec_ins), len(row_outs)

    def kern(*refs):
        rows = [r[...].astype(F32) if r.dtype == BF16 else r[...] for r in refs[:nr]]
        vecs = [r[...] for r in refs[nr:nr + nv]]
        outs, reds = body(rows, vecs)
        for r, o in zip(refs[nr + nv:nr + nv + no], outs):
            r[...] = o.astype(r.dtype)
        red_refs = refs[nr + nv + no:]
        if red_refs:
            @pl.when(pl.program_id(0) == 0)
            def _():
                for r in red_refs:
                    r[...] = jnp.zeros(r.shape, r.dtype)
            for r, v in zip(red_refs, reds):
                r[...] += v

    in_specs = [pl.BlockSpec((tile, w), (lambda i, cb=cb: (i, cb))) for (_, w, cb) in row_ins]
    in_specs += [pl.BlockSpec(v.shape, (lambda i, nd=v.ndim: (0,) * nd)) for v in vec_ins]
    out_specs = [pl.BlockSpec((tile, w), lambda i: (i, 0)) for (w, _) in row_outs]
    out_specs += [pl.BlockSpec(sh, lambda i: (0, 0)) for sh in red_outs]
    out_shape = [jax.ShapeDtypeStruct((s, w), dt) for (w, dt) in row_outs]
    out_shape += [jax.ShapeDtypeStruct(sh, F32) for sh in red_outs]
    res = pl.pallas_call(
        kern, name=name, grid=(s // tile,), in_specs=in_specs, out_specs=out_specs, out_shape=out_shape,
        compiler_params=_cparams(("arbitrary",)),
    )(*[a for (a, _, _) in row_ins], *vec_ins)
    return res[:no], res[no:]


def _mm(name, a, b, mode, out_dtype=F32, bias=None, comm=None):
    if mode == "tn":
        k, m = a.shape
        n = b.shape[1]
        tm = m if m <= 1664 else m // 2
        tk = min(k, 2048 if tm * n <= 1024 * 1024 else 1024)
        nk = k // tk

        def kern(a_ref, b_ref, o_ref, acc_ref):
            kk = pl.program_id(1)
            part = _dot(a_ref[...], b_ref[...], "tn")

            @pl.when(kk == 0)
            def _():
                acc_ref[...] = part

            @pl.when(kk > 0)
            def _():
                acc_ref[...] += part

            @pl.when(kk == nk - 1)
            def _():
                o_ref[...] = acc_ref[...].astype(o_ref.dtype)

        return pl.pallas_call(
            kern, name=name, grid=(m // tm, nk),
            in_specs=[pl.BlockSpec((tk, tm), lambda i, kk: (kk, i)),
                      pl.BlockSpec((tk, n), lambda i, kk: (kk, 0))],
            out_specs=pl.BlockSpec((tm, n), lambda i, kk: (i, 0)),
            out_shape=jax.ShapeDtypeStruct((m, n), out_dtype),
            scratch_shapes=[pltpu.VMEM((tm, n), F32)],
            compiler_params=_cparams(("parallel", "arbitrary")),
        )(a, b)

    m, k = a.shape
    n = b.shape[1] if mode == "nn" else b.shape[0]
    tm = min(m, 1024)
    tn = n if n <= 1664 else n // 2
    has_bias = bias is not None

    def kern(*refs):
        a_ref, b_ref = refs[0], refs[1]
        o_ref = refs[-1]
        r = _dot(a_ref[...], b_ref[...], mode)
        if has_bias:
            r = r + refs[2][...]
        o_ref[...] = r.astype(o_ref.dtype)

    b_spec = (pl.BlockSpec((k, tn), lambda i, j: (0, j)) if mode == "nn"
              else pl.BlockSpec((tn, k), lambda i, j: (j, 0)))
    in_specs = [pl.BlockSpec((tm, k), lambda i, j: (i, 0)), b_spec]
    args = [a, b]
    if has_bias:
        in_specs.append(pl.BlockSpec((1, tn), lambda i, j: (0, j)))
        args.append(bias)
    outs, got = _call(kern, name=name, grid=(m // tm, n // tn), in_specs=in_specs,
                      out_specs=[pl.BlockSpec((tm, tn), lambda i, j: (i, j))],
                      out_shape=[jax.ShapeDtypeStruct((m, n), out_dtype)], args=args, sem=("parallel", "parallel"),
                      comm=comm)
    return (outs[0], got) if comm else outs[0]


def _mm_cols_nn(name, pieces, offsets, b, comm=None):
    m = pieces[0].shape[0]
    n = b.shape[1]
    tm = min(m, 1024)
    np_ = len(pieces)

    def kern(*refs):
        acc = _dot(refs[0][...], refs[np_][...], "nn")
        for p in range(1, np_):
            acc = acc + _dot(refs[p][...], refs[np_ + p][...], "nn")
        refs[2 * np_][...] = acc.astype(BF16)

    in_specs = [pl.BlockSpec((tm, a.shape[1]), lambda i: (i, 0)) for a in pieces]
    in_specs += [pl.BlockSpec((a.shape[1], n), (lambda i, r=off // a.shape[1]: (r, 0))) for a, off in zip(pieces, offsets)]
    outs, got = _call(kern, name=name, grid=(m // tm,), in_specs=in_specs,
                      out_specs=[pl.BlockSpec((tm, n), lambda i: (i, 0))],
                      out_shape=[jax.ShapeDtypeStruct((m, n), BF16)], args=list(pieces) + [b] * np_, sem=("parallel",),
                      comm=comm)
    return (outs[0], got) if comm else outs[0]


def _mm_cols_tn(name, pieces, b):
    k, n = b.shape
    tk = min(k, 512)
    nk = k // tk
    np_ = len(pieces)

    def kern(*refs):
        b_ref = refs[np_]
        o_refs, acc_refs = refs[np_ + 1:2 * np_ + 1], refs[2 * np_ + 1:]
        kk = pl.program_id(0)

        @pl.when(kk == 0)
        def _():
            for r in acc_refs:
                r[...] = jnp.zeros(r.shape, F32)

        bb = b_ref[...]
        for p in range(np_):
            acc_refs[p][...] += _dot(refs[p][...], bb, "tn")

        @pl.when(kk == nk - 1)
        def _():
            for o, r in zip(o_refs, acc_refs):
                o[...] = r[...].astype(o.dtype)

    return pl.pallas_call(
        kern, name=name, grid=(nk,),
        in_specs=[pl.BlockSpec((tk, a.shape[1]), lambda kk: (kk, 0)) for a in pieces] + [pl.BlockSpec((tk, n), lambda kk: (kk, 0))],
        out_specs=[pl.BlockSpec((a.shape[1], n), lambda kk: (0, 0)) for a in pieces],
        out_shape=[jax.ShapeDtypeStruct((a.shape[1], n), BF16) for a in pieces],
        scratch_shapes=[pltpu.VMEM((a.shape[1], n), F32) for a in pieces],
        compiler_params=_cparams(("arbitrary",)),
    )(*pieces, b)


def _mla_proj(name, qn, uq_t, kvn, ukv_t, krr, cos, sin):
    s = qn.shape[0]
    tm = min(s, 1024)

    def kern(qn_ref, uq_ref, kvn_ref, ukv_ref, kr_ref, cos_ref, sin_ref, q_out, k_out, v_out, kt_out):
        qp = _dot(qn_ref[...], uq_ref[...], "nt")
        kv = _dot(kvn_ref[...], ukv_ref[...], "nt")
        c, sn = cos_ref[...], sin_ref[...]
        kr_r = kr_ref[...].astype(F32)
        ones = jnp.ones((tm, LANES), F32)
        qs, ks, vs = [], [], []
        for hh in range(MLA_HEADS):
            qs += [qp[:, 256 * hh:256 * hh + 128], _rope(qp[:, 256 * hh + 128:256 * hh + 256], c, sn)]
            ks += [kv[:, 256 * hh:256 * hh + 128], kr_r]
            vs += [kv[:, 1024 + LANES * hh:1024 + LANES * hh + LANES], ones]
        q_out[...] = (jnp.concatenate(qs, axis=1) * MLA_SCALE).astype(BF16)
        k_out[...] = jnp.concatenate(ks, axis=1).astype(BF16)
        v_out[...] = jnp.concatenate(vs, axis=1).astype(BF16)
        for hh in range(MLA_HEADS):
            kt_out[hh] = jnp.concatenate([ks[2 * hh], ks[2 * hh + 1]], axis=1).T.astype(BF16)

    row = lambda w: pl.BlockSpec((tm, w), lambda i: (i, 0))
    whole = lambda a: pl.BlockSpec(a.shape, lambda i: (0, 0))
    return pl.pallas_call(
        kern, name=name, grid=(s // tm,),
        in_specs=[row(qn.shape[1]), whole(uq_t), row(kvn.shape[1]), whole(ukv_t), row(LANES), row(LANES), row(LANES)],
        out_specs=[row(1024)] * 3 + [pl.BlockSpec((MLA_HEADS, 256, tm), lambda i: (0, 0, i))],
        out_shape=[jax.ShapeDtypeStruct((s, 1024), BF16)] * 3 + [jax.ShapeDtypeStruct((MLA_HEADS, 256, s), BF16)],
        compiler_params=_cparams(("parallel",)),
    )(qn, uq_t, kvn, ukv_t, krr, cos, sin)


def _peers():
    mx, my, mc = lax.axis_index("x"), lax.axis_index("y"), lax.axis_index("c")
    me = 4 * mx + 2 * my + mc
    out = []
    for k in range(1, N_DEV):
        px = 1 - mx if (k >> 2) & 1 else mx
        py = 1 - my if (k >> 1) & 1 else my
        pc = 1 - mc if k & 1 else mc
        out.append((k, (px, py, pc), 4 * px + 2 * py + pc))
    return me, out


def _xchg_copies(x_refs, out_refs, scatter, send_sems, recv_sems, local_sems):
    me, peers = _peers()
    local, out, arrive = [], [], []
    for a, (x, o, sc) in enumerate(zip(x_refs, out_refs, scatter)):
        mine = x.at[me] if sc else x
        local.append(pltpu.make_async_copy(mine, o.at[me], local_sems.at[a]))
        for k, dev, p in peers:
            out.append(pltpu.make_async_remote_copy(
                src_ref=x.at[p] if sc else x, dst_ref=o.at[me],
                send_sem=send_sems.at[a, k - 1], recv_sem=recv_sems.at[a, k - 1],
                device_id=dev, device_id_type=MESH))
            arrive.append(pltpu.make_async_remote_copy(
                src_ref=mine, dst_ref=o.at[p],
                send_sem=send_sems.at[a, k - 1], recv_sem=recv_sems.at[a, k - 1],
                device_id=dev, device_id_type=MESH))
    return local, out, arrive


def _xchg_start(*args):
    local, out, _ = _xchg_copies(*args)
    for cp in local + out:
        cp.start()


def _xchg_wait(*args):
    local, out, arrive = _xchg_copies(*args)
    for cp in out:
        cp.wait_send()
    for cp in arrive:
        cp.wait_recv()
    for cp in local:
        cp.wait()


def _call(kern, *, name, grid, in_specs, out_specs, out_shape, args, sem, scratch_shapes=(), vmem=VMEM_LIMIT,
          comm=None):
    if not comm:
        outs = pl.pallas_call(kern, name=name, grid=grid, in_specs=in_specs, out_specs=out_specs, out_shape=out_shape,
                              scratch_shapes=list(scratch_shapes), compiler_params=_cparams(sem, vmem))(*args)
        return outs, []
    n, ni, no, ns = len(comm), len(in_specs), len(out_specs), len(scratch_shapes)
    xs = [x for x, _ in comm]
    scatter = [sc for _, sc in comm]

    def body(*refs):
        ins, x_refs = refs[:ni], refs[ni:ni + n]
        outs, out_refs = refs[ni + n:ni + n + no], refs[ni + n + no:ni + 2 * n + no]
        scr = refs[ni + 2 * n + no:ni + 2 * n + no + ns]
        sems = refs[ni + 2 * n + no + ns:]
        ids = [pl.program_id(d) for d in range(len(grid))]
        first = functools.reduce(jnp.logical_and, [i == 0 for i in ids])
        last = functools.reduce(jnp.logical_and, [i == g - 1 for i, g in zip(ids, grid)])

        @pl.when(first)
        def _():
            _xchg_start(x_refs, out_refs, scatter, *sems)

        kern(*ins, *outs, *scr)

        @pl.when(last)
        def _():
            _xchg_wait(x_refs, out_refs, scatter, *sems)

    any_spec = pl.BlockSpec(memory_space=pl.ANY)
    res = pl.pallas_call(
        body, name=name, grid=grid,
        in_specs=list(in_specs) + [any_spec] * n, out_specs=list(out_specs) + [any_spec] * n,
        out_shape=list(out_shape) + [jax.ShapeDtypeStruct((N_DEV,) + tuple(x.shape[1:] if sc else x.shape), x.dtype)
                                     for x, sc in comm],
        scratch_shapes=list(scratch_shapes) + [pltpu.SemaphoreType.DMA((n, N_DEV - 1)),
                                               pltpu.SemaphoreType.DMA((n, N_DEV - 1)), pltpu.SemaphoreType.DMA((n,))],
        compiler_params=pltpu.CompilerParams(dimension_semantics=("arbitrary",) * len(grid), vmem_limit_bytes=vmem,
                                             has_side_effects=True),
    )(*args, *xs)
    return res[:no], res[no:]


def _exchange(name, xs, scatter):
    def nothing():
        pass

    return _call(nothing, name=name, grid=(1,), in_specs=[], out_specs=[], out_shape=[], args=[], sem=("arbitrary",),
                 comm=[(x, scatter) for x in xs])[1]


def _sum_parts(name, x):
    p, r, c = x.shape
    tr = r if r * c * p * x.dtype.itemsize <= (8 << 20) else _pick(r, (256, 128, 64, 16))

    def kern(x_ref, o_ref):
        acc = x_ref[0].astype(F32)
        for i in range(1, p):
            acc = acc + x_ref[i].astype(F32)
        o_ref[...] = acc

    return pl.pallas_call(
        kern, name=name, grid=(r // tr,),
        in_specs=[pl.BlockSpec((p, tr, c), lambda i: (0, i, 0))],
        out_specs=pl.BlockSpec((tr, c), lambda i: (i, 0)),
        out_shape=jax.ShapeDtypeStruct((r, c), F32),
        compiler_params=_cparams(("parallel",)),
    )(x)


def _adamw(name, w, m, v, g):
    r, c = w.shape
    parts = g.shape[0] if g.ndim == 3 else 0
    tr = 512 if (r > 512 and r % 512 == 0) else r

    def kern(w_ref, m_ref, v_ref, g_ref, go_ref, d_ref, mo_ref, vo_ref):
        if parts:
            gg = g_ref[0]
            for i in range(1, parts):
                gg = gg + g_ref[i]
        else:
            gg = g_ref[...]
        mm = ADAM_B1 * m_ref[...] + (1.0 - ADAM_B1) * gg
        vv = ADAM_B2 * v_ref[...] + (1.0 - ADAM_B2) * (gg * gg)
        m_hat = mm / (1.0 - ADAM_B1 ** ADAM_STEP)
        v_hat = vv / (1.0 - ADAM_B2 ** ADAM_STEP)
        go_ref[...] = gg
        d_ref[...] = -ADAM_LR * (m_hat / (jnp.sqrt(v_hat) + ADAM_EPS) + ADAM_WD * w_ref[...])
        mo_ref[...] = mm
        vo_ref[...] = vv

    spec = pl.BlockSpec((tr, c), lambda i: (i, 0))
    gspec = pl.BlockSpec((parts, tr, c), lambda i: (0, i, 0)) if parts else spec
    sh = jax.ShapeDtypeStruct((r, c), F32)
    return pl.pallas_call(
        kern, name=name, grid=(r // tr,), in_specs=[spec, spec, spec, gspec],
        out_specs=[spec] * 4, out_shape=[sh] * 4,
        compiler_params=_cparams(("parallel",)),
    )(w, m, v, g)


def _ret_tables(dec_cc, dec_cd, dec_dd, reverse):
    c = dec_cc.shape[0]
    row = lax.broadcasted_iota(jnp.int32, (c, c), 0).astype(F32)
    col = lax.broadcasted_iota(jnp.int32, (c, c), 1).astype(F32)
    pos = lax.broadcasted_iota(jnp.int32, (c, LANES), 0).astype(F32)
    if reverse:
        diff, mask = col - row, col > row
        q_exp, k_exp = c - pos, pos
    else:
        diff, mask = row - col, row >= col
        q_exp, k_exp = pos + 1.0, c - 1.0 - pos
    decay = jnp.where(mask, jnp.exp(_log_sigmoid(dec_cc) * jnp.maximum(diff, 0.0)), 0.0)
    lam_cd = _log_sigmoid(dec_cd)
    return decay, jnp.exp(lam_cd * q_exp), jnp.exp(lam_cd * k_exp), jnp.exp(_log_sigmoid(dec_dd) * float(c))


def _ret_chunk(q, k, v, st, decay, qw, kw, sd):
    scores = _bdot(q, k, "nt") * decay
    o = _bdot(scores, v, "nn") + _bdot(q * qw, st, "nn")
    st_new = st * sd + _bdot(k * kw, v, "tn")
    return o, st_new


def _ret_dec(dec_ref, h, c):
    d = dec_ref[:, h:h + 1]
    return (jnp.broadcast_to(d, (c, c)), jnp.broadcast_to(d, (c, LANES)), jnp.broadcast_to(d, (LANES, LANES)))


def _ret_per_step(n_chunks):
    return 2 if (n_chunks % 2 == 0 and n_chunks >= 4) else 1


def _ret_table_scratch(c):
    return [pltpu.VMEM((RET_HEADS, c, c), F32), pltpu.VMEM((RET_HEADS, c, LANES), F32),
            pltpu.VMEM((RET_HEADS, c, LANES), F32), pltpu.VMEM((RET_HEADS, LANES, LANES), F32)]


def _ret_fwd(name, q, k, z, dec, reverse, chunk):
    s = q.shape[0]
    chunk = min(chunk, s)
    per = _ret_per_step(s // chunk)
    n = s // (chunk * per)
    cmap = (lambda i: (n - 1 - i, 0)) if reverse else (lambda i: (i, 0))
    smap = (lambda i: (n - 1 - i, 0, 0, 0)) if reverse else (lambda i: (i, 0, 0, 0))

    def kern(q_ref, k_ref, v_ref, dec_ref, o_ref, st_out_ref, st_ref, *tab_refs):
        @pl.when(pl.program_id(0) == 0)
        def _():
            st_ref[...] = jnp.zeros(st_ref.shape, F32)
            for h in range(RET_HEADS):
                for r, t in zip(tab_refs, _ret_tables(*_ret_dec(dec_ref, h, chunk), reverse)):
                    r[h] = t

        for c2 in (range(per - 1, -1, -1) if reverse else range(per)):
            rows = pl.ds(c2 * chunk, chunk)
            for h in range(RET_HEADS):
                sl = slice(LANES * h, LANES * (h + 1))
                st = st_ref[h]
                st_out_ref[c2, h] = st
                o, st_new = _ret_chunk(q_ref[rows, sl].astype(F32), k_ref[rows, sl].astype(F32),
                                       v_ref[rows, sl].astype(F32), st, *[r[h] for r in tab_refs])
                o_ref[rows, sl] = o.astype(BF16)
                st_ref[h] = st_new

    blk = chunk * per
    return pl.pallas_call(
        kern, name=name, grid=(n,),
        in_specs=[pl.BlockSpec((blk, 512), cmap), pl.BlockSpec((blk, 512), cmap),
                  pl.BlockSpec((blk, 512), cmap), pl.BlockSpec((1, RET_HEADS), lambda i: (0, 0))],
        out_specs=[pl.BlockSpec((blk, 512), cmap), pl.BlockSpec((per, RET_HEADS, LANES, LANES), smap)],
        out_shape=[jax.ShapeDtypeStruct((s, 512), BF16), jax.ShapeDtypeStruct((n * per, RET_HEADS, LANES, LANES), F32)],
        scratch_shapes=[pltpu.VMEM((RET_HEADS, LANES, LANES), F32)] + _ret_table_scratch(chunk),
        compiler_params=_cparams(("arbitrary",)),
    )(q, k, z, dec)


def _ret_bwd(name, q, k, z, dec, states, do, reverse, chunk):
    s = q.shape[0]
    chunk = min(chunk, s)
    per = _ret_per_step(s // chunk)
    n = s // (chunk * per)
    cmap = (lambda i: (i, 0)) if reverse else (lambda i: (n - 1 - i, 0))
    smap = (lambda i: (i, 0, 0, 0)) if reverse else (lambda i: (n - 1 - i, 0, 0, 0))

    def kern(q_ref, k_ref, v_ref, dec_ref, st_in_ref, do_ref, dq_ref, dk_ref, dv_ref, ddec_ref, dst_ref, *scr):
        tab_refs, gtab_refs = scr[:4], scr[4:]
        step = pl.program_id(0)

        @pl.when(step == 0)
        def _():
            dst_ref[...] = jnp.zeros(dst_ref.shape, F32)
            for r in gtab_refs:
                r[...] = jnp.zeros(r.shape, F32)
            for h in range(RET_HEADS):
                for r, t in zip(tab_refs, _ret_tables(*_ret_dec(dec_ref, h, chunk), reverse)):
                    r[h] = t

        for c2 in (range(per) if reverse else range(per - 1, -1, -1)):
            rows = pl.ds(c2 * chunk, chunk)
            for h in range(RET_HEADS):
                sl = slice(LANES * h, LANES * (h + 1))
                _, vjp = jax.vjp(_ret_chunk, q_ref[rows, sl].astype(F32), k_ref[rows, sl].astype(F32),
                                 v_ref[rows, sl].astype(F32), st_in_ref[c2, h], *[r[h] for r in tab_refs])
                grads = vjp((do_ref[rows, sl].astype(F32), dst_ref[h]))
                dq_ref[rows, sl] = grads[0].astype(BF16)
                dk_ref[rows, sl] = grads[1].astype(BF16)
                dv_ref[rows, sl] = grads[2].astype(BF16)
                dst_ref[h] = grads[3]
                for r, g in zip(gtab_refs, grads[4:]):
                    r[h] += g

        @pl.when(step == n - 1)
        def _():
            lane = lax.broadcasted_iota(jnp.int32, (1, LANES), 1)
            ddec = jnp.zeros((1, LANES), F32)
            for h in range(RET_HEADS):
                _, vjp_t = jax.vjp(functools.partial(_ret_tables, reverse=reverse), *_ret_dec(dec_ref, h, chunk))
                parts = vjp_t(tuple(r[h] for r in gtab_refs))
                tot = sum(jnp.sum(jnp.sum(p, axis=1, keepdims=True), axis=0, keepdims=True) for p in parts)
                ddec = ddec + jnp.where(lane == h, tot, 0.0)
            ddec_ref[...] = ddec

    cspec = pl.BlockSpec((chunk * per, 512), cmap)
    return pl.pallas_call(
        kern, name=name, grid=(n,),
        in_specs=[cspec, cspec, cspec, pl.BlockSpec((1, RET_HEADS), lambda i: (0, 0)),
                  pl.BlockSpec((per, RET_HEADS, LANES, LANES), smap), cspec],
        out_specs=[cspec, cspec, cspec, pl.BlockSpec((1, LANES), lambda i: (0, 0))],
        out_shape=[jax.ShapeDtypeStruct((s, 512), BF16)] * 3 + [jax.ShapeDtypeStruct((1, LANES), F32)],
        scratch_shapes=[pltpu.VMEM((RET_HEADS, LANES, LANES), F32)] + _ret_table_scratch(chunk) * 2,
        compiler_params=_cparams(("arbitrary",)),
    )(q, k, z, dec, states, do)


def _fa_fwd(name, q, k, vx, tq, tk, nsub, comm=None):
    s = q.shape[0]
    tq, tk = min(tq, s), min(tk, s)
    nk = s // tk
    sq = tq // nsub

    def kern(q_ref, k_ref, v_ref, o_ref, lse_ref, m_ref, acc_ref):
        j = pl.program_id(2)

        @pl.when(j == 0)
        def _():
            m_ref[...] = jnp.full(m_ref.shape, -jnp.inf, F32)
            acc_ref[...] = jnp.zeros(acc_ref.shape, F32)

        kb, vb = k_ref[...], v_ref[...]
        for c in range(nsub):
            rows = pl.ds(c * sq, sq)
            sc = _dot(q_ref[rows, :], kb, "nt")
            m_prev = m_ref[rows, :]
            m_new = jnp.maximum(m_prev, jnp.max(sc, axis=1, keepdims=True))
            alpha = jnp.exp(m_prev - m_new)
            p = jnp.exp(sc - m_new)
            acc_ref[rows, :] = alpha * acc_ref[rows, :] + _dot(p, vb, "nn")
            m_ref[rows, :] = m_new

        @pl.when(j == nk - 1)
        def _():
            den = acc_ref[:, LANES:]
            o_ref[...] = (acc_ref[:, :LANES] / den).astype(BF16)
            lse_ref[...] = m_ref[...] + jnp.log(den[:, :1])

    (o, lse), got = _call(
        kern, name=name, grid=(MLA_HEADS, s // tq, nk),
        in_specs=[pl.BlockSpec((tq, 256), lambda h, i, j: (i, h)),
                  pl.BlockSpec((tk, 256), lambda h, i, j: (j, h)),
                  pl.BlockSpec((tk, 256), lambda h, i, j: (j, h))],
        out_specs=[pl.BlockSpec((tq, LANES), lambda h, i, j: (i, h)),
                   pl.BlockSpec((None, tq, 1), lambda h, i, j: (h, i, 0))],
        out_shape=[jax.ShapeDtypeStruct((s, 512), BF16), jax.ShapeDtypeStruct((MLA_HEADS, s, 1), F32)],
        scratch_shapes=[pltpu.VMEM((tq, 1), F32), pltpu.VMEM((tq, 256), F32)],
        args=[q, k, vx], sem=("parallel", "parallel", "arbitrary"), comm=comm)
    return o, lse, got


def _fa_bwd(name, q, k, kt, vx, do, lse, delta, tq, tk, nsub, comm=None):
    s = q.shape[0]
    tq, tk = min(tq, s), min(tk, s)
    nq = s // tq
    sk = tk // nsub

    nkb = s // tk

    def kern(q_ref, k_ref, kt_ref, v_ref, do_ref, lse_ref, dl_ref, dqt_ref, dk_ref, dv_ref, dqt_acc, dk_acc, dv_acc):
        j, i = pl.program_id(1), pl.program_id(2)

        @pl.when(i == 0)
        def _():
            dv_acc[...] = jnp.zeros(dv_acc.shape, F32)
            dk_acc[...] = jnp.zeros(dk_acc.shape, F32)

        @pl.when(j == 0)
        def _():
            dqt_acc[i] = jnp.zeros((256, tq), F32)

        qb, dob = q_ref[...], do_ref[...]
        lse_row, dl_row = lse_ref[...], dl_ref[...]
        dqt = dqt_acc[i]
        for c in range(nsub):
            rows = pl.ds(c * sk, sk)
            st = _dot(k_ref[rows, :], qb, "nt")
            pt = jnp.exp(st - lse_row)
            dpt = _dot(v_ref[rows, :], dob, "nt")
            dst = (pt * (dpt - dl_row)).astype(BF16)
            dv_acc[rows, :] += _dot(pt, dob, "nn")
            dk_acc[rows, :] += _dot(dst, qb, "nn")
            dqt = dqt + _dot(kt_ref[:, rows], dst, "nn")
        dqt_acc[i] = dqt

        @pl.when(i == nq - 1)
        def _():
            dv_ref[...] = dv_acc[...].astype(BF16)
            dk_ref[...] = dk_acc[...].astype(BF16)

        @pl.when(j == nkb - 1)
        def _():
            dqt_ref[i] = dqt.astype(BF16)

    outs, got = _call(
        kern, name=name, grid=(MLA_HEADS, s // tk, nq),
        in_specs=[pl.BlockSpec((tq, 256), lambda h, j, i: (i, h)),
                  pl.BlockSpec((tk, 256), lambda h, j, i: (j, h)),
                  pl.BlockSpec((None, 256, tk), lambda h, j, i: (h, 0, j)),
                  pl.BlockSpec((tk, LANES), lambda h, j, i: (j, 2 * h)),
                  pl.BlockSpec((tq, LANES), lambda h, j, i: (i, h)),
                  pl.BlockSpec((None, 1, tq), lambda h, j, i: (h, 0, i)),
                  pl.BlockSpec((None, 1, tq), lambda h, j, i: (h, 0, i))],
        out_specs=[pl.BlockSpec((None, nq, 256, tq), lambda h, j, i: (h, 0, 0, 0)),
                   pl.BlockSpec((tk, 256), lambda h, j, i: (j, h)),
                   pl.BlockSpec((tk, LANES), lambda h, j, i: (j, h))],
        out_shape=[jax.ShapeDtypeStruct((MLA_HEADS, nq, 256, tq), BF16),
                   jax.ShapeDtypeStruct((s, 1024), BF16), jax.ShapeDtypeStruct((s, 512), BF16)],
        scratch_shapes=[pltpu.VMEM((nq, 256, tq), F32), pltpu.VMEM((tk, 256), F32), pltpu.VMEM((tk, LANES), F32)],
        args=[q, k, kt, vx, do, lse, delta], sem=("parallel", "arbitrary", "arbitrary"), comm=comm)
    return outs[0], outs[1], outs[2], got


def _fill_padded(dst_ref, val, s):
    zeros = jnp.zeros((CONV_HALO, LANES), F32)
    dst_ref[pl.ds(0, CONV_HALO), :] = zeros
    dst_ref[pl.ds(CONV_HALO + s, CONV_HALO), :] = zeros
    dst_ref[pl.ds(CONV_HALO, s), :] = val


def _shifted_windows(win):
    n = win.shape[0]
    return [win] + [pltpu.roll(win, n - b, 0) for b in range(1, 8)]


def _conv_fwd(name, z, w, bias, rc=256):
    s = z.shape[0]
    rc = min(rc, s)

    def kern(a_ref, g_ref, w_ref, b_ref, o_ref, pad_ref):
        _fill_padded(pad_ref, _glu_fn(a_ref[...].astype(F32), g_ref[...].astype(F32)), s)
        wv = w_ref[...]
        bv = b_ref[...]

        def chunk(r, carry):
            base = pl.multiple_of(r * rc, rc)
            wins = _shifted_windows(pad_ref[pl.ds(base, rc + 2 * CONV_HALO), :])
            acc = jnp.broadcast_to(bv, (rc, LANES))
            for kk in range(CONV_K):
                a, b = divmod(kk + 1, 8)
                acc = acc + wv[kk:kk + 1, :] * wins[b][8 * a:8 * a + rc]
            o_ref[pl.ds(base, rc), :] = acc.astype(BF16)
            return carry

        lax.fori_loop(0, s // rc, chunk, 0)

    nblk = D // LANES
    return pl.pallas_call(
        kern, name=name, grid=(nblk,),
        in_specs=[pl.BlockSpec((s, LANES), lambda c: (0, c)), pl.BlockSpec((s, LANES), lambda c: (0, nblk + c)),
                  pl.BlockSpec((32, LANES), lambda c: (0, c)), pl.BlockSpec((1, LANES), lambda c: (0, c))],
        out_specs=pl.BlockSpec((s, LANES), lambda c: (0, c)),
        out_shape=jax.ShapeDtypeStruct((s, D), BF16),
        scratch_shapes=[pltpu.VMEM((s + 2 * CONV_HALO, LANES), F32)],
        compiler_params=_cparams(("parallel",)),
    )(z, z, w, bias)


def _conv_bwd(name, z, g, w, rc=256, comm=None):
    s = z.shape[0]
    rc = min(rc, s)

    def kern(a_ref, b_ref, g_ref, w_ref, da_ref, db_ref, dw_ref, dbias_ref, sa_ref, sb_ref, upad_ref, gpad_ref,
             dwacc_ref):
        _fill_padded(upad_ref, _glu_fn(a_ref[...].astype(F32), b_ref[...].astype(F32)), s)
        _fill_padded(gpad_ref, g_ref[...].astype(F32), s)
        dwacc_ref[...] = jnp.zeros(dwacc_ref.shape, F32)
        wv = w_ref[...]

        def chunk(r, carry):
            sum_a, sum_b = carry
            base = pl.multiple_of(r * rc, rc)
            gwins = _shifted_windows(gpad_ref[pl.ds(base, rc + 2 * CONV_HALO), :])
            uwins = _shifted_windows(upad_ref[pl.ds(base, rc + 2 * CONV_HALO), :])
            gc = g_ref[pl.ds(base, rc), :].astype(F32)
            acc = jnp.zeros((rc, LANES), F32)
            for kk in range(CONV_K):
                a, b = divmod(CONV_K - kk, 8)
                acc = acc + wv[kk:kk + 1, :] * gwins[b][8 * a:8 * a + rc]
                a, b = divmod(kk + 1, 8)
                prod = gc * uwins[b][8 * a:8 * a + rc]
                dwacc_ref[kk] += jnp.sum(prod.reshape(rc // 8, 8, LANES), axis=0)
            dwacc_ref[CONV_K] += jnp.sum(gc.reshape(rc // 8, 8, LANES), axis=0)
            av = a_ref[pl.ds(base, rc), :].astype(F32)
            sg = _sigmoid(b_ref[pl.ds(base, rc), :].astype(F32))
            d_a = acc * sg
            d_b = acc * av * sg * (1.0 - sg)
            da_ref[pl.ds(base, rc), :] = d_a.astype(BF16)
            db_ref[pl.ds(base, rc), :] = d_b.astype(BF16)
            return (sum_a + jnp.sum(d_a.reshape(rc // 8, 8, LANES), axis=0),
                    sum_b + jnp.sum(d_b.reshape(rc // 8, 8, LANES), axis=0))

        zero = jnp.zeros((8, LANES), F32)
        sum_a, sum_b = lax.fori_loop(0, s // rc, chunk, (zero, zero))
        sa_ref[...] = jnp.sum(sum_a, axis=0, keepdims=True)
        sb_ref[...] = jnp.sum(sum_b, axis=0, keepdims=True)
        tot = jnp.sum(dwacc_ref[...], axis=1)
        lane_row = lax.broadcasted_iota(jnp.int32, (32, LANES), 0)
        dw_ref[...] = jnp.where(lane_row < CONV_K, tot, 0.0)
        dbias_ref[...] = tot[CONV_K:CONV_K + 1, :]

    nblk = D // LANES
    cs = pl.BlockSpec((s, LANES), lambda c: (0, c))
    vec = pl.BlockSpec((1, LANES), lambda c: (0, c))
    outs, got = _call(
        kern, name=name, grid=(nblk,),
        in_specs=[cs, pl.BlockSpec((s, LANES), lambda c: (0, nblk + c)), cs, pl.BlockSpec((32, LANES), lambda c: (0, c))],
        out_specs=[cs, cs, pl.BlockSpec((32, LANES), lambda c: (0, c)), vec, vec, vec],
        out_shape=[jax.ShapeDtypeStruct((s, D), BF16), jax.ShapeDtypeStruct((s, D), BF16),
                   jax.ShapeDtypeStruct((32, D), F32)] + [jax.ShapeDtypeStruct((1, D), F32)] * 3,
        scratch_shapes=[pltpu.VMEM((s + 2 * CONV_HALO, LANES), F32), pltpu.VMEM((s + 2 * CONV_HALO, LANES), F32),
                        pltpu.VMEM((32, 8, LANES), F32)],
        args=[z, z, g, w], sem=("parallel",), comm=comm)
    return outs, got


def _mod_local(name, c_all, ada_w):
    def kern(c_ref, w_ref, o_ref):
        o_ref[...] = jnp.dot(_silu(c_ref[...]), w_ref[...], preferred_element_type=F32,
                             precision=lax.Precision.HIGHEST)

    return pl.pallas_call(
        kern, name=name, grid=(DEPTH,),
        in_specs=[pl.BlockSpec((N_DEV, D), lambda l: (0, 0)), pl.BlockSpec((None, D, 384), lambda l: (l, 0, 0))],
        out_specs=pl.BlockSpec((None, N_DEV, 384), lambda l: (l, 0, 0)),
        out_shape=jax.ShapeDtypeStruct((DEPTH, N_DEV, 384), F32),
        compiler_params=_cparams(("parallel",)),
    )(c_all, ada_w)


def _ada_w_grad(name, c_all_t, dmod):
    def kern(c_ref, d_ref, o_ref):
        o_ref[...] = jnp.dot(_silu(c_ref[...]), d_ref[...], preferred_element_type=F32,
                             precision=lax.Precision.HIGHEST)

    return pl.pallas_call(
        kern, name=name, grid=(DEPTH,),
        in_specs=[pl.BlockSpec((D, LANES), lambda l: (0, 0)), pl.BlockSpec((None, LANES, 384), lambda l: (l, 0, 0))],
        out_specs=pl.BlockSpec((None, D, 384), lambda l: (l, 0, 0)),
        out_shape=jax.ShapeDtypeStruct((DEPTH, D, 384), F32),
        compiler_params=_cparams(("parallel",)),
    )(c_all_t, dmod)


def _pre_fn(x, g, scale, shift):
    return _rms(x, g) * (1.0 + scale) + shift


def _post_fn(y, g, gate):
    return gate * _rms(y, g)


def _ev_post_fn(o_heads, rg, a, mg):
    normed = []
    for oh in o_heads:
        mu = jnp.mean(oh, axis=-1, keepdims=True)
        var = jnp.mean(jnp.square(oh - mu), axis=-1, keepdims=True)
        normed.append((oh - mu) * lax.rsqrt(var + EPS))
    return jnp.concatenate([jnp.concatenate(normed, axis=1) * _silu(rg), a * _silu(mg)], axis=1)


def _od_post_fn(u, g, ln_g, ln_b):
    mu = jnp.mean(u, axis=-1, keepdims=True)
    var = jnp.mean(jnp.square(u - mu), axis=-1, keepdims=True)
    y = (u - mu) * lax.rsqrt(var + EPS) * ln_g + ln_b
    return _silu(y) * _silu(g)


def _rms_bwd(x, dxh_of):
    r = lax.rsqrt(jnp.mean(x * x, axis=-1, keepdims=True) + EPS)
    xh = x * r
    dxh = dxh_of(xh)
    return xh, r * (dxh - xh * jnp.mean(dxh * xh, axis=-1, keepdims=True))


def _pre_bwd(x, dh, g, scale):
    mod = 1.0 + scale
    xh, dx = _rms_bwd(x, lambda xh_: dh * (g * mod))
    t = dh * xh
    return dx, _colsum(t) * mod, _colsum(t) * g, _colsum(dh)


def _post_bwd(y, dout, g, gate):
    yh, dy = _rms_bwd(y, lambda yh_: dout * (gate * g))
    t = _colsum(dout * yh)
    return dy, t * gate, t * g


def _dsilu(x, s):
    return s * (1.0 + x * (1.0 - s))


def _ev_post_bwd(o_heads, rg, a, mg, da):
    d_ret, d_mla = da[:, :512], da[:, 512:]
    s_rg, s_mg = _sigmoid(rg), _sigmoid(mg)
    d_on = _heads(d_ret * (rg * s_rg))
    normed, do_heads = [], []
    for oh, dn in zip(o_heads, d_on):
        xc = oh - jnp.mean(oh, axis=-1, keepdims=True)
        r = lax.rsqrt(jnp.mean(xc * xc, axis=-1, keepdims=True) + EPS)
        xh = xc * r
        normed.append(xh)
        m1 = jnp.mean(dn, axis=-1, keepdims=True)
        m2 = jnp.mean(dn * xh, axis=-1, keepdims=True)
        do_heads.append(r * (dn - m1 - xh * m2))
    drg = d_ret * jnp.concatenate(normed, axis=1) * _dsilu(rg, s_rg)
    return do_heads, drg, d_mla * (mg * s_mg), d_mla * a * _dsilu(mg, s_mg)


def _od_post_bwd(u, g, da, ln_g, ln_b):
    xc = u - jnp.mean(u, axis=-1, keepdims=True)
    r = lax.rsqrt(jnp.mean(xc * xc, axis=-1, keepdims=True) + EPS)
    xh = xc * r
    y = xh * ln_g + ln_b
    s1, s2 = _sigmoid(y), _sigmoid(g)
    dg = da * (y * s1) * (s2 * (1.0 + g * (1.0 - s2)))
    dy = da * (g * s2) * (s1 * (1.0 + y * (1.0 - s1)))
    dxh = dy * ln_g
    m1 = jnp.mean(dxh, axis=-1, keepdims=True)
    m2 = jnp.mean(dxh * xh, axis=-1, keepdims=True)
    return r * (dxh - m1 - xh * m2), dg, _colsum(dy * xh), _colsum(dy)


def _glu_fn(a, b):
    return a * _sigmoid(b)


def _heads(x):
    return [x[:, LANES * h:LANES * (h + 1)] for h in range(4)]


def _colsum(x):
    return jnp.sum(x, axis=0, keepdims=True)


def _zrows(n, c):
    return jnp.zeros((n, c), BF16)


def _ev_win_layout(wt):
    rq = [p for h in range(4) for p in (wt[64 * h:64 * h + 64], _zrows(64, D))]
    rk = [p for h in range(4) for p in (wt[256 + 64 * h:256 + 64 * h + 64], _zrows(64, D))]
    return jnp.concatenate([wt[512:1024], wt[1024:1536], wt[2240:2752], wt[1536:1920],
                            wt[2176:2240], _zrows(64, D)] + rq + rk + [wt[1920:2176]], axis=0)


def _uq_layout(wt):
    return jnp.concatenate([p for h in range(4) for p in (wt[192 * h:192 * h + 192], _zrows(64, 384))], axis=0)


def _uq_unlayout(g):
    return jnp.concatenate([g[256 * h:256 * h + 192] for h in range(4)], axis=0)


def _ukv_layout(wt):
    kpart = [p for h in range(4) for p in (wt[256 * h:256 * h + 128], _zrows(128, 256))]
    vpart = [wt[256 * h + 128:256 * h + 256] for h in range(4)]
    return jnp.concatenate(kpart + vpart, axis=0)


def _ukv_unlayout(g):
    return jnp.concatenate([p for h in range(4) for p in (g[256 * h:256 * h + 128], g[1024 + 128 * h:1024 + 128 * h + 128])],
                           axis=0)


def kernel(x, c, positions, ada_w, ada_b, pre_g, post_g, ev_w_in, ev_dec_f, ev_dec_b, ev_q_norm_g, ev_w_uq, ev_kv_norm_g, ev_w_ukv, ev_w_out, od_w_in, od_b_in, od_dw_w, od_dw_b, od_ln_g, od_ln_b, od_w_out, loss_target, m_ada_w, m_ada_b, m_pre_g, m_post_g, m_ev_w_in, m_ev_dec_f, m_ev_dec_b, m_ev_q_norm_g, m_ev_w_uq, m_ev_kv_norm_g, m_ev_w_ukv, m_ev_w_out, m_od_w_in, m_od_b_in, m_od_dw_w, m_od_dw_b, m_od_ln_g, m_od_ln_b, m_od_w_out, v_ada_w, v_ada_b, v_pre_g, v_post_g, v_ev_w_in, v_ev_dec_f, v_ev_dec_b, v_ev_q_norm_g, v_ev_w_uq, v_ev_kv_norm_g, v_ev_w_ukv, v_ev_w_out, v_od_w_in, v_od_b_in, v_od_dw_w, v_od_dw_b, v_od_ln_g, v_od_ln_b, v_od_w_out):
    s = x.shape[1]
    me = 4 * lax.axis_index("x") + 2 * lax.axis_index("y") + lax.axis_index("c")
    x0 = x.reshape(s, D)
    tgt = loss_target.reshape(s, D)
    ret_chunk = 256
    fa_cfg_f = ((min(4096, s // 2), min(2048, s // 2), min(16, s // 512)),) * 2
    fa_cfg_b = ((min(2048, s // 2), min(4096, s // 2), min(16, s // 512)),) * 2

    start_parts = [c.reshape(-1), od_b_in.reshape(-1), od_dw_w.reshape(-1), od_dw_b.reshape(-1),
                   od_ln_g.reshape(-1), od_ln_b.reshape(-1)]
    start_sizes = [p.shape[0] for p in start_parts]
    start_len = -(-sum(start_sizes) // 1024) * 1024
    start_vec = jnp.concatenate(start_parts + [jnp.zeros((start_len - sum(start_sizes),), F32)])
    start_all, win0_all = _exchange("gather_start", [start_vec.reshape(-1, LANES), ev_w_in[0].T.astype(BF16)], False)
    start_all = start_all.reshape(N_DEV, start_len)
    offs = np.cumsum([0] + start_sizes)
    c_all = start_all[:, offs[0]:offs[1]]
    b_in_all = start_all[:, offs[1]:offs[2]].reshape(N_DEV, 2, 384).transpose(1, 0, 2).reshape(2, 1, ZW_OD)
    dw_w_all = start_all[:, offs[2]:offs[3]].reshape(N_DEV, 2, CONV_K, LANES).transpose(1, 2, 0, 3).reshape(2, CONV_K, D)
    dw_w_all = jnp.concatenate([dw_w_all, jnp.zeros((2, 1, D), F32)], axis=1)
    dw_b_all = start_all[:, offs[3]:offs[4]].reshape(N_DEV, 2, LANES).transpose(1, 0, 2).reshape(2, 1, D)
    ln_g_all = start_all[:, offs[4]:offs[5]].reshape(N_DEV, 2, LANES).transpose(1, 0, 2).reshape(2, 1, D)
    ln_b_all = start_all[:, offs[5]:offs[6]].reshape(N_DEV, 2, LANES).transpose(1, 0, 2).reshape(2, 1, D)

    mod_loc = _mod_local("mod_local", c_all, ada_w)
    mod_all = _exchange("gather_mod", [mod_loc.reshape(DEPTH * N_DEV, 384)], False)[0].reshape(N_DEV, DEPTH, N_DEV, 384)
    mod = lax.dynamic_index_in_dim(mod_all, me, axis=2, keepdims=False)
    mod = mod.transpose(1, 0, 2).reshape(DEPTH, 3 * D) + ada_b
    shift = [mod[l:l + 1, 0:D] for l in range(DEPTH)]
    scale = [mod[l:l + 1, D:2 * D] for l in range(DEPTH)]
    gate = [mod[l:l + 1, 2 * D:3 * D] for l in range(DEPTH)]

    def ev_shards(i):
        return [ev_w_in[i].T.astype(BF16), ev_w_uq[i].T.astype(BF16), ev_w_ukv[i].T.astype(BF16), ev_w_out[i].astype(BF16)]

    def od_shards(i):
        return [od_w_in[i].T.astype(BF16), od_w_out[i].astype(BF16)]

    def full(got):
        return [g.reshape(N_DEV * g.shape[1], g.shape[2]) for g in got]

    def ev_full(got):
        win_t, uq_t, ukv_t, wout = full(got)
        return (_ev_win_layout(win_t), _uq_layout(uq_t), _ukv_layout(ukv_t), wout)

    win0_t = _ev_win_layout(full([win0_all])[0])
    ev_w = [None, None]
    od_w = [None, None]
    later_w = [(t, False) for t in od_shards(0) + ev_shards(1) + od_shards(1)]

    inv_freq = ROPE_BASE ** (-jnp.arange(0, 64, 2, dtype=F32) / 64)
    invf = jnp.tile(inv_freq, 4).reshape(1, LANES)
    sgn = jnp.tile(jnp.concatenate([-jnp.ones((32,), F32), jnp.ones((32,), F32)]), 2).reshape(1, LANES)

    def rope_body(rows, vecs):
        ang = rows[0].astype(F32) * vecs[0]
        return [jnp.cos(ang), jnp.sin(ang) * vecs[1]], []

    (cos_t, sin_t), _ = _rowwise("rope_tables", rope_body, [(positions.reshape(s, 1), 1, 0)], [invf, sgn],
                                 [(LANES, F32), (LANES, F32)])

    saved = []
    xl = x0
    for l in range(DEPTH):
        i = l // 2
        sv = dict(x=xl)

        if l == 0:
            def pre_body(rows, vecs):
                return [_pre_fn(rows[0], *vecs)], []

            (h,), _ = _rowwise("pre0", pre_body, [(xl, D, 0)], [pre_g[0:1], scale[0], shift[0]], [(D, BF16)])
        sv["h"] = h
        if l % 2 == 0:
            if l == 0:
                z, got = _mm("ev_in0", h, win0_t, "nt", out_dtype=BF16, comm=[(t, False) for t in ev_shards(0)[1:]])
                uq0_t, ukv0_t, wout0 = full(got)
                ev_w[0] = (win0_t, _uq_layout(uq0_t), _ukv_layout(ukv0_t), wout0)
                win_t, uq_t, ukv_t, wout = ev_w[0]
            else:
                win_t, uq_t, ukv_t, wout = ev_w[i]
                z = _mm(f"ev_in{l}", h, win_t, "nt", out_dtype=BF16)
            sv["z"] = z
            dec_f, dec_b = ev_dec_f[i:i + 1], ev_dec_b[i:i + 1]

            def prep_body(rows, vecs):
                rq, rk, cq, ckv, kr, cos, sin = rows
                return [_rope(rq, cos, sin), _rope(rk, cos, sin) * RET_SCALE, _rms(cq, vecs[0]), _rms(ckv, vecs[1]),
                        _rope(kr, cos, sin)], []

            (rq_r, rk_r, qn, kvn, krr), _ = _rowwise(
                f"ev_prep{l}", prep_body,
                [(z, 512, 4), (z, 512, 5), (z, 384, 4), (z, 256, 12), (z, LANES, 15), (cos_t, LANES, 0), (sin_t, LANES, 0)],
                [ev_q_norm_g[i:i + 1], ev_kv_norm_g[i:i + 1]],
                [(512, BF16), (512, BF16), (384, BF16), (256, BF16), (LANES, BF16)])
            sv.update(rq_r=rq_r, rk_r=rk_r, qn=qn, kvn=kvn)
            o_f, st_f = _ret_fwd(f"ret_f{l}", rq_r, rk_r, z, dec_f, False, ret_chunk)
            o_b, st_b = _ret_fwd(f"ret_b{l}", rq_r, rk_r, z, dec_b, True, ret_chunk)
            sv.update(o_f=o_f, o_b=o_b, st_f=st_f, st_b=st_b)
            qcat, kcat, v_x, kcat_t = _mla_proj(f"mla_proj{l}", qn, uq_t, kvn, ukv_t, krr, cos_t, sin_t)
            sv["kcat_t"] = kcat_t
            a_mla, lse, got = _fa_fwd(f"fa_fwd{l}", qcat, kcat, v_x, *fa_cfg_f[i], comm=later_w if l == 0 else None)
            if l == 0:
                od_w[0], ev_w[1], od_w[1] = tuple(full(got[0:2])), ev_full(got[2:6]), tuple(full(got[6:8]))
            sv.update(qcat=qcat, kcat=kcat, v_x=v_x, a_mla=a_mla, lse=lse)

            def ev_post_body(rows, vecs):
                of, ob, rg, a, mg = rows
                return [_ev_post_fn(_heads(of + ob), rg, a, mg)], []

            (act,), _ = _rowwise(f"ev_post{l}", ev_post_body,
                                 [(o_f, 512, 0), (o_b, 512, 0), (z, 512, 1), (a_mla, 512, 0), (z, 512, 2)], [], [(D, BF16)])
        else:
            win_t, wout = od_w[i]
            z = _mm(f"od_in{l}", h, win_t, "nt", out_dtype=BF16, bias=b_in_all[i])
            sv["z"] = z

            u2 = _conv_fwd(f"conv{l}", z, dw_w_all[i], dw_b_all[i])
            sv.update(u2=u2)

            def od_post_body(rows, vecs):
                return [_od_post_fn(rows[0], rows[1], vecs[0], vecs[1])], []

            (act,), _ = _rowwise(f"od_post{l}", od_post_body, [(u2, D, 0), (z, D, 2)], [ln_g_all[i], ln_b_all[i]],
                                 [(D, BF16)])
        sv["act"] = act
        y = _mm(f"out{l}", act, wout, "nn", out_dtype=BF16)
        sv["y"] = y

        saved.append(sv)
        if l < DEPTH - 1:
            def post_body(rows, vecs):
                xn = rows[0] + _post_fn(rows[1], vecs[0], vecs[1])
                return [xn, _pre_fn(xn, vecs[2], vecs[3], vecs[4])], []

            (xl, h), _ = _rowwise(f"post{l}", post_body, [(xl, D, 0), (y, D, 0)],
                                  [post_g[l:l + 1], gate[l], pre_g[l + 1:l + 2], scale[l + 1], shift[l + 1]],
                                  [(D, F32), (D, BF16)])
        else:
            def post_body(rows, vecs):
                diff = rows[0] + _post_fn(rows[1], vecs[0], vecs[1]) - rows[2]
                return [diff * (1.0 / D)], [_colsum(diff * diff) * (0.5 / D)]

            (dx,), (loss_lanes,) = _rowwise(f"post{l}", post_body, [(xl, D, 0), (y, D, 0), (tgt, D, 0)],
                                            [post_g[l:l + 1], gate[l]], [(D, F32)], [(1, D)])
    loss = lax.psum(jnp.sum(loss_lanes), ("x", "y", "c"))

    g_pre, g_post, g_mod = [None] * DEPTH, [None] * DEPTH, [None] * DEPTH
    g_dec_f, g_dec_b, g_qn, g_kvn = [None] * 2, [None] * 2, [None] * 2, [None] * 2
    g_b_in, g_dw_w, g_dw_b, g_ln_g, g_ln_b = [None] * 2, [None] * 2, [None] * 2, [None] * 2, [None] * 2
    recv_ev, recv_od = [None] * 2, [None] * 2
    pending = []
    for l in reversed(range(DEPTH)):
        i = l // 2
        sv = saved[l]

        if l == DEPTH - 1:
            def post_bwd_body(rows, vecs):
                yv, dxn = rows
                d_y, dg, d_gate = _post_bwd(yv, dxn, vecs[0], vecs[1])
                return [d_y], [dg, d_gate]

            (dy,), (dpost, dgate) = _rowwise(f"post_bwd{l}", post_bwd_body, [(sv["y"], D, 0), (dx, D, 0)],
                                             [post_g[l:l + 1], gate[l]], [(D, BF16)], [(1, D), (1, D)])
        g_post[l] = dpost
        dgate_l = dgate
        wout = ev_w[i][3] if l % 2 == 0 else od_w[i][1]
        dact = _mm(f"out_dgrad{l}", dy, wout, "nt", out_dtype=BF16)
        d_wout = _mm(f"out_wgrad{l}", sv["act"], dy, "tn", out_dtype=BF16)
        z = sv["z"]
        if l % 2 == 0:
            win_t, uq_t, ukv_t, _ = ev_w[i]
            dec_f, dec_b = ev_dec_f[i:i + 1], ev_dec_b[i:i + 1]

            def ev_post_bwd_body(rows, vecs):
                of, ob, rg, a, mg, da = rows
                do_heads, drg, d_a, dmg = _ev_post_bwd(_heads(of + ob), rg, a, mg, da)
                deltas = [jnp.sum(dh_ * ah_, axis=1, keepdims=True) for dh_, ah_ in zip(_heads(d_a), _heads(a))]
                return [jnp.concatenate(do_heads, axis=1), drg, d_a, dmg] + deltas, []

            (do_ret, drg, do_mla, dmg, dl0, dl1, dl2, dl3), _ = _rowwise(
                f"ev_post_bwd{l}", ev_post_bwd_body,
                [(sv["o_f"], 512, 0), (sv["o_b"], 512, 0), (z, 512, 1), (sv["a_mla"], 512, 0), (z, 512, 2), (dact, D, 0)],
                [], [(512, BF16), (512, BF16), (512, BF16), (512, BF16)] + [(1, F32)] * 4)
            delta = jnp.stack([dl0, dl1, dl2, dl3]).reshape(MLA_HEADS, 1, s)
            lse = sv["lse"].reshape(MLA_HEADS, 1, s)
            kt = sv["kcat_t"]
            if l == 0:
                pending = pending + [(d_wout.reshape(N_DEV, 128, D), True)]
            dqt, dkcat, dv, got = _fa_bwd(f"fa_bwd{l}", sv["qcat"], sv["kcat"], kt, sv["v_x"], do_mla, lse, delta,
                                          *fa_cfg_b[i], comm=pending)
            recv_od[i] = got[0:2]
            wout_recv = got[2:]
            dqcat = dqt.transpose(1, 3, 0, 2).reshape(s, 1024)

            def mla_prep_bwd_body(rows, vecs):
                dq, dk, dvv, cos, sin = rows
                qs = []
                dkrr = jnp.zeros((dq.shape[0], LANES), F32)
                for hh in range(4):
                    qs += [dq[:, 256 * hh:256 * hh + 128], _rope_t(dq[:, 256 * hh + 128:256 * hh + 256], cos, sin)]
                    dkrr = dkrr + dk[:, 256 * hh + 128:256 * hh + 256]
                return [jnp.concatenate(qs, axis=1) * MLA_SCALE, jnp.concatenate([dk, dvv], axis=1), dkrr], []

            (dq_pad, dkv_pad, dkrr), _ = _rowwise(
                f"mla_prep_bwd{l}", mla_prep_bwd_body,
                [(dqcat, 1024, 0), (dkcat, 1024, 0), (dv, 512, 0), (cos_t, LANES, 0), (sin_t, LANES, 0)], [],
                [(1024, BF16), (1536, BF16), (LANES, F32)])
            dqn = _mm(f"uq_dgrad{l}", dq_pad, uq_t, "nn")
            d_uq = _mm(f"uq_wgrad{l}", dq_pad, sv["qn"], "tn", out_dtype=BF16)
            dkvn = _mm(f"ukv_dgrad{l}", dkv_pad, ukv_t, "nn")
            d_ukv = _mm(f"ukv_wgrad{l}", dkv_pad, sv["kvn"], "tn", out_dtype=BF16)
            dq_f, dk_f, dv_f, ddec_f = _ret_bwd(f"ret_f_bwd{l}", sv["rq_r"], sv["rk_r"], z, dec_f, sv["st_f"], do_ret,
                                                False, ret_chunk)
            dq_b, dk_b, dv_b, ddec_b = _ret_bwd(f"ret_b_bwd{l}", sv["rq_r"], sv["rk_r"], z, dec_b, sv["st_b"], do_ret,
                                                True, ret_chunk)
            g_dec_f[i], g_dec_b[i] = ddec_f[:, :RET_HEADS], ddec_b[:, :RET_HEADS]

            def prep_bwd_body(rows, vecs):
                cq, ckv, cos, sin, dqf, dqb, dkf, dkb, dvf, dvb, d_qn, d_kvn, d_krr = rows
                _, vjp_q = jax.vjp(_rms, cq, vecs[0])
                dcq, dgq = vjp_q(d_qn)
                _, vjp_kv = jax.vjp(_rms, ckv, vecs[1])
                dckv, dgkv = vjp_kv(d_kvn)
                return [dvf + dvb, dcq, _rope_t(d_krr, cos, sin), _rope_t(dqf + dqb, cos, sin),
                        _rope_t(dkf + dkb, cos, sin) * RET_SCALE, dckv], [dgq, dgkv]

            (drv, dcq, dkr, drq, drk, dckv), (dgq, dgkv) = _rowwise(
                f"ev_prep_bwd{l}", prep_bwd_body,
                [(z, 384, 4), (z, 256, 12), (cos_t, LANES, 0), (sin_t, LANES, 0), (dq_f, 512, 0), (dq_b, 512, 0),
                 (dk_f, 512, 0), (dk_b, 512, 0), (dv_f, 512, 0), (dv_b, 512, 0), (dqn, 384, 0), (dkvn, 256, 0),
                 (dkrr, LANES, 0)],
                [ev_q_norm_g[i:i + 1], ev_kv_norm_g[i:i + 1]],
                [(512, BF16), (384, BF16), (LANES, BF16), (512, BF16), (512, BF16), (256, BF16)], [(1, 384), (1, 256)])
            g_qn[i], g_kvn[i] = dgq, dgkv
            dz = [drv, drg, dmg, dcq, dkr, drq, drk, dckv]
            dz_off = [ZL_EV[nm][0] for nm in ("rv", "rg", "mg", "cq", "kr", "rq", "rk", "ckv")]
            g_rv, g_rg, g_mg, g_cq, g_kr, g_rq, g_rk, g_ckv = _mm_cols_tn(f"in_wgrad{l}", dz, sv["h"])
            d_win = jnp.concatenate([g_rq[128 * hh:128 * hh + 64] for hh in range(4)]
                                    + [g_rk[128 * hh:128 * hh + 64] for hh in range(4)]
                                    + [g_rv, g_rg, g_cq, g_ckv, g_kr[:64], g_mg], axis=0)
            pending = [(d_win.reshape(N_DEV, 344, D), True), (_uq_unlayout(d_uq).reshape(N_DEV, 96, 384), True),
                       (_ukv_unlayout(d_ukv).reshape(N_DEV, 128, 256), True)]
            if l > 0:
                pending.append((d_wout.reshape(N_DEV, 128, D), True))
        else:
            win_t, _ = od_w[i]

            def od_post_bwd_body(rows, vecs):
                u2, gg, da = rows
                du2, dgg, dlg, dlb = _od_post_bwd(u2, gg, da, vecs[0], vecs[1])
                return [du2, dgg], [dlg, dlb, _colsum(dgg)]

            (du2, dg_gate), (dlg, dlb, dbg) = _rowwise(
                f"od_post_bwd{l}", od_post_bwd_body, [(sv["u2"], D, 0), (z, D, 2), (dact, D, 0)],
                [ln_g_all[i], ln_b_all[i]], [(D, BF16), (D, BF16)], [(1, D), (1, D), (1, D)])
            g_ln_g[i], g_ln_b[i] = dlg, dlb
            (d_a, d_b, d_dw, d_dwb, dba, dbb), got = _conv_bwd(f"conv_bwd{l}", z, du2, dw_w_all[i], comm=pending)
            if pending:
                recv_ev[i + 1] = got
            g_dw_w[i], g_dw_b[i] = d_dw[:CONV_K], d_dwb
            g_b_in[i] = jnp.concatenate([dba, dbb, dbg], axis=1)
            dz, dz_off = [d_a, d_b, dg_gate], [0, D, 2 * D]
            d_win = jnp.concatenate(_mm_cols_tn(f"in_wgrad{l}", dz, sv["h"]), axis=0)
            pending = [(d_win.reshape(N_DEV, 384, D), True), (d_wout.reshape(N_DEV, 128, D), True)]
        if l == 0:
            dh, got = _mm_cols_nn(f"in_dgrad{l}", dz, dz_off, win_t, comm=pending)
            recv_ev[0] = list(got) + list(wout_recv)
        else:
            dh = _mm_cols_nn(f"in_dgrad{l}", dz, dz_off, win_t)

        if l > 0:
            def pre_bwd_body(rows, vecs):
                xv, d_h, dxn, yv = rows
                d_x, dg, dsc, dsh = _pre_bwd(xv, d_h, vecs[0], vecs[1])
                d_x = d_x + dxn
                d_y, dgp, d_gate = _post_bwd(yv, d_x, vecs[3], vecs[4])
                return [d_x, d_y], [dg, dsc, dsh, dgp, d_gate]

            (dx, dy), (dpre, dscale, dshift, dpost, dgate) = _rowwise(
                f"pre_bwd{l}", pre_bwd_body, [(sv["x"], D, 0), (dh, D, 0), (dx, D, 0), (saved[l - 1]["y"], D, 0)],
                [pre_g[l:l + 1], scale[l], shift[l], post_g[l - 1:l], gate[l - 1]], [(D, F32), (D, BF16)], [(1, D)] * 5)
        else:
            def pre_bwd_body(rows, vecs):
                xv, d_h, dxn = rows
                d_x, dg, dsc, dsh = _pre_bwd(xv, d_h, vecs[0], vecs[1])
                return [d_x + dxn], [dg, dsc, dsh]

            (dx,), (dpre, dscale, dshift) = _rowwise(f"pre_bwd{l}", pre_bwd_body, [(sv["x"], D, 0), (dh, D, 0), (dx, D, 0)],
                                                     [pre_g[l:l + 1], scale[l], shift[l]], [(D, F32)], [(1, D)] * 3)
        g_pre[l] = dpre
        g_mod[l] = jnp.concatenate([dshift, dscale, dgate_l], axis=1)

    grad_x = dx.reshape(1, s, D)

    end_parts = [jnp.concatenate(g_mod, axis=0), jnp.concatenate(g_pre, axis=0), jnp.concatenate(g_post, axis=0),
                 jnp.concatenate(g_dec_f, axis=0), jnp.concatenate(g_dec_b, axis=0), jnp.concatenate(g_qn, axis=0),
                 jnp.concatenate(g_kvn, axis=0), jnp.concatenate(g_b_in, axis=0), jnp.stack(g_dw_w),
                 jnp.concatenate(g_dw_b, axis=0), jnp.concatenate(g_ln_g, axis=0), jnp.concatenate(g_ln_b, axis=0)]
    end_shapes = [p.shape for p in end_parts]
    end_sizes = [int(np.prod(sh)) for sh in end_shapes]
    end_len = -(-sum(end_sizes) // 1024) * 1024
    end_vec = jnp.concatenate([p.reshape(-1) for p in end_parts] + [jnp.zeros((end_len - sum(end_sizes),), F32)])
    end_all = _exchange("gather_end", [end_vec.reshape(-1, LANES)], False)[0].reshape(N_DEV, end_len)
    eo = np.cumsum([0] + end_sizes)
    ends = [end_all[:, eo[j]:eo[j + 1]].reshape((N_DEV,) + tuple(end_shapes[j])) for j in range(len(end_parts))]
    (dmod_all, pre_all, post_all, decf_all, decb_all, qn_all, kvn_all, bin_all, dww_all, dwb_all, lng_all,
     lnb_all) = ends

    def pack_rep(*ts):
        lead = ts[0].ndim - 2
        return jnp.concatenate([t.reshape(t.shape[:lead] + (-1,)) for t in ts], axis=-1)

    rep_sizes = [DEPTH * 3 * D, DEPTH * D, DEPTH * D, 8, 8, 2 * 384, 2 * 256]
    rep_len = -(-sum(rep_sizes) // 1024) * 1024
    rep_pad = rep_len - sum(rep_sizes)

    def rep_rows(flat):
        padz = jnp.zeros(flat.shape[:-1] + (rep_pad,), F32)
        return jnp.concatenate([flat, padz], axis=-1).reshape(flat.shape[:-1] + (rep_len // LANES, LANES))

    rep_w = rep_rows(pack_rep(ada_b, pre_g, post_g, ev_dec_f, ev_dec_b, ev_q_norm_g, ev_kv_norm_g))
    rep_m = rep_rows(pack_rep(m_ada_b, m_pre_g, m_post_g, m_ev_dec_f, m_ev_dec_b, m_ev_q_norm_g, m_ev_kv_norm_g))
    rep_v = rep_rows(pack_rep(v_ada_b, v_pre_g, v_post_g, v_ev_dec_f, v_ev_dec_b, v_ev_q_norm_g, v_ev_kv_norm_g))
    rep_g = rep_rows(pack_rep(dmod_all, pre_all, post_all, decf_all, decb_all, qn_all, kvn_all))
    rep_out = _adamw("adamw_rep", rep_w, rep_m, rep_v, rep_g)
    ro = np.cumsum([0] + rep_sizes)
    rep_shapes = [(DEPTH, 3 * D), (DEPTH, D), (DEPTH, D), (2, 4), (2, 4), (2, 384), (2, 256)]

    def unpack_rep(t):
        flat = t.reshape(-1)
        return [flat[ro[j]:ro[j + 1]].reshape(rep_shapes[j]) for j in range(len(rep_shapes))]

    rep_res = [unpack_rep(t) for t in rep_out]

    def my_cols(t, width):
        return lax.dynamic_slice_in_dim(t, me * width, width, axis=t.ndim - 1)

    def vec_adamw(name, w, m, v, g_all, width):
        g = my_cols(g_all, width)
        r = _adamw(name, w.reshape(-1, width), m.reshape(-1, width), v.reshape(-1, width),
                   g.reshape(N_DEV, -1, width))
        return [t.reshape(w.shape) for t in r]

    res_b_in = vec_adamw("adamw_b_in", od_b_in, m_od_b_in, v_od_b_in, bin_all, 384)
    res_dw_w = vec_adamw("adamw_dw_w", od_dw_w, m_od_dw_w, v_od_dw_w, dww_all, LANES)
    res_dw_b = vec_adamw("adamw_dw_b", od_dw_b, m_od_dw_b, v_od_dw_b, dwb_all, LANES)
    res_ln_g = vec_adamw("adamw_ln_g", od_ln_g, m_od_ln_g, v_od_ln_g, lng_all, LANES)
    res_ln_b = vec_adamw("adamw_ln_b", od_ln_b, m_od_ln_b, v_od_ln_b, lnb_all, LANES)

    dmod_mine = my_cols(dmod_all, 384).transpose(1, 0, 2)
    dmod_pad = jnp.concatenate([dmod_mine, jnp.zeros((DEPTH, LANES - N_DEV, 384), F32)], axis=1)
    c_all_t = jnp.concatenate([c_all.T, jnp.zeros((D, LANES - N_DEV), F32)], axis=1)
    g_ada_w = _ada_w_grad("ada_w_grad", c_all_t, dmod_pad)
    res_ada_w = [t.reshape(ada_w.shape) for t in
                 _adamw("adamw_ada_w", ada_w.reshape(-1, 384), m_ada_w.reshape(-1, 384), v_ada_w.reshape(-1, 384),
                        g_ada_w.reshape(-1, 384))]

    ev_sh = [[_sum_parts(f"sum_g_ev{i}_{j}", r) for j, r in enumerate(recv_ev[i])] for i in range(2)]
    od_sh = [[_sum_parts(f"sum_g_od{i}_{j}", r) for j, r in enumerate(recv_od[i])] for i in range(2)]

    def mat_adamw(name, w, m, v, g):
        r = _adamw(name, w.reshape(-1, w.shape[-1]), m.reshape(-1, w.shape[-1]), v.reshape(-1, w.shape[-1]),
                   g.reshape(-1, w.shape[-1]))
        return [t.reshape(w.shape) for t in r]

    res_ev_w_in = mat_adamw("adamw_ev_w_in", ev_w_in, m_ev_w_in, v_ev_w_in, jnp.stack([ev_sh[i][0].T for i in range(2)]))
    res_ev_w_uq = mat_adamw("adamw_ev_w_uq", ev_w_uq, m_ev_w_uq, v_ev_w_uq, jnp.stack([ev_sh[i][1].T for i in range(2)]))
    res_ev_w_ukv = mat_adamw("adamw_ev_w_ukv", ev_w_ukv, m_ev_w_ukv, v_ev_w_ukv,
                             jnp.stack([ev_sh[i][2].T for i in range(2)]))
    res_ev_w_out = mat_adamw("adamw_ev_w_out", ev_w_out, m_ev_w_out, v_ev_w_out, jnp.stack([ev_sh[i][3] for i in range(2)]))
    res_od_w_in = mat_adamw("adamw_od_w_in", od_w_in, m_od_w_in, v_od_w_in, jnp.stack([od_sh[i][0].T for i in range(2)]))
    res_od_w_out = mat_adamw("adamw_od_w_out", od_w_out, m_od_w_out, v_od_w_out, jnp.stack([od_sh[i][1] for i in range(2)]))

    per_weight = [res_ada_w] + [[rep_res[t][j] for t in range(4)] for j in range(3)]
    per_weight += [res_ev_w_in, [rep_res[t][3] for t in range(4)], [rep_res[t][4] for t in range(4)],
                   [rep_res[t][5] for t in range(4)], res_ev_w_uq, [rep_res[t][6] for t in range(4)], res_ev_w_ukv,
                   res_ev_w_out, res_od_w_in, res_b_in, res_dw_w, res_dw_b, res_ln_g, res_ln_b, res_od_w_out]
    outs = [loss, grad_x]
    for t in range(4):
        outs += [pw[t] for pw in per_weight]
    return tuple(outs)
```

```python
import functools

import numpy as np
import jax
import jax.numpy as jnp
from jax import lax
from jax.experimental import pallas as pl
from jax.experimental.pallas import tpu as pltpu

F32 = jnp.float32
BF16 = jnp.bfloat16
MESH = pl.DeviceIdType.MESH

N_DEV = 8
D = 1024
DEPTH = 4
EPS = 1e-6
RET_HEADS = 4
MLA_HEADS = 4
RET_SCALE = 64 ** -0.5
MLA_SCALE = 192 ** -0.5
LOG2E = 1.4426950408889634
LN2 = 1.0 / LOG2E
CONV_K = 31
CONV_HALO = 16
ROPE_BASE = 10000.0

ADAM_LR = 0.001
ADAM_B1 = 0.9
ADAM_B2 = 0.999
ADAM_EPS = 1e-08
ADAM_WD = 0.01
ADAM_STEP = 10

LANES = 128
VMEM_LIMIT = 48 * 1024 * 1024

ZL_EV = dict(rv=(0, 512), rg=(512, 512), mg=(1024, 512), cq=(1536, 384), kr=(1920, 128),
             rq=(2048, 512), rk=(2560, 512), ckv=(3072, 256))
ZW_EV = 3328
ZW_OD = 3072


def _cparams(sem, vmem=VMEM_LIMIT):
    return pltpu.CompilerParams(dimension_semantics=sem, vmem_limit_bytes=vmem)


def _pick(n, prefs):
    for p in prefs:
        if n % p == 0:
            return p
    return n


def _sigmoid(x):
    return 0.5 * (jnp.tanh(0.5 * x) + 1.0)


def _silu(x):
    return x * _sigmoid(x)


def _rms(x, g):
    return x * lax.rsqrt(jnp.mean(x * x, axis=-1, keepdims=True) + EPS) * g


def _log_sigmoid(x):
    return jnp.minimum(x, 0.0) - jnp.log(1.0 + jnp.exp(jnp.minimum(x, -x)))


def _tile_lanes(t, width):
    reps = width // t.shape[1]
    return t if reps == 1 else jnp.concatenate([t] * reps, axis=1)


def _rot_half(x):
    w = x.shape[1]
    lane = lax.broadcasted_iota(jnp.int32, x.shape, 1)
    first = jnp.bitwise_and(lane, 63) < 32
    return jnp.where(first, pltpu.roll(x, w - 32, 1), pltpu.roll(x, 32, 1))


def _rope(x, cos, sin):
    w = x.shape[1]
    return x * _tile_lanes(cos, w) + _rot_half(x) * _tile_lanes(sin, w)


def _rope_t(dy, cos, sin):
    w = dy.shape[1]
    return dy * _tile_lanes(cos, w) + _rot_half(dy * _tile_lanes(sin, w))


_DN = {"nn": (((1,), (0,)), ((), ())), "nt": (((1,), (1,)), ((), ())), "tn": (((0,), (0,)), ((), ()))}


def _dot(a, b, mode):
    return lax.dot_general(a.astype(BF16), b.astype(BF16), _DN[mode], preferred_element_type=F32)


@functools.partial(jax.custom_vjp, nondiff_argnums=(2,))
def _bdot(a, b, mode):
    return _dot(a, b, mode)


def _bdot_fwd(a, b, mode):
    return _dot(a, b, mode), (a, b)


def _bdot_bwd(mode, res, g):
    a, b = res
    if mode == "nn":
        return _dot(g, b, "nt"), _dot(a, g, "tn")
    if mode == "nt":
        return _dot(g, b, "nn"), _dot(g, a, "tn")
    return _dot(b, g, "nt"), _dot(a, g, "nn")


_bdot.defvjp(_bdot_fwd, _bdot_bwd)


def _rowwise(name, body, row_ins, vec_ins, row_outs, red_outs=(), tile=512):
    s = row_ins[0][0].shape[0]
    tile = min(tile, s)
    nr, nv, no = len(row_ins), len(vec_ins), len(row_outs)

    def kern(*refs):
        rows = [r[...].astype(F32) if r.dtype == BF16 else r[...] for r in refs[:nr]]
        vecs = [r[...] for r in refs[nr:nr + nv]]
        outs, reds = body(rows, vecs)
        for r, o in zip(refs[nr + nv:nr + nv + no], outs):
            r[...] = o.astype(r.dtype)
        red_refs = refs[nr + nv + no:]
        if red_refs:
            @pl.when(pl.program_id(0) == 0)
            def _():
                for r in red_refs:
                    r[...] = jnp.zeros(r.shape, r.dtype)
            for r, v in zip(red_refs, reds):
                r[...] += v

    in_specs = [pl.BlockSpec((tile, w), (lambda i, cb=cb: (i, cb))) for (_, w, cb) in row_ins]
    in_specs += [pl.BlockSpec(v.shape, (lambda i, nd=v.ndim: (0,) * nd)) for v in vec_ins]
    out_specs = [pl.BlockSpec((tile, w), lambda i: (i, 0)) for (w, _) in row_outs]
    out_specs += [pl.BlockSpec(sh, lambda i: (0, 0)) for sh in red_outs]
    out_shape = [jax.ShapeDtypeStruct((s, w), dt) for (w, dt) in row_outs]
    out_shape += [jax.ShapeDtypeStruct(sh, F32) for sh in red_outs]
    res = pl.pallas_call(
        kern, name=name, grid=(s // tile,), in_specs=in_specs, out_specs=out_specs, out_shape=out_shape,
        compiler_params=_cparams(("arbitrary",)),
    )(*[a for (a, _, _) in row_ins], *vec_ins)
    return res[:no], res[no:]


def _mm(name, a, b, mode, out_dtype=F32, bias=None, comm=None):
    if mode == "tn":
        k, m = a.shape
        n = b.shape[1]
        tm = m if m <= 1664 else m // 2
        tk = min(k, 2048 if tm * n <= 1024 * 1024 else 1024)
        nk = k // tk

        def kern(a_ref, b_ref, o_ref, acc_ref):
            kk = pl.program_id(1)
            part = _dot(a_ref[...], b_ref[...], "tn")

            @pl.when(kk == 0)
            def _():
                acc_ref[...] = part

            @pl.when(kk > 0)
            def _():
                acc_ref[...] += part

            @pl.when(kk == nk - 1)
            def _():
                o_ref[...] = acc_ref[...].astype(o_ref.dtype)

        return pl.pallas_call(
            kern, name=name, grid=(m // tm, nk),
            in_specs=[pl.BlockSpec((tk, tm), lambda i, kk: (kk, i)),
                      pl.BlockSpec((tk, n), lambda i, kk: (kk, 0))],
            out_specs=pl.BlockSpec((tm, n), lambda i, kk: (i, 0)),
            out_shape=jax.ShapeDtypeStruct((m, n), out_dtype),
            scratch_shapes=[pltpu.VMEM((tm, n), F32)],
            compiler_params=_cparams(("parallel", "arbitrary")),
        )(a, b)

    m, k = a.shape
    n = b.shape[1] if mode == "nn" else b.shape[0]
    tm = min(m, 1024)
    tn = n if n <= 1664 else n // 2
    has_bias = bias is not None

    def kern(*refs):
        a_ref, b_ref = refs[0], refs[1]
        o_ref = refs[-1]
        r = _dot(a_ref[...], b_ref[...], mode)
        if has_bias:
            r = r + refs[2][...]
        o_ref[...] = r.astype(o_ref.dtype)

    b_spec = (pl.BlockSpec((k, tn), lambda i, j: (0, j)) if mode == "nn"
              else pl.BlockSpec((tn, k), lambda i, j: (j, 0)))
    in_specs = [pl.BlockSpec((tm, k), lambda i, j: (i, 0)), b_spec]
    args = [a, b]
    if has_bias:
        in_specs.append(pl.BlockSpec((1, tn), lambda i, j: (0, j)))
        args.append(bias)
    outs, got = _call(kern, name=name, grid=(m // tm, n // tn), in_specs=in_specs,
                      out_specs=[pl.BlockSpec((tm, tn), lambda i, j: (i, j))],
                      out_shape=[jax.ShapeDtypeStruct((m, n), out_dtype)], args=args, sem=("parallel", "parallel"),
                      comm=comm)
    return (outs[0], got) if comm else outs[0]


def _mm_cols_nn(name, pieces, offsets, b, comm=None):
    m = pieces[0].shape[0]
    n = b.shape[1]
    tm = min(m, 1024)
    np_ = len(pieces)

    def kern(*refs):
        acc = _dot(refs[0][...], refs[np_][...], "nn")
        for p in range(1, np_):
            acc = acc + _dot(refs[p][...], refs[np_ + p][...], "nn")
        refs[2 * np_][...] = acc.astype(BF16)

    in_specs = [pl.BlockSpec((tm, a.shape[1]), lambda i: (i, 0)) for a in pieces]
    in_specs += [pl.BlockSpec((a.shape[1], n), (lambda i, r=off // a.shape[1]: (r, 0))) for a, off in zip(pieces, offsets)]
    outs, got = _call(kern, name=name, grid=(m // tm,), in_specs=in_specs,
                      out_specs=[pl.BlockSpec((tm, n), lambda i: (i, 0))],
                      out_shape=[jax.ShapeDtypeStruct((m, n), BF16)], args=list(pieces) + [b] * np_, sem=("parallel",),
                      comm=comm)
    return (outs[0], got) if comm else outs[0]


def _mm_cols_tn(name, pieces, b):
    k, n = b.shape
    tk = min(k, 512)
    nk = k // tk
    np_ = len(pieces)

    def kern(*refs):
        b_ref = refs[np_]
        o_refs, acc_refs = refs[np_ + 1:2 * np_ + 1], refs[2 * np_ + 1:]
        kk = pl.program_id(0)

        @pl.when(kk == 0)
        def _():
            for r in acc_refs:
                r[...] = jnp.zeros(r.shape, F32)

        bb = b_ref[...]
        for p in range(np_):
            acc_refs[p][...] += _dot(refs[p][...], bb, "tn")

        @pl.when(kk == nk - 1)
        def _():
            for o, r in zip(o_refs, acc_refs):
                o[...] = r[...].astype(o.dtype)

    return pl.pallas_call(
        kern, name=name, grid=(nk,),
        in_specs=[pl.BlockSpec((tk, a.shape[1]), lambda kk: (kk, 0)) for a in pieces] + [pl.BlockSpec((tk, n), lambda kk: (kk, 0))],
        out_specs=[pl.BlockSpec((a.shape[1], n), lambda kk: (0, 0)) for a in pieces],
        out_shape=[jax.ShapeDtypeStruct((a.shape[1], n), BF16) for a in pieces],
        scratch_shapes=[pltpu.VMEM((a.shape[1], n), F32) for a in pieces],
        compiler_params=_cparams(("arbitrary",)),
    )(*pieces, b)


def _mla_proj(name, qn, uq_t, kvn, ukv_t, krr, cos, sin):
    s = qn.shape[0]
    tm = min(s, 1024)

    def kern(qn_ref, uq_ref, kvn_ref, ukv_ref, kr_ref, cos_ref, sin_ref, q_out, k_out, v_out, kt_out):
        qp = _dot(qn_ref[...], uq_ref[...], "nt")
        kv = _dot(kvn_ref[...], ukv_ref[...], "nt")
        c, sn = cos_ref[...], sin_ref[...]
        kr_r = kr_ref[...].astype(F32)
        ones = jnp.ones((tm, LANES), F32)
        qs, ks, vs = [], [], []
        for hh in range(MLA_HEADS):
            qs += [qp[:, 256 * hh:256 * hh + 128], _rope(qp[:, 256 * hh + 128:256 * hh + 256], c, sn)]
            ks += [kv[:, 256 * hh:256 * hh + 128], kr_r]
            vs += [kv[:, 1024 + LANES * hh:1024 + LANES * hh + LANES], ones]
        q_out[...] = (jnp.concatenate(qs, axis=1) * (MLA_SCALE * LOG2E)).astype(BF16)
        k_out[...] = jnp.concatenate(ks, axis=1).astype(BF16)
        v_out[...] = jnp.concatenate(vs, axis=1).astype(BF16)
        for hh in range(MLA_HEADS):
            kt_out[hh] = jnp.concatenate([ks[2 * hh], ks[2 * hh + 1]], axis=1).T.astype(BF16)

    row = lambda w: pl.BlockSpec((tm, w), lambda i: (i, 0))
    whole = lambda a: pl.BlockSpec(a.shape, lambda i: (0, 0))
    return pl.pallas_call(
        kern, name=name, grid=(s // tm,),
        in_specs=[row(qn.shape[1]), whole(uq_t), row(kvn.shape[1]), whole(ukv_t), row(LANES), row(LANES), row(LANES)],
        out_specs=[row(1024)] * 3 + [pl.BlockSpec((MLA_HEADS, 256, tm), lambda i: (0, 0, i))],
        out_shape=[jax.ShapeDtypeStruct((s, 1024), BF16)] * 3 + [jax.ShapeDtypeStruct((MLA_HEADS, 256, s), BF16)],
        compiler_params=_cparams(("parallel",)),
    )(qn, uq_t, kvn, ukv_t, krr, cos, sin)


def _peers():
    mx, my, mc = lax.axis_index("x"), lax.axis_index("y"), lax.axis_index("c")
    me = 4 * mx + 2 * my + mc
    out = []
    for k in range(1, N_DEV):
        px = 1 - mx if (k >> 2) & 1 else mx
        py = 1 - my if (k >> 1) & 1 else my
        pc = 1 - mc if k & 1 else mc
        out.append((k, (px, py, pc), 4 * px + 2 * py + pc))
    return me, out


def _xchg_copies(x_refs, out_refs, scatter, send_sems, recv_sems, local_sems):
    me, peers = _peers()
    local, out, arrive = [], [], []
    for a, (x, o, sc) in enumerate(zip(x_refs, out_refs, scatter)):
        mine = x.at[me] if sc else x
        local.append(pltpu.make_async_copy(mine, o.at[me], local_sems.at[a]))
        for k, dev, p in peers:
            out.append(pltpu.make_async_remote_copy(
                src_ref=x.at[p] if sc else x, dst_ref=o.at[me],
                send_sem=send_sems.at[a, k - 1], recv_sem=recv_sems.at[a, k - 1],
                device_id=dev, device_id_type=MESH))
            arrive.append(pltpu.make_async_remote_copy(
                src_ref=mine, dst_ref=o.at[p],
                send_sem=send_sems.at[a, k - 1], recv_sem=recv_sems.at[a, k - 1],
                device_id=dev, device_id_type=MESH))
    return local, out, arrive


def _xchg_start(*args):
    local, out, _ = _xchg_copies(*args)
    for cp in local + out:
        cp.start()


def _xchg_wait(*args):
    local, out, arrive = _xchg_copies(*args)
    for cp in out:
        cp.wait_send()
    for cp in arrive:
        cp.wait_recv()
    for cp in local:
        cp.wait()


def _call(kern, *, name, grid, in_specs, out_specs, out_shape, args, sem, scratch_shapes=(), vmem=VMEM_LIMIT,
          comm=None):
    if not comm:
        outs = pl.pallas_call(kern, name=name, grid=grid, in_specs=in_specs, out_specs=out_specs, out_shape=out_shape,
                              scratch_shapes=list(scratch_shapes), compiler_params=_cparams(sem, vmem))(*args)
        return outs, []
    n, ni, no, ns = len(comm), len(in_specs), len(out_specs), len(scratch_shapes)
    xs = [x for x, _ in comm]
    scatter = [sc for _, sc in comm]

    def body(*refs):
        ins, x_refs = refs[:ni], refs[ni:ni + n]
        outs, out_refs = refs[ni + n:ni + n + no], refs[ni + n + no:ni + 2 * n + no]
        scr = refs[ni + 2 * n + no:ni + 2 * n + no + ns]
        sems = refs[ni + 2 * n + no + ns:]
        ids = [pl.program_id(d) for d in range(len(grid))]
        first = functools.reduce(jnp.logical_and, [i == 0 for i in ids])
        last = functools.reduce(jnp.logical_and, [i == g - 1 for i, g in zip(ids, grid)])

        @pl.when(first)
        def _():
            _xchg_start(x_refs, out_refs, scatter, *sems)

        kern(*ins, *outs, *scr)

        @pl.when(last)
        def _():
            _xchg_wait(x_refs, out_refs, scatter, *sems)

    any_spec = pl.BlockSpec(memory_space=pl.ANY)
    res = pl.pallas_call(
        body, name=name, grid=grid,
        in_specs=list(in_specs) + [any_spec] * n, out_specs=list(out_specs) + [any_spec] * n,
        out_shape=list(out_shape) + [jax.ShapeDtypeStruct((N_DEV,) + tuple(x.shape[1:] if sc else x.shape), x.dtype)
                                     for x, sc in comm],
        scratch_shapes=list(scratch_shapes) + [pltpu.SemaphoreType.DMA((n, N_DEV - 1)),
                                               pltpu.SemaphoreType.DMA((n, N_DEV - 1)), pltpu.SemaphoreType.DMA((n,))],
        compiler_params=pltpu.CompilerParams(dimension_semantics=("arbitrary",) * len(grid), vmem_limit_bytes=vmem,
                                             has_side_effects=True),
    )(*args, *xs)
    return res[:no], res[no:]


def _exchange(name, xs, scatter):
    def nothing():
        pass

    return _call(nothing, name=name, grid=(1,), in_specs=[], out_specs=[], out_shape=[], args=[], sem=("arbitrary",),
                 comm=[(x, scatter) for x in xs])[1]


def _sum_parts(name, x):
    p, r, c = x.shape
    tr = r if r * c * p * x.dtype.itemsize <= (8 << 20) else _pick(r, (256, 128, 64, 16))

    def kern(x_ref, o_ref):
        acc = x_ref[0].astype(F32)
        for i in range(1, p):
            acc = acc + x_ref[i].astype(F32)
        o_ref[...] = acc

    return pl.pallas_call(
        kern, name=name, grid=(r // tr,),
        in_specs=[pl.BlockSpec((p, tr, c), lambda i: (0, i, 0))],
        out_specs=pl.BlockSpec((tr, c), lambda i: (i, 0)),
        out_shape=jax.ShapeDtypeStruct((r, c), F32),
        compiler_params=_cparams(("parallel",)),
    )(x)


def _adamw(name, w, m, v, g):
    r, c = w.shape
    parts = g.shape[0] if g.ndim == 3 else 0
    tr = 512 if (r > 512 and r % 512 == 0) else r

    def kern(w_ref, m_ref, v_ref, g_ref, go_ref, d_ref, mo_ref, vo_ref):
        if parts:
            gg = g_ref[0]
            for i in range(1, parts):
                gg = gg + g_ref[i]
        else:
            gg = g_ref[...]
        mm = ADAM_B1 * m_ref[...] + (1.0 - ADAM_B1) * gg
        vv = ADAM_B2 * v_ref[...] + (1.0 - ADAM_B2) * (gg * gg)
        m_hat = mm / (1.0 - ADAM_B1 ** ADAM_STEP)
        v_hat = vv / (1.0 - ADAM_B2 ** ADAM_STEP)
        go_ref[...] = gg
        d_ref[...] = -ADAM_LR * (m_hat / (jnp.sqrt(v_hat) + ADAM_EPS) + ADAM_WD * w_ref[...])
        mo_ref[...] = mm
        vo_ref[...] = vv

    spec = pl.BlockSpec((tr, c), lambda i: (i, 0))
    gspec = pl.BlockSpec((parts, tr, c), lambda i: (0, i, 0)) if parts else spec
    sh = jax.ShapeDtypeStruct((r, c), F32)
    return pl.pallas_call(
        kern, name=name, grid=(r // tr,), in_specs=[spec, spec, spec, gspec],
        out_specs=[spec] * 4, out_shape=[sh] * 4,
        compiler_params=_cparams(("parallel",)),
    )(w, m, v, g)


def _ret_tables(dec_cc, dec_cd, dec_dd, reverse):
    c = dec_cc.shape[0]
    row = lax.broadcasted_iota(jnp.int32, (c, c), 0).astype(F32)
    col = lax.broadcasted_iota(jnp.int32, (c, c), 1).astype(F32)
    pos = lax.broadcasted_iota(jnp.int32, (c, LANES), 0).astype(F32)
    if reverse:
        diff, mask = col - row, col > row
        q_exp, k_exp = c - pos, pos
    else:
        diff, mask = row - col, row >= col
        q_exp, k_exp = pos + 1.0, c - 1.0 - pos
    decay = jnp.where(mask, jnp.exp(_log_sigmoid(dec_cc) * jnp.maximum(diff, 0.0)), 0.0)
    lam_cd = _log_sigmoid(dec_cd)
    return decay, jnp.exp(lam_cd * q_exp), jnp.exp(lam_cd * k_exp), jnp.exp(_log_sigmoid(dec_dd) * float(c))


def _ret_chunk(q, k, v, st, decay, qw, kw, sd):
    scores = _bdot(q, k, "nt") * decay
    o = _bdot(scores, v, "nn") + _bdot(q * qw, st, "nn")
    st_new = st * sd + _bdot(k * kw, v, "tn")
    return o, st_new


def _ret_dec(dec_ref, h, c):
    d = dec_ref[:, h:h + 1]
    return (jnp.broadcast_to(d, (c, c)), jnp.broadcast_to(d, (c, LANES)), jnp.broadcast_to(d, (LANES, LANES)))


def _ret_per_step(n_chunks):
    return 2 if (n_chunks % 2 == 0 and n_chunks >= 4) else 1


def _ret_table_scratch(c):
    return [pltpu.VMEM((RET_HEADS, c, c), F32), pltpu.VMEM((RET_HEADS, c, LANES), F32),
            pltpu.VMEM((RET_HEADS, c, LANES), F32), pltpu.VMEM((RET_HEADS, LANES, LANES), F32)]


def _ret_fwd(name, q, k, z, dec, reverse, chunk):
    s = q.shape[0]
    chunk = min(chunk, s)
    per = _ret_per_step(s // chunk)
    n = s // (chunk * per)
    cmap = (lambda i: (n - 1 - i, 0)) if reverse else (lambda i: (i, 0))
    smap = (lambda i: (n - 1 - i, 0, 0, 0)) if reverse else (lambda i: (i, 0, 0, 0))

    def kern(q_ref, k_ref, v_ref, dec_ref, o_ref, st_out_ref, st_ref, *tab_refs):
        @pl.when(pl.program_id(0) == 0)
        def _():
            st_ref[...] = jnp.zeros(st_ref.shape, F32)
            for h in range(RET_HEADS):
                for r, t in zip(tab_refs, _ret_tables(*_ret_dec(dec_ref, h, chunk), reverse)):
                    r[h] = t

        for c2 in (range(per - 1, -1, -1) if reverse else range(per)):
            rows = pl.ds(c2 * chunk, chunk)
            for h in range(RET_HEADS):
                sl = slice(LANES * h, LANES * (h + 1))
                st = st_ref[h]
                st_out_ref[c2, h] = st
                o, st_new = _ret_chunk(q_ref[rows, sl].astype(F32), k_ref[rows, sl].astype(F32),
                                       v_ref[rows, sl].astype(F32), st, *[r[h] for r in tab_refs])
                o_ref[rows, sl] = o.astype(BF16)
                st_ref[h] = st_new

    blk = chunk * per
    return pl.pallas_call(
        kern, name=name, grid=(n,),
        in_specs=[pl.BlockSpec((blk, 512), cmap), pl.BlockSpec((blk, 512), cmap),
                  pl.BlockSpec((blk, 512), cmap), pl.BlockSpec((1, RET_HEADS), lambda i: (0, 0))],
        out_specs=[pl.BlockSpec((blk, 512), cmap), pl.BlockSpec((per, RET_HEADS, LANES, LANES), smap)],
        out_shape=[jax.ShapeDtypeStruct((s, 512), BF16), jax.ShapeDtypeStruct((n * per, RET_HEADS, LANES, LANES), F32)],
        scratch_shapes=[pltpu.VMEM((RET_HEADS, LANES, LANES), F32)] + _ret_table_scratch(chunk),
        compiler_params=_cparams(("arbitrary",)),
    )(q, k, z, dec)


def _ret_bwd(name, q, k, z, dec, states, do, reverse, chunk):
    s = q.shape[0]
    chunk = min(chunk, s)
    per = _ret_per_step(s // chunk)
    n = s // (chunk * per)
    cmap = (lambda i: (i, 0)) if reverse else (lambda i: (n - 1 - i, 0))
    smap = (lambda i: (i, 0, 0, 0)) if reverse else (lambda i: (n - 1 - i, 0, 0, 0))

    def kern(q_ref, k_ref, v_ref, dec_ref, st_in_ref, do_ref, dq_ref, dk_ref, dv_ref, ddec_ref, dst_ref, *scr):
        tab_refs, gtab_refs = scr[:4], scr[4:]
        step = pl.program_id(0)

        @pl.when(step == 0)
        def _():
            dst_ref[...] = jnp.zeros(dst_ref.shape, F32)
            for r in gtab_refs:
                r[...] = jnp.zeros(r.shape, F32)
            for h in range(RET_HEADS):
                for r, t in zip(tab_refs, _ret_tables(*_ret_dec(dec_ref, h, chunk), reverse)):
                    r[h] = t

        for c2 in (range(per) if reverse else range(per - 1, -1, -1)):
            rows = pl.ds(c2 * chunk, chunk)
            for h in range(RET_HEADS):
                sl = slice(LANES * h, LANES * (h + 1))
                _, vjp = jax.vjp(_ret_chunk, q_ref[rows, sl].astype(F32), k_ref[rows, sl].astype(F32),
                                 v_ref[rows, sl].astype(F32), st_in_ref[c2, h], *[r[h] for r in tab_refs])
                grads = vjp((do_ref[rows, sl].astype(F32), dst_ref[h]))
                dq_ref[rows, sl] = grads[0].astype(BF16)
                dk_ref[rows, sl] = grads[1].astype(BF16)
                dv_ref[rows, sl] = grads[2].astype(BF16)
                dst_ref[h] = grads[3]
                for r, g in zip(gtab_refs, grads[4:]):
                    r[h] += g

        @pl.when(step == n - 1)
        def _():
            lane = lax.broadcasted_iota(jnp.int32, (1, LANES), 1)
            ddec = jnp.zeros((1, LANES), F32)
            for h in range(RET_HEADS):
                _, vjp_t = jax.vjp(functools.partial(_ret_tables, reverse=reverse), *_ret_dec(dec_ref, h, chunk))
                parts = vjp_t(tuple(r[h] for r in gtab_refs))
                tot = sum(jnp.sum(jnp.sum(p, axis=1, keepdims=True), axis=0, keepdims=True) for p in parts)
                ddec = ddec + jnp.where(lane == h, tot, 0.0)
            ddec_ref[...] = ddec

    cspec = pl.BlockSpec((chunk * per, 512), cmap)
    return pl.pallas_call(
        kern, name=name, grid=(n,),
        in_specs=[cspec, cspec, cspec, pl.BlockSpec((1, RET_HEADS), lambda i: (0, 0)),
                  pl.BlockSpec((per, RET_HEADS, LANES, LANES), smap), cspec],
        out_specs=[cspec, cspec, cspec, pl.BlockSpec((1, LANES), lambda i: (0, 0))],
        out_shape=[jax.ShapeDtypeStruct((s, 512), BF16)] * 3 + [jax.ShapeDtypeStruct((1, LANES), F32)],
        scratch_shapes=[pltpu.VMEM((RET_HEADS, LANES, LANES), F32)] + _ret_table_scratch(chunk) * 2,
        compiler_params=_cparams(("arbitrary",)),
    )(q, k, z, dec, states, do)


def _fa_fwd(name, q, k, vx, tq, tk, nsub, comm=None):
    s = q.shape[0]
    tq, tk = min(tq, s), min(tk, s)
    nk = s // tk
    sq = tq // nsub

    def kern(q_ref, k_ref, v_ref, o_ref, lse_ref, m_ref, acc_ref):
        j = pl.program_id(2)

        @pl.when(j == 0)
        def _():
            m_ref[...] = jnp.full(m_ref.shape, -jnp.inf, F32)
            acc_ref[...] = jnp.zeros(acc_ref.shape, F32)

        kb, vb = k_ref[...], v_ref[...]
        for c in range(nsub):
            rows = pl.ds(c * sq, sq)
            sc = _dot(q_ref[rows, :], kb, "nt")
            m_prev = m_ref[rows, :]
            m_new = jnp.maximum(m_prev, jnp.max(sc, axis=1, keepdims=True))
            alpha = jnp.exp2(m_prev - m_new)
            p = jnp.exp2(sc - m_new)
            acc_ref[rows, :] = alpha * acc_ref[rows, :] + _dot(p, vb, "nn")
            m_ref[rows, :] = m_new

        @pl.when(j == nk - 1)
        def _():
            den = acc_ref[:, LANES:]
            o_ref[...] = (acc_ref[:, :LANES] / den).astype(BF16)
            lse_ref[...] = m_ref[...] + jnp.log(den[:, :1]) * LOG2E

    (o, lse), got = _call(
        kern, name=name, grid=(MLA_HEADS, s // tq, nk),
        in_specs=[pl.BlockSpec((tq, 256), lambda h, i, j: (i, h)),
                  pl.BlockSpec((tk, 256), lambda h, i, j: (j, h)),
                  pl.BlockSpec((tk, 256), lambda h, i, j: (j, h))],
        out_specs=[pl.BlockSpec((tq, LANES), lambda h, i, j: (i, h)),
                   pl.BlockSpec((None, tq, 1), lambda h, i, j: (h, i, 0))],
        out_shape=[jax.ShapeDtypeStruct((s, 512), BF16), jax.ShapeDtypeStruct((MLA_HEADS, s, 1), F32)],
        scratch_shapes=[pltpu.VMEM((tq, 1), F32), pltpu.VMEM((tq, 256), F32)],
        args=[q, k, vx], sem=("parallel", "parallel", "arbitrary"), comm=comm)
    return o, lse, got


def _fa_bwd(name, q, k, kt, vx, do, lse, delta, tq, tk, nsub, comm=None):
    s = q.shape[0]
    tq, tk = min(tq, s), min(tk, s)
    nq = s // tq
    sk = tk // nsub

    nkb = s // tk

    def kern(q_ref, k_ref, kt_ref, v_ref, do_ref, lse_ref, dl_ref, dqt_ref, dk_ref, dv_ref, dqt_acc, dk_acc, dv_acc):
        j, i = pl.program_id(1), pl.program_id(2)

        @pl.when(i == 0)
        def _():
            dv_acc[...] = jnp.zeros(dv_acc.shape, F32)
            dk_acc[...] = jnp.zeros(dk_acc.shape, F32)

        @pl.when(j == 0)
        def _():
            dqt_acc[i] = jnp.zeros((256, tq), F32)

        qb, dob = q_ref[...], do_ref[...]
        lse_row, dl_row = lse_ref[...], dl_ref[...]
        dqt = dqt_acc[i]
        for c in range(nsub):
            rows = pl.ds(c * sk, sk)
            st = _dot(k_ref[rows, :], qb, "nt")
            pt = jnp.exp2(st - lse_row)
            dpt = _dot(v_ref[rows, :], dob, "nt")
            dst = (pt * (dpt - dl_row)).astype(BF16)
            dv_acc[rows, :] += _dot(pt, dob, "nn")
            dk_acc[rows, :] += _dot(dst, qb, "nn")
            dqt = dqt + _dot(kt_ref[:, rows], dst, "nn")
        dqt_acc[i] = dqt

        @pl.when(i == nq - 1)
        def _():
            dv_ref[...] = dv_acc[...].astype(BF16)
            dk_ref[...] = dk_acc[...].astype(BF16)

        @pl.when(j == nkb - 1)
        def _():
            dqt_ref[i] = dqt.astype(BF16)

    outs, got = _call(
        kern, name=name, grid=(MLA_HEADS, s // tk, nq),
        in_specs=[pl.BlockSpec((tq, 256), lambda h, j, i: (i, h)),
                  pl.BlockSpec((tk, 256), lambda h, j, i: (j, h)),
                  pl.BlockSpec((None, 256, tk), lambda h, j, i: (h, 0, j)),
                  pl.BlockSpec((tk, LANES), lambda h, j, i: (j, 2 * h)),
                  pl.BlockSpec((tq, LANES), lambda h, j, i: (i, h)),
                  pl.BlockSpec((None, 1, tq), lambda h, j, i: (h, 0, i)),
                  pl.BlockSpec((None, 1, tq), lambda h, j, i: (h, 0, i))],
        out_specs=[pl.BlockSpec((None, nq, 256, tq), lambda h, j, i: (h, 0, 0, 0)),
                   pl.BlockSpec((tk, 256), lambda h, j, i: (j, h)),
                   pl.BlockSpec((tk, LANES), lambda h, j, i: (j, h))],
        out_shape=[jax.ShapeDtypeStruct((MLA_HEADS, nq, 256, tq), BF16),
                   jax.ShapeDtypeStruct((s, 1024), BF16), jax.ShapeDtypeStruct((s, 512), BF16)],
        scratch_shapes=[pltpu.VMEM((nq, 256, tq), F32), pltpu.VMEM((tk, 256), F32), pltpu.VMEM((tk, LANES), F32)],
        args=[q, k, kt, vx, do, lse, delta], sem=("parallel", "arbitrary", "arbitrary"), comm=comm)
    return outs[0], outs[1], outs[2], got


def _fill_padded(dst_ref, val, s):
    zeros = jnp.zeros((CONV_HALO, LANES), F32)
    dst_ref[pl.ds(0, CONV_HALO), :] = zeros
    dst_ref[pl.ds(CONV_HALO + s, CONV_HALO), :] = zeros
    dst_ref[pl.ds(CONV_HALO, s), :] = val


def _shifted_windows(win):
    n = win.shape[0]
    return [win] + [pltpu.roll(win, n - b, 0) for b in range(1, 8)]


def _conv_fwd(name, z, w, bias, rc=256):
    s = z.shape[0]
    rc = min(rc, s)

    def kern(a_ref, g_ref, w_ref, b_ref, o_ref, pad_ref):
        _fill_padded(pad_ref, _glu_fn(a_ref[...].astype(F32), g_ref[...].astype(F32)), s)
        wv = w_ref[...]
        bv = b_ref[...]

        def chunk(r, carry):
            base = pl.multiple_of(r * rc, rc)
            wins = _shifted_windows(pad_ref[pl.ds(base, rc + 2 * CONV_HALO), :])
            acc = jnp.broadcast_to(bv, (rc, LANES))
            for kk in range(CONV_K):
                a, b = divmod(kk + 1, 8)
                acc = acc + wv[kk:kk + 1, :] * wins[b][8 * a:8 * a + rc]
            o_ref[pl.ds(base, rc), :] = acc.astype(BF16)
            return carry

        lax.fori_loop(0, s // rc, chunk, 0)

    nblk = D // LANES
    return pl.pallas_call(
        kern, name=name, grid=(nblk,),
        in_specs=[pl.BlockSpec((s, LANES), lambda c: (0, c)), pl.BlockSpec((s, LANES), lambda c: (0, nblk + c)),
                  pl.BlockSpec((32, LANES), lambda c: (0, c)), pl.BlockSpec((1, LANES), lambda c: (0, c))],
        out_specs=pl.BlockSpec((s, LANES), lambda c: (0, c)),
        out_shape=jax.ShapeDtypeStruct((s, D), BF16),
        scratch_shapes=[pltpu.VMEM((s + 2 * CONV_HALO, LANES), F32)],
        compiler_params=_cparams(("parallel",)),
    )(z, z, w, bias)


def _conv_bwd(name, z, g, w, rc=256, comm=None):
    s = z.shape[0]
    rc = min(rc, s)

    def kern(a_ref, b_ref, g_ref, w_ref, da_ref, db_ref, dw_ref, dbias_ref, sa_ref, sb_ref, upad_ref, gpad_ref,
             dwacc_ref):
        _fill_padded(upad_ref, _glu_fn(a_ref[...].astype(F32), b_ref[...].astype(F32)), s)
        _fill_padded(gpad_ref, g_ref[...].astype(F32), s)
        dwacc_ref[...] = jnp.zeros(dwacc_ref.shape, F32)
        wv = w_ref[...]

        def chunk(r, carry):
            sum_a, sum_b = carry
            base = pl.multiple_of(r * rc, rc)
            gwins = _shifted_windows(gpad_ref[pl.ds(base, rc + 2 * CONV_HALO), :])
            uwins = _shifted_windows(upad_ref[pl.ds(base, rc + 2 * CONV_HALO), :])
            gc = g_ref[pl.ds(base, rc), :].astype(F32)
            acc = jnp.zeros((rc, LANES), F32)
            for kk in range(CONV_K):
                a, b = divmod(CONV_K - kk, 8)
                acc = acc + wv[kk:kk + 1, :] * gwins[b][8 * a:8 * a + rc]
                a, b = divmod(kk + 1, 8)
                prod = gc * uwins[b][8 * a:8 * a + rc]
                dwacc_ref[kk] += jnp.sum(prod.reshape(rc // 8, 8, LANES), axis=0)
            dwacc_ref[CONV_K] += jnp.sum(gc.reshape(rc // 8, 8, LANES), axis=0)
            av = a_ref[pl.ds(base, rc), :].astype(F32)
            sg = _sigmoid(b_ref[pl.ds(base, rc), :].astype(F32))
            d_a = acc * sg
            d_b = acc * av * sg * (1.0 - sg)
            da_ref[pl.ds(base, rc), :] = d_a.astype(BF16)
            db_ref[pl.ds(base, rc), :] = d_b.astype(BF16)
            return (sum_a + jnp.sum(d_a.reshape(rc // 8, 8, LANES), axis=0),
                    sum_b + jnp.sum(d_b.reshape(rc // 8, 8, LANES), axis=0))

        zero = jnp.zeros((8, LANES), F32)
        sum_a, sum_b = lax.fori_loop(0, s // rc, chunk, (zero, zero))
        sa_ref[...] = jnp.sum(sum_a, axis=0, keepdims=True)
        sb_ref[...] = jnp.sum(sum_b, axis=0, keepdims=True)
        tot = jnp.sum(dwacc_ref[...], axis=1)
        lane_row = lax.broadcasted_iota(jnp.int32, (32, LANES), 0)
        dw_ref[...] = jnp.where(lane_row < CONV_K, tot, 0.0)
        dbias_ref[...] = tot[CONV_K:CONV_K + 1, :]

    nblk = D // LANES
    cs = pl.BlockSpec((s, LANES), lambda c: (0, c))
    vec = pl.BlockSpec((1, LANES), lambda c: (0, c))
    outs, got = _call(
        kern, name=name, grid=(nblk,),
        in_specs=[cs, pl.BlockSpec((s, LANES), lambda c: (0, nblk + c)), cs, pl.BlockSpec((32, LANES), lambda c: (0, c))],
        out_specs=[cs, cs, pl.BlockSpec((32, LANES), lambda c: (0, c)), vec, vec, vec],
        out_shape=[jax.ShapeDtypeStruct((s, D), BF16), jax.ShapeDtypeStruct((s, D), BF16),
                   jax.ShapeDtypeStruct((32, D), F32)] + [jax.ShapeDtypeStruct((1, D), F32)] * 3,
        scratch_shapes=[pltpu.VMEM((s + 2 * CONV_HALO, LANES), F32), pltpu.VMEM((s + 2 * CONV_HALO, LANES), F32),
                        pltpu.VMEM((32, 8, LANES), F32)],
        args=[z, z, g, w], sem=("parallel",), comm=comm)
    return outs, got


def _mod_local(name, c_all, ada_w):
    def kern(c_ref, w_ref, o_ref):
        o_ref[...] = jnp.dot(_silu(c_ref[...]), w_ref[...], preferred_element_type=F32,
                             precision=lax.Precision.HIGHEST)

    return pl.pallas_call(
        kern, name=name, grid=(DEPTH,),
        in_specs=[pl.BlockSpec((N_DEV, D), lambda l: (0, 0)), pl.BlockSpec((None, D, 384), lambda l: (l, 0, 0))],
        out_specs=pl.BlockSpec((None, N_DEV, 384), lambda l: (l, 0, 0)),
        out_shape=jax.ShapeDtypeStruct((DEPTH, N_DEV, 384), F32),
        compiler_params=_cparams(("parallel",)),
    )(c_all, ada_w)


def _ada_w_grad(name, c_all_t, dmod):
    def kern(c_ref, d_ref, o_ref):
        o_ref[...] = jnp.dot(_silu(c_ref[...]), d_ref[...], preferred_element_type=F32,
                             precision=lax.Precision.HIGHEST)

    return pl.pallas_call(
        kern, name=name, grid=(DEPTH,),
        in_specs=[pl.BlockSpec((D, LANES), lambda l: (0, 0)), pl.BlockSpec((None, LANES, 384), lambda l: (l, 0, 0))],
        out_specs=pl.BlockSpec((None, D, 384), lambda l: (l, 0, 0)),
        out_shape=jax.ShapeDtypeStruct((DEPTH, D, 384), F32),
        compiler_params=_cparams(("parallel",)),
    )(c_all_t, dmod)


def _pre_fn(x, g, scale, shift):
    return _rms(x, g) * (1.0 + scale) + shift


def _post_fn(y, g, gate):
    return gate * _rms(y, g)


def _ev_post_fn(o_heads, rg, a, mg):
    normed = []
    for oh in o_heads:
        mu = jnp.mean(oh, axis=-1, keepdims=True)
        var = jnp.mean(jnp.square(oh - mu), axis=-1, keepdims=True)
        normed.append((oh - mu) * lax.rsqrt(var + EPS))
    return jnp.concatenate([jnp.concatenate(normed, axis=1) * _silu(rg), a * _silu(mg)], axis=1)


def _od_post_fn(u, g, ln_g, ln_b):
    mu = jnp.mean(u, axis=-1, keepdims=True)
    var = jnp.mean(jnp.square(u - mu), axis=-1, keepdims=True)
    y = (u - mu) * lax.rsqrt(var + EPS) * ln_g + ln_b
    return _silu(y) * _silu(g)


def _rms_bwd(x, dxh_of):
    r = lax.rsqrt(jnp.mean(x * x, axis=-1, keepdims=True) + EPS)
    xh = x * r
    dxh = dxh_of(xh)
    return xh, r * (dxh - xh * jnp.mean(dxh * xh, axis=-1, keepdims=True))


def _pre_bwd(x, dh, g, scale):
    mod = 1.0 + scale
    xh, dx = _rms_bwd(x, lambda xh_: dh * (g * mod))
    t = dh * xh
    return dx, _colsum(t) * mod, _colsum(t) * g, _colsum(dh)


def _post_bwd(y, dout, g, gate):
    yh, dy = _rms_bwd(y, lambda yh_: dout * (gate * g))
    t = _colsum(dout * yh)
    return dy, t * gate, t * g


def _dsilu(x, s):
    return s * (1.0 + x * (1.0 - s))


def _ev_post_bwd(o_heads, rg, a, mg, da):
    d_ret, d_mla = da[:, :512], da[:, 512:]
    s_rg, s_mg = _sigmoid(rg), _sigmoid(mg)
    d_on = _heads(d_ret * (rg * s_rg))
    normed, do_heads = [], []
    for oh, dn in zip(o_heads, d_on):
        xc = oh - jnp.mean(oh, axis=-1, keepdims=True)
        r = lax.rsqrt(jnp.mean(xc * xc, axis=-1, keepdims=True) + EPS)
        xh = xc * r
        normed.append(xh)
        m1 = jnp.mean(dn, axis=-1, keepdims=True)
        m2 = jnp.mean(dn * xh, axis=-1, keepdims=True)
        do_heads.append(r * (dn - m1 - xh * m2))
    drg = d_ret * jnp.concatenate(normed, axis=1) * _dsilu(rg, s_rg)
    return do_heads, drg, d_mla * (mg * s_mg), d_mla * a * _dsilu(mg, s_mg)


def _od_post_bwd(u, g, da, ln_g, ln_b):
    xc = u - jnp.mean(u, axis=-1, keepdims=True)
    r = lax.rsqrt(jnp.mean(xc * xc, axis=-1, keepdims=True) + EPS)
    xh = xc * r
    y = xh * ln_g + ln_b
    s1, s2 = _sigmoid(y), _sigmoid(g)
    dg = da * (y * s1) * (s2 * (1.0 + g * (1.0 - s2)))
    dy = da * (g * s2) * (s1 * (1.0 + y * (1.0 - s1)))
    dxh = dy * ln_g
    m1 = jnp.mean(dxh, axis=-1, keepdims=True)
    m2 = jnp.mean(dxh * xh, axis=-1, keepdims=True)
    return r * (dxh - m1 - xh * m2), dg, _colsum(dy * xh), _colsum(dy)


def _glu_fn(a, b):
    return a * _sigmoid(b)


def _heads(x):
    return [x[:, LANES * h:LANES * (h + 1)] for h in range(4)]


def _colsum(x):
    return jnp.sum(x, axis=0, keepdims=True)


def _zrows(n, c):
    return jnp.zeros((n, c), BF16)


def _ev_win_layout(wt):
    rq = [p for h in range(4) for p in (wt[64 * h:64 * h + 64], _zrows(64, D))]
    rk = [p for h in range(4) for p in (wt[256 + 64 * h:256 + 64 * h + 64], _zrows(64, D))]
    return jnp.concatenate([wt[512:1024], wt[1024:1536], wt[2240:2752], wt[1536:1920],
                            wt[2176:2240], _zrows(64, D)] + rq + rk + [wt[1920:2176]], axis=0)


def _uq_layout(wt):
    return jnp.concatenate([p for h in range(4) for p in (wt[192 * h:192 * h + 192], _zrows(64, 384))], axis=0)


def _uq_unlayout(g):
    return jnp.concatenate([g[256 * h:256 * h + 192] for h in range(4)], axis=0)


def _ukv_layout(wt):
    kpart = [p for h in range(4) for p in (wt[256 * h:256 * h + 128], _zrows(128, 256))]
    vpart = [wt[256 * h + 128:256 * h + 256] for h in range(4)]
    return jnp.concatenate(kpart + vpart, axis=0)


def _ukv_unlayout(g):
    return jnp.concatenate([p for h in range(4) for p in (g[256 * h:256 * h + 128], g[1024 + 128 * h:1024 + 128 * h + 128])],
                           axis=0)


def kernel(x, c, positions, ada_w, ada_b, pre_g, post_g, ev_w_in, ev_dec_f, ev_dec_b, ev_q_norm_g, ev_w_uq, ev_kv_norm_g, ev_w_ukv, ev_w_out, od_w_in, od_b_in, od_dw_w, od_dw_b, od_ln_g, od_ln_b, od_w_out, loss_target, m_ada_w, m_ada_b, m_pre_g, m_post_g, m_ev_w_in, m_ev_dec_f, m_ev_dec_b, m_ev_q_norm_g, m_ev_w_uq, m_ev_kv_norm_g, m_ev_w_ukv, m_ev_w_out, m_od_w_in, m_od_b_in, m_od_dw_w, m_od_dw_b, m_od_ln_g, m_od_ln_b, m_od_w_out, v_ada_w, v_ada_b, v_pre_g, v_post_g, v_ev_w_in, v_ev_dec_f, v_ev_dec_b, v_ev_q_norm_g, v_ev_w_uq, v_ev_kv_norm_g, v_ev_w_ukv, v_ev_w_out, v_od_w_in, v_od_b_in, v_od_dw_w, v_od_dw_b, v_od_ln_g, v_od_ln_b, v_od_w_out):
    s = x.shape[1]
    me = 4 * lax.axis_index("x") + 2 * lax.axis_index("y") + lax.axis_index("c")
    x0 = x.reshape(s, D)
    tgt = loss_target.reshape(s, D)
    ret_chunk = 256
    fa_cfg_f = ((min(4096, s // 2), min(2048, s // 2), min(16, s // 512)),) * 2
    fa_cfg_b = ((min(2048, s // 2), min(4096, s // 2), min(16, s // 512)),) * 2

    start_parts = [c.reshape(-1), od_b_in.reshape(-1), od_dw_w.reshape(-1), od_dw_b.reshape(-1),
                   od_ln_g.reshape(-1), od_ln_b.reshape(-1)]
    start_sizes = [p.shape[0] for p in start_parts]
    start_len = -(-sum(start_sizes) // 1024) * 1024
    start_vec = jnp.concatenate(start_parts + [jnp.zeros((start_len - sum(start_sizes),), F32)])
    start_all, win0_all = _exchange("gather_start", [start_vec.reshape(-1, LANES), ev_w_in[0].T.astype(BF16)], False)
    start_all = start_all.reshape(N_DEV, start_len)
    offs = np.cumsum([0] + start_sizes)
    c_all = start_all[:, offs[0]:offs[1]]
    b_in_all = start_all[:, offs[1]:offs[2]].reshape(N_DEV, 2, 384).transpose(1, 0, 2).reshape(2, 1, ZW_OD)
    dw_w_all = start_all[:, offs[2]:offs[3]].reshape(N_DEV, 2, CONV_K, LANES).transpose(1, 2, 0, 3).reshape(2, CONV_K, D)
    dw_w_all = jnp.concatenate([dw_w_all, jnp.zeros((2, 1, D), F32)], axis=1)
    dw_b_all = start_all[:, offs[3]:offs[4]].reshape(N_DEV, 2, LANES).transpose(1, 0, 2).reshape(2, 1, D)
    ln_g_all = start_all[:, offs[4]:offs[5]].reshape(N_DEV, 2, LANES).transpose(1, 0, 2).reshape(2, 1, D)
    ln_b_all = start_all[:, offs[5]:offs[6]].reshape(N_DEV, 2, LANES).transpose(1, 0, 2).reshape(2, 1, D)

    mod_loc = _mod_local("mod_local", c_all, ada_w)
    mod_all = _exchange("gather_mod", [mod_loc.reshape(DEPTH * N_DEV, 384)], False)[0].reshape(N_DEV, DEPTH, N_DEV, 384)
    mod = lax.dynamic_index_in_dim(mod_all, me, axis=2, keepdims=False)
    mod = mod.transpose(1, 0, 2).reshape(DEPTH, 3 * D) + ada_b
    shift = [mod[l:l + 1, 0:D] for l in range(DEPTH)]
    scale = [mod[l:l + 1, D:2 * D] for l in range(DEPTH)]
    gate = [mod[l:l + 1, 2 * D:3 * D] for l in range(DEPTH)]

    def ev_shards(i):
        return [ev_w_in[i].T.astype(BF16), ev_w_uq[i].T.astype(BF16), ev_w_ukv[i].T.astype(BF16), ev_w_out[i].astype(BF16)]

    def od_shards(i):
        return [od_w_in[i].T.astype(BF16), od_w_out[i].astype(BF16)]

    def full(got):
        return [g.reshape(N_DEV * g.shape[1], g.shape[2]) for g in got]

    def ev_full(got):
        win_t, uq_t, ukv_t, wout = full(got)
        return (_ev_win_layout(win_t), _uq_layout(uq_t), _ukv_layout(ukv_t), wout)

    win0_t = _ev_win_layout(full([win0_all])[0])
    ev_w = [None, None]
    od_w = [None, None]
    later_w = [(t, False) for t in od_shards(0) + ev_shards(1) + od_shards(1)]

    inv_freq = ROPE_BASE ** (-jnp.arange(0, 64, 2, dtype=F32) / 64)
    invf = jnp.tile(inv_freq, 4).reshape(1, LANES)
    sgn = jnp.tile(jnp.concatenate([-jnp.ones((32,), F32), jnp.ones((32,), F32)]), 2).reshape(1, LANES)

    def rope_body(rows, vecs):
        ang = rows[0].astype(F32) * vecs[0]
        return [jnp.cos(ang), jnp.sin(ang) * vecs[1]], []

    (cos_t, sin_t), _ = _rowwise("rope_tables", rope_body, [(positions.reshape(s, 1), 1, 0)], [invf, sgn],
                                 [(LANES, F32), (LANES, F32)])

    saved = []
    xl = x0
    for l in range(DEPTH):
        i = l // 2
        sv = dict(x=xl)

        if l == 0:
            def pre_body(rows, vecs):
                return [_pre_fn(rows[0], *vecs)], []

            (h,), _ = _rowwise("pre0", pre_body, [(xl, D, 0)], [pre_g[0:1], scale[0], shift[0]], [(D, BF16)])
        sv["h"] = h
        if l % 2 == 0:
            if l == 0:
                z, got = _mm("ev_in0", h, win0_t, "nt", out_dtype=BF16, comm=[(t, False) for t in ev_shards(0)[1:]])
                uq0_t, ukv0_t, wout0 = full(got)
                ev_w[0] = (win0_t, _uq_layout(uq0_t), _ukv_layout(ukv0_t), wout0)
                win_t, uq_t, ukv_t, wout = ev_w[0]
            else:
                win_t, uq_t, ukv_t, wout = ev_w[i]
                z = _mm(f"ev_in{l}", h, win_t, "nt", out_dtype=BF16)
            sv["z"] = z
            dec_f, dec_b = ev_dec_f[i:i + 1], ev_dec_b[i:i + 1]

            def prep_body(rows, vecs):
                rq, rk, cq, ckv, kr, cos, sin = rows
                return [_rope(rq, cos, sin), _rope(rk, cos, sin) * RET_SCALE, _rms(cq, vecs[0]), _rms(ckv, vecs[1]),
                        _rope(kr, cos, sin)], []

            (rq_r, rk_r, qn, kvn, krr), _ = _rowwise(
                f"ev_prep{l}", prep_body,
                [(z, 512, 4), (z, 512, 5), (z, 384, 4), (z, 256, 12), (z, LANES, 15), (cos_t, LANES, 0), (sin_t, LANES, 0)],
                [ev_q_norm_g[i:i + 1], ev_kv_norm_g[i:i + 1]],
                [(512, BF16), (512, BF16), (384, BF16), (256, BF16), (LANES, BF16)])
            sv.update(rq_r=rq_r, rk_r=rk_r, qn=qn, kvn=kvn)
            o_f, st_f = _ret_fwd(f"ret_f{l}", rq_r, rk_r, z, dec_f, False, ret_chunk)
            o_b, st_b = _ret_fwd(f"ret_b{l}", rq_r, rk_r, z, dec_b, True, ret_chunk)
            sv.update(o_f=o_f, o_b=o_b, st_f=st_f, st_b=st_b)
            qcat, kcat, v_x, kcat_t = _mla_proj(f"mla_proj{l}", qn, uq_t, kvn, ukv_t, krr, cos_t, sin_t)
            sv["kcat_t"] = kcat_t
            a_mla, lse, got = _fa_fwd(f"fa_fwd{l}", qcat, kcat, v_x, *fa_cfg_f[i], comm=later_w if l == 0 else None)
            if l == 0:
                od_w[0], ev_w[1], od_w[1] = tuple(full(got[0:2])), ev_full(got[2:6]), tuple(full(got[6:8]))
            sv.update(qcat=qcat, kcat=kcat, v_x=v_x, a_mla=a_mla, lse=lse)

            def ev_post_body(rows, vecs):
                of, ob, rg, a, mg = rows
                return [_ev_post_fn(_heads(of + ob), rg, a, mg)], []

            (act,), _ = _rowwise(f"ev_post{l}", ev_post_body,
                                 [(o_f, 512, 0), (o_b, 512, 0), (z, 512, 1), (a_mla, 512, 0), (z, 512, 2)], [], [(D, BF16)])
        else:
            win_t, wout = od_w[i]
            z = _mm(f"od_in{l}", h, win_t, "nt", out_dtype=BF16, bias=b_in_all[i])
            sv["z"] = z

            u2 = _conv_fwd(f"conv{l}", z, dw_w_all[i], dw_b_all[i])
            sv.update(u2=u2)

            def od_post_body(rows, vecs):
                return [_od_post_fn(rows[0], rows[1], vecs[0], vecs[1])], []

            (act,), _ = _rowwise(f"od_post{l}", od_post_body, [(u2, D, 0), (z, D, 2)], [ln_g_all[i], ln_b_all[i]],
                                 [(D, BF16)])
        sv["act"] = act
        y = _mm(f"out{l}", act, wout, "nn", out_dtype=BF16)
        sv["y"] = y

        saved.append(sv)
        if l < DEPTH - 1:
            def post_body(rows, vecs):
                xn = rows[0] + _post_fn(rows[1], vecs[0], vecs[1])
                return [xn, _pre_fn(xn, vecs[2], vecs[3], vecs[4])], []

            (xl, h), _ = _rowwise(f"post{l}", post_body, [(xl, D, 0), (y, D, 0)],
                                  [post_g[l:l + 1], gate[l], pre_g[l + 1:l + 2], scale[l + 1], shift[l + 1]],
                                  [(D, F32), (D, BF16)])
        else:
            def post_body(rows, vecs):
                diff = rows[0] + _post_fn(rows[1], vecs[0], vecs[1]) - rows[2]
                return [diff * (1.0 / D)], [_colsum(diff * diff) * (0.5 / D)]

            (dx,), (loss_lanes,) = _rowwise(f"post{l}", post_body, [(xl, D, 0), (y, D, 0), (tgt, D, 0)],
                                            [post_g[l:l + 1], gate[l]], [(D, F32)], [(1, D)])
    loss = lax.psum(jnp.sum(loss_lanes), ("x", "y", "c"))

    g_pre, g_post, g_mod = [None] * DEPTH, [None] * DEPTH, [None] * DEPTH
    g_dec_f, g_dec_b, g_qn, g_kvn = [None] * 2, [None] * 2, [None] * 2, [None] * 2
    g_b_in, g_dw_w, g_dw_b, g_ln_g, g_ln_b = [None] * 2, [None] * 2, [None] * 2, [None] * 2, [None] * 2
    recv_ev, recv_od = [None] * 2, [None] * 2
    pending = []
    for l in reversed(range(DEPTH)):
        i = l // 2
        sv = saved[l]

        if l == DEPTH - 1:
            def post_bwd_body(rows, vecs):
                yv, dxn = rows
                d_y, dg, d_gate = _post_bwd(yv, dxn, vecs[0], vecs[1])
                return [d_y], [dg, d_gate]

            (dy,), (dpost, dgate) = _rowwise(f"post_bwd{l}", post_bwd_body, [(sv["y"], D, 0), (dx, D, 0)],
                                             [post_g[l:l + 1], gate[l]], [(D, BF16)], [(1, D), (1, D)])
        g_post[l] = dpost
        dgate_l = dgate
        wout = ev_w[i][3] if l % 2 == 0 else od_w[i][1]
        dact = _mm(f"out_dgrad{l}", dy, wout, "nt", out_dtype=BF16)
        d_wout = _mm(f"out_wgrad{l}", sv["act"], dy, "tn", out_dtype=BF16)
        z = sv["z"]
        if l % 2 == 0:
            win_t, uq_t, ukv_t, _ = ev_w[i]
            dec_f, dec_b = ev_dec_f[i:i + 1], ev_dec_b[i:i + 1]

            def ev_post_bwd_body(rows, vecs):
                of, ob, rg, a, mg, da = rows
                do_heads, drg, d_a, dmg = _ev_post_bwd(_heads(of + ob), rg, a, mg, da)
                d_a = d_a * LN2
                deltas = [jnp.sum(dh_ * ah_, axis=1, keepdims=True) for dh_, ah_ in zip(_heads(d_a), _heads(a))]
                return [jnp.concatenate(do_heads, axis=1), drg, d_a, dmg] + deltas, []

            (do_ret, drg, do_mla, dmg, dl0, dl1, dl2, dl3), _ = _rowwise(
                f"ev_post_bwd{l}", ev_post_bwd_body,
                [(sv["o_f"], 512, 0), (sv["o_b"], 512, 0), (z, 512, 1), (sv["a_mla"], 512, 0), (z, 512, 2), (dact, D, 0)],
                [], [(512, BF16), (512, BF16), (512, BF16), (512, BF16)] + [(1, F32)] * 4)
            delta = jnp.stack([dl0, dl1, dl2, dl3]).reshape(MLA_HEADS, 1, s)
            lse = sv["lse"].reshape(MLA_HEADS, 1, s)
            kt = sv["kcat_t"]
            if l == 0:
                pending = pending + [(d_wout.reshape(N_DEV, 128, D), True)]
            dqt, dkcat, dv, got = _fa_bwd(f"fa_bwd{l}", sv["qcat"], sv["kcat"], kt, sv["v_x"], do_mla, lse, delta,
                                          *fa_cfg_b[i], comm=pending)
            recv_od[i] = got[0:2]
            wout_recv = got[2:]
            dqcat = dqt.transpose(1, 3, 0, 2).reshape(s, 1024)

            def mla_prep_bwd_body(rows, vecs):
                dq, dk, dvv, cos, sin = rows
                qs = []
                dkrr = jnp.zeros((dq.shape[0], LANES), F32)
                for hh in range(4):
                    qs += [dq[:, 256 * hh:256 * hh + 128], _rope_t(dq[:, 256 * hh + 128:256 * hh + 256], cos, sin)]
                    dkrr = dkrr + dk[:, 256 * hh + 128:256 * hh + 256]
                return [jnp.concatenate(qs, axis=1) * (MLA_SCALE * LOG2E), jnp.concatenate([dk, dvv * LOG2E], axis=1), dkrr], []

            (dq_pad, dkv_pad, dkrr), _ = _rowwise(
                f"mla_prep_bwd{l}", mla_prep_bwd_body,
                [(dqcat, 1024, 0), (dkcat, 1024, 0), (dv, 512, 0), (cos_t, LANES, 0), (sin_t, LANES, 0)], [],
                [(1024, BF16), (1536, BF16), (LANES, F32)])
            dqn = _mm(f"uq_dgrad{l}", dq_pad, uq_t, "nn")
            d_uq = _mm(f"uq_wgrad{l}", dq_pad, sv["qn"], "tn", out_dtype=BF16)
            dkvn = _mm(f"ukv_dgrad{l}", dkv_pad, ukv_t, "nn")
            d_ukv = _mm(f"ukv_wgrad{l}", dkv_pad, sv["kvn"], "tn", out_dtype=BF16)
            dq_f, dk_f, dv_f, ddec_f = _ret_bwd(f"ret_f_bwd{l}", sv["rq_r"], sv["rk_r"], z, dec_f, sv["st_f"], do_ret,
                                                False, ret_chunk)
            dq_b, dk_b, dv_b, ddec_b = _ret_bwd(f"ret_b_bwd{l}", sv["rq_r"], sv["rk_r"], z, dec_b, sv["st_b"], do_ret,
                                                True, ret_chunk)
            g_dec_f[i], g_dec_b[i] = ddec_f[:, :RET_HEADS], ddec_b[:, :RET_HEADS]

            def prep_bwd_body(rows, vecs):
                cq, ckv, cos, sin, dqf, dqb, dkf, dkb, dvf, dvb, d_qn, d_kvn, d_krr = rows
                _, vjp_q = jax.vjp(_rms, cq, vecs[0])
                dcq, dgq = vjp_q(d_qn)
                _, vjp_kv = jax.vjp(_rms, ckv, vecs[1])
                dckv, dgkv = vjp_kv(d_kvn)
                return [dvf + dvb, dcq, _rope_t(d_krr, cos, sin), _rope_t(dqf + dqb, cos, sin),
                        _rope_t(dkf + dkb, cos, sin) * RET_SCALE, dckv], [dgq, dgkv]

            (drv, dcq, dkr, drq, drk, dckv), (dgq, dgkv) = _rowwise(
                f"ev_prep_bwd{l}", prep_bwd_body,
                [(z, 384, 4), (z, 256, 12), (cos_t, LANES, 0), (sin_t, LANES, 0), (dq_f, 512, 0), (dq_b, 512, 0),
                 (dk_f, 512, 0), (dk_b, 512, 0), (dv_f, 512, 0), (dv_b, 512, 0), (dqn, 384, 0), (dkvn, 256, 0),
                 (dkrr, LANES, 0)],
                [ev_q_norm_g[i:i + 1], ev_kv_norm_g[i:i + 1]],
                [(512, BF16), (384, BF16), (LANES, BF16), (512, BF16), (512, BF16), (256, BF16)], [(1, 384), (1, 256)])
            g_qn[i], g_kvn[i] = dgq, dgkv
            dz = [drv, drg, dmg, dcq, dkr, drq, drk, dckv]
            dz_off = [ZL_EV[nm][0] for nm in ("rv", "rg", "mg", "cq", "kr", "rq", "rk", "ckv")]
            g_rv, g_rg, g_mg, g_cq, g_kr, g_rq, g_rk, g_ckv = _mm_cols_tn(f"in_wgrad{l}", dz, sv["h"])
            d_win = jnp.concatenate([g_rq[128 * hh:128 * hh + 64] for hh in range(4)]
                                    + [g_rk[128 * hh:128 * hh + 64] for hh in range(4)]
                                    + [g_rv, g_rg, g_cq, g_ckv, g_kr[:64], g_mg], axis=0)
            pending = [(d_win.reshape(N_DEV, 344, D), True), (_uq_unlayout(d_uq).reshape(N_DEV, 96, 384), True),
                       (_ukv_unlayout(d_ukv).reshape(N_DEV, 128, 256), True)]
            if l > 0:
                pending.append((d_wout.reshape(N_DEV, 128, D), True))
        else:
            win_t, _ = od_w[i]

            def od_post_bwd_body(rows, vecs):
                u2, gg, da = rows
                du2, dgg, dlg, dlb = _od_post_bwd(u2, gg, da, vecs[0], vecs[1])
                return [du2, dgg], [dlg, dlb, _colsum(dgg)]

            (du2, dg_gate), (dlg, dlb, dbg) = _rowwise(
                f"od_post_bwd{l}", od_post_bwd_body, [(sv["u2"], D, 0), (z, D, 2), (dact, D, 0)],
                [ln_g_all[i], ln_b_all[i]], [(D, BF16), (D, BF16)], [(1, D), (1, D), (1, D)])
            g_ln_g[i], g_ln_b[i] = dlg, dlb
            (d_a, d_b, d_dw, d_dwb, dba, dbb), got = _conv_bwd(f"conv_bwd{l}", z, du2, dw_w_all[i], comm=pending)
            if pending:
                recv_ev[i + 1] = got
            g_dw_w[i], g_dw_b[i] = d_dw[:CONV_K], d_dwb
            g_b_in[i] = jnp.concatenate([dba, dbb, dbg], axis=1)
            dz, dz_off = [d_a, d_b, dg_gate], [0, D, 2 * D]
            d_win = jnp.concatenate(_mm_cols_tn(f"in_wgrad{l}", dz, sv["h"]), axis=0)
            pending = [(d_win.reshape(N_DEV, 384, D), True), (d_wout.reshape(N_DEV, 128, D), True)]
        if l == 0:
            dh, got = _mm_cols_nn(f"in_dgrad{l}", dz, dz_off, win_t, comm=pending)
            recv_ev[0] = list(got) + list(wout_recv)
        else:
            dh = _mm_cols_nn(f"in_dgrad{l}", dz, dz_off, win_t)

        if l > 0:
            def pre_bwd_body(rows, vecs):
                xv, d_h, dxn, yv = rows
                d_x, dg, dsc, dsh = _pre_bwd(xv, d_h, vecs[0], vecs[1])
                d_x = d_x + dxn
                d_y, dgp, d_gate = _post_bwd(yv, d_x, vecs[3], vecs[4])
                return [d_x, d_y], [dg, dsc, dsh, dgp, d_gate]

            (dx, dy), (dpre, dscale, dshift, dpost, dgate) = _rowwise(
                f"pre_bwd{l}", pre_bwd_body, [(sv["x"], D, 0), (dh, D, 0), (dx, D, 0), (saved[l - 1]["y"], D, 0)],
                [pre_g[l:l + 1], scale[l], shift[l], post_g[l - 1:l], gate[l - 1]], [(D, F32), (D, BF16)], [(1, D)] * 5)
        else:
            def pre_bwd_body(rows, vecs):
                xv, d_h, dxn = rows
                d_x, dg, dsc, dsh = _pre_bwd(xv, d_h, vecs[0], vecs[1])
                return [d_x + dxn], [dg, dsc, dsh]

            (dx,), (dpre, dscale, dshift) = _rowwise(f"pre_bwd{l}", pre_bwd_body, [(sv["x"], D, 0), (dh, D, 0), (dx, D, 0)],
                                                     [pre_g[l:l + 1], scale[l], shift[l]], [(D, F32)], [(1, D)] * 3)
        g_pre[l] = dpre
        g_mod[l] = jnp.concatenate([dshift, dscale, dgate_l], axis=1)

    grad_x = dx.reshape(1, s, D)

    end_parts = [jnp.concatenate(g_mod, axis=0), jnp.concatenate(g_pre, axis=0), jnp.concatenate(g_post, axis=0),
                 jnp.concatenate(g_dec_f, axis=0), jnp.concatenate(g_dec_b, axis=0), jnp.concatenate(g_qn, axis=0),
                 jnp.concatenate(g_kvn, axis=0), jnp.concatenate(g_b_in, axis=0), jnp.stack(g_dw_w),
                 jnp.concatenate(g_dw_b, axis=0), jnp.concatenate(g_ln_g, axis=0), jnp.concatenate(g_ln_b, axis=0)]
    end_shapes = [p.shape for p in end_parts]
    end_sizes = [int(np.prod(sh)) for sh in end_shapes]
    end_len = -(-sum(end_sizes) // 1024) * 1024
    end_vec = jnp.concatenate([p.reshape(-1) for p in end_parts] + [jnp.zeros((end_len - sum(end_sizes),), F32)])
    end_all = _exchange("gather_end", [end_vec.reshape(-1, LANES)], False)[0].reshape(N_DEV, end_len)
    eo = np.cumsum([0] + end_sizes)
    ends = [end_all[:, eo[j]:eo[j + 1]].reshape((N_DEV,) + tuple(end_shapes[j])) for j in range(len(end_parts))]
    (dmod_all, pre_all, post_all, decf_all, decb_all, qn_all, kvn_all, bin_all, dww_all, dwb_all, lng_all,
     lnb_all) = ends

    def pack_rep(*ts):
        lead = ts[0].ndim - 2
        return jnp.concatenate([t.reshape(t.shape[:lead] + (-1,)) for t in ts], axis=-1)

    rep_sizes = [DEPTH * 3 * D, DEPTH * D, DEPTH * D, 8, 8, 2 * 384, 2 * 256]
    rep_len = -(-sum(rep_sizes) // 1024) * 1024
    rep_pad = rep_len - sum(rep_sizes)

    def rep_rows(flat):
        padz = jnp.zeros(flat.shape[:-1] + (rep_pad,), F32)
        return jnp.concatenate([flat, padz], axis=-1).reshape(flat.shape[:-1] + (rep_len // LANES, LANES))

    rep_w = rep_rows(pack_rep(ada_b, pre_g, post_g, ev_dec_f, ev_dec_b, ev_q_norm_g, ev_kv_norm_g))
    rep_m = rep_rows(pack_rep(m_ada_b, m_pre_g, m_post_g, m_ev_dec_f, m_ev_dec_b, m_ev_q_norm_g, m_ev_kv_norm_g))
    rep_v = rep_rows(pack_rep(v_ada_b, v_pre_g, v_post_g, v_ev_dec_f, v_ev_dec_b, v_ev_q_norm_g, v_ev_kv_norm_g))
    rep_g = rep_rows(pack_rep(dmod_all, pre_all, post_all, decf_all, decb_all, qn_all, kvn_all))
    rep_out = _adamw("adamw_rep", rep_w, rep_m, rep_v, rep_g)
    ro = np.cumsum([0] + rep_sizes)
    rep_shapes = [(DEPTH, 3 * D), (DEPTH, D), (DEPTH, D), (2, 4), (2, 4), (2, 384), (2, 256)]

    def unpack_rep(t):
        flat = t.reshape(-1)
        return [flat[ro[j]:ro[j + 1]].reshape(rep_shapes[j]) for j in range(len(rep_shapes))]

    rep_res = [unpack_rep(t) for t in rep_out]

    def my_cols(t, width):
        return lax.dynamic_slice_in_dim(t, me * width, width, axis=t.ndim - 1)

    def vec_adamw(name, w, m, v, g_all, width):
        g = my_cols(g_all, width)
        r = _adamw(name, w.reshape(-1, width), m.reshape(-1, width), v.reshape(-1, width),
                   g.reshape(N_DEV, -1, width))
        return [t.reshape(w.shape) for t in r]

    res_b_in = vec_adamw("adamw_b_in", od_b_in, m_od_b_in, v_od_b_in, bin_all, 384)
    res_dw_w = vec_adamw("adamw_dw_w", od_dw_w, m_od_dw_w, v_od_dw_w, dww_all, LANES)
    res_dw_b = vec_adamw("adamw_dw_b", od_dw_b, m_od_dw_b, v_od_dw_b, dwb_all, LANES)
    res_ln_g = vec_adamw("adamw_ln_g", od_ln_g, m_od_ln_g, v_od_ln_g, lng_all, LANES)
    res_ln_b = vec_adamw("adamw_ln_b", od_ln_b, m_od_ln_b, v_od_ln_b, lnb_all, LANES)

    dmod_mine = my_cols(dmod_all, 384).transpose(1, 0, 2)
    dmod_pad = jnp.concatenate([dmod_mine, jnp.zeros((DEPTH, LANES - N_DEV, 384), F32)], axis=1)
    c_all_t = jnp.concatenate([c_all.T, jnp.zeros((D, LANES - N_DEV), F32)], axis=1)
    g_ada_w = _ada_w_grad("ada_w_grad", c_all_t, dmod_pad)
    res_ada_w = [t.reshape(ada_w.shape) for t in
                 _adamw("adamw_ada_w", ada_w.reshape(-1, 384), m_ada_w.reshape(-1, 384), v_ada_w.reshape(-1, 384),
                        g_ada_w.reshape(-1, 384))]

    ev_sh = [[_sum_parts(f"sum_g_ev{i}_{j}", r) for j, r in enumerate(recv_ev[i])] for i in range(2)]
    od_sh = [[_sum_parts(f"sum_g_od{i}_{j}", r) for j, r in enumerate(recv_od[i])] for i in range(2)]

    def mat_adamw(name, w, m, v, g):
        r = _adamw(name, w.reshape(-1, w.shape[-1]), m.reshape(-1, w.shape[-1]), v.reshape(-1, w.shape[-1]),
                   g.reshape(-1, w.shape[-1]))
        return [t.reshape(w.shape) for t in r]

    res_ev_w_in = mat_adamw("adamw_ev_w_in", ev_w_in, m_ev_w_in, v_ev_w_in, jnp.stack([ev_sh[i][0].T for i in range(2)]))
    res_ev_w_uq = mat_adamw("adamw_ev_w_uq", ev_w_uq, m_ev_w_uq, v_ev_w_uq, jnp.stack([ev_sh[i][1].T for i in range(2)]))
    res_ev_w_ukv = mat_adamw("adamw_ev_w_ukv", ev_w_ukv, m_ev_w_ukv, v_ev_w_ukv,
                             jnp.stack([ev_sh[i][2].T for i in range(2)]))
    res_ev_w_out = mat_adamw("adamw_ev_w_out", ev_w_out, m_ev_w_out, v_ev_w_out, jnp.stack([ev_sh[i][3] for i in range(2)]))
    res_od_w_in = mat_adamw("adamw_od_w_in", od_w_in, m_od_w_in, v_od_w_in, jnp.stack([od_sh[i][0].T for i in range(2)]))
    res_od_w_out = mat_adamw("adamw_od_w_out", od_w_out, m_od_w_out, v_od_w_out, jnp.stack([od_sh[i][1] for i in range(2)]))

    per_weight = [res_ada_w] + [[rep_res[t][j] for t in range(4)] for j in range(3)]
    per_weight += [res_ev_w_in, [rep_res[t][3] for t in range(4)], [rep_res[t][4] for t in range(4)],
                   [rep_res[t][5] for t in range(4)], res_ev_w_uq, [rep_res[t][6] for t in range(4)], res_ev_w_ukv,
                   res_ev_w_out, res_od_w_in, res_b_in, res_dw_w, res_dw_b, res_ln_g, res_ln_b, res_od_w_out]
    outs = [loss, grad_x]
    for t in range(4):
        outs += [pw[t] for pw in per_weight]
    return tuple(outs)
```

```python
import functools

import numpy as np
import jax
import jax.numpy as jnp
from jax import lax
from jax.experimental import pallas as pl
from jax.experimental.pallas import tpu as pltpu

F32 = jnp.float32
BF16 = jnp.bfloat16
MESH = pl.DeviceIdType.MESH

N_DEV = 8
D = 1024
DEPTH = 4
EPS = 1e-6
RET_HEADS = 4
MLA_HEADS = 4
RET_SCALE = 64 ** -0.5
MLA_SCALE = 192 ** -0.5
LOG2E = 1.4426950408889634
LN2 = 1.0 / LOG2E
CONV_K = 31
CONV_HALO = 16
ROPE_BASE = 10000.0

ADAM_LR = 0.001
ADAM_B1 = 0.9
ADAM_B2 = 0.999
ADAM_EPS = 1e-08
ADAM_WD = 0.01
ADAM_STEP = 10

LANES = 128
VMEM_LIMIT = 48 * 1024 * 1024

ZL_EV = dict(rv=(0, 512), rg=(512, 512), mg=(1024, 512), cq=(1536, 384), kr=(1920, 128),
             rq=(2048, 512), rk=(2560, 512), ckv=(3072, 256))
ZW_EV = 3328
ZW_OD = 3072


def _cparams(sem, vmem=VMEM_LIMIT):
    return pltpu.CompilerParams(dimension_semantics=sem, vmem_limit_bytes=vmem)


def _pick(n, prefs):
    for p in prefs:
        if n % p == 0:
            return p
    return n


def _sigmoid(x):
    return 0.5 * (jnp.tanh(0.5 * x) + 1.0)


def _silu(x):
    return x * _sigmoid(x)


def _rms(x, g):
    return x * lax.rsqrt(jnp.mean(x * x, axis=-1, keepdims=True) + EPS) * g


def _log_sigmoid(x):
    return jnp.minimum(x, 0.0) - jnp.log(1.0 + jnp.exp(jnp.minimum(x, -x)))


def _tile_lanes(t, width):
    reps = width // t.shape[1]
    return t if reps == 1 else jnp.concatenate([t] * reps, axis=1)


def _rot_half(x):
    w = x.shape[1]
    lane = lax.broadcasted_iota(jnp.int32, x.shape, 1)
    first = jnp.bitwise_and(lane, 63) < 32
    return jnp.where(first, pltpu.roll(x, w - 32, 1), pltpu.roll(x, 32, 1))


def _rope(x, cos, sin):
    w = x.shape[1]
    return x * _tile_lanes(cos, w) + _rot_half(x) * _tile_lanes(sin, w)


def _rope_t(dy, cos, sin):
    w = dy.shape[1]
    return dy * _tile_lanes(cos, w) + _rot_half(dy * _tile_lanes(sin, w))


_DN = {"nn": (((1,), (0,)), ((), ())), "nt": (((1,), (1,)), ((), ())), "tn": (((0,), (0,)), ((), ()))}


def _dot(a, b, mode):
    return lax.dot_general(a.astype(BF16), b.astype(BF16), _DN[mode], preferred_element_type=F32)


@functools.partial(jax.custom_vjp, nondiff_argnums=(2,))
def _bdot(a, b, mode):
    return _dot(a, b, mode)


def _bdot_fwd(a, b, mode):
    return _dot(a, b, mode), (a, b)


def _bdot_bwd(mode, res, g):
    a, b = res
    if mode == "nn":
        return _dot(g, b, "nt"), _dot(a, g, "tn")
    if mode == "nt":
        return _dot(g, b, "nn"), _dot(g, a, "tn")
    return _dot(b, g, "nt"), _dot(a, g, "nn")


_bdot.defvjp(_bdot_fwd, _bdot_bwd)


def _rowwise(name, body, row_ins, vec_ins, row_outs, red_outs=(), tile=512):
    s = row_ins[0][0].shape[0]
    tile = min(tile, s)
    nr, nv, no = len(row_ins), len(vec_ins), len(row_outs)

    def kern(*refs):
        rows = [r[...].astype(F32) if r.dtype == BF16 else r[...] for r in refs[:nr]]
        vecs = [r[...] for r in refs[nr:nr + nv]]
        outs, reds = body(rows, vecs)
        for r, o in zip(refs[nr + nv:nr + nv + no], outs):
            r[...] = o.astype(r.dtype)
        red_refs = refs[nr + nv + no:]
        if red_refs:
            @pl.when(pl.program_id(0) == 0)
            def _():
                for r in red_refs:
                    r[...] = jnp.zeros(r.shape, r.dtype)
            for r, v in zip(red_refs, reds):
                r[...] += v

    in_specs = [pl.BlockSpec((tile, w), (lambda i, cb=cb: (i, cb))) for (_, w, cb) in row_ins]
    in_specs += [pl.BlockSpec(v.shape, (lambda i, nd=v.ndim: (0,) * nd)) for v in vec_ins]
    out_specs = [pl.BlockSpec((tile, w), lambda i: (i, 0)) for (w, _) in row_outs]
    out_specs += [pl.BlockSpec(sh, lambda i: (0, 0)) for sh in red_outs]
    out_shape = [jax.ShapeDtypeStruct((s, w), dt) for (w, dt) in row_outs]
    out_shape += [jax.ShapeDtypeStruct(sh, F32) for sh in red_outs]
    res = pl.pallas_call(
        kern, name=name, grid=(s // tile,), in_specs=in_specs, out_specs=out_specs, out_shape=out_shape,
        compiler_params=_cparams(("arbitrary",)),
    )(*[a for (a, _, _) in row_ins], *vec_ins)
    return res[:no], res[no:]


def _mm(name, a, b, mode, out_dtype=F32, bias=None, comm=None):
    if mode == "tn":
        k, m = a.shape
        n = b.shape[1]
        tm = m if m <= 1664 else m // 2
        tk = min(k, 2048 if tm * n <= 1024 * 1024 else 1024)
        nk = k // tk

        def kern(a_ref, b_ref, o_ref, acc_ref):
            kk = pl.program_id(1)
            part = _dot(a_ref[...], b_ref[...], "tn")

            @pl.when(kk == 0)
            def _():
                acc_ref[...] = part

            @pl.when(kk > 0)
            def _():
                acc_ref[...] += part

            @pl.when(kk == nk - 1)
            def _():
                o_ref[...] = acc_ref[...].astype(o_ref.dtype)

        return pl.pallas_call(
            kern, name=name, grid=(m // tm, nk),
            in_specs=[pl.BlockSpec((tk, tm), lambda i, kk: (kk, i)),
                      pl.BlockSpec((tk, n), lambda i, kk: (kk, 0))],
            out_specs=pl.BlockSpec((tm, n), lambda i, kk: (i, 0)),
            out_shape=jax.ShapeDtypeStruct((m, n), out_dtype),
            scratch_shapes=[pltpu.VMEM((tm, n), F32)],
            compiler_params=_cparams(("parallel", "arbitrary")),
        )(a, b)

    m, k = a.shape
    n = b.shape[1] if mode == "nn" else b.shape[0]
    tm = min(m, 1024)
    tn = n if n <= 1664 else n // 2
    has_bias = bias is not None

    def kern(*refs):
        a_ref, b_ref = refs[0], refs[1]
        o_ref = refs[-1]
        r = _dot(a_ref[...], b_ref[...], mode)
        if has_bias:
            r = r + refs[2][...]
        o_ref[...] = r.astype(o_ref.dtype)

    b_spec = (pl.BlockSpec((k, tn), lambda i, j: (0, j)) if mode == "nn"
              else pl.BlockSpec((tn, k), lambda i, j: (j, 0)))
    in_specs = [pl.BlockSpec((tm, k), lambda i, j: (i, 0)), b_spec]
    args = [a, b]
    if has_bias:
        in_specs.append(pl.BlockSpec((1, tn), lambda i, j: (0, j)))
        args.append(bias)
    outs, got = _call(kern, name=name, grid=(m // tm, n // tn), in_specs=in_specs,
                      out_specs=[pl.BlockSpec((tm, tn), lambda i, j: (i, j))],
                      out_shape=[jax.ShapeDtypeStruct((m, n), out_dtype)], args=args, sem=("parallel", "parallel"),
                      comm=comm)
    return (outs[0], got) if comm else outs[0]


def _mm_cols_nn(name, pieces, offsets, b, comm=None):
    m = pieces[0].shape[0]
    n = b.shape[1]
    tm = min(m, 1024)
    np_ = len(pieces)

    def kern(*refs):
        acc = _dot(refs[0][...], refs[np_][...], "nn")
        for p in range(1, np_):
            acc = acc + _dot(refs[p][...], refs[np_ + p][...], "nn")
        refs[2 * np_][...] = acc.astype(BF16)

    in_specs = [pl.BlockSpec((tm, a.shape[1]), lambda i: (i, 0)) for a in pieces]
    in_specs += [pl.BlockSpec((a.shape[1], n), (lambda i, r=off // a.shape[1]: (r, 0))) for a, off in zip(pieces, offsets)]
    outs, got = _call(kern, name=name, grid=(m // tm,), in_specs=in_specs,
                      out_specs=[pl.BlockSpec((tm, n), lambda i: (i, 0))],
                      out_shape=[jax.ShapeDtypeStruct((m, n), BF16)], args=list(pieces) + [b] * np_, sem=("parallel",),
                      comm=comm)
    return (outs[0], got) if comm else outs[0]


def _mm_cols_tn(name, pieces, b):
    k, n = b.shape
    tk = min(k, 1024)
    nk = k // tk
    np_ = len(pieces)

    def kern(*refs):
        b_ref = refs[np_]
        o_refs, acc_refs = refs[np_ + 1:2 * np_ + 1], refs[2 * np_ + 1:]
        kk = pl.program_id(0)

        @pl.when(kk == 0)
        def _():
            for r in acc_refs:
                r[...] = jnp.zeros(r.shape, F32)

        bb = b_ref[...]
        for p in range(np_):
            acc_refs[p][...] += _dot(refs[p][...], bb, "tn")

        @pl.when(kk == nk - 1)
        def _():
            for o, r in zip(o_refs, acc_refs):
                o[...] = r[...].astype(o.dtype)

    return pl.pallas_call(
        kern, name=name, grid=(nk,),
        in_specs=[pl.BlockSpec((tk, a.shape[1]), lambda kk: (kk, 0)) for a in pieces] + [pl.BlockSpec((tk, n), lambda kk: (kk, 0))],
        out_specs=[pl.BlockSpec((a.shape[1], n), lambda kk: (0, 0), pipeline_mode=pl.Buffered(1)) for a in pieces],
        out_shape=[jax.ShapeDtypeStruct((a.shape[1], n), BF16) for a in pieces],
        scratch_shapes=[pltpu.VMEM((a.shape[1], n), F32) for a in pieces],
        compiler_params=_cparams(("arbitrary",)),
    )(*pieces, b)


def _mla_proj(name, qn, uq_t, kvn, ukv_t, krr, cos, sin):
    s = qn.shape[0]
    tm = min(s, 1024)

    def kern(qn_ref, uq_ref, kvn_ref, ukv_ref, kr_ref, cos_ref, sin_ref, q_out, k_out, v_out, kt_out):
        qp = _dot(qn_ref[...], uq_ref[...], "nt")
        kv = _dot(kvn_ref[...], ukv_ref[...], "nt")
        c, sn = cos_ref[...], sin_ref[...]
        kr_r = kr_ref[...].astype(F32)
        ones = jnp.ones((tm, LANES), F32)
        qs, ks, vs = [], [], []
        for hh in range(MLA_HEADS):
            qs += [qp[:, 256 * hh:256 * hh + 128], _rope(qp[:, 256 * hh + 128:256 * hh + 256], c, sn)]
            ks += [kv[:, 256 * hh:256 * hh + 128], kr_r]
            vs += [kv[:, 1024 + LANES * hh:1024 + LANES * hh + LANES], ones]
        q_out[...] = (jnp.concatenate(qs, axis=1) * (MLA_SCALE * LOG2E)).astype(BF16)
        k_out[...] = jnp.concatenate(ks, axis=1).astype(BF16)
        v_out[...] = jnp.concatenate(vs, axis=1).astype(BF16)
        for hh in range(MLA_HEADS):
            kt_out[hh] = jnp.concatenate([ks[2 * hh], ks[2 * hh + 1]], axis=1).T.astype(BF16)

    row = lambda w: pl.BlockSpec((tm, w), lambda i: (i, 0))
    whole = lambda a: pl.BlockSpec(a.shape, lambda i: (0, 0))
    return pl.pallas_call(
        kern, name=name, grid=(s // tm,),
        in_specs=[row(qn.shape[1]), whole(uq_t), row(kvn.shape[1]), whole(ukv_t), row(LANES), row(LANES), row(LANES)],
        out_specs=[row(1024)] * 3 + [pl.BlockSpec((MLA_HEADS, 256, tm), lambda i: (0, 0, i))],
        out_shape=[jax.ShapeDtypeStruct((s, 1024), BF16)] * 3 + [jax.ShapeDtypeStruct((MLA_HEADS, 256, s), BF16)],
        compiler_params=_cparams(("parallel",)),
    )(qn, uq_t, kvn, ukv_t, krr, cos, sin)


def _peers():
    mx, my, mc = lax.axis_index("x"), lax.axis_index("y"), lax.axis_index("c")
    me = 4 * mx + 2 * my + mc
    out = []
    for k in range(1, N_DEV):
        px = 1 - mx if (k >> 2) & 1 else mx
        py = 1 - my if (k >> 1) & 1 else my
        pc = 1 - mc if k & 1 else mc
        out.append((k, (px, py, pc), 4 * px + 2 * py + pc))
    return me, out


def _xchg_copies(x_refs, out_refs, scatter, send_sems, recv_sems, local_sems):
    me, peers = _peers()
    local, out, arrive = [], [], []
    for a, (x, o, sc) in enumerate(zip(x_refs, out_refs, scatter)):
        mine = x.at[me] if sc else x
        local.append(pltpu.make_async_copy(mine, o.at[me], local_sems.at[a]))
        for k, dev, p in peers:
            out.append(pltpu.make_async_remote_copy(
                src_ref=x.at[p] if sc else x, dst_ref=o.at[me],
                send_sem=send_sems.at[a, k - 1], recv_sem=recv_sems.at[a, k - 1],
                device_id=dev, device_id_type=MESH))
            arrive.append(pltpu.make_async_remote_copy(
                src_ref=mine, dst_ref=o.at[p],
                send_sem=send_sems.at[a, k - 1], recv_sem=recv_sems.at[a, k - 1],
                device_id=dev, device_id_type=MESH))
    return local, out, arrive


def _xchg_start(*args):
    local, out, _ = _xchg_copies(*args)
    for cp in local + out:
        cp.start()


def _xchg_wait(*args):
    local, out, arrive = _xchg_copies(*args)
    for cp in out:
        cp.wait_send()
    for cp in arrive:
        cp.wait_recv()
    for cp in local:
        cp.wait()


def _call(kern, *, name, grid, in_specs, out_specs, out_shape, args, sem, scratch_shapes=(), vmem=VMEM_LIMIT,
          comm=None):
    if not comm:
        outs = pl.pallas_call(kern, name=name, grid=grid, in_specs=in_specs, out_specs=out_specs, out_shape=out_shape,
                              scratch_shapes=list(scratch_shapes), compiler_params=_cparams(sem, vmem))(*args)
        return outs, []
    n, ni, no, ns = len(comm), len(in_specs), len(out_specs), len(scratch_shapes)
    xs = [x for x, _ in comm]
    scatter = [sc for _, sc in comm]

    def body(*refs):
        ins, x_refs = refs[:ni], refs[ni:ni + n]
        outs, out_refs = refs[ni + n:ni + n + no], refs[ni + n + no:ni + 2 * n + no]
        scr = refs[ni + 2 * n + no:ni + 2 * n + no + ns]
        sems = refs[ni + 2 * n + no + ns:]
        ids = [pl.program_id(d) for d in range(len(grid))]
        first = functools.reduce(jnp.logical_and, [i == 0 for i in ids])
        last = functools.reduce(jnp.logical_and, [i == g - 1 for i, g in zip(ids, grid)])

        @pl.when(first)
        def _():
            _xchg_start(x_refs, out_refs, scatter, *sems)

        kern(*ins, *outs, *scr)

        @pl.when(last)
        def _():
            _xchg_wait(x_refs, out_refs, scatter, *sems)

    any_spec = pl.BlockSpec(memory_space=pl.ANY)
    res = pl.pallas_call(
        body, name=name, grid=grid,
        in_specs=list(in_specs) + [any_spec] * n, out_specs=list(out_specs) + [any_spec] * n,
        out_shape=list(out_shape) + [jax.ShapeDtypeStruct((N_DEV,) + tuple(x.shape[1:] if sc else x.shape), x.dtype)
                                     for x, sc in comm],
        scratch_shapes=list(scratch_shapes) + [pltpu.SemaphoreType.DMA((n, N_DEV - 1)),
                                               pltpu.SemaphoreType.DMA((n, N_DEV - 1)), pltpu.SemaphoreType.DMA((n,))],
        compiler_params=pltpu.CompilerParams(dimension_semantics=("arbitrary",) * len(grid), vmem_limit_bytes=vmem,
                                             has_side_effects=True),
    )(*args, *xs)
    return res[:no], res[no:]


def _exchange(name, xs, scatter):
    def nothing():
        pass

    return _call(nothing, name=name, grid=(1,), in_specs=[], out_specs=[], out_shape=[], args=[], sem=("arbitrary",),
                 comm=[(x, scatter) for x in xs])[1]


def _sum_parts(name, x):
    p, r, c = x.shape
    tr = r if r * c * p * x.dtype.itemsize <= (8 << 20) else _pick(r, (256, 128, 64, 16))

    def kern(x_ref, o_ref):
        acc = x_ref[0].astype(F32)
        for i in range(1, p):
            acc = acc + x_ref[i].astype(F32)
        o_ref[...] = acc

    return pl.pallas_call(
        kern, name=name, grid=(r // tr,),
        in_specs=[pl.BlockSpec((p, tr, c), lambda i: (0, i, 0))],
        out_specs=pl.BlockSpec((tr, c), lambda i: (i, 0)),
        out_shape=jax.ShapeDtypeStruct((r, c), F32),
        compiler_params=_cparams(("parallel",)),
    )(x)


def _adamw(name, w, m, v, g):
    r, c = w.shape
    parts = g.shape[0] if g.ndim == 3 else 0
    tr = 512 if (r > 512 and r % 512 == 0) else r

    def kern(w_ref, m_ref, v_ref, g_ref, go_ref, d_ref, mo_ref, vo_ref):
        if parts:
            gg = g_ref[0]
            for i in range(1, parts):
                gg = gg + g_ref[i]
        else:
            gg = g_ref[...]
        mm = ADAM_B1 * m_ref[...] + (1.0 - ADAM_B1) * gg
        vv = ADAM_B2 * v_ref[...] + (1.0 - ADAM_B2) * (gg * gg)
        m_hat = mm / (1.0 - ADAM_B1 ** ADAM_STEP)
        v_hat = vv / (1.0 - ADAM_B2 ** ADAM_STEP)
        go_ref[...] = gg
        d_ref[...] = -ADAM_LR * (m_hat / (jnp.sqrt(v_hat) + ADAM_EPS) + ADAM_WD * w_ref[...])
        mo_ref[...] = mm
        vo_ref[...] = vv

    spec = pl.BlockSpec((tr, c), lambda i: (i, 0))
    gspec = pl.BlockSpec((parts, tr, c), lambda i: (0, i, 0)) if parts else spec
    sh = jax.ShapeDtypeStruct((r, c), F32)
    return pl.pallas_call(
        kern, name=name, grid=(r // tr,), in_specs=[spec, spec, spec, gspec],
        out_specs=[spec] * 4, out_shape=[sh] * 4,
        compiler_params=_cparams(("parallel",)),
    )(w, m, v, g)


def _ret_tables(dec_cc, dec_cd, dec_dd, reverse):
    c = dec_cc.shape[0]
    row = lax.broadcasted_iota(jnp.int32, (c, c), 0).astype(F32)
    col = lax.broadcasted_iota(jnp.int32, (c, c), 1).astype(F32)
    pos = lax.broadcasted_iota(jnp.int32, (c, LANES), 0).astype(F32)
    if reverse:
        diff, mask = col - row, col > row
        q_exp, k_exp = c - pos, pos
    else:
        diff, mask = row - col, row >= col
        q_exp, k_exp = pos + 1.0, c - 1.0 - pos
    decay = jnp.where(mask, jnp.exp(_log_sigmoid(dec_cc) * jnp.maximum(diff, 0.0)), 0.0)
    lam_cd = _log_sigmoid(dec_cd)
    return decay, jnp.exp(lam_cd * q_exp), jnp.exp(lam_cd * k_exp), jnp.exp(_log_sigmoid(dec_dd) * float(c))


def _ret_chunk(q, k, v, st, decay, qw, kw, sd):
    scores = _bdot(q, k, "nt") * decay
    o = _bdot(scores, v, "nn") + _bdot(q * qw, st, "nn")
    st_new = st * sd + _bdot(k * kw, v, "tn")
    return o, st_new


def _ret_dec(dec_ref, h, c):
    d = dec_ref[:, h:h + 1]
    return (jnp.broadcast_to(d, (c, c)), jnp.broadcast_to(d, (c, LANES)), jnp.broadcast_to(d, (LANES, LANES)))


def _ret_per_step(n_chunks):
    return 2 if (n_chunks % 2 == 0 and n_chunks >= 4) else 1


def _ret_table_scratch(c):
    return [pltpu.VMEM((RET_HEADS, c, c), F32), pltpu.VMEM((RET_HEADS, c, LANES), F32),
            pltpu.VMEM((RET_HEADS, c, LANES), F32), pltpu.VMEM((RET_HEADS, LANES, LANES), F32)]


def _ret_fwd(name, q, k, z, dec, reverse, chunk):
    s = q.shape[0]
    chunk = min(chunk, s)
    per = _ret_per_step(s // chunk)
    n = s // (chunk * per)
    cmap = (lambda i: (n - 1 - i, 0)) if reverse else (lambda i: (i, 0))
    smap = (lambda i: (n - 1 - i, 0, 0, 0)) if reverse else (lambda i: (i, 0, 0, 0))

    def kern(q_ref, k_ref, v_ref, dec_ref, o_ref, st_out_ref, st_ref, *tab_refs):
        @pl.when(pl.program_id(0) == 0)
        def _():
            st_ref[...] = jnp.zeros(st_ref.shape, F32)
            for h in range(RET_HEADS):
                for r, t in zip(tab_refs, _ret_tables(*_ret_dec(dec_ref, h, chunk), reverse)):
                    r[h] = t

        for c2 in (range(per - 1, -1, -1) if reverse else range(per)):
            rows = pl.ds(c2 * chunk, chunk)
            for h in range(RET_HEADS):
                sl = slice(LANES * h, LANES * (h + 1))
                st = st_ref[h]
                st_out_ref[c2, h] = st
                o, st_new = _ret_chunk(q_ref[rows, sl].astype(F32), k_ref[rows, sl].astype(F32),
                                       v_ref[rows, sl].astype(F32), st, *[r[h] for r in tab_refs])
                o_ref[rows, sl] = o.astype(BF16)
                st_ref[h] = st_new

    blk = chunk * per
    return pl.pallas_call(
        kern, name=name, grid=(n,),
        in_specs=[pl.BlockSpec((blk, 512), cmap), pl.BlockSpec((blk, 512), cmap),
                  pl.BlockSpec((blk, 512), cmap), pl.BlockSpec((1, RET_HEADS), lambda i: (0, 0))],
        out_specs=[pl.BlockSpec((blk, 512), cmap), pl.BlockSpec((per, RET_HEADS, LANES, LANES), smap)],
        out_shape=[jax.ShapeDtypeStruct((s, 512), BF16), jax.ShapeDtypeStruct((n * per, RET_HEADS, LANES, LANES), F32)],
        scratch_shapes=[pltpu.VMEM((RET_HEADS, LANES, LANES), F32)] + _ret_table_scratch(chunk),
        compiler_params=_cparams(("arbitrary",)),
    )(q, k, z, dec)


def _ret_bwd(name, q, k, z, dec, states, do, reverse, chunk):
    s = q.shape[0]
    chunk = min(chunk, s)
    per = _ret_per_step(s // chunk)
    n = s // (chunk * per)
    cmap = (lambda i: (i, 0)) if reverse else (lambda i: (n - 1 - i, 0))
    smap = (lambda i: (i, 0, 0, 0)) if reverse else (lambda i: (n - 1 - i, 0, 0, 0))

    def kern(q_ref, k_ref, v_ref, dec_ref, st_in_ref, do_ref, dq_ref, dk_ref, dv_ref, ddec_ref, dst_ref, *scr):
        tab_refs, gtab_refs = scr[:4], scr[4:]
        step = pl.program_id(0)

        @pl.when(step == 0)
        def _():
            dst_ref[...] = jnp.zeros(dst_ref.shape, F32)
            for r in gtab_refs:
                r[...] = jnp.zeros(r.shape, F32)
            for h in range(RET_HEADS):
                for r, t in zip(tab_refs, _ret_tables(*_ret_dec(dec_ref, h, chunk), reverse)):
                    r[h] = t

        for c2 in (range(per) if reverse else range(per - 1, -1, -1)):
            rows = pl.ds(c2 * chunk, chunk)
            for h in range(RET_HEADS):
                sl = slice(LANES * h, LANES * (h + 1))
                _, vjp = jax.vjp(_ret_chunk, q_ref[rows, sl].astype(F32), k_ref[rows, sl].astype(F32),
                                 v_ref[rows, sl].astype(F32), st_in_ref[c2, h], *[r[h] for r in tab_refs])
                grads = vjp((do_ref[rows, sl].astype(F32), dst_ref[h]))
                dq_ref[rows, sl] = grads[0].astype(BF16)
                dk_ref[rows, sl] = grads[1].astype(BF16)
                dv_ref[rows, sl] = grads[2].astype(BF16)
                dst_ref[h] = grads[3]
                for r, g in zip(gtab_refs, grads[4:]):
                    r[h] += g

        @pl.when(step == n - 1)
        def _():
            lane = lax.broadcasted_iota(jnp.int32, (1, LANES), 1)
            ddec = jnp.zeros((1, LANES), F32)
            for h in range(RET_HEADS):
                _, vjp_t = jax.vjp(functools.partial(_ret_tables, reverse=reverse), *_ret_dec(dec_ref, h, chunk))
                parts = vjp_t(tuple(r[h] for r in gtab_refs))
                tot = sum(jnp.sum(jnp.sum(p, axis=1, keepdims=True), axis=0, keepdims=True) for p in parts)
                ddec = ddec + jnp.where(lane == h, tot, 0.0)
            ddec_ref[...] = ddec

    cspec = pl.BlockSpec((chunk * per, 512), cmap)
    return pl.pallas_call(
        kern, name=name, grid=(n,),
        in_specs=[cspec, cspec, cspec, pl.BlockSpec((1, RET_HEADS), lambda i: (0, 0)),
                  pl.BlockSpec((per, RET_HEADS, LANES, LANES), smap), cspec],
        out_specs=[cspec, cspec, cspec, pl.BlockSpec((1, LANES), lambda i: (0, 0))],
        out_shape=[jax.ShapeDtypeStruct((s, 512), BF16)] * 3 + [jax.ShapeDtypeStruct((1, LANES), F32)],
        scratch_shapes=[pltpu.VMEM((RET_HEADS, LANES, LANES), F32)] + _ret_table_scratch(chunk) * 2,
        compiler_params=_cparams(("arbitrary",)),
    )(q, k, z, dec, states, do)


def _fa_fwd(name, q, k, vx, tq, tk, nsub, comm=None):
    s = q.shape[0]
    tq, tk = min(tq, s), min(tk, s)
    nk = s // tk
    sq = tq // nsub

    def kern(q_ref, k_ref, v_ref, o_ref, lse_ref, m_ref, acc_ref):
        j = pl.program_id(2)

        @pl.when(j == 0)
        def _():
            m_ref[...] = jnp.full(m_ref.shape, -jnp.inf, F32)
            acc_ref[...] = jnp.zeros(acc_ref.shape, F32)

        kb, vb = k_ref[...], v_ref[...]
        for c in range(nsub):
            rows = pl.ds(c * sq, sq)
            sc = _dot(q_ref[rows, :], kb, "nt")
            m_prev = m_ref[rows, :]
            m_new = jnp.maximum(m_prev, jnp.max(sc, axis=1, keepdims=True))
            alpha = jnp.exp2(m_prev - m_new)
            p = jnp.exp2(sc - m_new)
            acc_ref[rows, :] = alpha * acc_ref[rows, :] + _dot(p, vb, "nn")
            m_ref[rows, :] = m_new

        @pl.when(j == nk - 1)
        def _():
            den = acc_ref[:, LANES:]
            o_ref[...] = (acc_ref[:, :LANES] / den).astype(BF16)
            lse_ref[...] = m_ref[...] + jnp.log(den[:, :1]) * LOG2E

    (o, lse), got = _call(
        kern, name=name, grid=(MLA_HEADS, s // tq, nk),
        in_specs=[pl.BlockSpec((tq, 256), lambda h, i, j: (i, h)),
                  pl.BlockSpec((tk, 256), lambda h, i, j: (j, h)),
                  pl.BlockSpec((tk, 256), lambda h, i, j: (j, h))],
        out_specs=[pl.BlockSpec((tq, LANES), lambda h, i, j: (i, h)),
                   pl.BlockSpec((None, tq, 1), lambda h, i, j: (h, i, 0))],
        out_shape=[jax.ShapeDtypeStruct((s, 512), BF16), jax.ShapeDtypeStruct((MLA_HEADS, s, 1), F32)],
        scratch_shapes=[pltpu.VMEM((tq, 1), F32), pltpu.VMEM((tq, 256), F32)],
        args=[q, k, vx], sem=("parallel", "parallel", "arbitrary"), comm=comm)
    return o, lse, got


def _fa_bwd(name, q, k, kt, vx, do, lse, delta, tq, tk, nsub, comm=None):
    s = q.shape[0]
    tq, tk = min(tq, s), min(tk, s)
    nq = s // tq
    sk = tk // nsub

    nkb = s // tk

    def kern(q_ref, k_ref, kt_ref, v_ref, do_ref, lse_ref, dl_ref, dqt_ref, dk_ref, dv_ref, dqt_acc, dk_acc, dv_acc):
        j, i = pl.program_id(1), pl.program_id(2)

        @pl.when(i == 0)
        def _():
            dv_acc[...] = jnp.zeros(dv_acc.shape, F32)
            dk_acc[...] = jnp.zeros(dk_acc.shape, F32)

        @pl.when(j == 0)
        def _():
            dqt_acc[i] = jnp.zeros((256, tq), F32)

        qb, dob = q_ref[...], do_ref[...]
        lse_row, dl_row = lse_ref[...], dl_ref[...]
        dqt = dqt_acc[i]
        for c in range(nsub):
            rows = pl.ds(c * sk, sk)
            st = _dot(k_ref[rows, :], qb, "nt")
            pt = jnp.exp2(st - lse_row)
            dpt = _dot(v_ref[rows, :], dob, "nt")
            dst = (pt * (dpt - dl_row)).astype(BF16)
            dv_acc[rows, :] += _dot(pt, dob, "nn")
            dk_acc[rows, :] += _dot(dst, qb, "nn")
            dqt = dqt + _dot(kt_ref[:, rows], dst, "nn")
        dqt_acc[i] = dqt

        @pl.when(i == nq - 1)
        def _():
            dv_ref[...] = dv_acc[...].astype(BF16)
            dk_ref[...] = dk_acc[...].astype(BF16)

        @pl.when(j == nkb - 1)
        def _():
            dqt_ref[i] = dqt.astype(BF16)

    outs, got = _call(
        kern, name=name, grid=(MLA_HEADS, s // tk, nq),
        in_specs=[pl.BlockSpec((tq, 256), lambda h, j, i: (i, h)),
                  pl.BlockSpec((tk, 256), lambda h, j, i: (j, h)),
                  pl.BlockSpec((None, 256, tk), lambda h, j, i: (h, 0, j)),
                  pl.BlockSpec((tk, LANES), lambda h, j, i: (j, 2 * h)),
                  pl.BlockSpec((tq, LANES), lambda h, j, i: (i, h)),
                  pl.BlockSpec((None, 1, tq), lambda h, j, i: (h, 0, i)),
                  pl.BlockSpec((None, 1, tq), lambda h, j, i: (h, 0, i))],
        out_specs=[pl.BlockSpec((None, nq, 256, tq), lambda h, j, i: (h, 0, 0, 0)),
                   pl.BlockSpec((tk, 256), lambda h, j, i: (j, h)),
                   pl.BlockSpec((tk, LANES), lambda h, j, i: (j, h))],
        out_shape=[jax.ShapeDtypeStruct((MLA_HEADS, nq, 256, tq), BF16),
                   jax.ShapeDtypeStruct((s, 1024), BF16), jax.ShapeDtypeStruct((s, 512), BF16)],
        scratch_shapes=[pltpu.VMEM((nq, 256, tq), F32), pltpu.VMEM((tk, 256), F32), pltpu.VMEM((tk, LANES), F32)],
        args=[q, k, kt, vx, do, lse, delta], sem=("parallel", "arbitrary", "arbitrary"), comm=comm)
    return outs[0], outs[1], outs[2], got


def _fill_padded(dst_ref, val, s):
    zeros = jnp.zeros((CONV_HALO, LANES), F32)
    dst_ref[pl.ds(0, CONV_HALO), :] = zeros
    dst_ref[pl.ds(CONV_HALO + s, CONV_HALO), :] = zeros
    dst_ref[pl.ds(CONV_HALO, s), :] = val


def _shifted_windows(win):
    n = win.shape[0]
    return [win] + [pltpu.roll(win, n - b, 0) for b in range(1, 8)]


def _conv_fwd(name, z, w, bias, rc=256):
    s = z.shape[0]
    rc = min(rc, s)

    def kern(a_ref, g_ref, w_ref, b_ref, o_ref, pad_ref):
        _fill_padded(pad_ref, _glu_fn(a_ref[...].astype(F32), g_ref[...].astype(F32)), s)
        wv = w_ref[...]
        bv = b_ref[...]

        def chunk(r, carry):
            base = pl.multiple_of(r * rc, rc)
            wins = _shifted_windows(pad_ref[pl.ds(base, rc + 2 * CONV_HALO), :])
            acc = jnp.broadcast_to(bv, (rc, LANES))
            for kk in range(CONV_K):
                a, b = divmod(kk + 1, 8)
                acc = acc + wv[kk:kk + 1, :] * wins[b][8 * a:8 * a + rc]
            o_ref[pl.ds(base, rc), :] = acc.astype(BF16)
            return carry

        lax.fori_loop(0, s // rc, chunk, 0)

    nblk = D // LANES
    return pl.pallas_call(
        kern, name=name, grid=(nblk,),
        in_specs=[pl.BlockSpec((s, LANES), lambda c: (0, c)), pl.BlockSpec((s, LANES), lambda c: (0, nblk + c)),
                  pl.BlockSpec((32, LANES), lambda c: (0, c)), pl.BlockSpec((1, LANES), lambda c: (0, c))],
        out_specs=pl.BlockSpec((s, LANES), lambda c: (0, c)),
        out_shape=jax.ShapeDtypeStruct((s, D), BF16),
        scratch_shapes=[pltpu.VMEM((s + 2 * CONV_HALO, LANES), F32)],
        compiler_params=_cparams(("parallel",)),
    )(z, z, w, bias)


def _conv_bwd(name, z, g, w, rc=256, comm=None):
    s = z.shape[0]
    rc = min(rc, s)

    def kern(a_ref, b_ref, g_ref, w_ref, da_ref, db_ref, dw_ref, dbias_ref, sa_ref, sb_ref, upad_ref, gpad_ref,
             dwacc_ref):
        _fill_padded(upad_ref, _glu_fn(a_ref[...].astype(F32), b_ref[...].astype(F32)), s)
        _fill_padded(gpad_ref, g_ref[...].astype(F32), s)
        dwacc_ref[...] = jnp.zeros(dwacc_ref.shape, F32)
        wv = w_ref[...]

        def chunk(r, carry):
            sum_a, sum_b = carry
            base = pl.multiple_of(r * rc, rc)
            gwins = _shifted_windows(gpad_ref[pl.ds(base, rc + 2 * CONV_HALO), :])
            uwins = _shifted_windows(upad_ref[pl.ds(base, rc + 2 * CONV_HALO), :])
            gc = g_ref[pl.ds(base, rc), :].astype(F32)
            acc = jnp.zeros((rc, LANES), F32)
            for kk in range(CONV_K):
                a, b = divmod(CONV_K - kk, 8)
                acc = acc + wv[kk:kk + 1, :] * gwins[b][8 * a:8 * a + rc]
                a, b = divmod(kk + 1, 8)
                prod = gc * uwins[b][8 * a:8 * a + rc]
                dwacc_ref[kk] += jnp.sum(prod.reshape(rc // 8, 8, LANES), axis=0)
            dwacc_ref[CONV_K] += jnp.sum(gc.reshape(rc // 8, 8, LANES), axis=0)
            av = a_ref[pl.ds(base, rc), :].astype(F32)
            sg = _sigmoid(b_ref[pl.ds(base, rc), :].astype(F32))
            d_a = acc * sg
            d_b = acc * av * sg * (1.0 - sg)
            da_ref[pl.ds(base, rc), :] = d_a.astype(BF16)
            db_ref[pl.ds(base, rc), :] = d_b.astype(BF16)
            return (sum_a + jnp.sum(d_a.reshape(rc // 8, 8, LANES), axis=0),
                    sum_b + jnp.sum(d_b.reshape(rc // 8, 8, LANES), axis=0))

        zero = jnp.zeros((8, LANES), F32)
        sum_a, sum_b = lax.fori_loop(0, s // rc, chunk, (zero, zero))
        sa_ref[...] = jnp.sum(sum_a, axis=0, keepdims=True)
        sb_ref[...] = jnp.sum(sum_b, axis=0, keepdims=True)
        tot = jnp.sum(dwacc_ref[...], axis=1)
        lane_row = lax.broadcasted_iota(jnp.int32, (32, LANES), 0)
        dw_ref[...] = jnp.where(lane_row < CONV_K, tot, 0.0)
        dbias_ref[...] = tot[CONV_K:CONV_K + 1, :]

    nblk = D // LANES
    cs = pl.BlockSpec((s, LANES), lambda c: (0, c))
    vec = pl.BlockSpec((1, LANES), lambda c: (0, c))
    outs, got = _call(
        kern, name=name, grid=(nblk,),
        in_specs=[cs, pl.BlockSpec((s, LANES), lambda c: (0, nblk + c)), cs, pl.BlockSpec((32, LANES), lambda c: (0, c))],
        out_specs=[cs, cs, pl.BlockSpec((32, LANES), lambda c: (0, c)), vec, vec, vec],
        out_shape=[jax.ShapeDtypeStruct((s, D), BF16), jax.ShapeDtypeStruct((s, D), BF16),
                   jax.ShapeDtypeStruct((32, D), F32)] + [jax.ShapeDtypeStruct((1, D), F32)] * 3,
        scratch_shapes=[pltpu.VMEM((s + 2 * CONV_HALO, LANES), F32), pltpu.VMEM((s + 2 * CONV_HALO, LANES), F32),
                        pltpu.VMEM((32, 8, LANES), F32)],
        args=[z, z, g, w], sem=("parallel",), comm=comm)
    return outs, got


def _mod_local(name, c_all, ada_w):
    def kern(c_ref, w_ref, o_ref):
        o_ref[...] = jnp.dot(_silu(c_ref[...]), w_ref[...], preferred_element_type=F32,
                             precision=lax.Precision.HIGHEST)

    return pl.pallas_call(
        kern, name=name, grid=(DEPTH,),
        in_specs=[pl.BlockSpec((N_DEV, D), lambda l: (0, 0)), pl.BlockSpec((None, D, 384), lambda l: (l, 0, 0))],
        out_specs=pl.BlockSpec((None, N_DEV, 384), lambda l: (l, 0, 0)),
        out_shape=jax.ShapeDtypeStruct((DEPTH, N_DEV, 384), F32),
        compiler_params=_cparams(("parallel",)),
    )(c_all, ada_w)


def _ada_w_grad(name, c_all_t, dmod):
    def kern(c_ref, d_ref, o_ref):
        o_ref[...] = jnp.dot(_silu(c_ref[...]), d_ref[...], preferred_element_type=F32,
                             precision=lax.Precision.HIGHEST)

    return pl.pallas_call(
        kern, name=name, grid=(DEPTH,),
        in_specs=[pl.BlockSpec((D, LANES), lambda l: (0, 0)), pl.BlockSpec((None, LANES, 384), lambda l: (l, 0, 0))],
        out_specs=pl.BlockSpec((None, D, 384), lambda l: (l, 0, 0)),
        out_shape=jax.ShapeDtypeStruct((DEPTH, D, 384), F32),
        compiler_params=_cparams(("parallel",)),
    )(c_all_t, dmod)


def _pre_fn(x, g, scale, shift):
    return _rms(x, g) * (1.0 + scale) + shift


def _post_fn(y, g, gate):
    return gate * _rms(y, g)


def _ev_post_fn(o_heads, rg, a, mg):
    normed = []
    for oh in o_heads:
        mu = jnp.mean(oh, axis=-1, keepdims=True)
        var = jnp.mean(jnp.square(oh - mu), axis=-1, keepdims=True)
        normed.append((oh - mu) * lax.rsqrt(var + EPS))
    return jnp.concatenate([jnp.concatenate(normed, axis=1) * _silu(rg), a * _silu(mg)], axis=1)


def _od_post_fn(u, g, ln_g, ln_b):
    mu = jnp.mean(u, axis=-1, keepdims=True)
    var = jnp.mean(jnp.square(u - mu), axis=-1, keepdims=True)
    y = (u - mu) * lax.rsqrt(var + EPS) * ln_g + ln_b
    return _silu(y) * _silu(g)


def _rms_bwd(x, dxh_of):
    r = lax.rsqrt(jnp.mean(x * x, axis=-1, keepdims=True) + EPS)
    xh = x * r
    dxh = dxh_of(xh)
    return xh, r * (dxh - xh * jnp.mean(dxh * xh, axis=-1, keepdims=True))


def _pre_bwd(x, dh, g, scale):
    mod = 1.0 + scale
    xh, dx = _rms_bwd(x, lambda xh_: dh * (g * mod))
    t = dh * xh
    return dx, _colsum(t) * mod, _colsum(t) * g, _colsum(dh)


def _post_bwd(y, dout, g, gate):
    yh, dy = _rms_bwd(y, lambda yh_: dout * (gate * g))
    t = _colsum(dout * yh)
    return dy, t * gate, t * g


def _dsilu(x, s):
    return s * (1.0 + x * (1.0 - s))


def _ev_post_bwd(o_heads, rg, a, mg, da):
    d_ret, d_mla = da[:, :512], da[:, 512:]
    s_rg, s_mg = _sigmoid(rg), _sigmoid(mg)
    d_on = _heads(d_ret * (rg * s_rg))
    normed, do_heads = [], []
    for oh, dn in zip(o_heads, d_on):
        xc = oh - jnp.mean(oh, axis=-1, keepdims=True)
        r = lax.rsqrt(jnp.mean(xc * xc, axis=-1, keepdims=True) + EPS)
        xh = xc * r
        normed.append(xh)
        m1 = jnp.mean(dn, axis=-1, keepdims=True)
        m2 = jnp.mean(dn * xh, axis=-1, keepdims=True)
        do_heads.append(r * (dn - m1 - xh * m2))
    drg = d_ret * jnp.concatenate(normed, axis=1) * _dsilu(rg, s_rg)
    return do_heads, drg, d_mla * (mg * s_mg), d_mla * a * _dsilu(mg, s_mg)


def _od_post_bwd(u, g, da, ln_g, ln_b):
    xc = u - jnp.mean(u, axis=-1, keepdims=True)
    r = lax.rsqrt(jnp.mean(xc * xc, axis=-1, keepdims=True) + EPS)
    xh = xc * r
    y = xh * ln_g + ln_b
    s1, s2 = _sigmoid(y), _sigmoid(g)
    dg = da * (y * s1) * (s2 * (1.0 + g * (1.0 - s2)))
    dy = da * (g * s2) * (s1 * (1.0 + y * (1.0 - s1)))
    dxh = dy * ln_g
    m1 = jnp.mean(dxh, axis=-1, keepdims=True)
    m2 = jnp.mean(dxh * xh, axis=-1, keepdims=True)
    return r * (dxh - m1 - xh * m2), dg, _colsum(dy * xh), _colsum(dy)


def _glu_fn(a, b):
    return a * _sigmoid(b)


def _heads(x):
    return [x[:, LANES * h:LANES * (h + 1)] for h in range(4)]


def _colsum(x):
    return jnp.sum(x, axis=0, keepdims=True)


def _zrows(n, c):
    return jnp.zeros((n, c), BF16)


def _ev_win_layout(wt):
    rq = [p for h in range(4) for p in (wt[64 * h:64 * h + 64], _zrows(64, D))]
    rk = [p for h in range(4) for p in (wt[256 + 64 * h:256 + 64 * h + 64], _zrows(64, D))]
    return jnp.concatenate([wt[512:1024], wt[1024:1536], wt[2240:2752], wt[1536:1920],
                            wt[2176:2240], _zrows(64, D)] + rq + rk + [wt[1920:2176]], axis=0)


def _uq_layout(wt):
    return jnp.concatenate([p for h in range(4) for p in (wt[192 * h:192 * h + 192], _zrows(64, 384))], axis=0)


def _uq_unlayout(g):
    return jnp.concatenate([g[256 * h:256 * h + 192] for h in range(4)], axis=0)


def _ukv_layout(wt):
    kpart = [p for h in range(4) for p in (wt[256 * h:256 * h + 128], _zrows(128, 256))]
    vpart = [wt[256 * h + 128:256 * h + 256] for h in range(4)]
    return jnp.concatenate(kpart + vpart, axis=0)


def _ukv_unlayout(g):
    return jnp.concatenate([p for h in range(4) for p in (g[256 * h:256 * h + 128], g[1024 + 128 * h:1024 + 128 * h + 128])],
                           axis=0)


def kernel(x, c, positions, ada_w, ada_b, pre_g, post_g, ev_w_in, ev_dec_f, ev_dec_b, ev_q_norm_g, ev_w_uq, ev_kv_norm_g, ev_w_ukv, ev_w_out, od_w_in, od_b_in, od_dw_w, od_dw_b, od_ln_g, od_ln_b, od_w_out, loss_target, m_ada_w, m_ada_b, m_pre_g, m_post_g, m_ev_w_in, m_ev_dec_f, m_ev_dec_b, m_ev_q_norm_g, m_ev_w_uq, m_ev_kv_norm_g, m_ev_w_ukv, m_ev_w_out, m_od_w_in, m_od_b_in, m_od_dw_w, m_od_dw_b, m_od_ln_g, m_od_ln_b, m_od_w_out, v_ada_w, v_ada_b, v_pre_g, v_post_g, v_ev_w_in, v_ev_dec_f, v_ev_dec_b, v_ev_q_norm_g, v_ev_w_uq, v_ev_kv_norm_g, v_ev_w_ukv, v_ev_w_out, v_od_w_in, v_od_b_in, v_od_dw_w, v_od_dw_b, v_od_ln_g, v_od_ln_b, v_od_w_out):
    s = x.shape[1]
    me = 4 * lax.axis_index("x") + 2 * lax.axis_index("y") + lax.axis_index("c")
    x0 = x.reshape(s, D)
    tgt = loss_target.reshape(s, D)
    ret_chunk = 256
    fa_cfg_f = ((min(4096, s // 2), min(2048, s // 2), min(16, s // 512)),) * 2
    fa_cfg_b = ((min(2048, s // 2), min(4096, s // 2), min(16, s // 512)),) * 2

    start_parts = [c.reshape(-1), od_b_in.reshape(-1), od_dw_w.reshape(-1), od_dw_b.reshape(-1),
                   od_ln_g.reshape(-1), od_ln_b.reshape(-1)]
    start_sizes = [p.shape[0] for p in start_parts]
    start_len = -(-sum(start_sizes) // 1024) * 1024
    start_vec = jnp.concatenate(start_parts + [jnp.zeros((start_len - sum(start_sizes),), F32)])
    start_all, win0_all = _exchange("gather_start", [start_vec.reshape(-1, LANES), ev_w_in[0].T.astype(BF16)], False)
    start_all = start_all.reshape(N_DEV, start_len)
    offs = np.cumsum([0] + start_sizes)
    c_all = start_all[:, offs[0]:offs[1]]
    b_in_all = start_all[:, offs[1]:offs[2]].reshape(N_DEV, 2, 384).transpose(1, 0, 2).reshape(2, 1, ZW_OD)
    dw_w_all = start_all[:, offs[2]:offs[3]].reshape(N_DEV, 2, CONV_K, LANES).transpose(1, 2, 0, 3).reshape(2, CONV_K, D)
    dw_w_all = jnp.concatenate([dw_w_all, jnp.zeros((2, 1, D), F32)], axis=1)
    dw_b_all = start_all[:, offs[3]:offs[4]].reshape(N_DEV, 2, LANES).transpose(1, 0, 2).reshape(2, 1, D)
    ln_g_all = start_all[:, offs[4]:offs[5]].reshape(N_DEV, 2, LANES).transpose(1, 0, 2).reshape(2, 1, D)
    ln_b_all = start_all[:, offs[5]:offs[6]].reshape(N_DEV, 2, LANES).transpose(1, 0, 2).reshape(2, 1, D)

    mod_loc = _mod_local("mod_local", c_all, ada_w)
    mod_all = _exchange("gather_mod", [mod_loc.reshape(DEPTH * N_DEV, 384)], False)[0].reshape(N_DEV, DEPTH, N_DEV, 384)
    mod = lax.dynamic_index_in_dim(mod_all, me, axis=2, keepdims=False)
    mod = mod.transpose(1, 0, 2).reshape(DEPTH, 3 * D) + ada_b
    shift = [mod[l:l + 1, 0:D] for l in range(DEPTH)]
    scale = [mod[l:l + 1, D:2 * D] for l in range(DEPTH)]
    gate = [mod[l:l + 1, 2 * D:3 * D] for l in range(DEPTH)]

    def ev_shards(i):
        return [ev_w_in[i].T.astype(BF16), ev_w_uq[i].T.astype(BF16), ev_w_ukv[i].T.astype(BF16), ev_w_out[i].astype(BF16)]

    def od_shards(i):
        return [od_w_in[i].T.astype(BF16), od_w_out[i].astype(BF16)]

    def full(got):
        return [g.reshape(N_DEV * g.shape[1], g.shape[2]) for g in got]

    def ev_full(got):
        win_t, uq_t, ukv_t, wout = full(got)
        return (_ev_win_layout(win_t), _uq_layout(uq_t), _ukv_layout(ukv_t), wout)

    win0_t = _ev_win_layout(full([win0_all])[0])
    ev_w = [None, None]
    od_w = [None, None]
    later_w = [(t, False) for t in od_shards(0) + ev_shards(1) + od_shards(1)]

    inv_freq = ROPE_BASE ** (-jnp.arange(0, 64, 2, dtype=F32) / 64)
    invf = jnp.tile(inv_freq, 4).reshape(1, LANES)
    sgn = jnp.tile(jnp.concatenate([-jnp.ones((32,), F32), jnp.ones((32,), F32)]), 2).reshape(1, LANES)

    def rope_body(rows, vecs):
        ang = rows[0].astype(F32) * vecs[0]
        return [jnp.cos(ang), jnp.sin(ang) * vecs[1]], []

    (cos_t, sin_t), _ = _rowwise("rope_tables", rope_body, [(positions.reshape(s, 1), 1, 0)], [invf, sgn],
                                 [(LANES, F32), (LANES, F32)])

    saved = []
    xl = x0
    for l in range(DEPTH):
        i = l // 2
        sv = dict(x=xl)

        if l == 0:
            def pre_body(rows, vecs):
                return [_pre_fn(rows[0], *vecs)], []

            (h,), _ = _rowwise("pre0", pre_body, [(xl, D, 0)], [pre_g[0:1], scale[0], shift[0]], [(D, BF16)])
        sv["h"] = h
        if l % 2 == 0:
            if l == 0:
                z, got = _mm("ev_in0", h, win0_t, "nt", out_dtype=BF16, comm=[(t, False) for t in ev_shards(0)[1:]])
                uq0_t, ukv0_t, wout0 = full(got)
                ev_w[0] = (win0_t, _uq_layout(uq0_t), _ukv_layout(ukv0_t), wout0)
                win_t, uq_t, ukv_t, wout = ev_w[0]
            else:
                win_t, uq_t, ukv_t, wout = ev_w[i]
                z = _mm(f"ev_in{l}", h, win_t, "nt", out_dtype=BF16)
            sv["z"] = z
            dec_f, dec_b = ev_dec_f[i:i + 1], ev_dec_b[i:i + 1]

            def prep_body(rows, vecs):
                rq, rk, cq, ckv, kr, cos, sin = rows
                return [_rope(rq, cos, sin), _rope(rk, cos, sin) * RET_SCALE, _rms(cq, vecs[0]), _rms(ckv, vecs[1]),
                        _rope(kr, cos, sin)], []

            (rq_r, rk_r, qn, kvn, krr), _ = _rowwise(
                f"ev_prep{l}", prep_body,
                [(z, 512, 4), (z, 512, 5), (z, 384, 4), (z, 256, 12), (z, LANES, 15), (cos_t, LANES, 0), (sin_t, LANES, 0)],
                [ev_q_norm_g[i:i + 1], ev_kv_norm_g[i:i + 1]],
                [(512, BF16), (512, BF16), (384, BF16), (256, BF16), (LANES, BF16)])
            sv.update(rq_r=rq_r, rk_r=rk_r, qn=qn, kvn=kvn)
            o_f, st_f = _ret_fwd(f"ret_f{l}", rq_r, rk_r, z, dec_f, False, ret_chunk)
            o_b, st_b = _ret_fwd(f"ret_b{l}", rq_r, rk_r, z, dec_b, True, ret_chunk)
            sv.update(o_f=o_f, o_b=o_b, st_f=st_f, st_b=st_b)
            qcat, kcat, v_x, kcat_t = _mla_proj(f"mla_proj{l}", qn, uq_t, kvn, ukv_t, krr, cos_t, sin_t)
            sv["kcat_t"] = kcat_t
            a_mla, lse, got = _fa_fwd(f"fa_fwd{l}", qcat, kcat, v_x, *fa_cfg_f[i], comm=later_w if l == 0 else None)
            if l == 0:
                od_w[0], ev_w[1], od_w[1] = tuple(full(got[0:2])), ev_full(got[2:6]), tuple(full(got[6:8]))
            sv.update(qcat=qcat, kcat=kcat, v_x=v_x, a_mla=a_mla, lse=lse)

            def ev_post_body(rows, vecs):
                of, ob, rg, a, mg = rows
                return [_ev_post_fn(_heads(of + ob), rg, a, mg)], []

            (act,), _ = _rowwise(f"ev_post{l}", ev_post_body,
                                 [(o_f, 512, 0), (o_b, 512, 0), (z, 512, 1), (a_mla, 512, 0), (z, 512, 2)], [], [(D, BF16)])
        else:
            win_t, wout = od_w[i]
            z = _mm(f"od_in{l}", h, win_t, "nt", out_dtype=BF16, bias=b_in_all[i])
            sv["z"] = z

            u2 = _conv_fwd(f"conv{l}", z, dw_w_all[i], dw_b_all[i])
            sv.update(u2=u2)

            def od_post_body(rows, vecs):
                return [_od_post_fn(rows[0], rows[1], vecs[0], vecs[1])], []

            (act,), _ = _rowwise(f"od_post{l}", od_post_body, [(u2, D, 0), (z, D, 2)], [ln_g_all[i], ln_b_all[i]],
                                 [(D, BF16)])
        sv["act"] = act
        y = _mm(f"out{l}", act, wout, "nn", out_dtype=BF16)
        sv["y"] = y

        saved.append(sv)
        if l < DEPTH - 1:
            def post_body(rows, vecs):
                xn = rows[0] + _post_fn(rows[1], vecs[0], vecs[1])
                return [xn, _pre_fn(xn, vecs[2], vecs[3], vecs[4])], []

            (xl, h), _ = _rowwise(f"post{l}", post_body, [(xl, D, 0), (y, D, 0)],
                                  [post_g[l:l + 1], gate[l], pre_g[l + 1:l + 2], scale[l + 1], shift[l + 1]],
                                  [(D, F32), (D, BF16)])
        else:
            def post_body(rows, vecs):
                diff = rows[0] + _post_fn(rows[1], vecs[0], vecs[1]) - rows[2]
                return [diff * (1.0 / D)], [_colsum(diff * diff) * (0.5 / D)]

            (dx,), (loss_lanes,) = _rowwise(f"post{l}", post_body, [(xl, D, 0), (y, D, 0), (tgt, D, 0)],
                                            [post_g[l:l + 1], gate[l]], [(D, F32)], [(1, D)])
    loss = lax.psum(jnp.sum(loss_lanes), ("x", "y", "c"))

    g_pre, g_post, g_mod = [None] * DEPTH, [None] * DEPTH, [None] * DEPTH
    g_dec_f, g_dec_b, g_qn, g_kvn = [None] * 2, [None] * 2, [None] * 2, [None] * 2
    g_b_in, g_dw_w, g_dw_b, g_ln_g, g_ln_b = [None] * 2, [None] * 2, [None] * 2, [None] * 2, [None] * 2
    recv_ev, recv_od = [None] * 2, [None] * 2
    pending = []
    for l in reversed(range(DEPTH)):
        i = l // 2
        sv = saved[l]

        if l == DEPTH - 1:
            def post_bwd_body(rows, vecs):
                yv, dxn = rows
                d_y, dg, d_gate = _post_bwd(yv, dxn, vecs[0], vecs[1])
                return [d_y], [dg, d_gate]

            (dy,), (dpost, dgate) = _rowwise(f"post_bwd{l}", post_bwd_body, [(sv["y"], D, 0), (dx, D, 0)],
                                             [post_g[l:l + 1], gate[l]], [(D, BF16)], [(1, D), (1, D)])
        g_post[l] = dpost
        dgate_l = dgate
        wout = ev_w[i][3] if l % 2 == 0 else od_w[i][1]
        dact = _mm(f"out_dgrad{l}", dy, wout, "nt", out_dtype=BF16)
        d_wout = _mm(f"out_wgrad{l}", sv["act"], dy, "tn", out_dtype=BF16)
        z = sv["z"]
        if l % 2 == 0:
            win_t, uq_t, ukv_t, _ = ev_w[i]
            dec_f, dec_b = ev_dec_f[i:i + 1], ev_dec_b[i:i + 1]

            def ev_post_bwd_body(rows, vecs):
                of, ob, rg, a, mg, da = rows
                do_heads, drg, d_a, dmg = _ev_post_bwd(_heads(of + ob), rg, a, mg, da)
                d_a = d_a * LN2
                deltas = [jnp.sum(dh_ * ah_, axis=1, keepdims=True) for dh_, ah_ in zip(_heads(d_a), _heads(a))]
                return [jnp.concatenate(do_heads, axis=1), drg, d_a, dmg] + deltas, []

            (do_ret, drg, do_mla, dmg, dl0, dl1, dl2, dl3), _ = _rowwise(
                f"ev_post_bwd{l}", ev_post_bwd_body,
                [(sv["o_f"], 512, 0), (sv["o_b"], 512, 0), (z, 512, 1), (sv["a_mla"], 512, 0), (z, 512, 2), (dact, D, 0)],
                [], [(512, BF16), (512, BF16), (512, BF16), (512, BF16)] + [(1, F32)] * 4)
            delta = jnp.stack([dl0, dl1, dl2, dl3]).reshape(MLA_HEADS, 1, s)
            lse = sv["lse"].reshape(MLA_HEADS, 1, s)
            kt = sv["kcat_t"]
            if l == 0:
                pending = pending + [(d_wout.reshape(N_DEV, 128, D), True)]
            dqt, dkcat, dv, got = _fa_bwd(f"fa_bwd{l}", sv["qcat"], sv["kcat"], kt, sv["v_x"], do_mla, lse, delta,
                                          *fa_cfg_b[i], comm=pending)
            recv_od[i] = got[0:2]
            wout_recv = got[2:]
            dqcat = dqt.transpose(1, 3, 0, 2).reshape(s, 1024)

            def mla_prep_bwd_body(rows, vecs):
                dq, dk, dvv, cos, sin = rows
                qs = []
                dkrr = jnp.zeros((dq.shape[0], LANES), F32)
                for hh in range(4):
                    qs += [dq[:, 256 * hh:256 * hh + 128], _rope_t(dq[:, 256 * hh + 128:256 * hh + 256], cos, sin)]
                    dkrr = dkrr + dk[:, 256 * hh + 128:256 * hh + 256]
                return [jnp.concatenate(qs, axis=1) * (MLA_SCALE * LOG2E), jnp.concatenate([dk, dvv * LOG2E], axis=1), dkrr], []

            (dq_pad, dkv_pad, dkrr), _ = _rowwise(
                f"mla_prep_bwd{l}", mla_prep_bwd_body,
                [(dqcat, 1024, 0), (dkcat, 1024, 0), (dv, 512, 0), (cos_t, LANES, 0), (sin_t, LANES, 0)], [],
                [(1024, BF16), (1536, BF16), (LANES, F32)])
            dqn = _mm(f"uq_dgrad{l}", dq_pad, uq_t, "nn")
            d_uq = _mm(f"uq_wgrad{l}", dq_pad, sv["qn"], "tn", out_dtype=BF16)
            dkvn = _mm(f"ukv_dgrad{l}", dkv_pad, ukv_t, "nn")
            d_ukv = _mm(f"ukv_wgrad{l}", dkv_pad, sv["kvn"], "tn", out_dtype=BF16)
            dq_f, dk_f, dv_f, ddec_f = _ret_bwd(f"ret_f_bwd{l}", sv["rq_r"], sv["rk_r"], z, dec_f, sv["st_f"], do_ret,
                                                False, ret_chunk)
            dq_b, dk_b, dv_b, ddec_b = _ret_bwd(f"ret_b_bwd{l}", sv["rq_r"], sv["rk_r"], z, dec_b, sv["st_b"], do_ret,
                                                True, ret_chunk)
            g_dec_f[i], g_dec_b[i] = ddec_f[:, :RET_HEADS], ddec_b[:, :RET_HEADS]

            def prep_bwd_body(rows, vecs):
                cq, ckv, cos, sin, dqf, dqb, dkf, dkb, dvf, dvb, d_qn, d_kvn, d_krr = rows
                _, vjp_q = jax.vjp(_rms, cq, vecs[0])
                dcq, dgq = vjp_q(d_qn)
                _, vjp_kv = jax.vjp(_rms, ckv, vecs[1])
                dckv, dgkv = vjp_kv(d_kvn)
                return [dvf + dvb, dcq, _rope_t(d_krr, cos, sin), _rope_t(dqf + dqb, cos, sin),
                        _rope_t(dkf + dkb, cos, sin) * RET_SCALE, dckv], [dgq, dgkv]

            (drv, dcq, dkr, drq, drk, dckv), (dgq, dgkv) = _rowwise(
                f"ev_prep_bwd{l}", prep_bwd_body,
                [(z, 384, 4), (z, 256, 12), (cos_t, LANES, 0), (sin_t, LANES, 0), (dq_f, 512, 0), (dq_b, 512, 0),
                 (dk_f, 512, 0), (dk_b, 512, 0), (dv_f, 512, 0), (dv_b, 512, 0), (dqn, 384, 0), (dkvn, 256, 0),
                 (dkrr, LANES, 0)],
                [ev_q_norm_g[i:i + 1], ev_kv_norm_g[i:i + 1]],
                [(512, BF16), (384, BF16), (LANES, BF16), (512, BF16), (512, BF16), (256, BF16)], [(1, 384), (1, 256)])
            g_qn[i], g_kvn[i] = dgq, dgkv
            dz = [drv, drg, dmg, dcq, dkr, drq, drk, dckv]
            dz_off = [ZL_EV[nm][0] for nm in ("rv", "rg", "mg", "cq", "kr", "rq", "rk", "ckv")]
            g_rv, g_rg, g_mg, g_cq, g_kr, g_rq, g_rk, g_ckv = _mm_cols_tn(f"in_wgrad{l}", dz, sv["h"])
            d_win = jnp.concatenate([g_rq[128 * hh:128 * hh + 64] for hh in range(4)]
                                    + [g_rk[128 * hh:128 * hh + 64] for hh in range(4)]
                                    + [g_rv, g_rg, g_cq, g_ckv, g_kr[:64], g_mg], axis=0)
            pending = [(d_win.reshape(N_DEV, 344, D), True), (_uq_unlayout(d_uq).reshape(N_DEV, 96, 384), True),
                       (_ukv_unlayout(d_ukv).reshape(N_DEV, 128, 256), True)]
            if l > 0:
                pending.append((d_wout.reshape(N_DEV, 128, D), True))
        else:
            win_t, _ = od_w[i]

            def od_post_bwd_body(rows, vecs):
                u2, gg, da = rows
                du2, dgg, dlg, dlb = _od_post_bwd(u2, gg, da, vecs[0], vecs[1])
                return [du2, dgg], [dlg, dlb, _colsum(dgg)]

            (du2, dg_gate), (dlg, dlb, dbg) = _rowwise(
                f"od_post_bwd{l}", od_post_bwd_body, [(sv["u2"], D, 0), (z, D, 2), (dact, D, 0)],
                [ln_g_all[i], ln_b_all[i]], [(D, BF16), (D, BF16)], [(1, D), (1, D), (1, D)])
            g_ln_g[i], g_ln_b[i] = dlg, dlb
            (d_a, d_b, d_dw, d_dwb, dba, dbb), got = _conv_bwd(f"conv_bwd{l}", z, du2, dw_w_all[i], comm=pending)
            if pending:
                recv_ev[i + 1] = got
            g_dw_w[i], g_dw_b[i] = d_dw[:CONV_K], d_dwb
            g_b_in[i] = jnp.concatenate([dba, dbb, dbg], axis=1)
            dz, dz_off = [d_a, d_b, dg_gate], [0, D, 2 * D]
            d_win = jnp.concatenate(_mm_cols_tn(f"in_wgrad{l}", dz, sv["h"]), axis=0)
            pending = [(d_win.reshape(N_DEV, 384, D), True), (d_wout.reshape(N_DEV, 128, D), True)]
        if l == 0:
            dh, got = _mm_cols_nn(f"in_dgrad{l}", dz, dz_off, win_t, comm=pending)
            recv_ev[0] = list(got) + list(wout_recv)
        else:
            dh = _mm_cols_nn(f"in_dgrad{l}", dz, dz_off, win_t)

        if l > 0:
            def pre_bwd_body(rows, vecs):
                xv, d_h, dxn, yv = rows
                d_x, dg, dsc, dsh = _pre_bwd(xv, d_h, vecs[0], vecs[1])
                d_x = d_x + dxn
                d_y, dgp, d_gate = _post_bwd(yv, d_x, vecs[3], vecs[4])
                return [d_x, d_y], [dg, dsc, dsh, dgp, d_gate]

            (dx, dy), (dpre, dscale, dshift, dpost, dgate) = _rowwise(
                f"pre_bwd{l}", pre_bwd_body, [(sv["x"], D, 0), (dh, D, 0), (dx, D, 0), (saved[l - 1]["y"], D, 0)],
                [pre_g[l:l + 1], scale[l], shift[l], post_g[l - 1:l], gate[l - 1]], [(D, F32), (D, BF16)], [(1, D)] * 5)
        else:
            def pre_bwd_body(rows, vecs):
                xv, d_h, dxn = rows
                d_x, dg, dsc, dsh = _pre_bwd(xv, d_h, vecs[0], vecs[1])
                return [d_x + dxn], [dg, dsc, dsh]

            (dx,), (dpre, dscale, dshift) = _rowwise(f"pre_bwd{l}", pre_bwd_body, [(sv["x"], D, 0), (dh, D, 0), (dx, D, 0)],
                                                     [pre_g[l:l + 1], scale[l], shift[l]], [(D, F32)], [(1, D)] * 3)
        g_pre[l] = dpre
        g_mod[l] = jnp.concatenate([dshift, dscale, dgate_l], axis=1)

    grad_x = dx.reshape(1, s, D)

    end_parts = [jnp.concatenate(g_mod, axis=0), jnp.concatenate(g_pre, axis=0), jnp.concatenate(g_post, axis=0),
                 jnp.concatenate(g_dec_f, axis=0), jnp.concatenate(g_dec_b, axis=0), jnp.concatenate(g_qn, axis=0),
                 jnp.concatenate(g_kvn, axis=0), jnp.concatenate(g_b_in, axis=0), jnp.stack(g_dw_w),
                 jnp.concatenate(g_dw_b, axis=0), jnp.concatenate(g_ln_g, axis=0), jnp.concatenate(g_ln_b, axis=0)]
    end_shapes = [p.shape for p in end_parts]
    end_sizes = [int(np.prod(sh)) for sh in end_shapes]
    end_len = -(-sum(end_sizes) // 1024) * 1024
    end_vec = jnp.concatenate([p.reshape(-1) for p in end_parts] + [jnp.zeros((end_len - sum(end_sizes),), F32)])
    end_all = _exchange("gather_end", [end_vec.reshape(-1, LANES)], False)[0].reshape(N_DEV, end_len)
    eo = np.cumsum([0] + end_sizes)
    ends = [end_all[:, eo[j]:eo[j + 1]].reshape((N_DEV,) + tuple(end_shapes[j])) for j in range(len(end_parts))]
    (dmod_all, pre_all, post_all, decf_all, decb_all, qn_all, kvn_all, bin_all, dww_all, dwb_all, lng_all,
     lnb_all) = ends

    def pack_rep(*ts):
        lead = ts[0].ndim - 2
        return jnp.concatenate([t.reshape(t.shape[:lead] + (-1,)) for t in ts], axis=-1)

    rep_sizes = [DEPTH * 3 * D, DEPTH * D, DEPTH * D, 8, 8, 2 * 384, 2 * 256]
    rep_len = -(-sum(rep_sizes) // 1024) * 1024
    rep_pad = rep_len - sum(rep_sizes)

    def rep_rows(flat):
        padz = jnp.zeros(flat.shape[:-1] + (rep_pad,), F32)
        return jnp.concatenate([flat, padz], axis=-1).reshape(flat.shape[:-1] + (rep_len // LANES, LANES))

    rep_w = rep_rows(pack_rep(ada_b, pre_g, post_g, ev_dec_f, ev_dec_b, ev_q_norm_g, ev_kv_norm_g))
    rep_m = rep_rows(pack_rep(m_ada_b, m_pre_g, m_post_g, m_ev_dec_f, m_ev_dec_b, m_ev_q_norm_g, m_ev_kv_norm_g))
    rep_v = rep_rows(pack_rep(v_ada_b, v_pre_g, v_post_g, v_ev_dec_f, v_ev_dec_b, v_ev_q_norm_g, v_ev_kv_norm_g))
    rep_g = rep_rows(pack_rep(dmod_all, pre_all, post_all, decf_all, decb_all, qn_all, kvn_all))
    rep_out = _adamw("adamw_rep", rep_w, rep_m, rep_v, rep_g)
    ro = np.cumsum([0] + rep_sizes)
    rep_shapes = [(DEPTH, 3 * D), (DEPTH, D), (DEPTH, D), (2, 4), (2, 4), (2, 384), (2, 256)]

    def unpack_rep(t):
        flat = t.reshape(-1)
        return [flat[ro[j]:ro[j + 1]].reshape(rep_shapes[j]) for j in range(len(rep_shapes))]

    rep_res = [unpack_rep(t) for t in rep_out]

    def my_cols(t, width):
        return lax.dynamic_slice_in_dim(t, me * width, width, axis=t.ndim - 1)

    def vec_adamw(name, w, m, v, g_all, width):
        g = my_cols(g_all, width)
        r = _adamw(name, w.reshape(-1, width), m.reshape(-1, width), v.reshape(-1, width),
                   g.reshape(N_DEV, -1, width))
        return [t.reshape(w.shape) for t in r]

    res_b_in = vec_adamw("adamw_b_in", od_b_in, m_od_b_in, v_od_b_in, bin_all, 384)
    res_dw_w = vec_adamw("adamw_dw_w", od_dw_w, m_od_dw_w, v_od_dw_w, dww_all, LANES)
    res_dw_b = vec_adamw("adamw_dw_b", od_dw_b, m_od_dw_b, v_od_dw_b, dwb_all, LANES)
    res_ln_g = vec_adamw("adamw_ln_g", od_ln_g, m_od_ln_g, v_od_ln_g, lng_all, LANES)
    res_ln_b = vec_adamw("adamw_ln_b", od_ln_b, m_od_ln_b, v_od_ln_b, lnb_all, LANES)

    dmod_mine = my_cols(dmod_all, 384).transpose(1, 0, 2)
    dmod_pad = jnp.concatenate([dmod_mine, jnp.zeros((DEPTH, LANES - N_DEV, 384), F32)], axis=1)
    c_all_t = jnp.concatenate([c_all.T, jnp.zeros((D, LANES - N_DEV), F32)], axis=1)
    g_ada_w = _ada_w_grad("ada_w_grad", c_all_t, dmod_pad)
    res_ada_w = [t.reshape(ada_w.shape) for t in
                 _adamw("adamw_ada_w", ada_w.reshape(-1, 384), m_ada_w.reshape(-1, 384), v_ada_w.reshape(-1, 384),
                        g_ada_w.reshape(-1, 384))]

    ev_sh = [[_sum_parts(f"sum_g_ev{i}_{j}", r) for j, r in enumerate(recv_ev[i])] for i in range(2)]
    od_sh = [[_sum_parts(f"sum_g_od{i}_{j}", r) for j, r in enumerate(recv_od[i])] for i in range(2)]

    def mat_adamw(name, w, m, v, g):
        r = _adamw(name, w.reshape(-1, w.shape[-1]), m.reshape(-1, w.shape[-1]), v.reshape(-1, w.shape[-1]),
                   g.reshape(-1, w.shape[-1]))
        return [t.reshape(w.shape) for t in r]

    res_ev_w_in = mat_adamw("adamw_ev_w_in", ev_w_in, m_ev_w_in, v_ev_w_in, jnp.stack([ev_sh[i][0].T for i in range(2)]))
    res_ev_w_uq = mat_adamw("adamw_ev_w_uq", ev_w_uq, m_ev_w_uq, v_ev_w_uq, jnp.stack([ev_sh[i][1].T for i in range(2)]))
    res_ev_w_ukv = mat_adamw("adamw_ev_w_ukv", ev_w_ukv, m_ev_w_ukv, v_ev_w_ukv,
                             jnp.stack([ev_sh[i][2].T for i in range(2)]))
    res_ev_w_out = mat_adamw("adamw_ev_w_out", ev_w_out, m_ev_w_out, v_ev_w_out, jnp.stack([ev_sh[i][3] for i in range(2)]))
    res_od_w_in = mat_adamw("adamw_od_w_in", od_w_in, m_od_w_in, v_od_w_in, jnp.stack([od_sh[i][0].T for i in range(2)]))
    res_od_w_out = mat_adamw("adamw_od_w_out", od_w_out, m_od_w_out, v_od_w_out, jnp.stack([od_sh[i][1] for i in range(2)]))

    per_weight = [res_ada_w] + [[rep_res[t][j] for t in range(4)] for j in range(3)]
    per_weight += [res_ev_w_in, [rep_res[t][3] for t in range(4)], [rep_res[t][4] for t in range(4)],
                   [rep_res[t][5] for t in range(4)], res_ev_w_uq, [rep_res[t][6] for t in range(4)], res_ev_w_ukv,
                   res_ev_w_out, res_od_w_in, res_b_in, res_dw_w, res_dw_b, res_ln_g, res_ln_b, res_od_w_out]
    outs = [loss, grad_x]
    for t in range(4):
        outs += [pw[t] for pw in per_weight]
    return tuple(outs)
```
